```python
import jax, jax.numpy as jnp
from jax import lax
import numpy as np

D_MODEL = 1024
BATCH = 8
SEQ = 2048
DEPTH = 2

CTX_LEN = 256
GRID_W = 64
HEAD_DIM = 64
GLA_HEADS = 4
GLA_DK = 32
GLA_DV = 64
GLA_RANK = 16
GLA_TAU = 16.0
GLA_CHUNK = 64
NAT_HEADS = 4
NAT_WIN_ROWS = 8
NAT_WIN_COLS = 16
SWA_Q_HEADS = 8
SWA_KV_HEADS = 2
SWA_WINDOW = 128
SWA_BLOCK = SWA_WINDOW
ROPE_BASE = 10000.0
N_EXPERTS = 16
EXPERT_FF = 1024
CAPACITY_FACTOR = 2
NORM_EPS = 1e-6

GLA_WIDTH = GLA_HEADS * GLA_DV
NAT_WIDTH = NAT_HEADS * HEAD_DIM
SWA_WIDTH = SWA_Q_HEADS * HEAD_DIM
MIX_WIDTH = GLA_WIDTH + NAT_WIDTH + SWA_WIDTH
KV_WIDTHS = (GLA_HEADS * GLA_DK, GLA_WIDTH, GLA_RANK, GLA_RANK, NAT_WIDTH, NAT_WIDTH, SWA_KV_HEADS * HEAD_DIM, SWA_KV_HEADS * HEAD_DIM)
Q_WIDTHS = (GLA_HEADS * GLA_DK, GLA_WIDTH, NAT_WIDTH, SWA_WIDTH)
KV_WIDTH = sum(KV_WIDTHS)
IN_WIDTH = KV_WIDTH + sum(Q_WIDTHS)

kernel_name = 'hybrid_dit_gla_natten_swa_ecmoe'


def _split(a, widths):
    idx = np.cumsum(widths)[:-1].tolist()
    return jnp.split(a, idx, axis=-1)


def _heads(a, n):
    return a.reshape(a.shape[0], a.shape[1], n, -1)


def _flip(a):
    return jnp.flip(a, axis=1)


def rms_norm(x, gain):
    xf = x.astype(jnp.float32)
    y = xf * lax.rsqrt(jnp.mean(xf * xf, axis=-1, keepdims=True) + NORM_EPS)
    return (y * gain.astype(jnp.float32)).astype(x.dtype)


def modulate(x, shift, scale):
    return x * (1 + scale) + shift


def _rope_axis(x, pos):
    half = x.shape[-1] // 2
    freqs = ROPE_BASE ** (-jnp.arange(half, dtype=jnp.float32) / half)
    ang = pos.astype(jnp.float32)[:, None] * freqs
    cos = jnp.cos(ang)[:, None, :]
    sin = jnp.sin(ang)[:, None, :]
    xf = x.astype(jnp.float32)
    x1, x2 = xf[..., :half], xf[..., half:]
    return jnp.concatenate([x1 * cos - x2 * sin, x2 * cos + x1 * sin], axis=-1).astype(x.dtype)


def axial_rope(x):
    t = jnp.arange(x.shape[1])
    half = x.shape[-1] // 2
    return jnp.concatenate([_rope_axis(x[..., :half], t // GRID_W), _rope_axis(x[..., half:], t % GRID_W)], axis=-1)


def gla_log_decay(a_low, w_up, b_up):
    z = (a_low @ w_up + b_up).astype(jnp.float32)
    la = jax.nn.log_sigmoid(z) / GLA_TAU
    return la.reshape(la.shape[0], la.shape[1], GLA_HEADS, GLA_DK)


def gla_chunk_scan(q, k, v, log_a, s0):
    bsz, t, h, _ = q.shape
    nc = t // GLA_CHUNK

    def to_chunks(a):
        return a.reshape(bsz, nc, GLA_CHUNK, *a.shape[2:]).swapaxes(0, 1)

    causal = jnp.tril(jnp.ones((GLA_CHUNK, GLA_CHUNK), dtype=bool))[None, :, :, None, None]

    def step(state, inp):
        qc, kc, vc, lc = inp
        cum = jnp.cumsum(lc, axis=1)
        inter = jnp.einsum('bchk,bhkv->bchv', qc * jnp.exp(cum), state)
        diff = jnp.where(causal, cum[:, :, None] - cum[:, None, :], -jnp.inf)
        scores = jnp.einsum('bihk,bjhk,bijhk->bhij', qc, kc, jnp.exp(diff))
        intra = jnp.einsum('bhij,bjhv->bihv', scores, vc)
        last = cum[:, -1]
        state = jnp.exp(last)[..., None] * state + jnp.einsum('bjhk,bjhv->bhkv', kc * jnp.exp(last[:, None] - cum), vc)
        return state, inter + intra

    s_fin, o = lax.scan(step, s0, (to_chunks(q), to_chunks(k), to_chunks(v), to_chunks(log_a)))
    return o.swapaxes(0, 1).reshape(bsz, t, h, v.shape[-1]), s_fin


def gla_bidir(q, k, v, la_f, la_b, s0_f, s0_b):
    o_f, s_f = gla_chunk_scan(q, k, v, la_f, s0_f)
    o_b, s_b = gla_chunk_scan(_flip(q), _flip(k), _flip(v), _flip(la_b), s0_b)
    return o_f + _flip(o_b), s_f, s_b


def gla_final_state(k, v, log_a):
    cum = jnp.cumsum(log_a, axis=1)
    return jnp.einsum('bthk,bthv->bhkv', k * jnp.exp(cum[:, -1:] - cum), v)


def gla_group_out(o, gate, gain, dtype):
    o = rms_norm(o, gain)
    return (o.reshape(o.shape[0], o.shape[1], -1) * jax.nn.silu(gate.astype(jnp.float32))).astype(dtype)


def neighborhood_attention(q, k, v, k_ctx, v_ctx, rpb):
    bsz, s, h, d = q.shape
    rows = s // GRID_W
    wr = min(NAT_WIN_ROWS, rows)
    n_loc = wr * NAT_WIN_COLS
    qg = q.reshape(bsz, rows, GRID_W, h, d)
    kg = k.reshape(bsz, rows, GRID_W, h, d)
    vg = v.reshape(bsz, rows, GRID_W, h, d)
    cols = jnp.arange(GRID_W)
    col_start = jnp.clip(cols - NAT_WIN_COLS // 2, 0, GRID_W - NAT_WIN_COLS)
    col_idx = col_start[:, None] + jnp.arange(NAT_WIN_COLS)
    col_bias_idx = col_idx - cols[:, None] + NAT_WIN_COLS - 1
    scale = d ** -0.5

    def row_block(r):
        r_start = jnp.clip(r - wr // 2, 0, rows - wr)
        q_r = lax.dynamic_index_in_dim(qg, r, axis=1, keepdims=False)
        k_nb = lax.dynamic_slice_in_dim(kg, r_start, wr, axis=1)[:, :, col_idx]
        v_nb = lax.dynamic_slice_in_dim(vg, r_start, wr, axis=1)[:, :, col_idx]
        s_loc = jnp.einsum('bchd,brckhd->bhcrk', q_r, k_nb).astype(jnp.float32) * scale
        row_bias_idx = r_start + jnp.arange(wr) - r + NAT_WIN_ROWS - 1
        bias = rpb[:, row_bias_idx][:, :, col_bias_idx]
        s_loc = s_loc + bias.transpose(0, 2, 1, 3).astype(jnp.float32)[None]
        s_ctx = jnp.einsum('bchd,bmhd->bhcm', q_r, k_ctx).astype(jnp.float32) * scale
        logits = jnp.concatenate([s_loc.reshape(bsz, h, GRID_W, n_loc), s_ctx], axis=-1)
        p = jax.nn.softmax(logits, axis=-1).astype(v.dtype)
        p_loc = p[..., :n_loc].reshape(bsz, h, GRID_W, wr, NAT_WIN_COLS)
        return (jnp.einsum('bhcrk,brckhd->bchd', p_loc, v_nb)
                + jnp.einsum('bhcm,bmhd->bchd', p[..., n_loc:], v_ctx))

    out = lax.map(row_block, jnp.arange(rows))
    return out.transpose(1, 0, 2, 3, 4).reshape(bsz, s, h * d)


def sliding_window_attention(q, k, v, k_ctx, v_ctx, sink):
    bsz, s, hq, d = q.shape
    hkv = k.shape[2]
    g = hq // hkv
    nb = s // SWA_BLOCK
    n_loc = 3 * SWA_BLOCK
    m = k_ctx.shape[1]
    scale = d ** -0.5
    qb = q.reshape(bsz, nb, SWA_BLOCK, hkv, g, d)

    def band(a):
        ap = jnp.pad(a, ((0, 0), (SWA_BLOCK, SWA_BLOCK), (0, 0), (0, 0)))
        ab = ap.reshape(bsz, nb + 2, SWA_BLOCK, hkv, d)
        return jnp.concatenate([ab[:, :-2], ab[:, 1:-1], ab[:, 2:]], axis=2)

    kb, vb = band(k), band(v)
    blk = jnp.arange(nb)[:, None] * SWA_BLOCK
    q_pos = (blk + jnp.arange(SWA_BLOCK))[:, :, None]
    k_pos = (blk - SWA_BLOCK + jnp.arange(n_loc))[:, None, :]
    valid = (jnp.abs(k_pos - q_pos) <= SWA_WINDOW) & (k_pos >= 0) & (k_pos < s)
    s_loc = jnp.einsum('bnqhgd,bnkhd->bnhgqk', qb, kb).astype(jnp.float32) * scale
    s_loc = jnp.where(valid[None, :, None, None], s_loc, -jnp.inf)
    s_ctx = jnp.einsum('bnqhgd,bmhd->bnhgqm', qb, k_ctx).astype(jnp.float32) * scale
    sink_col = jnp.broadcast_to(sink.astype(jnp.float32).reshape(1, 1, hkv, g, 1, 1), (bsz, nb, hkv, g, SWA_BLOCK, 1))
    p = jax.nn.softmax(jnp.concatenate([s_loc, s_ctx, sink_col], axis=-1), axis=-1).astype(v.dtype)
    out = (jnp.einsum('bnhgqk,bnkhd->bnqhgd', p[..., :n_loc], vb)
           + jnp.einsum('bnhgqm,bmhd->bnqhgd', p[..., n_loc:n_loc + m], v_ctx))
    return out.reshape(bsz, s, hq * d)


def dense_ctx_attention(q, k, v, sink):
    bsz, m, hq, d = q.shape
    hkv = k.shape[2]
    g = hq // hkv
    n_keys = k.shape[1]
    qg = q.reshape(bsz, m, hkv, g, d)
    logits = jnp.einsum('bqhgd,bkhd->bhgqk', qg, k).astype(jnp.float32) * d ** -0.5
    if sink is not None:
        sink_col = jnp.broadcast_to(sink.astype(jnp.float32).reshape(1, hkv, g, 1, 1), (bsz, hkv, g, m, 1))
        logits = jnp.concatenate([logits, sink_col], axis=-1)
    p = jax.nn.softmax(logits, axis=-1)[..., :n_keys].astype(v.dtype)
    return jnp.einsum('bhgqk,bkhd->bqhgd', p, v).reshape(bsz, m, hq * d)


def token_mixers(h, hc, w_in, a_up, a_bias, gla_gain, rpb, sink, update_ctx):
    f32 = jnp.float32
    bsz = h.shape[0]
    (gk, gv, gaf, gab, nk, nv, sk, sv, gq, gg, nq, sq) = _split(h @ w_in, KV_WIDTHS + Q_WIDTHS)
    if update_ctx:
        (cgk, cgv, cgaf, cgab, cnk, cnv, csk, csv, cgq, cgg, cnq, csq) = _split(hc @ w_in, KV_WIDTHS + Q_WIDTHS)
    else:
        (cgk, cgv, cgaf, cgab, cnk, cnv, csk, csv) = _split(hc @ w_in[:, :KV_WIDTH], KV_WIDTHS)

    ck = _heads(cgk, GLA_HEADS).astype(f32)
    cv = _heads(cgv, GLA_HEADS).astype(f32)
    cla_f = gla_log_decay(cgaf, a_up[0], a_bias[0])
    cla_b = gla_log_decay(cgab, a_up[1], a_bias[1])
    if update_ctx:
        cq = _heads(cgq, GLA_HEADS).astype(f32) * GLA_DK ** -0.5
        zero = jnp.zeros((bsz, GLA_HEADS, GLA_DK, GLA_DV), f32)
        co, s_f, s_b = gla_bidir(cq, ck, cv, cla_f, cla_b, zero, zero)
    else:
        s_f = gla_final_state(ck, cv, cla_f)
        s_b = gla_final_state(_flip(ck), _flip(cv), _flip(cla_b))
    q = _heads(gq, GLA_HEADS).astype(f32) * GLA_DK ** -0.5
    k = _heads(gk, GLA_HEADS).astype(f32)
    v = _heads(gv, GLA_HEADS).astype(f32)
    la_f = gla_log_decay(gaf, a_up[0], a_bias[0])
    la_b = gla_log_decay(gab, a_up[1], a_bias[1])
    o, _, _ = gla_bidir(q, k, v, la_f, la_b, s_f, s_b)
    gla_lat = gla_group_out(o, gg, gla_gain, h.dtype)

    nk_c, nv_c = _heads(cnk, NAT_HEADS), _heads(cnv, NAT_HEADS)
    nat_lat = neighborhood_attention(_heads(nq, NAT_HEADS), _heads(nk, NAT_HEADS), _heads(nv, NAT_HEADS), nk_c, nv_c, rpb)

    sk_c, sv_c = _heads(csk, SWA_KV_HEADS), _heads(csv, SWA_KV_HEADS)
    swa_lat = sliding_window_attention(axial_rope(_heads(sq, SWA_Q_HEADS)), axial_rope(_heads(sk, SWA_KV_HEADS)),
                                       _heads(sv, SWA_KV_HEADS), sk_c, sv_c, sink)

    lat = jnp.concatenate([gla_lat, nat_lat, swa_lat], axis=-1)
    if not update_ctx:
        return lat, None
    ctx_out = jnp.concatenate([
        gla_group_out(co, cgg, gla_gain, hc.dtype),
        dense_ctx_attention(_heads(cnq, NAT_HEADS), nk_c, nv_c, None),
        dense_ctx_attention(_heads(csq, SWA_Q_HEADS), sk_c, sv_c, sink),
    ], axis=-1)
    return lat, ctx_out


def expert_choice_ffn(h, w_router, w_gate, w_up, w_down):
    bsz, n, _ = h.shape
    cap = CAPACITY_FACTOR * n // N_EXPERTS
    aff = jax.nn.softmax(jnp.einsum('bnd,de->bne', h, w_router).astype(jnp.float32), axis=-1)
    g, idx = lax.top_k(aff.transpose(0, 2, 1), cap)
    bidx = jnp.arange(bsz)[:, None, None]
    xs = h[bidx, idx]
    hid = jax.nn.silu(jnp.einsum('becd,edf->becf', xs, w_gate)) * jnp.einsum('becd,edf->becf', xs, w_up)
    y = jnp.einsum('becf,efd->becd', hid, w_down) * g[..., None].astype(h.dtype)
    return jnp.zeros_like(h).at[bidx, idx].add(y)


def setup_inputs(seed: int = 0) -> dict:
    key = jax.random.key(seed)
    ks = jax.random.split(key, 18)

    def nrm(k, shape, scale):
        return jax.random.normal(k, shape, jnp.float32) * scale

    return {
        'x': nrm(ks[0], (BATCH, SEQ, D_MODEL), 1.0),
        'c': nrm(ks[1], (BATCH, D_MODEL), 1.0),
        'ctx': nrm(ks[2], (BATCH, CTX_LEN, D_MODEL), 1.0),
        'c_ctx': nrm(ks[3], (D_MODEL,), 1.0),
        'w_mod': nrm(ks[4], (DEPTH, D_MODEL, 6 * D_MODEL), 0.5 * D_MODEL ** -0.5),
        'b_mod': nrm(ks[5], (DEPTH, 6 * D_MODEL), 0.02),
        'norm_gains': 1.0 + nrm(ks[6], (DEPTH, 4, D_MODEL), 0.05),
        'w_in': nrm(ks[7], (DEPTH, D_MODEL, IN_WIDTH), D_MODEL ** -0.5),
        'w_out': nrm(ks[8], (DEPTH, MIX_WIDTH, D_MODEL), MIX_WIDTH ** -0.5),
        'gla_a_up': nrm(ks[9], (DEPTH, 2, GLA_RANK, GLA_HEADS * GLA_DK), GLA_RANK ** -0.5),
        'gla_a_bias': 1.0 + nrm(ks[10], (DEPTH, 2, GLA_HEADS * GLA_DK), 0.5),
        'gla_norm': 1.0 + nrm(ks[11], (DEPTH, GLA_DV), 0.05),
        'nat_rpb': nrm(ks[12], (DEPTH, NAT_HEADS, 2 * NAT_WIN_ROWS - 1, 2 * NAT_WIN_COLS - 1), 0.02),
        'swa_sink': nrm(ks[13], (DEPTH, SWA_Q_HEADS), 0.5),
        'w_router': nrm(ks[14], (DEPTH, D_MODEL, N_EXPERTS), D_MODEL ** -0.5),
        'w_gate': nrm(ks[15], (DEPTH, N_EXPERTS, D_MODEL, EXPERT_FF), D_MODEL ** -0.5),
        'w_up': nrm(ks[16], (DEPTH, N_EXPERTS, D_MODEL, EXPERT_FF), D_MODEL ** -0.5),
        'w_down': nrm(ks[17], (DEPTH, N_EXPERTS, EXPERT_FF, D_MODEL), EXPERT_FF ** -0.5),
    }


def reference(x, c, ctx, c_ctx, w_mod, b_mod, norm_gains, w_in, w_out, gla_a_up, gla_a_bias, gla_norm,
              nat_rpb, swa_sink, w_router, w_gate, w_up, w_down):
    xc = ctx
    for l in range(DEPTH):
        update_ctx = l < DEPTH - 1
        mod = jax.nn.silu(c) @ w_mod[l] + b_mod[l]
        mod_c = jax.nn.silu(c_ctx) @ w_mod[l] + b_mod[l]
        sh_a, sc_a, gt_a, sh_f, sc_f, gt_f = jnp.split(mod[:, None, :], 6, axis=-1)
        csh_a, csc_a, cgt_a, csh_f, csc_f, cgt_f = jnp.split(mod_c, 6, axis=-1)
        g_pre_a, g_post_a, g_pre_f, g_post_f = norm_gains[l]

        h = modulate(rms_norm(x, g_pre_a), sh_a, sc_a)
        hc = modulate(rms_norm(xc, g_pre_a), csh_a, csc_a)
        mix, mix_c = token_mixers(h, hc, w_in[l], gla_a_up[l], gla_a_bias[l], gla_norm[l], nat_rpb[l], swa_sink[l], update_ctx)
        x = x + gt_a * rms_norm(mix @ w_out[l], g_post_a)
        if update_ctx:
            xc = xc + cgt_a * rms_norm(mix_c @ w_out[l], g_post_a)

        h = modulate(rms_norm(x, g_pre_f), sh_f, sc_f)
        x = x + gt_f * rms_norm(expert_choice_ffn(h, w_router[l], w_gate[l], w_up[l], w_down[l]), g_post_f)
        if update_ctx:
            hc = modulate(rms_norm(xc, g_pre_f), csh_f, csc_f)
            xc = xc + cgt_f * rms_norm(expert_choice_ffn(hc, w_router[l], w_gate[l], w_up[l], w_down[l]), g_post_f)
    return x
```

```python
import functools

import jax
import jax.numpy as jnp
import numpy as np
from jax import lax
from jax.experimental import pallas as pl
from jax.experimental.pallas import tpu as pltpu

F32 = jnp.float32
BF16 = jnp.bfloat16

D_MODEL = 1024
GRID_W = 64
HEAD_DIM = 64
GLA_HEADS = 4
GLA_DK = 32
GLA_DV = 64
GLA_RANK = 16
GLA_TAU = 16.0
GLA_CHUNK = 64
NAT_HEADS = 4
NAT_WIN_ROWS = 8
NAT_WIN_COLS = 16
SWA_Q_HEADS = 8
SWA_KV_HEADS = 2
SWA_WINDOW = 128
SWA_BLOCK = 128
ROPE_BASE = 10000.0
N_EXPERTS = 16
EXPERT_FF = 1024
CAPACITY_FACTOR = 2
NORM_EPS = 1e-6

GLA_KW = GLA_HEADS * GLA_DK
GLA_VW = GLA_HEADS * GLA_DV
NAT_W = NAT_HEADS * HEAD_DIM
SWA_QW = SWA_Q_HEADS * HEAD_DIM
SWA_KW = SWA_KV_HEADS * HEAD_DIM
LANES = 128
NEG_BIG = -1e30

GLA_PROJ_W = 2 * GLA_KW + 2 * GLA_VW + LANES
GLA_OUT_W = 2 * GLA_KW + 2 * GLA_VW + 2 * GLA_KW
NAT_PROJ_W = 3 * NAT_W
SWA_PROJ_W = SWA_QW + 4 * SWA_KW
IN_PROJ_W = GLA_PROJ_W + NAT_PROJ_W + SWA_PROJ_W

VMEM_LIMIT = 56 * 1024 * 1024


def _cparams(*sem):
    return pltpu.CompilerParams(dimension_semantics=sem, vmem_limit_bytes=VMEM_LIMIT)


def _dot(a, b):
    return jnp.dot(a, b, preferred_element_type=F32)


def _dot_nt(a, b):
    return lax.dot_general(a, b, (((1,), (1,)), ((), ())), preferred_element_type=F32)


def _dot_tn(a, b):
    return lax.dot_general(a, b, (((0,), (0,)), ((), ())), preferred_element_type=F32)


def _split2(a):
    hi = a.astype(BF16)
    lo = (a - hi.astype(F32)).astype(BF16)
    return hi, lo


def _split3(a):
    hi = a.astype(BF16)
    r = a - hi.astype(F32)
    mid = r.astype(BF16)
    lo = (r - mid.astype(F32)).astype(BF16)
    return hi, mid, lo


def _dot_f32(a, b):
    ah, al = _split2(a)
    bh, bl = _split2(b)
    return _dot(ah, bh) + _dot(al, bh) + _dot(ah, bl)


def _rms(x, gain):
    return x * lax.rsqrt(jnp.mean(x * x, axis=-1, keepdims=True) + NORM_EPS) * gain


def _silu(x):
    return x * jax.nn.sigmoid(x)


def _mod_kernel(c_ref, w_ref, b_ref, o_ref):
    a = _silu(c_ref[...])
    o_ref[0] = _dot_f32(a, w_ref[0]) + b_ref[0]


def _modulation(cc, w_mod, b_mod):
    depth, d, n = w_mod.shape
    r = cc.shape[0]
    tn = 1536
    return pl.pallas_call(
        _mod_kernel,
        grid=(depth, n // tn),
        in_specs=[
            pl.BlockSpec((r, d), lambda l, j: (0, 0)),
            pl.BlockSpec((1, d, tn), lambda l, j: (l, 0, j)),
            pl.BlockSpec((1, 1, tn), lambda l, j: (l, 0, j)),
        ],
        out_specs=pl.BlockSpec((1, r, tn), lambda l, j: (l, 0, j)),
        out_shape=jax.ShapeDtypeStruct((depth, r, n), F32),
        compiler_params=_cparams("parallel", "parallel"),
        name="adaln_mod",
    )(cc, w_mod, b_mod.reshape(depth, 1, n))


def _rope_rotate(x, first_half):
    up = pltpu.roll(x, LANES - 16, 1)
    down = pltpu.roll(x, 16, 1)
    return jnp.where(first_half, up, down)


def _in_proj_kernel(*refs, rope):
    if rope:
        x_ref, mod_ref, gain_ref, w_ref, aup_ref, ab_ref, cos_ref, sin_ref, gla_ref, nat_ref, swa_ref = refs
    else:
        x_ref, mod_ref, gain_ref, w_ref, aup_ref, ab_ref, gla_ref, nat_ref, swa_ref = refs
    x = x_ref[0]
    y = _rms(x, gain_ref[0:1, :])
    h = (y * (1.0 + mod_ref[0, 1:2, :]) + mod_ref[0, 0:1, :]).astype(BF16)

    kq = 2 * GLA_KW + 2 * GLA_VW
    pg = _dot(h, w_ref[:, 0:GLA_PROJ_W])
    gla_ref[0, :, 0:kq] = pg[:, 0:kq]
    z = _dot_f32(pg[:, kq:kq + LANES], aup_ref[...]) + ab_ref[...]
    log_sig = jnp.minimum(z, 0.0) - jnp.log1p(jnp.exp(-jnp.abs(z)))
    gla_ref[0, :, kq:kq + 2 * GLA_KW] = log_sig * (1.0 / GLA_TAU)

    o = GLA_PROJ_W
    pn = _dot(h, w_ref[:, o:o + NAT_PROJ_W])
    scale = HEAD_DIM ** -0.5
    nat_ref[0, :, 0:NAT_W] = (pn[:, 0:NAT_W] * scale).astype(BF16)
    nat_ref[0, :, NAT_W:] = pn[:, NAT_W:].astype(BF16)

    o = GLA_PROJ_W + NAT_PROJ_W
    ps = _dot(h, w_ref[:, o:o + SWA_PROJ_W])
    n_rot = (SWA_QW + 2 * SWA_KW) // LANES
    if rope:
        cos = cos_ref[...]
        sin = sin_ref[...]
        lane = lax.broadcasted_iota(jnp.int32, cos.shape, 1)
        first_half = (lane % 32) < 16
    for j in range(SWA_PROJ_W // LANES):
        t = ps[:, j * LANES:(j + 1) * LANES]
        if rope and j < n_rot:
            t = t * cos + _rope_rotate(t, first_half) * sin
        if j < SWA_QW // LANES:
            t = t * scale
        swa_ref[0, :, j * LANES:(j + 1) * LANES] = t.astype(BF16)


def _in_proj(x, mod, gains, w, aup, abias, rope_tabs, tm):
    bsz, t, d = x.shape
    rope = rope_tabs is not None
    in_specs = [
        pl.BlockSpec((1, tm, d), lambda b, i: (b, i, 0)),
        pl.BlockSpec((1, 6, d), lambda b, i: (b, 0, 0)),
        pl.BlockSpec((4, d), lambda b, i: (0, 0)),
        pl.BlockSpec((d, IN_PROJ_W), lambda b, i: (0, 0)),
        pl.BlockSpec((LANES, 2 * GLA_KW), lambda b, i: (0, 0)),
        pl.BlockSpec((1, 2 * GLA_KW), lambda b, i: (0, 0)),
    ]
    args = [x, mod, gains, w, aup, abias]
    if rope:
        in_specs += [pl.BlockSpec((tm, LANES), lambda b, i: (i, 0))] * 2
        args += list(rope_tabs)
    return pl.pallas_call(
        functools.partial(_in_proj_kernel, rope=rope),
        grid=(bsz, t // tm),
        in_specs=in_specs,
        out_specs=[
            pl.BlockSpec((1, tm, GLA_OUT_W), lambda b, i: (b, i, 0)),
            pl.BlockSpec((1, tm, NAT_PROJ_W), lambda b, i: (b, i, 0)),
            pl.BlockSpec((1, tm, SWA_PROJ_W), lambda b, i: (b, i, 0)),
        ],
        out_shape=[
            jax.ShapeDtypeStruct((bsz, t, GLA_OUT_W), F32),
            jax.ShapeDtypeStruct((bsz, t, NAT_PROJ_W), BF16),
            jax.ShapeDtypeStruct((bsz, t, SWA_PROJ_W), BF16),
        ],
        compiler_params=_cparams("parallel", "parallel"),
        name="in_proj_rope" if rope else "in_proj",
    )(*args)


def _gla_kernel(p_ref, s0f_ref, s0b_ref, gain_ref, o_ref, sff_ref, sfb_ref, acc_ref, sf_ref, sb_ref):
    t = p_ref.shape[1]
    c = GLA_CHUNK
    nc = t // c
    row = lax.broadcasted_iota(jnp.int32, (c, c), 0)
    col = lax.broadcasted_iota(jnp.int32, (c, c), 1)
    lower = row >= col
    upper = row <= col
    tri_l = lower.astype(BF16)
    tri_u = upper.astype(BF16)
    klane = lax.broadcasted_iota(jnp.int32, (1, GLA_KW), 1) // GLA_DK
    vlane = lax.broadcasted_iota(jnp.int32, (1, GLA_VW), 1) // GLA_DV
    srow = lax.broadcasted_iota(jnp.int32, (GLA_VW, GLA_KW), 0) // GLA_DV
    scol = lax.broadcasted_iota(jnp.int32, (GLA_VW, GLA_KW), 1) // GLA_DK
    head_diag = srow == scol
    q_scale = GLA_DK ** -0.5

    sf_ref[...] = s0f_ref[0]
    sb_ref[...] = s0b_ref[0]
    acc_ref[...] = jnp.zeros_like(acc_ref)

    def cumsum(tri, la):
        hi, mid, lo = _split3(la)
        return _dot(tri, hi) + _dot(tri, mid) + _dot(tri, lo)

    def one_direction(base, tri, keep, end_row, la_off, st_ref):
        rows = pl.ds(base, c)
        k = p_ref[0, rows, 0:GLA_KW]
        q = p_ref[0, rows, GLA_KW:2 * GLA_KW] * q_scale
        v = p_ref[0, rows, 2 * GLA_KW:2 * GLA_KW + GLA_VW]
        la = p_ref[0, rows, la_off:la_off + GLA_KW]
        cum = cumsum(tri, la)
        mid = cum[c // 2:c // 2 + 1, :]
        total = cum[end_row:end_row + 1, :]
        st = st_ref[...]
        inter = _dot_nt((q * jnp.exp(cum)).astype(BF16), st.astype(BF16))
        qs = q * jnp.exp(cum - mid)
        ks = (k * jnp.exp(mid - cum)).astype(BF16)
        vb = v.astype(BF16)
        intra = jnp.zeros((c, GLA_VW), F32)
        for hd in range(GLA_HEADS):
            s = _dot_nt(jnp.where(klane == hd, qs, 0.0).astype(BF16), ks)
            s = jnp.where(keep, s, 0.0).astype(BF16)
            intra = intra + _dot(s, jnp.where(vlane == hd, vb, jnp.zeros_like(vb)))
        acc_ref[rows, :] += inter + intra
        kd = (k * jnp.exp(total - cum)).astype(BF16)
        upd = _dot_tn(vb, kd)
        st_ref[...] = st * jnp.exp(total) + jnp.where(head_diag, upd, 0.0)

    def body(i, carry):
        one_direction(pl.multiple_of(i * c, c), tri_l, lower, c - 1, 2 * GLA_KW + 2 * GLA_VW, sf_ref)
        one_direction(pl.multiple_of((nc - 1 - i) * c, c), tri_u, upper, 0, 3 * GLA_KW + 2 * GLA_VW, sb_ref)
        return carry

    lax.fori_loop(0, nc, body, 0)
    sff_ref[0] = sf_ref[...]
    sfb_ref[0] = sb_ref[...]

    hr = lax.broadcasted_iota(jnp.int32, (GLA_VW, GLA_VW), 0) // GLA_DV
    hc = lax.broadcasted_iota(jnp.int32, (GLA_VW, GLA_VW), 1) // GLA_DV
    head_ones = (hr == hc).astype(BF16)
    blk = 256
    for j in range(t // blk):
        rows = pl.ds(j * blk, blk)
        o = acc_ref[rows, :]
        hi, mid, lo = _split3(o * o)
        ms = (_dot(hi, head_ones) + _dot(mid, head_ones) + _dot(lo, head_ones)) * (1.0 / GLA_DV)
        gate = p_ref[0, rows, 2 * GLA_KW + GLA_VW:2 * GLA_KW + 2 * GLA_VW]
        o_ref[0, rows, :] = (o * lax.rsqrt(ms + NORM_EPS) * gain_ref[...] * _silu(gate)).astype(BF16)


def _gla(p, s0f, s0b, gain):
    bsz, t, w = p.shape
    st_spec = pl.BlockSpec((1, GLA_VW, GLA_KW), lambda b: (b, 0, 0))
    st_shape = jax.ShapeDtypeStruct((bsz, GLA_VW, GLA_KW), F32)
    return pl.pallas_call(
        _gla_kernel,
        grid=(bsz,),
        in_specs=[
            pl.BlockSpec((1, t, w), lambda b: (b, 0, 0)),
            st_spec, st_spec,
            pl.BlockSpec((1, GLA_VW), lambda b: (0, 0)),
        ],
        out_specs=[pl.BlockSpec((1, t, GLA_VW), lambda b: (b, 0, 0)), st_spec, st_spec],
        out_shape=[jax.ShapeDtypeStruct((bsz, t, GLA_VW), BF16), st_shape, st_shape],
        scratch_shapes=[
            pltpu.VMEM((t, GLA_VW), F32),
            pltpu.VMEM((GLA_VW, GLA_KW), F32),
            pltpu.VMEM((GLA_VW, GLA_KW), F32),
        ],
        compiler_params=_cparams("parallel"),
        name="gla_bidir",
    )(p, s0f, s0b, gain)


def _head_stack(q, width, n):
    lane = lax.broadcasted_iota(jnp.int32, (1, q.shape[1]), 1) // width
    return jnp.concatenate([jnp.where(lane == h, q, jnp.zeros_like(q)) for h in range(n)], axis=0)


def _head_unstack(o, width, n):
    rows = o.shape[0] // n
    lane = lax.broadcasted_iota(jnp.int32, (1, o.shape[1]), 1) // width
    out = o[0:rows]
    for h in range(1, n):
        out = jnp.where(lane == h, o[h * rows:(h + 1) * rows], out)
    return out


def _nat_kernel(lat_ref, ctx_ref, bias_ref, o_ref):
    r = pl.program_id(1)
    n_rows = lat_ref.shape[1] // GRID_W
    r0 = jnp.clip(r - NAT_WIN_ROWS // 2, 0, n_rows - NAT_WIN_ROWS)
    q = lat_ref[0, pl.ds(pl.multiple_of(r * GRID_W, GRID_W), GRID_W), 0:NAT_W]
    krows = pl.ds(pl.multiple_of(r0 * GRID_W, GRID_W), NAT_WIN_ROWS * GRID_W)
    k = lat_ref[0, krows, NAT_W:2 * NAT_W]
    v = lat_ref[0, krows, 2 * NAT_W:3 * NAT_W]
    kc = ctx_ref[0, :, NAT_W:2 * NAT_W]
    vc = ctx_ref[0, :, 2 * NAT_W:3 * NAT_W]
    qs = _head_stack(q, HEAD_DIM, NAT_HEADS)
    s_loc = _dot_nt(qs, k) + bias_ref[0]
    s_ctx = _dot_nt(qs, kc)
    m = jnp.maximum(jnp.max(s_loc, axis=-1, keepdims=True), jnp.max(s_ctx, axis=-1, keepdims=True))
    p_loc = jnp.exp(s_loc - m)
    p_ctx = jnp.exp(s_ctx - m)
    den = jnp.sum(p_loc, axis=-1, keepdims=True) + jnp.sum(p_ctx, axis=-1, keepdims=True)
    o = (_dot(p_loc.astype(BF16), v) + _dot(p_ctx.astype(BF16), vc)) / den
    o_ref[0] = _head_unstack(o, HEAD_DIM, NAT_HEADS).astype(BF16)


def _nat(lat, ctx, bias):
    bsz, t, w = lat.shape
    m = ctx.shape[1]
    n_rows = t // GRID_W
    half = NAT_WIN_ROWS // 2

    def bias_idx(b, r):
        return (jnp.clip(r - half, 0, n_rows - NAT_WIN_ROWS) - r + NAT_WIN_ROWS - 1, 0, 0)

    return pl.pallas_call(
        _nat_kernel,
        grid=(bsz, n_rows),
        in_specs=[
            pl.BlockSpec((1, t, w), lambda b, r: (b, 0, 0)),
            pl.BlockSpec((1, m, w), lambda b, r: (b, 0, 0)),
            pl.BlockSpec((1, NAT_HEADS * GRID_W, NAT_WIN_ROWS * GRID_W), bias_idx),
        ],
        out_specs=pl.BlockSpec((1, GRID_W, NAT_W), lambda b, r: (b, r, 0)),
        out_shape=jax.ShapeDtypeStruct((bsz, t, NAT_W), BF16),
        compiler_params=_cparams("parallel", "arbitrary"),
        name="nat_attn",
    )(lat, ctx, bias)


def _nat_bias_table(rpb):
    cols = np.arange(GRID_W)
    col_start = np.clip(cols - NAT_WIN_COLS // 2, 0, GRID_W - NAT_WIN_COLS)
    kc = np.arange(GRID_W)
    inside = (kc[None, :] >= col_start[:, None]) & (kc[None, :] < col_start[:, None] + NAT_WIN_COLS)
    cidx = np.clip(kc[None, :] - cols[:, None] + NAT_WIN_COLS - 1, 0, 2 * NAT_WIN_COLS - 2)
    ridx = np.arange(NAT_WIN_ROWS)[:, None] + np.arange(NAT_WIN_ROWS)[None, :]
    tab = rpb.astype(F32)[:, ridx][:, :, :, cidx]
    tab = jnp.where(inside[None, None, None], tab, NEG_BIG)
    tab = tab.transpose(1, 0, 3, 2, 4)
    return tab.reshape(NAT_WIN_ROWS, NAT_HEADS * GRID_W, NAT_WIN_ROWS * GRID_W)


def _swa_kernel(q_ref, kp_ref, kn_ref, kx_ref, vp_ref, vn_ref, vx_ref, kc_ref, vc_ref, sink_ref, o_ref):
    n = pl.program_id(1)
    nb = pl.num_programs(1)
    blk = SWA_BLOCK
    qi = lax.broadcasted_iota(jnp.int32, (blk, 3 * blk), 0)
    kj = lax.broadcasted_iota(jnp.int32, (blk, 3 * blk), 1)
    kpos = (n - 1) * blk + kj
    valid = (jnp.abs(kj - blk - qi) <= SWA_WINDOW) & (kpos >= 0) & (kpos < nb * blk)
    mask_add = jnp.where(valid, 0.0, NEG_BIG)[None]
    group = SWA_Q_HEADS // SWA_KV_HEADS
    for g in range(SWA_KV_HEADS):
        gl = pl.ds(g * LANES, LANES)
        k = jnp.concatenate([kp_ref[0, :, gl], kn_ref[0, :, gl], kx_ref[0, :, gl]], axis=0)
        v = jnp.concatenate([vp_ref[0, :, gl], vn_ref[0, :, gl], vx_ref[0, :, gl]], axis=0)
        kc = kc_ref[0, :, gl]
        vc = vc_ref[0, :, gl]
        qs = jnp.concatenate(
            [_head_stack(q_ref[0, :, pl.ds((g * 2 + p) * LANES, LANES)], HEAD_DIM, 2) for p in range(2)], axis=0)
        s_loc = (_dot_nt(qs, k).reshape(group, blk, 3 * blk) + mask_add).reshape(group * blk, 3 * blk)
        s_ctx = _dot_nt(qs, kc)
        sink = jnp.concatenate(
            [jnp.broadcast_to(sink_ref[g * group + j:g * group + j + 1, 0:1], (blk, 1)) for j in range(group)], axis=0)
        m = jnp.maximum(jnp.maximum(jnp.max(s_loc, axis=-1, keepdims=True), jnp.max(s_ctx, axis=-1, keepdims=True)), sink)
        p_loc = jnp.exp(s_loc - m)
        p_ctx = jnp.exp(s_ctx - m)
        den = jnp.sum(p_loc, axis=-1, keepdims=True) + jnp.sum(p_ctx, axis=-1, keepdims=True) + jnp.exp(sink - m)
        o = (_dot(p_loc.astype(BF16), v) + _dot(p_ctx.astype(BF16), vc)) / den
        for p in range(2):
            pair = _head_unstack(o[2 * p * blk:(2 * p + 2) * blk], HEAD_DIM, 2)
            o_ref[0, :, pl.ds((g * 2 + p) * LANES, LANES)] = pair.astype(BF16)


def _swa(lat, ctx, sink_tab):
    bsz, t, w = lat.shape
    m = ctx.shape[1]
    nb = t // SWA_BLOCK
    qb = SWA_QW // (2 * SWA_KW)
    kw = 2 * SWA_KW

    def spec(which, lane_blk):
        if which == "prev":
            return pl.BlockSpec((1, SWA_BLOCK, kw), lambda b, n: (b, jnp.maximum(n - 1, 0), lane_blk))
        if which == "next":
            return pl.BlockSpec((1, SWA_BLOCK, kw), lambda b, n: (b, jnp.minimum(n + 1, nb - 1), lane_blk))
        return pl.BlockSpec((1, SWA_BLOCK, kw), lambda b, n: (b, n, lane_blk))

    return pl.pallas_call(
        _swa_kernel,
        grid=(bsz, nb),
        in_specs=[
            pl.BlockSpec((1, SWA_BLOCK, SWA_QW), lambda b, n: (b, n, 0)),
            spec("prev", qb), spec("cur", qb), spec("next", qb),
            spec("prev", qb + 1), spec("cur", qb + 1), spec("next", qb + 1),
            pl.BlockSpec((1, m, kw), lambda b, n: (b, 0, qb)),
            pl.BlockSpec((1, m, kw), lambda b, n: (b, 0, qb + 1)),
            pl.BlockSpec((SWA_Q_HEADS, LANES), lambda b, n: (0, 0)),
        ],
        out_specs=pl.BlockSpec((1, SWA_BLOCK, SWA_QW), lambda b, n: (b, n, 0)),
        out_shape=jax.ShapeDtypeStruct((bsz, t, SWA_QW), BF16),
        compiler_params=_cparams("parallel", "arbitrary"),
        name="swa_attn",
    )(lat, lat, lat, lat, lat, lat, lat, ctx, ctx, sink_tab)


def _ctx_attn_kernel(nat_ref, swa_ref, sink_ref, on_ref, os_ref):
    m = nat_ref.shape[1]
    q = nat_ref[0, :, 0:NAT_W]
    k = nat_ref[0, :, NAT_W:2 * NAT_W]
    v = nat_ref[0, :, 2 * NAT_W:3 * NAT_W]
    s = _dot_nt(_head_stack(q, HEAD_DIM, NAT_HEADS), k)
    p = jnp.exp(s - jnp.max(s, axis=-1, keepdims=True))
    o = _dot(p.astype(BF16), v) / jnp.sum(p, axis=-1, keepdims=True)
    on_ref[0] = _head_unstack(o, HEAD_DIM, NAT_HEADS).astype(BF16)

    group = SWA_Q_HEADS // SWA_KV_HEADS
    for g in range(SWA_KV_HEADS):
        k = swa_ref[0, :, pl.ds(SWA_QW + g * LANES, LANES)]
        v = swa_ref[0, :, pl.ds(SWA_QW + 2 * SWA_KW + g * LANES, LANES)]
        qs = jnp.concatenate(
            [_head_stack(swa_ref[0, :, pl.ds((g * 2 + p) * LANES, LANES)], HEAD_DIM, 2) for p in range(2)], axis=0)
        s = _dot_nt(qs, k)
        sink = jnp.concatenate(
            [jnp.broadcast_to(sink_ref[g * group + j:g * group + j + 1, 0:1], (m, 1)) for j in range(group)], axis=0)
        mx = jnp.maximum(jnp.max(s, axis=-1, keepdims=True), sink)
        p = jnp.exp(s - mx)
        den = jnp.sum(p, axis=-1, keepdims=True) + jnp.exp(sink - mx)
        o = _dot(p.astype(BF16), v) / den
        for pr in range(2):
            pair = _head_unstack(o[2 * pr * m:(2 * pr + 2) * m], HEAD_DIM, 2)
            os_ref[0, :, pl.ds((g * 2 + pr) * LANES, LANES)] = pair.astype(BF16)


def _ctx_attn(nat_c, swa_c, sink_tab):
    bsz, m, _ = nat_c.shape
    return pl.pallas_call(
        _ctx_attn_kernel,
        grid=(bsz,),
        in_specs=[
            pl.BlockSpec((1, m, NAT_PROJ_W), lambda b: (b, 0, 0)),
            pl.BlockSpec((1, m, SWA_PROJ_W), lambda b: (b, 0, 0)),
            pl.BlockSpec((SWA_Q_HEADS, LANES), lambda b: (0, 0)),
        ],
        out_specs=[
            pl.BlockSpec((1, m, NAT_W), lambda b: (b, 0, 0)),
            pl.BlockSpec((1, m, SWA_QW), lambda b: (b, 0, 0)),
        ],
        out_shape=[
            jax.ShapeDtypeStruct((bsz, m, NAT_W), BF16),
            jax.ShapeDtypeStruct((bsz, m, SWA_QW), BF16),
        ],
        compiler_params=_cparams("parallel"),
        name="ctx_attn",
    )(nat_c, swa_c, sink_tab)


def _out_proj_kernel(gla_ref, nat_ref, swa_ref, x_ref, mod_ref, gain_ref, w_ref, wrh_ref, wrl_ref,
                     x1_ref, h2_ref, aff_ref, afft_ref):
    mix = (_dot(gla_ref[0], w_ref[0:GLA_VW, :])
           + _dot(nat_ref[0], w_ref[GLA_VW:GLA_VW + NAT_W, :])
           + _dot(swa_ref[0], w_ref[GLA_VW + NAT_W:, :]))
    x1 = x_ref[0] + mod_ref[0, 2:3, :] * _rms(mix, gain_ref[1:2, :])
    x1_ref[0] = x1
    h2 = _rms(x1, gain_ref[2:3, :]) * (1.0 + mod_ref[0, 4:5, :]) + mod_ref[0, 3:4, :]
    hi, lo = _split2(h2)
    h2_ref[0] = hi
    wrh = wrh_ref[...]
    logits = _dot(hi, wrh) + _dot(lo, wrh) + _dot(hi, wrl_ref[...])
    lane = lax.broadcasted_iota(jnp.int32, logits.shape, 1)
    logits = jnp.where(lane < N_EXPERTS, logits, NEG_BIG)
    e = jnp.exp(logits - jnp.max(logits, axis=-1, keepdims=True))
    aff = e / jnp.sum(e, axis=-1, keepdims=True)
    aff_ref[0] = aff
    afft_ref[0] = aff.T[0:N_EXPERTS, :]


def _out_proj(gla_o, nat_o, swa_o, x, mod, gains, w_out, wr_hi, wr_lo, tm):
    bsz, t, d = x.shape
    tok = lambda b, i: (b, i, 0)
    const = lambda b, i: (0, 0)
    return pl.pallas_call(
        _out_proj_kernel,
        grid=(bsz, t // tm),
        in_specs=[
            pl.BlockSpec((1, tm, GLA_VW), tok),
            pl.BlockSpec((1, tm, NAT_W), tok),
            pl.BlockSpec((1, tm, SWA_QW), tok),
            pl.BlockSpec((1, tm, d), tok),
            pl.BlockSpec((1, 6, d), lambda b, i: (b, 0, 0)),
            pl.BlockSpec((4, d), const),
            pl.BlockSpec((d, d), const),
            pl.BlockSpec((d, LANES), const),
            pl.BlockSpec((d, LANES), const),
        ],
        out_specs=[
            pl.BlockSpec((1, tm, d), tok),
            pl.BlockSpec((1, tm, d), tok),
            pl.BlockSpec((1, tm, LANES), tok),
            pl.BlockSpec((1, N_EXPERTS, tm), lambda b, i: (b, 0, i)),
        ],
        out_shape=[
            jax.ShapeDtypeStruct((bsz, t, d), F32),
            jax.ShapeDtypeStruct((bsz, t, d), BF16),
            jax.ShapeDtypeStruct((bsz, t, LANES), F32),
            jax.ShapeDtypeStruct((bsz, N_EXPERTS, t), F32),
        ],
        compiler_params=_cparams("parallel", "parallel"),
        name="out_proj_router",
    )(gla_o, nat_o, swa_o, x, mod, gains, w_out, wr_hi, wr_lo)


def _excl_prefix(x):
    rows, n = x.shape
    nblk = n // LANES
    r = lax.broadcasted_iota(jnp.int32, (LANES, LANES), 0)
    c = lax.broadcasted_iota(jnp.int32, (LANES, LANES), 1)
    strict_upper = (r < c).astype(BF16)
    stacked = jnp.concatenate([x[:, j * LANES:(j + 1) * LANES] for j in range(nblk)], axis=0).astype(BF16)
    local = _dot(stacked, strict_upper)
    totals = jnp.sum(stacked.astype(F32), axis=-1, keepdims=True)
    out = []
    off = jnp.zeros((rows, 1), F32)
    for j in range(nblk):
        out.append(local[j * rows:(j + 1) * rows] + off)
        off = off + totals[j * rows:(j + 1) * rows]
    return jnp.concatenate(out, axis=1)


def _topc_kernel(aff_ref, rank_ref, rankcol_ref, *, cap):
    aff = aff_ref[0]
    bits = pltpu.bitcast(aff, jnp.int32)

    def step(i, thr):
        cand = thr | (jnp.int32(1) << (30 - i))
        cnt = jnp.sum((bits >= cand).astype(F32), axis=-1, keepdims=True)
        return jnp.where(cnt >= cap, cand, thr)

    thr = lax.fori_loop(0, 31, step, jnp.zeros((aff.shape[0], 1), jnp.int32))
    gt = bits > thr
    eq = bits == thr
    need = cap - jnp.sum(gt.astype(F32), axis=-1, keepdims=True)
    eq_before = _excl_prefix(eq.astype(F32))
    sel = gt | (eq & (eq_before < need))
    rank = jnp.where(sel, _excl_prefix(sel.astype(F32)), -1.0)
    rank_ref[0] = rank
    padded = jnp.concatenate([rank, jnp.full((LANES - rank.shape[0], rank.shape[1]), -1.0, F32)], axis=0)
    rankcol_ref[0] = padded.T


def _topc(aff_t, cap):
    bsz, e, n = aff_t.shape
    return pl.pallas_call(
        functools.partial(_topc_kernel, cap=cap),
        grid=(bsz,),
        in_specs=[pl.BlockSpec((1, e, n), lambda b: (b, 0, 0))],
        out_specs=[pl.BlockSpec((1, e, n), lambda b: (b, 0, 0)), pl.BlockSpec((1, n, LANES), lambda b: (b, 0, 0))],
        out_shape=[jax.ShapeDtypeStruct((bsz, e, n), F32), jax.ShapeDtypeStruct((bsz, n, LANES), F32)],
        compiler_params=_cparams("parallel"),
        name="expert_topc",
    )(aff_t)


def _gather_kernel(h_ref, rank_ref, xs_ref, *, cap):
    n = h_ref.shape[1]
    slot = lax.broadcasted_iota(jnp.int32, (cap, n), 0).astype(F32)
    h = h_ref[0]
    for e in range(N_EXPERTS):
        onehot = (slot == rank_ref[0, e:e + 1, :]).astype(BF16)
        xs_ref[e, 0] = _dot(onehot, h).astype(BF16)


def _gather(h, rank, cap):
    bsz, n, d = h.shape
    return pl.pallas_call(
        functools.partial(_gather_kernel, cap=cap),
        grid=(bsz,),
        in_specs=[
            pl.BlockSpec((1, n, d), lambda b: (b, 0, 0)),
            pl.BlockSpec((1, N_EXPERTS, n), lambda b: (b, 0, 0)),
        ],
        out_specs=pl.BlockSpec((N_EXPERTS, 1, cap, d), lambda b: (0, b, 0, 0)),
        out_shape=jax.ShapeDtypeStruct((N_EXPERTS, bsz, cap, d), BF16),
        compiler_params=_cparams("parallel"),
        name="moe_gather",
    )(h, rank)


def _ffn_kernel(*refs, n_streams):
    xs_refs = refs[:n_streams]
    wg_ref, wu_ref, wd_ref = refs[n_streams:n_streams + 3]
    y_refs = refs[n_streams + 3:2 * n_streams + 3]
    acc_refs = refs[2 * n_streams + 3:]
    f = pl.program_id(1)
    wg = wg_ref[0].astype(BF16)
    wu = wu_ref[0].astype(BF16)
    wd = wd_ref[0].astype(BF16)
    for xs_ref, y_ref, acc_ref in zip(xs_refs, y_refs, acc_refs):
        xs = xs_ref[0]
        hid = (_silu(_dot(xs, wg)) * _dot(xs, wu)).astype(BF16)
        part = _dot(hid, wd)

        @pl.when(f == 0)
        def _():
            acc_ref[...] = part

        @pl.when(f > 0)
        def _():
            acc_ref[...] += part

        @pl.when(f == pl.num_programs(1) - 1)
        def _():
            y_ref[0] = acc_ref[...].astype(BF16)


def _ffn(xs_list, w_gate, w_up, w_down, tf):
    e, d, ff = w_gate.shape
    n_streams = len(xs_list)
    xspecs = [pl.BlockSpec((1, xs.shape[1], d), lambda i, f: (i, 0, 0)) for xs in xs_list]
    return pl.pallas_call(
        functools.partial(_ffn_kernel, n_streams=n_streams),
        grid=(e, ff // tf),
        in_specs=xspecs + [
            pl.BlockSpec((1, d, tf), lambda i, f: (i, 0, f)),
            pl.BlockSpec((1, d, tf), lambda i, f: (i, 0, f)),
            pl.BlockSpec((1, tf, d), lambda i, f: (i, f, 0)),
        ],
        out_specs=xspecs,
        out_shape=[jax.ShapeDtypeStruct(xs.shape, BF16) for xs in xs_list],
        scratch_shapes=[pltpu.VMEM((xs.shape[1], d), F32) for xs in xs_list],
        compiler_params=_cparams("parallel", "arbitrary"),
        name="moe_ffn",
    )(*xs_list, w_gate, w_up, w_down)


def _combine_kernel(y_ref, rankcol_ref, aff_ref, x_ref, mod_ref, gain_ref, o_ref, *, cap):
    rb = x_ref.shape[1]
    slot = lax.broadcasted_iota(jnp.int32, (rb, cap), 1).astype(F32)
    rc = rankcol_ref[0]
    af = aff_ref[0]
    acc = jnp.zeros((rb, x_ref.shape[2]), F32)
    for e in range(N_EXPERTS):
        onehot = (rc[:, e:e + 1] == slot).astype(BF16)
        acc = acc + af[:, e:e + 1] * _dot(onehot, y_ref[e, 0])
    o_ref[0] = x_ref[0] + mod_ref[0, 5:6, :] * _rms(acc, gain_ref[3:4, :])


def _combine(y, rankcol, aff, x, mod, gains, cap):
    bsz, n, d = x.shape
    rb = min(n, 512)
    tok = lambda b, i: (b, i, 0)
    return pl.pallas_call(
        functools.partial(_combine_kernel, cap=cap),
        grid=(bsz, n // rb),
        in_specs=[
            pl.BlockSpec((N_EXPERTS, 1, cap, d), lambda b, i: (0, b, 0, 0)),
            pl.BlockSpec((1, rb, LANES), tok),
            pl.BlockSpec((1, rb, LANES), tok),
            pl.BlockSpec((1, rb, d), tok),
            pl.BlockSpec((1, 6, d), lambda b, i: (b, 0, 0)),
            pl.BlockSpec((4, d), lambda b, i: (0, 0)),
        ],
        out_specs=pl.BlockSpec((1, rb, d), tok),
        out_shape=jax.ShapeDtypeStruct((bsz, n, d), F32),
        compiler_params=_cparams("parallel", "arbitrary"),
        name="moe_combine",
    )(y, rankcol, aff, x, mod, gains)


def _relayout_w_in(w):
    widths = (GLA_KW, GLA_VW, GLA_RANK, GLA_RANK, NAT_W, NAT_W, SWA_KW, SWA_KW, GLA_KW, GLA_VW, NAT_W, SWA_QW)
    gk, gv, gaf, gab, nk, nv, sk, sv, gq, gg, nq, sq = jnp.split(w, np.cumsum(widths)[:-1].tolist(), axis=1)
    pad = jnp.zeros((w.shape[0], LANES - 2 * GLA_RANK), w.dtype)
    sk0, sk1 = sk[:, :HEAD_DIM], sk[:, HEAD_DIM:]
    sv0, sv1 = sv[:, :HEAD_DIM], sv[:, HEAD_DIM:]
    cols = [gk, gq, gv, gg, gaf, gab, pad, nq, nk, nv, sq, sk0, sk0, sk1, sk1, sv0, sv0, sv1, sv1]
    return jnp.concatenate(cols, axis=1).astype(BF16)


def _rope_tables(t):
    half = HEAD_DIM // 4
    freqs = ROPE_BASE ** (-np.arange(half, dtype=np.float32) / half)
    pos = np.arange(t)
    ang_r = (pos // GRID_W).astype(np.float32)[:, None] * freqs
    ang_c = (pos % GRID_W).astype(np.float32)[:, None] * freqs
    cos = np.concatenate([np.cos(ang_r), np.cos(ang_r), np.cos(ang_c), np.cos(ang_c)], axis=1)
    sin = np.concatenate([-np.sin(ang_r), np.sin(ang_r), -np.sin(ang_c), np.sin(ang_c)], axis=1)
    reps = LANES // HEAD_DIM
    return jnp.asarray(np.tile(cos, (1, reps)), F32), jnp.asarray(np.tile(sin, (1, reps)), F32)


def _moe(h_list, afft_list, aff_list, x_list, mod_list, gains, w_gate, w_up, w_down):
    caps = [CAPACITY_FACTOR * h.shape[1] // N_EXPERTS for h in h_list]
    ranks = [_topc(a, cap) for a, cap in zip(afft_list, caps)]
    xs = [_gather(h, r[0], cap) for h, r, cap in zip(h_list, ranks, caps)]
    e = N_EXPERTS
    flat = [x.reshape(e, x.shape[1] * x.shape[2], x.shape[3]) for x in xs]
    ys = _ffn(flat, w_gate, w_up, w_down, 256)
    outs = []
    for y, x4, r, aff, x, mod, cap in zip(ys, xs, ranks, aff_list, x_list, mod_list, caps):
        outs.append(_combine(y.reshape(x4.shape), r[1], aff, x, mod, gains, cap))
    return outs


def kernel(x, c, ctx, c_ctx, w_mod, b_mod, norm_gains, w_in, w_out, gla_a_up, gla_a_bias, gla_norm,
           nat_rpb, swa_sink, w_router, w_gate, w_up, w_down):
    bsz, t, d = x.shape
    m = ctx.shape[1]
    depth = w_mod.shape[0]

    cc = jnp.concatenate([c, c_ctx[None], jnp.zeros((16 - bsz - 1, d), F32)], axis=0)
    mod_all = _modulation(cc, w_mod, b_mod)
    rope_tabs = _rope_tables(t)
    zero_state = jnp.zeros((bsz, GLA_VW, GLA_KW), F32)

    xc = ctx
    for l in range(depth):
        update_ctx = l < depth - 1
        mod = mod_all[l, :bsz].reshape(bsz, 6, d)
        mod_c = jnp.broadcast_to(mod_all[l, bsz].reshape(1, 6, d), (bsz, 6, d))
        gains = norm_gains[l]
        w_in_l = _relayout_w_in(w_in[l])
        aup = jnp.zeros((LANES, 2 * GLA_KW), F32)
        aup = aup.at[0:GLA_RANK, 0:GLA_KW].set(gla_a_up[l, 0]).at[GLA_RANK:2 * GLA_RANK, GLA_KW:].set(gla_a_up[l, 1])
        abias = gla_a_bias[l].reshape(1, 2 * GLA_KW)
        gla_gain = jnp.tile(gla_norm[l], GLA_HEADS).reshape(1, GLA_VW)
        sink_tab = jnp.broadcast_to(swa_sink[l][:, None], (SWA_Q_HEADS, LANES))
        w_out_l = w_out[l].astype(BF16)
        wr = jnp.pad(w_router[l], ((0, 0), (0, LANES - N_EXPERTS)))
        wr_hi = wr.astype(BF16)
        wr_lo = (wr - wr_hi.astype(F32)).astype(BF16)

        gla_p, nat_p, swa_p = _in_proj(x, mod, gains, w_in_l, aup, abias, rope_tabs, 512)
        gla_c, nat_c, swa_c = _in_proj(xc, mod_c, gains, w_in_l, aup, abias, None, m)

        gla_co, s_f, s_b = _gla(gla_c, zero_state, zero_state, gla_gain)
        gla_o, _, _ = _gla(gla_p, s_f, s_b, gla_gain)
        nat_o = _nat(nat_p, nat_c, _nat_bias_table(nat_rpb[l]))
        swa_o = _swa(swa_p, swa_c, sink_tab)

        x1, h2, aff, aff_t = _out_proj(gla_o, nat_o, swa_o, x, mod, gains, w_out_l, wr_hi, wr_lo, 512)
        if update_ctx:
            nat_co, swa_co = _ctx_attn(nat_c, swa_c, sink_tab)
            xc1, hc2, aff_c, aff_ct = _out_proj(gla_co, nat_co, swa_co, xc, mod_c, gains, w_out_l, wr_hi, wr_lo, m)
            x, xc = _moe([h2, hc2], [aff_t, aff_ct], [aff, aff_c], [x1, xc1], [mod, mod_c], gains,
                         w_gate[l], w_up[l], w_down[l])
        else:
            (x,) = _moe([h2], [aff_t], [aff], [x1], [mod], gains, w_gate[l], w_up[l], w_down[l])
    return x
```

```python
import functools

import jax
import jax.numpy as jnp
import numpy as np
from jax import lax
from jax.experimental import pallas as pl
from jax.experimental.pallas import tpu as pltpu

F32 = jnp.float32
BF16 = jnp.bfloat16

D_MODEL = 1024
GRID_W = 64
HEAD_DIM = 64
GLA_HEADS = 4
GLA_DK = 32
GLA_DV = 64
GLA_RANK = 16
GLA_TAU = 16.0
GLA_CHUNK = 64
NAT_HEADS = 4
NAT_WIN_ROWS = 8
NAT_WIN_COLS = 16
SWA_Q_HEADS = 8
SWA_KV_HEADS = 2
SWA_WINDOW = 128
SWA_BLOCK = 128
ROPE_BASE = 10000.0
N_EXPERTS = 16
EXPERT_FF = 1024
CAPACITY_FACTOR = 2
NORM_EPS = 1e-6

GLA_KW = GLA_HEADS * GLA_DK
GLA_VW = GLA_HEADS * GLA_DV
NAT_W = NAT_HEADS * HEAD_DIM
SWA_QW = SWA_Q_HEADS * HEAD_DIM
SWA_KW = SWA_KV_HEADS * HEAD_DIM
LANES = 128
NEG_BIG = -1e30

GLA_PROJ_W = 2 * GLA_KW + 2 * GLA_VW + LANES
GLA_OUT_W = 2 * GLA_KW + 2 * GLA_VW + 2 * GLA_KW
NAT_PROJ_W = 3 * NAT_W
SWA_PROJ_W = SWA_QW + 4 * SWA_KW
IN_PROJ_W = GLA_PROJ_W + NAT_PROJ_W + SWA_PROJ_W

VMEM_LIMIT = 56 * 1024 * 1024


def _cparams(*sem):
    return pltpu.CompilerParams(dimension_semantics=sem, vmem_limit_bytes=VMEM_LIMIT)


def _dot(a, b):
    return jnp.dot(a, b, preferred_element_type=F32)


def _dot_nt(a, b):
    return lax.dot_general(a, b, (((1,), (1,)), ((), ())), preferred_element_type=F32)


def _dot_tn(a, b):
    return lax.dot_general(a, b, (((0,), (0,)), ((), ())), preferred_element_type=F32)


def _split2(a):
    hi = a.astype(BF16)
    lo = (a - hi.astype(F32)).astype(BF16)
    return hi, lo


def _split3(a):
    hi = a.astype(BF16)
    r = a - hi.astype(F32)
    mid = r.astype(BF16)
    lo = (r - mid.astype(F32)).astype(BF16)
    return hi, mid, lo


def _dot_f32(a, b):
    ah, al = _split2(a)
    bh, bl = _split2(b)
    return _dot(ah, bh) + _dot(al, bh) + _dot(ah, bl)


def _rms(x, gain):
    return x * lax.rsqrt(jnp.mean(x * x, axis=-1, keepdims=True) + NORM_EPS) * gain


def _silu(x):
    return x * jax.nn.sigmoid(x)


def _mod_kernel(c_ref, w_ref, b_ref, o_ref):
    a = _silu(c_ref[...])
    o_ref[0] = _dot_f32(a, w_ref[0]) + b_ref[0]


def _modulation(cc, w_mod, b_mod):
    depth, d, n = w_mod.shape
    r = cc.shape[0]
    tn = 1536
    return pl.pallas_call(
        _mod_kernel,
        grid=(depth, n // tn),
        in_specs=[
            pl.BlockSpec((r, d), lambda l, j: (0, 0)),
            pl.BlockSpec((1, d, tn), lambda l, j: (l, 0, j)),
            pl.BlockSpec((1, 1, tn), lambda l, j: (l, 0, j)),
        ],
        out_specs=pl.BlockSpec((1, r, tn), lambda l, j: (l, 0, j)),
        out_shape=jax.ShapeDtypeStruct((depth, r, n), F32),
        compiler_params=_cparams("parallel", "parallel"),
        name="adaln_mod",
    )(cc, w_mod, b_mod.reshape(depth, 1, n))


def _rope_rotate(x, first_half):
    up = pltpu.roll(x, LANES - 16, 1)
    down = pltpu.roll(x, 16, 1)
    return jnp.where(first_half, up, down)


def _in_proj_kernel(*refs, rope):
    if rope:
        x_ref, mod_ref, gain_ref, w_ref, aup_ref, ab_ref, cos_ref, sin_ref, gla_ref, nat_ref, swa_ref = refs
    else:
        x_ref, mod_ref, gain_ref, w_ref, aup_ref, ab_ref, gla_ref, nat_ref, swa_ref = refs
    x = x_ref[0]
    y = _rms(x, gain_ref[0:1, :])
    h = (y * (1.0 + mod_ref[0, 1:2, :]) + mod_ref[0, 0:1, :]).astype(BF16)

    kq = 2 * GLA_KW + 2 * GLA_VW
    pg = _dot(h, w_ref[:, 0:GLA_PROJ_W])
    gla_ref[0, :, 0:kq] = pg[:, 0:kq]
    z = _dot_f32(pg[:, kq:kq + LANES], aup_ref[...]) + ab_ref[...]
    log_sig = jnp.minimum(z, 0.0) - jnp.log1p(jnp.exp(-jnp.abs(z)))
    gla_ref[0, :, kq:kq + 2 * GLA_KW] = log_sig * (1.0 / GLA_TAU)

    o = GLA_PROJ_W
    pn = _dot(h, w_ref[:, o:o + NAT_PROJ_W])
    scale = HEAD_DIM ** -0.5
    nat_ref[0, :, 0:NAT_W] = (pn[:, 0:NAT_W] * scale).astype(BF16)
    nat_ref[0, :, NAT_W:] = pn[:, NAT_W:].astype(BF16)

    o = GLA_PROJ_W + NAT_PROJ_W
    ps = _dot(h, w_ref[:, o:o + SWA_PROJ_W])
    n_rot = (SWA_QW + 2 * SWA_KW) // LANES
    if rope:
        cos = cos_ref[...]
        sin = sin_ref[...]
        lane = lax.broadcasted_iota(jnp.int32, cos.shape, 1)
        first_half = (lane % 32) < 16
    for j in range(SWA_PROJ_W // LANES):
        t = ps[:, j * LANES:(j + 1) * LANES]
        if rope and j < n_rot:
            t = t * cos + _rope_rotate(t, first_half) * sin
        if j < SWA_QW // LANES:
            t = t * scale
        swa_ref[0, :, j * LANES:(j + 1) * LANES] = t.astype(BF16)


def _in_proj(x, mod, gains, w, aup, abias, rope_tabs, tm):
    bsz, t, d = x.shape
    rope = rope_tabs is not None
    in_specs = [
        pl.BlockSpec((1, tm, d), lambda b, i: (b, i, 0)),
        pl.BlockSpec((1, 6, d), lambda b, i: (b, 0, 0)),
        pl.BlockSpec((4, d), lambda b, i: (0, 0)),
        pl.BlockSpec((d, IN_PROJ_W), lambda b, i: (0, 0)),
        pl.BlockSpec((LANES, 2 * GLA_KW), lambda b, i: (0, 0)),
        pl.BlockSpec((1, 2 * GLA_KW), lambda b, i: (0, 0)),
    ]
    args = [x, mod, gains, w, aup, abias]
    if rope:
        in_specs += [pl.BlockSpec((tm, LANES), lambda b, i: (i, 0))] * 2
        args += list(rope_tabs)
    return pl.pallas_call(
        functools.partial(_in_proj_kernel, rope=rope),
        grid=(bsz, t // tm),
        in_specs=in_specs,
        out_specs=[
            pl.BlockSpec((1, tm, GLA_OUT_W), lambda b, i: (b, i, 0)),
            pl.BlockSpec((1, tm, NAT_PROJ_W), lambda b, i: (b, i, 0)),
            pl.BlockSpec((1, tm, SWA_PROJ_W), lambda b, i: (b, i, 0)),
        ],
        out_shape=[
            jax.ShapeDtypeStruct((bsz, t, GLA_OUT_W), F32),
            jax.ShapeDtypeStruct((bsz, t, NAT_PROJ_W), BF16),
            jax.ShapeDtypeStruct((bsz, t, SWA_PROJ_W), BF16),
        ],
        compiler_params=_cparams("parallel", "parallel"),
        name="in_proj_rope" if rope else "in_proj",
    )(*args)


def _gla_kernel(p_ref, s0f_ref, s0b_ref, gain_ref, o_ref, sff_ref, sfb_ref, acc_ref, sf_ref, sb_ref):
    t = p_ref.shape[1]
    c = GLA_CHUNK
    nc = t // c
    row = lax.broadcasted_iota(jnp.int32, (c, c), 0)
    col = lax.broadcasted_iota(jnp.int32, (c, c), 1)
    lower = row >= col
    upper = row <= col
    tri_l = lower.astype(BF16)
    tri_u = upper.astype(BF16)
    klane = lax.broadcasted_iota(jnp.int32, (1, GLA_KW), 1) // GLA_DK
    vlane = lax.broadcasted_iota(jnp.int32, (1, GLA_VW), 1) // GLA_DV
    srow = lax.broadcasted_iota(jnp.int32, (GLA_VW, GLA_KW), 0) // GLA_DV
    scol = lax.broadcasted_iota(jnp.int32, (GLA_VW, GLA_KW), 1) // GLA_DK
    head_diag = srow == scol
    q_scale = GLA_DK ** -0.5

    sf_ref[...] = s0f_ref[0]
    sb_ref[...] = s0b_ref[0]
    acc_ref[...] = jnp.zeros_like(acc_ref)

    def cumsum(tri, la):
        hi, mid, lo = _split3(la)
        return _dot(tri, hi) + _dot(tri, mid) + _dot(tri, lo)

    def one_direction(base, tri, keep, end_row, la_off, st_ref):
        rows = pl.ds(base, c)
        k = p_ref[0, rows, 0:GLA_KW]
        q = p_ref[0, rows, GLA_KW:2 * GLA_KW] * q_scale
        v = p_ref[0, rows, 2 * GLA_KW:2 * GLA_KW + GLA_VW]
        la = p_ref[0, rows, la_off:la_off + GLA_KW]
        cum = cumsum(tri, la)
        mid = cum[c // 2:c // 2 + 1, :]
        total = cum[end_row:end_row + 1, :]
        st = st_ref[...]
        inter = _dot_nt((q * jnp.exp(cum)).astype(BF16), st.astype(BF16))
        qs = q * jnp.exp(cum - mid)
        ks = (k * jnp.exp(mid - cum)).astype(BF16)
        vb = v.astype(BF16)
        intra = jnp.zeros((c, GLA_VW), F32)
        for hd in range(GLA_HEADS):
            s = _dot_nt(jnp.where(klane == hd, qs, 0.0).astype(BF16), ks)
            s = jnp.where(keep, s, 0.0).astype(BF16)
            intra = intra + _dot(s, jnp.where(vlane == hd, vb, jnp.zeros_like(vb)))
        acc_ref[rows, :] += inter + intra
        kd = (k * jnp.exp(total - cum)).astype(BF16)
        upd = _dot_tn(vb, kd)
        st_ref[...] = st * jnp.exp(total) + jnp.where(head_diag, upd, 0.0)

    def body(i, carry):
        one_direction(pl.multiple_of(i * c, c), tri_l, lower, c - 1, 2 * GLA_KW + 2 * GLA_VW, sf_ref)
        one_direction(pl.multiple_of((nc - 1 - i) * c, c), tri_u, upper, 0, 3 * GLA_KW + 2 * GLA_VW, sb_ref)
        return carry

    lax.fori_loop(0, nc, body, 0)
    sff_ref[0] = sf_ref[...]
    sfb_ref[0] = sb_ref[...]

    hr = lax.broadcasted_iota(jnp.int32, (GLA_VW, GLA_VW), 0) // GLA_DV
    hc = lax.broadcasted_iota(jnp.int32, (GLA_VW, GLA_VW), 1) // GLA_DV
    head_ones = (hr == hc).astype(BF16)
    blk = 256
    for j in range(t // blk):
        rows = pl.ds(j * blk, blk)
        o = acc_ref[rows, :]
        hi, mid, lo = _split3(o * o)
        ms = (_dot(hi, head_ones) + _dot(mid, head_ones) + _dot(lo, head_ones)) * (1.0 / GLA_DV)
        gate = p_ref[0, rows, 2 * GLA_KW + GLA_VW:2 * GLA_KW + 2 * GLA_VW]
        o_ref[0, rows, :] = (o * lax.rsqrt(ms + NORM_EPS) * gain_ref[...] * _silu(gate)).astype(BF16)


def _gla(p, s0f, s0b, gain):
    bsz, t, w = p.shape
    st_spec = pl.BlockSpec((1, GLA_VW, GLA_KW), lambda b: (b, 0, 0))
    st_shape = jax.ShapeDtypeStruct((bsz, GLA_VW, GLA_KW), F32)
    return pl.pallas_call(
        _gla_kernel,
        grid=(bsz,),
        in_specs=[
            pl.BlockSpec((1, t, w), lambda b: (b, 0, 0)),
            st_spec, st_spec,
            pl.BlockSpec((1, GLA_VW), lambda b: (0, 0)),
        ],
        out_specs=[pl.BlockSpec((1, t, GLA_VW), lambda b: (b, 0, 0)), st_spec, st_spec],
        out_shape=[jax.ShapeDtypeStruct((bsz, t, GLA_VW), BF16), st_shape, st_shape],
        scratch_shapes=[
            pltpu.VMEM((t, GLA_VW), F32),
            pltpu.VMEM((GLA_VW, GLA_KW), F32),
            pltpu.VMEM((GLA_VW, GLA_KW), F32),
        ],
        compiler_params=_cparams("parallel"),
        name="gla_bidir",
    )(p, s0f, s0b, gain)


def _head_stack(q, width, n):
    lane = lax.broadcasted_iota(jnp.int32, (1, q.shape[1]), 1) // width
    return jnp.concatenate([jnp.where(lane == h, q, jnp.zeros_like(q)) for h in range(n)], axis=0)


def _head_unstack(o, width, n):
    rows = o.shape[0] // n
    lane = lax.broadcasted_iota(jnp.int32, (1, o.shape[1]), 1) // width
    out = o[0:rows]
    for h in range(1, n):
        out = jnp.where(lane == h, o[h * rows:(h + 1) * rows], out)
    return out


def _nat_kernel(lat_ref, ctx_ref, bias_ref, o_ref):
    r = pl.program_id(1)
    n_rows = lat_ref.shape[1] // GRID_W
    r0 = jnp.clip(r - NAT_WIN_ROWS // 2, 0, n_rows - NAT_WIN_ROWS)
    q = lat_ref[0, pl.ds(pl.multiple_of(r * GRID_W, GRID_W), GRID_W), 0:NAT_W]
    krows = pl.ds(pl.multiple_of(r0 * GRID_W, GRID_W), NAT_WIN_ROWS * GRID_W)
    k = jnp.concatenate([lat_ref[0, krows, NAT_W:2 * NAT_W], ctx_ref[0, :, NAT_W:2 * NAT_W]], axis=0)
    v = jnp.concatenate([lat_ref[0, krows, 2 * NAT_W:3 * NAT_W], ctx_ref[0, :, 2 * NAT_W:3 * NAT_W]], axis=0)
    s = _dot_nt(_head_stack(q, HEAD_DIM, NAT_HEADS), k) + bias_ref[0]
    p = jnp.exp(s - jnp.max(s, axis=-1, keepdims=True))
    o = _dot(p.astype(BF16), v) / jnp.sum(p, axis=-1, keepdims=True)
    o_ref[0] = _head_unstack(o, HEAD_DIM, NAT_HEADS).astype(BF16)


def _nat(lat, ctx, bias):
    bsz, t, w = lat.shape
    m = ctx.shape[1]
    n_rows = t // GRID_W
    half = NAT_WIN_ROWS // 2

    def bias_idx(b, r):
        return (jnp.clip(r - half, 0, n_rows - NAT_WIN_ROWS) - r + NAT_WIN_ROWS - 1, 0, 0)

    return pl.pallas_call(
        _nat_kernel,
        grid=(bsz, n_rows),
        in_specs=[
            pl.BlockSpec((1, t, w), lambda b, r: (b, 0, 0)),
            pl.BlockSpec((1, m, w), lambda b, r: (b, 0, 0)),
            pl.BlockSpec((1, NAT_HEADS * GRID_W, NAT_WIN_ROWS * GRID_W + m), bias_idx),
        ],
        out_specs=pl.BlockSpec((1, GRID_W, NAT_W), lambda b, r: (b, r, 0)),
        out_shape=jax.ShapeDtypeStruct((bsz, t, NAT_W), BF16),
        compiler_params=_cparams("parallel", "arbitrary"),
        name="nat_attn",
    )(lat, ctx, bias)


def _nat_bias_table(rpb, m):
    cols = np.arange(GRID_W)
    col_start = np.clip(cols - NAT_WIN_COLS // 2, 0, GRID_W - NAT_WIN_COLS)
    kc = np.arange(GRID_W)
    inside = (kc[None, :] >= col_start[:, None]) & (kc[None, :] < col_start[:, None] + NAT_WIN_COLS)
    rel = kc[None, :] - cols[:, None] + NAT_WIN_COLS - 1
    col_sel = (rel[:, :, None] == np.arange(2 * NAT_WIN_COLS - 1)) & inside[:, :, None]
    ridx = np.arange(NAT_WIN_ROWS)[:, None] + np.arange(NAT_WIN_ROWS)[None, :]
    row_sel = ridx[:, :, None] == np.arange(2 * NAT_WIN_ROWS - 1)
    tab = jnp.einsum("hab,oia,ckb->ohcik", rpb.astype(F32), jnp.asarray(row_sel, F32), jnp.asarray(col_sel, F32),
                     precision=lax.Precision.HIGHEST)
    tab = jnp.where(jnp.asarray(inside)[None, None, :, None, :], tab, NEG_BIG)
    tab = tab.reshape(NAT_WIN_ROWS, NAT_HEADS * GRID_W, NAT_WIN_ROWS * GRID_W)
    return jnp.concatenate([tab, jnp.zeros(tab.shape[:2] + (m,), F32)], axis=2)


def _sink_column(sink_ref, first_head, n_heads, rows):
    return jnp.concatenate(
        [jnp.broadcast_to(sink_ref[first_head + j:first_head + j + 1, 0:1], (rows, 1)) for j in range(n_heads)], axis=0)


def _softmax_pv_sink(s, v, sink):
    m = jnp.maximum(jnp.max(s, axis=-1, keepdims=True), sink)
    p = jnp.exp(s - m).astype(BF16)
    o = _dot(p, jnp.concatenate([v, jnp.ones_like(v)], axis=1))
    w = v.shape[1]
    return o[:, 0:w] / (o[:, w:] + jnp.exp(sink - m))


def _swa_kernel(q_ref, kp_ref, kn_ref, kx_ref, vp_ref, vn_ref, vx_ref, kc_ref, vc_ref, sink_ref, o_ref):
    n = pl.program_id(1)
    nb = pl.num_programs(1)
    blk = SWA_BLOCK
    m = kc_ref.shape[1]
    qi = lax.broadcasted_iota(jnp.int32, (blk, 3 * blk + m), 0)
    kj = lax.broadcasted_iota(jnp.int32, (blk, 3 * blk + m), 1)
    kpos = (n - 1) * blk + kj
    local_ok = (jnp.abs(kj - blk - qi) <= SWA_WINDOW) & (kpos >= 0) & (kpos < nb * blk)
    mask_add = jnp.where(local_ok | (kj >= 3 * blk), 0.0, NEG_BIG)[None]
    group = SWA_Q_HEADS // SWA_KV_HEADS
    for g in range(SWA_KV_HEADS):
        gl = pl.ds(g * LANES, LANES)
        k = jnp.concatenate([kp_ref[0, :, gl], kn_ref[0, :, gl], kx_ref[0, :, gl], kc_ref[0, :, gl]], axis=0)
        v = jnp.concatenate([vp_ref[0, :, gl], vn_ref[0, :, gl], vx_ref[0, :, gl], vc_ref[0, :, gl]], axis=0)
        qs = jnp.concatenate(
            [_head_stack(q_ref[0, :, pl.ds((g * 2 + p) * LANES, LANES)], HEAD_DIM, 2) for p in range(2)], axis=0)
        s = (_dot_nt(qs, k).reshape(group, blk, 3 * blk + m) + mask_add).reshape(group * blk, 3 * blk + m)
        o = _softmax_pv_sink(s, v, _sink_column(sink_ref, g * group, group, blk))
        for p in range(2):
            pair = _head_unstack(o[2 * p * blk:(2 * p + 2) * blk], HEAD_DIM, 2)
            o_ref[0, :, pl.ds((g * 2 + p) * LANES, LANES)] = pair.astype(BF16)


def _swa(lat, ctx, sink_tab):
    bsz, t, w = lat.shape
    m = ctx.shape[1]
    nb = t // SWA_BLOCK
    qb = SWA_QW // (2 * SWA_KW)
    kw = 2 * SWA_KW

    def spec(which, lane_blk):
        if which == "prev":
            return pl.BlockSpec((1, SWA_BLOCK, kw), lambda b, n: (b, jnp.maximum(n - 1, 0), lane_blk))
        if which == "next":
            return pl.BlockSpec((1, SWA_BLOCK, kw), lambda b, n: (b, jnp.minimum(n + 1, nb - 1), lane_blk))
        return pl.BlockSpec((1, SWA_BLOCK, kw), lambda b, n: (b, n, lane_blk))

    return pl.pallas_call(
        _swa_kernel,
        grid=(bsz, nb),
        in_specs=[
            pl.BlockSpec((1, SWA_BLOCK, SWA_QW), lambda b, n: (b, n, 0)),
            spec("prev", qb), spec("cur", qb), spec("next", qb),
            spec("prev", qb + 1), spec("cur", qb + 1), spec("next", qb + 1),
            pl.BlockSpec((1, m, kw), lambda b, n: (b, 0, qb)),
            pl.BlockSpec((1, m, kw), lambda b, n: (b, 0, qb + 1)),
            pl.BlockSpec((SWA_Q_HEADS, LANES), lambda b, n: (0, 0)),
        ],
        out_specs=pl.BlockSpec((1, SWA_BLOCK, SWA_QW), lambda b, n: (b, n, 0)),
        out_shape=jax.ShapeDtypeStruct((bsz, t, SWA_QW), BF16),
        compiler_params=_cparams("parallel", "arbitrary"),
        name="swa_attn",
    )(lat, lat, lat, lat, lat, lat, lat, ctx, ctx, sink_tab)


def _ctx_attn_kernel(nat_ref, swa_ref, sink_ref, on_ref, os_ref):
    m = nat_ref.shape[1]
    q = nat_ref[0, :, 0:NAT_W]
    k = nat_ref[0, :, NAT_W:2 * NAT_W]
    v = nat_ref[0, :, 2 * NAT_W:3 * NAT_W]
    s = _dot_nt(_head_stack(q, HEAD_DIM, NAT_HEADS), k)
    p = jnp.exp(s - jnp.max(s, axis=-1, keepdims=True))
    o = _dot(p.astype(BF16), v) / jnp.sum(p, axis=-1, keepdims=True)
    on_ref[0] = _head_unstack(o, HEAD_DIM, NAT_HEADS).astype(BF16)

    group = SWA_Q_HEADS // SWA_KV_HEADS
    for g in range(SWA_KV_HEADS):
        k = swa_ref[0, :, pl.ds(SWA_QW + g * LANES, LANES)]
        v = swa_ref[0, :, pl.ds(SWA_QW + 2 * SWA_KW + g * LANES, LANES)]
        qs = jnp.concatenate(
            [_head_stack(swa_ref[0, :, pl.ds((g * 2 + p) * LANES, LANES)], HEAD_DIM, 2) for p in range(2)], axis=0)
        o = _softmax_pv_sink(_dot_nt(qs, k), v, _sink_column(sink_ref, g * group, group, m))
        for pr in range(2):
            pair = _head_unstack(o[2 * pr * m:(2 * pr + 2) * m], HEAD_DIM, 2)
            os_ref[0, :, pl.ds((g * 2 + pr) * LANES, LANES)] = pair.astype(BF16)


def _ctx_attn(nat_c, swa_c, sink_tab):
    bsz, m, _ = nat_c.shape
    return pl.pallas_call(
        _ctx_attn_kernel,
        grid=(bsz,),
        in_specs=[
            pl.BlockSpec((1, m, NAT_PROJ_W), lambda b: (b, 0, 0)),
            pl.BlockSpec((1, m, SWA_PROJ_W), lambda b: (b, 0, 0)),
            pl.BlockSpec((SWA_Q_HEADS, LANES), lambda b: (0, 0)),
        ],
        out_specs=[
            pl.BlockSpec((1, m, NAT_W), lambda b: (b, 0, 0)),
            pl.BlockSpec((1, m, SWA_QW), lambda b: (b, 0, 0)),
        ],
        out_shape=[
            jax.ShapeDtypeStruct((bsz, m, NAT_W), BF16),
            jax.ShapeDtypeStruct((bsz, m, SWA_QW), BF16),
        ],
        compiler_params=_cparams("parallel"),
        name="ctx_attn",
    )(nat_c, swa_c, sink_tab)


def _out_proj_kernel(gla_ref, nat_ref, swa_ref, x_ref, mod_ref, gain_ref, w_ref, wrh_ref, wrl_ref,
                     x1_ref, h2_ref, aff_ref, afft_ref):
    mix = (_dot(gla_ref[0], w_ref[0:GLA_VW, :])
           + _dot(nat_ref[0], w_ref[GLA_VW:GLA_VW + NAT_W, :])
           + _dot(swa_ref[0], w_ref[GLA_VW + NAT_W:, :]))
    x1 = x_ref[0] + mod_ref[0, 2:3, :] * _rms(mix, gain_ref[1:2, :])
    x1_ref[0] = x1
    h2 = _rms(x1, gain_ref[2:3, :]) * (1.0 + mod_ref[0, 4:5, :]) + mod_ref[0, 3:4, :]
    hi, lo = _split2(h2)
    h2_ref[0] = hi
    wrh = wrh_ref[...]
    logits = _dot(hi, wrh) + _dot(lo, wrh) + _dot(hi, wrl_ref[...])
    lane = lax.broadcasted_iota(jnp.int32, logits.shape, 1)
    logits = jnp.where(lane < N_EXPERTS, logits, NEG_BIG)
    e = jnp.exp(logits - jnp.max(logits, axis=-1, keepdims=True))
    aff = e / jnp.sum(e, axis=-1, keepdims=True)
    aff_ref[0] = aff
    afft_ref[0] = aff.T[0:N_EXPERTS, :]


def _out_proj(gla_o, nat_o, swa_o, x, mod, gains, w_out, wr_hi, wr_lo, tm):
    bsz, t, d = x.shape
    tok = lambda b, i: (b, i, 0)
    const = lambda b, i: (0, 0)
    return pl.pallas_call(
        _out_proj_kernel,
        grid=(bsz, t // tm),
        in_specs=[
            pl.BlockSpec((1, tm, GLA_VW), tok),
            pl.BlockSpec((1, tm, NAT_W), tok),
            pl.BlockSpec((1, tm, SWA_QW), tok),
            pl.BlockSpec((1, tm, d), tok),
            pl.BlockSpec((1, 6, d), lambda b, i: (b, 0, 0)),
            pl.BlockSpec((4, d), const),
            pl.BlockSpec((d, d), const),
            pl.BlockSpec((d, LANES), const),
            pl.BlockSpec((d, LANES), const),
        ],
        out_specs=[
            pl.BlockSpec((1, tm, d), tok),
            pl.BlockSpec((1, tm, d), tok),
            pl.BlockSpec((1, tm, LANES), tok),
            pl.BlockSpec((1, N_EXPERTS, tm), lambda b, i: (b, 0, i)),
        ],
        out_shape=[
            jax.ShapeDtypeStruct((bsz, t, d), F32),
            jax.ShapeDtypeStruct((bsz, t, d), BF16),
            jax.ShapeDtypeStruct((bsz, t, LANES), F32),
            jax.ShapeDtypeStruct((bsz, N_EXPERTS, t), F32),
        ],
        compiler_params=_cparams("parallel", "parallel"),
        name="out_proj_router",
    )(gla_o, nat_o, swa_o, x, mod, gains, w_out, wr_hi, wr_lo)


def _excl_prefix(x):
    rows, n = x.shape
    nblk = n // LANES
    r = lax.broadcasted_iota(jnp.int32, (LANES, LANES), 0)
    c = lax.broadcasted_iota(jnp.int32, (LANES, LANES), 1)
    strict_upper = (r < c).astype(BF16)
    stacked = jnp.concatenate([x[:, j * LANES:(j + 1) * LANES] for j in range(nblk)], axis=0).astype(BF16)
    local = _dot(stacked, strict_upper)
    totals = jnp.sum(stacked.astype(F32), axis=-1, keepdims=True)
    out = []
    off = jnp.zeros((rows, 1), F32)
    for j in range(nblk):
        out.append(local[j * rows:(j + 1) * rows] + off)
        off = off + totals[j * rows:(j + 1) * rows]
    return jnp.concatenate(out, axis=1)


def _topc_kernel(aff_ref, rank_ref, rankcol_ref, *, cap):
    aff = aff_ref[0]
    bits = pltpu.bitcast(aff, jnp.int32)

    def step(i, thr):
        cand = thr | (jnp.int32(1) << (30 - i))
        cnt = jnp.sum((bits >= cand).astype(F32), axis=-1, keepdims=True)
        return jnp.where(cnt >= cap, cand, thr)

    thr = lax.fori_loop(0, 31, step, jnp.zeros((aff.shape[0], 1), jnp.int32))
    gt = bits > thr
    eq = bits == thr
    need = cap - jnp.sum(gt.astype(F32), axis=-1, keepdims=True)
    eq_before = _excl_prefix(eq.astype(F32))
    sel = gt | (eq & (eq_before < need))
    rank = jnp.where(sel, _excl_prefix(sel.astype(F32)), -1.0)
    rank_ref[0] = rank
    padded = jnp.concatenate([rank, jnp.full((LANES - rank.shape[0], rank.shape[1]), -1.0, F32)], axis=0)
    rankcol_ref[0] = padded.T


def _topc(aff_t, cap):
    bsz, e, n = aff_t.shape
    return pl.pallas_call(
        functools.partial(_topc_kernel, cap=cap),
        grid=(bsz,),
        in_specs=[pl.BlockSpec((1, e, n), lambda b: (b, 0, 0))],
        out_specs=[pl.BlockSpec((1, e, n), lambda b: (b, 0, 0)), pl.BlockSpec((1, n, LANES), lambda b: (b, 0, 0))],
        out_shape=[jax.ShapeDtypeStruct((bsz, e, n), F32), jax.ShapeDtypeStruct((bsz, n, LANES), F32)],
        compiler_params=_cparams("parallel"),
        name="expert_topc",
    )(aff_t)


def _gather_kernel(h_ref, rank_ref, xs_ref, *, cap):
    n = h_ref.shape[1]
    slot = lax.broadcasted_iota(jnp.int32, (cap, n), 0).astype(F32)
    h = h_ref[0]
    for e in range(N_EXPERTS):
        onehot = (slot == rank_ref[0, e:e + 1, :]).astype(BF16)
        xs_ref[e, 0] = _dot(onehot, h).astype(BF16)


def _gather(h, rank, cap):
    bsz, n, d = h.shape
    return pl.pallas_call(
        functools.partial(_gather_kernel, cap=cap),
        grid=(bsz,),
        in_specs=[
            pl.BlockSpec((1, n, d), lambda b: (b, 0, 0)),
            pl.BlockSpec((1, N_EXPERTS, n), lambda b: (b, 0, 0)),
        ],
        out_specs=pl.BlockSpec((N_EXPERTS, 1, cap, d), lambda b: (0, b, 0, 0)),
        out_shape=jax.ShapeDtypeStruct((N_EXPERTS, bsz, cap, d), BF16),
        compiler_params=_cparams("parallel"),
        name="moe_gather",
    )(h, rank)


def _ffn_kernel(*refs, n_streams):
    xs_refs = refs[:n_streams]
    wg_ref, wu_ref, wd_ref = refs[n_streams:n_streams + 3]
    y_refs = refs[n_streams + 3:2 * n_streams + 3]
    acc_refs = refs[2 * n_streams + 3:]
    f = pl.program_id(1)
    wg = wg_ref[0, 0].astype(BF16)
    wu = wu_ref[0, 0].astype(BF16)
    wd = wd_ref[0, 0].astype(BF16)
    for xs_ref, y_ref, acc_ref in zip(xs_refs, y_refs, acc_refs):
        xs = xs_ref[0]
        hid = (_silu(_dot(xs, wg)) * _dot(xs, wu)).astype(BF16)
        part = _dot(hid, wd)

        @pl.when(f == 0)
        def _():
            acc_ref[...] = part

        @pl.when(f > 0)
        def _():
            acc_ref[...] += part

        @pl.when(f == pl.num_programs(1) - 1)
        def _():
            y_ref[0] = acc_ref[...].astype(BF16)


def _ffn(xs_list, w_gate, w_up, w_down, layer, tf):
    _, e, d, ff = w_gate.shape
    n_streams = len(xs_list)
    xspecs = [pl.BlockSpec((1, xs.shape[1], d), lambda i, f: (i, 0, 0)) for xs in xs_list]
    return pl.pallas_call(
        functools.partial(_ffn_kernel, n_streams=n_streams),
        grid=(e, ff // tf),
        in_specs=xspecs + [
            pl.BlockSpec((1, 1, d, tf), lambda i, f: (layer, i, 0, f)),
            pl.BlockSpec((1, 1, d, tf), lambda i, f: (layer, i, 0, f)),
            pl.BlockSpec((1, 1, tf, d), lambda i, f: (layer, i, f, 0)),
        ],
        out_specs=xspecs,
        out_shape=[jax.ShapeDtypeStruct(xs.shape, BF16) for xs in xs_list],
        scratch_shapes=[pltpu.VMEM((xs.shape[1], d), F32) for xs in xs_list],
        compiler_params=_cparams("parallel", "arbitrary"),
        name="moe_ffn",
    )(*xs_list, w_gate, w_up, w_down)


def _combine_kernel(y_ref, rankcol_ref, aff_ref, x_ref, mod_ref, gain_ref, o_ref, *, cap):
    rb = x_ref.shape[1]
    slot = lax.broadcasted_iota(jnp.int32, (rb, cap), 1).astype(F32)
    rc = rankcol_ref[0]
    af = aff_ref[0]
    acc = jnp.zeros((rb, x_ref.shape[2]), F32)
    for e in range(N_EXPERTS):
        onehot = (rc[:, e:e + 1] == slot).astype(BF16)
        acc = acc + af[:, e:e + 1] * _dot(onehot, y_ref[e, 0])
    o_ref[0] = x_ref[0] + mod_ref[0, 5:6, :] * _rms(acc, gain_ref[3:4, :])


def _combine(y, rankcol, aff, x, mod, gains, cap):
    bsz, n, d = x.shape
    rb = min(n, 512)
    tok = lambda b, i: (b, i, 0)
    return pl.pallas_call(
        functools.partial(_combine_kernel, cap=cap),
        grid=(bsz, n // rb),
        in_specs=[
            pl.BlockSpec((N_EXPERTS, 1, cap, d), lambda b, i: (0, b, 0, 0)),
            pl.BlockSpec((1, rb, LANES), tok),
            pl.BlockSpec((1, rb, LANES), tok),
            pl.BlockSpec((1, rb, d), tok),
            pl.BlockSpec((1, 6, d), lambda b, i: (b, 0, 0)),
            pl.BlockSpec((4, d), lambda b, i: (0, 0)),
        ],
        out_specs=pl.BlockSpec((1, rb, d), tok),
        out_shape=jax.ShapeDtypeStruct((bsz, n, d), F32),
        compiler_params=_cparams("parallel", "arbitrary"),
        name="moe_combine",
    )(y, rankcol, aff, x, mod, gains)


def _relayout_w_in(w):
    widths = (GLA_KW, GLA_VW, GLA_RANK, GLA_RANK, NAT_W, NAT_W, SWA_KW, SWA_KW, GLA_KW, GLA_VW, NAT_W, SWA_QW)
    gk, gv, gaf, gab, nk, nv, sk, sv, gq, gg, nq, sq = jnp.split(w, np.cumsum(widths)[:-1].tolist(), axis=1)
    pad = jnp.zeros((w.shape[0], LANES - 2 * GLA_RANK), w.dtype)
    sk0, sk1 = sk[:, :HEAD_DIM], sk[:, HEAD_DIM:]
    sv0, sv1 = sv[:, :HEAD_DIM], sv[:, HEAD_DIM:]
    cols = [gk, gq, gv, gg, gaf, gab, pad, nq, nk, nv, sq, sk0, sk0, sk1, sk1, sv0, sv0, sv1, sv1]
    return jnp.concatenate(cols, axis=1).astype(BF16)


def _rope_tables(t):
    half = HEAD_DIM // 4
    freqs = ROPE_BASE ** (-np.arange(half, dtype=np.float32) / half)
    pos = np.arange(t)
    ang_r = (pos // GRID_W).astype(np.float32)[:, None] * freqs
    ang_c = (pos % GRID_W).astype(np.float32)[:, None] * freqs
    cos = np.concatenate([np.cos(ang_r), np.cos(ang_r), np.cos(ang_c), np.cos(ang_c)], axis=1)
    sin = np.concatenate([-np.sin(ang_r), np.sin(ang_r), -np.sin(ang_c), np.sin(ang_c)], axis=1)
    reps = LANES // HEAD_DIM
    return jnp.asarray(np.tile(cos, (1, reps)), F32), jnp.asarray(np.tile(sin, (1, reps)), F32)


def _moe(h_list, afft_list, aff_list, x_list, mod_list, gains, w_gate, w_up, w_down, layer):
    caps = [CAPACITY_FACTOR * h.shape[1] // N_EXPERTS for h in h_list]
    ranks = [_topc(a, cap) for a, cap in zip(afft_list, caps)]
    xs = [_gather(h, r[0], cap) for h, r, cap in zip(h_list, ranks, caps)]
    e = N_EXPERTS
    flat = [x.reshape(e, x.shape[1] * x.shape[2], x.shape[3]) for x in xs]
    ys = _ffn(flat, w_gate, w_up, w_down, layer, 256)
    outs = []
    for y, x4, r, aff, x, mod, cap in zip(ys, xs, ranks, aff_list, x_list, mod_list, caps):
        outs.append(_combine(y.reshape(x4.shape), r[1], aff, x, mod, gains, cap))
    return outs


def kernel(x, c, ctx, c_ctx, w_mod, b_mod, norm_gains, w_in, w_out, gla_a_up, gla_a_bias, gla_norm,
           nat_rpb, swa_sink, w_router, w_gate, w_up, w_down):
    bsz, t, d = x.shape
    m = ctx.shape[1]
    depth = w_mod.shape[0]

    cc = jnp.concatenate([c, c_ctx[None], jnp.zeros((16 - bsz - 1, d), F32)], axis=0)
    mod_all = _modulation(cc, w_mod, b_mod)
    rope_tabs = _rope_tables(t)
    zero_state = jnp.zeros((bsz, GLA_VW, GLA_KW), F32)

    xc = ctx
    for l in range(depth):
        update_ctx = l < depth - 1
        mod = mod_all[l, :bsz].reshape(bsz, 6, d)
        mod_c = jnp.broadcast_to(mod_all[l, bsz].reshape(1, 6, d), (bsz, 6, d))
        gains = norm_gains[l]
        w_in_l = _relayout_w_in(w_in[l])
        aup = jnp.zeros((LANES, 2 * GLA_KW), F32)
        aup = aup.at[0:GLA_RANK, 0:GLA_KW].set(gla_a_up[l, 0]).at[GLA_RANK:2 * GLA_RANK, GLA_KW:].set(gla_a_up[l, 1])
        abias = gla_a_bias[l].reshape(1, 2 * GLA_KW)
        gla_gain = jnp.tile(gla_norm[l], GLA_HEADS).reshape(1, GLA_VW)
        sink_tab = jnp.broadcast_to(swa_sink[l][:, None], (SWA_Q_HEADS, LANES))
        w_out_l = w_out[l].astype(BF16)
        wr = jnp.pad(w_router[l], ((0, 0), (0, LANES - N_EXPERTS)))
        wr_hi = wr.astype(BF16)
        wr_lo = (wr - wr_hi.astype(F32)).astype(BF16)

        gla_p, nat_p, swa_p = _in_proj(x, mod, gains, w_in_l, aup, abias, rope_tabs, 512)
        gla_c, nat_c, swa_c = _in_proj(xc, mod_c, gains, w_in_l, aup, abias, None, m)

        gla_co, s_f, s_b = _gla(gla_c, zero_state, zero_state, gla_gain)
        gla_o, _, _ = _gla(gla_p, s_f, s_b, gla_gain)
        nat_o = _nat(nat_p, nat_c, _nat_bias_table(nat_rpb[l], m))
        swa_o = _swa(swa_p, swa_c, sink_tab)

        x1, h2, aff, aff_t = _out_proj(gla_o, nat_o, swa_o, x, mod, gains, w_out_l, wr_hi, wr_lo, 512)
        if update_ctx:
            nat_co, swa_co = _ctx_attn(nat_c, swa_c, sink_tab)
            xc1, hc2, aff_c, aff_ct = _out_proj(gla_co, nat_co, swa_co, xc, mod_c, gains, w_out_l, wr_hi, wr_lo, m)
            x, xc = _moe([h2, hc2], [aff_t, aff_ct], [aff, aff_c], [x1, xc1], [mod, mod_c], gains,
                         w_gate, w_up, w_down, l)
        else:
            (x,) = _moe([h2], [aff_t], [aff], [x1], [mod], gains, w_gate, w_up, w_down, l)
    return x
```

```python
import functools

import jax
import jax.numpy as jnp
import numpy as np
from jax import lax
from jax.experimental import pallas as pl
from jax.experimental.pallas import tpu as pltpu

F32 = jnp.float32
BF16 = jnp.bfloat16

D_MODEL = 1024
GRID_W = 64
HEAD_DIM = 64
GLA_HEADS = 4
GLA_DK = 32
GLA_DV = 64
GLA_RANK = 16
GLA_TAU = 16.0
GLA_CHUNK = 64
NAT_HEADS = 4
NAT_WIN_ROWS = 8
NAT_WIN_COLS = 16
SWA_Q_HEADS = 8
SWA_KV_HEADS = 2
SWA_WINDOW = 128
SWA_BLOCK = 128
ROPE_BASE = 10000.0
N_EXPERTS = 16
EXPERT_FF = 1024
CAPACITY_FACTOR = 2
NORM_EPS = 1e-6

GLA_KW = GLA_HEADS * GLA_DK
GLA_VW = GLA_HEADS * GLA_DV
NAT_W = NAT_HEADS * HEAD_DIM
SWA_QW = SWA_Q_HEADS * HEAD_DIM
SWA_KW = SWA_KV_HEADS * HEAD_DIM
LANES = 128
NEG_BIG = -1e30

GLA_PROJ_W = 2 * GLA_KW + 2 * GLA_VW + LANES
GLA_OUT_W = 2 * GLA_KW + 2 * GLA_VW + 2 * GLA_KW
NAT_PROJ_W = 3 * NAT_W
SWA_PROJ_W = SWA_QW + 4 * SWA_KW
IN_PROJ_W = GLA_PROJ_W + NAT_PROJ_W + SWA_PROJ_W

VMEM_LIMIT = 56 * 1024 * 1024


def _cparams(*sem):
    return pltpu.CompilerParams(dimension_semantics=sem, vmem_limit_bytes=VMEM_LIMIT)


def _dot(a, b):
    return jnp.dot(a, b, preferred_element_type=F32)


def _dot_nt(a, b):
    return lax.dot_general(a, b, (((1,), (1,)), ((), ())), preferred_element_type=F32)


def _dot_tn(a, b):
    return lax.dot_general(a, b, (((0,), (0,)), ((), ())), preferred_element_type=F32)


def _split2(a):
    hi = a.astype(BF16)
    lo = (a - hi.astype(F32)).astype(BF16)
    return hi, lo


def _split3(a):
    hi = a.astype(BF16)
    r = a - hi.astype(F32)
    mid = r.astype(BF16)
    lo = (r - mid.astype(F32)).astype(BF16)
    return hi, mid, lo


def _dot_f32(a, b):
    ah, al = _split2(a)
    bh, bl = _split2(b)
    return _dot(ah, bh) + _dot(al, bh) + _dot(ah, bl)


def _rms(x, gain):
    return x * lax.rsqrt(jnp.mean(x * x, axis=-1, keepdims=True) + NORM_EPS) * gain


def _silu(x):
    return x * jax.nn.sigmoid(x)


def _mod_kernel(c_ref, w_ref, b_ref, o_ref):
    a = _silu(c_ref[...])
    o_ref[0] = _dot_f32(a, w_ref[0]) + b_ref[0]


def _modulation(cc, w_mod, b_mod):
    depth, d, n = w_mod.shape
    r = cc.shape[0]
    tn = 1536
    return pl.pallas_call(
        _mod_kernel,
        grid=(depth, n // tn),
        in_specs=[
            pl.BlockSpec((r, d), lambda l, j: (0, 0)),
            pl.BlockSpec((1, d, tn), lambda l, j: (l, 0, j)),
            pl.BlockSpec((1, 1, tn), lambda l, j: (l, 0, j)),
        ],
        out_specs=pl.BlockSpec((1, r, tn), lambda l, j: (l, 0, j)),
        out_shape=jax.ShapeDtypeStruct((depth, r, n), F32),
        compiler_params=_cparams("parallel", "parallel"),
        name="adaln_mod",
    )(cc, w_mod, b_mod.reshape(depth, 1, n))


def _rope_rotate(x, first_half):
    up = pltpu.roll(x, LANES - 16, 1)
    down = pltpu.roll(x, 16, 1)
    return jnp.where(first_half, up, down)


def _in_proj_kernel(*refs, rope):
    if rope:
        x_ref, mod_ref, gain_ref, w_ref, aup_ref, ab_ref, cos_ref, sin_ref, gla_ref, nat_ref, swa_ref = refs
    else:
        x_ref, mod_ref, gain_ref, w_ref, aup_ref, ab_ref, gla_ref, nat_ref, swa_ref = refs
    x = x_ref[0]
    y = _rms(x, gain_ref[0:1, :])
    h = (y * (1.0 + mod_ref[0, 1:2, :]) + mod_ref[0, 0:1, :]).astype(BF16)

    kq = 2 * GLA_KW + 2 * GLA_VW
    pg = _dot(h, w_ref[:, 0:GLA_PROJ_W])
    gla_ref[0, :, 0:kq] = pg[:, 0:kq]
    z = _dot_f32(pg[:, kq:kq + LANES], aup_ref[...]) + ab_ref[...]
    log_sig = jnp.minimum(z, 0.0) - jnp.log1p(jnp.exp(-jnp.abs(z)))
    gla_ref[0, :, kq:kq + 2 * GLA_KW] = log_sig * (1.0 / GLA_TAU)

    o = GLA_PROJ_W
    pn = _dot(h, w_ref[:, o:o + NAT_PROJ_W])
    scale = HEAD_DIM ** -0.5
    nat_ref[0, :, 0:NAT_W] = (pn[:, 0:NAT_W] * scale).astype(BF16)
    nat_ref[0, :, NAT_W:] = pn[:, NAT_W:].astype(BF16)

    o = GLA_PROJ_W + NAT_PROJ_W
    ps = _dot(h, w_ref[:, o:o + SWA_PROJ_W])
    n_rot = (SWA_QW + 2 * SWA_KW) // LANES
    if rope:
        cos = cos_ref[...]
        sin = sin_ref[...]
        lane = lax.broadcasted_iota(jnp.int32, cos.shape, 1)
        first_half = (lane % 32) < 16
    for j in range(SWA_PROJ_W // LANES):
        t = ps[:, j * LANES:(j + 1) * LANES]
        if rope and j < n_rot:
            t = t * cos + _rope_rotate(t, first_half) * sin
        if j < SWA_QW // LANES:
            t = t * scale
        swa_ref[0, :, j * LANES:(j + 1) * LANES] = t.astype(BF16)


def _in_proj(x, mod, gains, w, aup, abias, rope_tabs, tm):
    bsz, t, d = x.shape
    rope = rope_tabs is not None
    in_specs = [
        pl.BlockSpec((1, tm, d), lambda b, i: (b, i, 0)),
        pl.BlockSpec((1, 6, d), lambda b, i: (b, 0, 0)),
        pl.BlockSpec((4, d), lambda b, i: (0, 0)),
        pl.BlockSpec((d, IN_PROJ_W), lambda b, i: (0, 0)),
        pl.BlockSpec((LANES, 2 * GLA_KW), lambda b, i: (0, 0)),
        pl.BlockSpec((1, 2 * GLA_KW), lambda b, i: (0, 0)),
    ]
    args = [x, mod, gains, w, aup, abias]
    if rope:
        in_specs += [pl.BlockSpec((tm, LANES), lambda b, i: (i, 0))] * 2
        args += list(rope_tabs)
    return pl.pallas_call(
        functools.partial(_in_proj_kernel, rope=rope),
        grid=(bsz, t // tm),
        in_specs=in_specs,
        out_specs=[
            pl.BlockSpec((1, tm, GLA_OUT_W), lambda b, i: (b, i, 0)),
            pl.BlockSpec((1, tm, NAT_PROJ_W), lambda b, i: (b, i, 0)),
            pl.BlockSpec((1, tm, SWA_PROJ_W), lambda b, i: (b, i, 0)),
        ],
        out_shape=[
            jax.ShapeDtypeStruct((bsz, t, GLA_OUT_W), F32),
            jax.ShapeDtypeStruct((bsz, t, NAT_PROJ_W), BF16),
            jax.ShapeDtypeStruct((bsz, t, SWA_PROJ_W), BF16),
        ],
        compiler_params=_cparams("parallel", "parallel"),
        name="in_proj_rope" if rope else "in_proj",
    )(*args)


def _gla_kernel(p_ref, s0f_ref, s0b_ref, gain_ref, o_ref, sff_ref, sfb_ref, acc_ref, sf_ref, sb_ref):
    t = p_ref.shape[1]
    c = GLA_CHUNK
    nc = t // c
    row = lax.broadcasted_iota(jnp.int32, (c, c), 0)
    col = lax.broadcasted_iota(jnp.int32, (c, c), 1)
    lower = row >= col
    upper = row <= col
    tri_l = lower.astype(BF16)
    tri_u = upper.astype(BF16)
    klane = lax.broadcasted_iota(jnp.int32, (1, GLA_KW), 1) // GLA_DK
    vlane = lax.broadcasted_iota(jnp.int32, (1, GLA_VW), 1) // GLA_DV
    srow = lax.broadcasted_iota(jnp.int32, (GLA_VW, GLA_KW), 0) // GLA_DV
    scol = lax.broadcasted_iota(jnp.int32, (GLA_VW, GLA_KW), 1) // GLA_DK
    head_diag = srow == scol
    q_scale = GLA_DK ** -0.5

    sf_ref[...] = s0f_ref[0]
    sb_ref[...] = s0b_ref[0]
    acc_ref[...] = jnp.zeros_like(acc_ref)

    def cumsum(tri, la):
        hi, mid, lo = _split3(la)
        return _dot(tri, hi) + _dot(tri, mid) + _dot(tri, lo)

    def one_direction(base, tri, keep, end_row, la_off, st_ref):
        rows = pl.ds(base, c)
        k = p_ref[0, rows, 0:GLA_KW]
        q = p_ref[0, rows, GLA_KW:2 * GLA_KW] * q_scale
        v = p_ref[0, rows, 2 * GLA_KW:2 * GLA_KW + GLA_VW]
        la = p_ref[0, rows, la_off:la_off + GLA_KW]
        cum = cumsum(tri, la)
        mid = cum[c // 2:c // 2 + 1, :]
        total = cum[end_row:end_row + 1, :]
        st = st_ref[...]
        inter = _dot_nt((q * jnp.exp(cum)).astype(BF16), st.astype(BF16))
        qs = q * jnp.exp(cum - mid)
        ks = (k * jnp.exp(mid - cum)).astype(BF16)
        vb = v.astype(BF16)
        intra = jnp.zeros((c, GLA_VW), F32)
        for hd in range(GLA_HEADS):
            s = _dot_nt(jnp.where(klane == hd, qs, 0.0).astype(BF16), ks)
            s = jnp.where(keep, s, 0.0).astype(BF16)
            intra = intra + _dot(s, jnp.where(vlane == hd, vb, jnp.zeros_like(vb)))
        acc_ref[rows, :] += inter + intra
        kd = (k * jnp.exp(total - cum)).astype(BF16)
        upd = _dot_tn(vb, kd)
        st_ref[...] = st * jnp.exp(total) + jnp.where(head_diag, upd, 0.0)

    def body(i, carry):
        one_direction(pl.multiple_of(i * c, c), tri_l, lower, c - 1, 2 * GLA_KW + 2 * GLA_VW, sf_ref)
        one_direction(pl.multiple_of((nc - 1 - i) * c, c), tri_u, upper, 0, 3 * GLA_KW + 2 * GLA_VW, sb_ref)
        return carry

    lax.fori_loop(0, nc, body, 0)
    sff_ref[0] = sf_ref[...]
    sfb_ref[0] = sb_ref[...]

    hr = lax.broadcasted_iota(jnp.int32, (GLA_VW, GLA_VW), 0) // GLA_DV
    hc = lax.broadcasted_iota(jnp.int32, (GLA_VW, GLA_VW), 1) // GLA_DV
    head_ones = (hr == hc).astype(BF16)
    blk = 256
    for j in range(t // blk):
        rows = pl.ds(j * blk, blk)
        o = acc_ref[rows, :]
        hi, mid, lo = _split3(o * o)
        ms = (_dot(hi, head_ones) + _dot(mid, head_ones) + _dot(lo, head_ones)) * (1.0 / GLA_DV)
        gate = p_ref[0, rows, 2 * GLA_KW + GLA_VW:2 * GLA_KW + 2 * GLA_VW]
        o_ref[0, rows, :] = (o * lax.rsqrt(ms + NORM_EPS) * gain_ref[...] * _silu(gate)).astype(BF16)


def _gla(p, s0f, s0b, gain):
    bsz, t, w = p.shape
    st_spec = pl.BlockSpec((1, GLA_VW, GLA_KW), lambda b: (b, 0, 0))
    st_shape = jax.ShapeDtypeStruct((bsz, GLA_VW, GLA_KW), F32)
    return pl.pallas_call(
        _gla_kernel,
        grid=(bsz,),
        in_specs=[
            pl.BlockSpec((1, t, w), lambda b: (b, 0, 0)),
            st_spec, st_spec,
            pl.BlockSpec((1, GLA_VW), lambda b: (0, 0)),
        ],
        out_specs=[pl.BlockSpec((1, t, GLA_VW), lambda b: (b, 0, 0)), st_spec, st_spec],
        out_shape=[jax.ShapeDtypeStruct((bsz, t, GLA_VW), BF16), st_shape, st_shape],
        scratch_shapes=[
            pltpu.VMEM((t, GLA_VW), F32),
            pltpu.VMEM((GLA_VW, GLA_KW), F32),
            pltpu.VMEM((GLA_VW, GLA_KW), F32),
        ],
        compiler_params=_cparams("parallel"),
        name="gla_bidir",
    )(p, s0f, s0b, gain)


def _head_stack(q, width, n):
    lane = lax.broadcasted_iota(jnp.int32, (1, q.shape[1]), 1) // width
    return jnp.concatenate([jnp.where(lane == h, q, jnp.zeros_like(q)) for h in range(n)], axis=0)


def _head_unstack(o, width, n):
    rows = o.shape[0] // n
    lane = lax.broadcasted_iota(jnp.int32, (1, o.shape[1]), 1) // width
    out = o[0:rows]
    for h in range(1, n):
        out = jnp.where(lane == h, o[h * rows:(h + 1) * rows], out)
    return out


NAT_ROWS_PER_STEP = 4


def _nat_kernel(lat_ref, ctx_ref, bias_ref, o_ref):
    n_rows = lat_ref.shape[1] // GRID_W
    kc = ctx_ref[0, :, NAT_W:2 * NAT_W]
    vc = ctx_ref[0, :, 2 * NAT_W:3 * NAT_W]
    n_loc = NAT_WIN_ROWS * GRID_W
    for i in range(NAT_ROWS_PER_STEP):
        r = pl.program_id(1) * NAT_ROWS_PER_STEP + i
        r0 = jnp.clip(r - NAT_WIN_ROWS // 2, 0, n_rows - NAT_WIN_ROWS)
        q = lat_ref[0, pl.ds(pl.multiple_of(r * GRID_W, GRID_W), GRID_W), 0:NAT_W]
        krows = pl.ds(pl.multiple_of(r0 * GRID_W, GRID_W), n_loc)
        qs = _head_stack(q, HEAD_DIM, NAT_HEADS)
        s = jnp.concatenate([_dot_nt(qs, lat_ref[0, krows, NAT_W:2 * NAT_W]), _dot_nt(qs, kc)], axis=1)
        s = s + bias_ref[r0 - r + NAT_WIN_ROWS - 1]
        p = jnp.exp(s - jnp.max(s, axis=-1, keepdims=True))
        den = jnp.sum(p, axis=-1, keepdims=True)
        p = p.astype(BF16)
        o = (_dot(p[:, 0:n_loc], lat_ref[0, krows, 2 * NAT_W:3 * NAT_W]) + _dot(p[:, n_loc:], vc)) / den
        o_ref[0, i * GRID_W:(i + 1) * GRID_W, :] = _head_unstack(o, HEAD_DIM, NAT_HEADS).astype(BF16)


def _nat(lat, ctx, bias):
    bsz, t, w = lat.shape
    m = ctx.shape[1]
    rows = NAT_ROWS_PER_STEP * GRID_W
    return pl.pallas_call(
        _nat_kernel,
        grid=(bsz, t // rows),
        in_specs=[
            pl.BlockSpec((1, t, w), lambda b, r: (b, 0, 0)),
            pl.BlockSpec((1, m, w), lambda b, r: (b, 0, 0)),
            pl.BlockSpec(bias.shape, lambda b, r: (0, 0, 0)),
        ],
        out_specs=pl.BlockSpec((1, rows, NAT_W), lambda b, r: (b, r, 0)),
        out_shape=jax.ShapeDtypeStruct((bsz, t, NAT_W), BF16),
        compiler_params=_cparams("parallel", "arbitrary"),
        name="nat_attn",
    )(lat, ctx, bias)


def _nat_bias_table(rpb, m):
    cols = np.arange(GRID_W)
    col_start = np.clip(cols - NAT_WIN_COLS // 2, 0, GRID_W - NAT_WIN_COLS)
    kc = np.arange(GRID_W)
    inside = (kc[None, :] >= col_start[:, None]) & (kc[None, :] < col_start[:, None] + NAT_WIN_COLS)
    rel = kc[None, :] - cols[:, None] + NAT_WIN_COLS - 1
    col_sel = (rel[:, :, None] == np.arange(2 * NAT_WIN_COLS - 1)) & inside[:, :, None]
    ridx = np.arange(NAT_WIN_ROWS)[:, None] + np.arange(NAT_WIN_ROWS)[None, :]
    row_sel = ridx[:, :, None] == np.arange(2 * NAT_WIN_ROWS - 1)
    tab = jnp.einsum("hab,oia,ckb->ohcik", rpb.astype(F32), jnp.asarray(row_sel, F32), jnp.asarray(col_sel, F32),
                     precision=lax.Precision.HIGHEST)
    tab = jnp.where(jnp.asarray(inside)[None, None, :, None, :], tab, NEG_BIG)
    tab = tab.reshape(NAT_WIN_ROWS, NAT_HEADS * GRID_W, NAT_WIN_ROWS * GRID_W)
    return jnp.concatenate([tab, jnp.zeros(tab.shape[:2] + (m,), F32)], axis=2)


def _sink_column(sink_ref, first_head, n_heads, rows):
    return jnp.concatenate(
        [jnp.broadcast_to(sink_ref[first_head + j:first_head + j + 1, 0:1], (rows, 1)) for j in range(n_heads)], axis=0)


def _softmax_pv_sink(s, v, sink):
    m = jnp.maximum(jnp.max(s, axis=-1, keepdims=True), sink)
    p = jnp.exp(s - m).astype(BF16)
    o = _dot(p, jnp.concatenate([v, jnp.ones_like(v)], axis=1))
    w = v.shape[1]
    return o[:, 0:w] / (o[:, w:] + jnp.exp(sink - m))


def _swa_kernel(q_ref, kp_ref, kn_ref, kx_ref, vp_ref, vn_ref, vx_ref, kc_ref, vc_ref, sink_ref, o_ref):
    n = pl.program_id(1)
    nb = pl.num_programs(1)
    blk = SWA_BLOCK
    m = kc_ref.shape[1]
    qi = lax.broadcasted_iota(jnp.int32, (blk, 3 * blk + m), 0)
    kj = lax.broadcasted_iota(jnp.int32, (blk, 3 * blk + m), 1)
    kpos = (n - 1) * blk + kj
    local_ok = (jnp.abs(kj - blk - qi) <= SWA_WINDOW) & (kpos >= 0) & (kpos < nb * blk)
    mask_add = jnp.where(local_ok | (kj >= 3 * blk), 0.0, NEG_BIG)[None]
    group = SWA_Q_HEADS // SWA_KV_HEADS
    for g in range(SWA_KV_HEADS):
        gl = pl.ds(g * LANES, LANES)
        k = jnp.concatenate([kp_ref[0, :, gl], kn_ref[0, :, gl], kx_ref[0, :, gl], kc_ref[0, :, gl]], axis=0)
        v = jnp.concatenate([vp_ref[0, :, gl], vn_ref[0, :, gl], vx_ref[0, :, gl], vc_ref[0, :, gl]], axis=0)
        qs = jnp.concatenate(
            [_head_stack(q_ref[0, :, pl.ds((g * 2 + p) * LANES, LANES)], HEAD_DIM, 2) for p in range(2)], axis=0)
        s = (_dot_nt(qs, k).reshape(group, blk, 3 * blk + m) + mask_add).reshape(group * blk, 3 * blk + m)
        o = _softmax_pv_sink(s, v, _sink_column(sink_ref, g * group, group, blk))
        for p in range(2):
            pair = _head_unstack(o[2 * p * blk:(2 * p + 2) * blk], HEAD_DIM, 2)
            o_ref[0, :, pl.ds((g * 2 + p) * LANES, LANES)] = pair.astype(BF16)


def _swa(lat, ctx, sink_tab):
    bsz, t, w = lat.shape
    m = ctx.shape[1]
    nb = t // SWA_BLOCK
    qb = SWA_QW // (2 * SWA_KW)
    kw = 2 * SWA_KW

    def spec(which, lane_blk):
        if which == "prev":
            return pl.BlockSpec((1, SWA_BLOCK, kw), lambda b, n: (b, jnp.maximum(n - 1, 0), lane_blk))
        if which == "next":
            return pl.BlockSpec((1, SWA_BLOCK, kw), lambda b, n: (b, jnp.minimum(n + 1, nb - 1), lane_blk))
        return pl.BlockSpec((1, SWA_BLOCK, kw), lambda b, n: (b, n, lane_blk))

    return pl.pallas_call(
        _swa_kernel,
        grid=(bsz, nb),
        in_specs=[
            pl.BlockSpec((1, SWA_BLOCK, SWA_QW), lambda b, n: (b, n, 0)),
            spec("prev", qb), spec("cur", qb), spec("next", qb),
            spec("prev", qb + 1), spec("cur", qb + 1), spec("next", qb + 1),
            pl.BlockSpec((1, m, kw), lambda b, n: (b, 0, qb)),
            pl.BlockSpec((1, m, kw), lambda b, n: (b, 0, qb + 1)),
            pl.BlockSpec((SWA_Q_HEADS, LANES), lambda b, n: (0, 0)),
        ],
        out_specs=pl.BlockSpec((1, SWA_BLOCK, SWA_QW), lambda b, n: (b, n, 0)),
        out_shape=jax.ShapeDtypeStruct((bsz, t, SWA_QW), BF16),
        compiler_params=_cparams("parallel", "arbitrary"),
        name="swa_attn",
    )(lat, lat, lat, lat, lat, lat, lat, ctx, ctx, sink_tab)


def _ctx_attn_kernel(nat_ref, swa_ref, sink_ref, on_ref, os_ref):
    m = nat_ref.shape[1]
    q = nat_ref[0, :, 0:NAT_W]
    k = nat_ref[0, :, NAT_W:2 * NAT_W]
    v = nat_ref[0, :, 2 * NAT_W:3 * NAT_W]
    s = _dot_nt(_head_stack(q, HEAD_DIM, NAT_HEADS), k)
    p = jnp.exp(s - jnp.max(s, axis=-1, keepdims=True))
    o = _dot(p.astype(BF16), v) / jnp.sum(p, axis=-1, keepdims=True)
    on_ref[0] = _head_unstack(o, HEAD_DIM, NAT_HEADS).astype(BF16)

    group = SWA_Q_HEADS // SWA_KV_HEADS
    for g in range(SWA_KV_HEADS):
        k = swa_ref[0, :, pl.ds(SWA_QW + g * LANES, LANES)]
        v = swa_ref[0, :, pl.ds(SWA_QW + 2 * SWA_KW + g * LANES, LANES)]
        qs = jnp.concatenate(
            [_head_stack(swa_ref[0, :, pl.ds((g * 2 + p) * LANES, LANES)], HEAD_DIM, 2) for p in range(2)], axis=0)
        o = _softmax_pv_sink(_dot_nt(qs, k), v, _sink_column(sink_ref, g * group, group, m))
        for pr in range(2):
            pair = _head_unstack(o[2 * pr * m:(2 * pr + 2) * m], HEAD_DIM, 2)
            os_ref[0, :, pl.ds((g * 2 + pr) * LANES, LANES)] = pair.astype(BF16)


def _ctx_attn(nat_c, swa_c, sink_tab):
    bsz, m, _ = nat_c.shape
    return pl.pallas_call(
        _ctx_attn_kernel,
        grid=(bsz,),
        in_specs=[
            pl.BlockSpec((1, m, NAT_PROJ_W), lambda b: (b, 0, 0)),
            pl.BlockSpec((1, m, SWA_PROJ_W), lambda b: (b, 0, 0)),
            pl.BlockSpec((SWA_Q_HEADS, LANES), lambda b: (0, 0)),
        ],
        out_specs=[
            pl.BlockSpec((1, m, NAT_W), lambda b: (b, 0, 0)),
            pl.BlockSpec((1, m, SWA_QW), lambda b: (b, 0, 0)),
        ],
        out_shape=[
            jax.ShapeDtypeStruct((bsz, m, NAT_W), BF16),
            jax.ShapeDtypeStruct((bsz, m, SWA_QW), BF16),
        ],
        compiler_params=_cparams("parallel"),
        name="ctx_attn",
    )(nat_c, swa_c, sink_tab)


def _out_proj_kernel(gla_ref, nat_ref, swa_ref, x_ref, mod_ref, gain_ref, w_ref, wrh_ref, wrl_ref,
                     x1_ref, h2_ref, aff_ref, afft_ref):
    mix = (_dot(gla_ref[0], w_ref[0:GLA_VW, :])
           + _dot(nat_ref[0], w_ref[GLA_VW:GLA_VW + NAT_W, :])
           + _dot(swa_ref[0], w_ref[GLA_VW + NAT_W:, :]))
    x1 = x_ref[0] + mod_ref[0, 2:3, :] * _rms(mix, gain_ref[1:2, :])
    x1_ref[0] = x1
    h2 = _rms(x1, gain_ref[2:3, :]) * (1.0 + mod_ref[0, 4:5, :]) + mod_ref[0, 3:4, :]
    hi, lo = _split2(h2)
    h2_ref[0] = hi
    wrh = wrh_ref[...]
    logits = _dot(hi, wrh) + _dot(lo, wrh) + _dot(hi, wrl_ref[...])
    lane = lax.broadcasted_iota(jnp.int32, logits.shape, 1)
    logits = jnp.where(lane < N_EXPERTS, logits, NEG_BIG)
    e = jnp.exp(logits - jnp.max(logits, axis=-1, keepdims=True))
    aff = e / jnp.sum(e, axis=-1, keepdims=True)
    aff_ref[0] = aff
    afft_ref[0] = aff.T[0:N_EXPERTS, :]


def _out_proj(gla_o, nat_o, swa_o, x, mod, gains, w_out, wr_hi, wr_lo, tm):
    bsz, t, d = x.shape
    tok = lambda b, i: (b, i, 0)
    const = lambda b, i: (0, 0)
    return pl.pallas_call(
        _out_proj_kernel,
        grid=(bsz, t // tm),
        in_specs=[
            pl.BlockSpec((1, tm, GLA_VW), tok),
            pl.BlockSpec((1, tm, NAT_W), tok),
            pl.BlockSpec((1, tm, SWA_QW), tok),
            pl.BlockSpec((1, tm, d), tok),
            pl.BlockSpec((1, 6, d), lambda b, i: (b, 0, 0)),
            pl.BlockSpec((4, d), const),
            pl.BlockSpec((d, d), const),
            pl.BlockSpec((d, LANES), const),
            pl.BlockSpec((d, LANES), const),
        ],
        out_specs=[
            pl.BlockSpec((1, tm, d), tok),
            pl.BlockSpec((1, tm, d), tok),
            pl.BlockSpec((1, tm, LANES), tok),
            pl.BlockSpec((1, N_EXPERTS, tm), lambda b, i: (b, 0, i)),
        ],
        out_shape=[
            jax.ShapeDtypeStruct((bsz, t, d), F32),
            jax.ShapeDtypeStruct((bsz, t, d), BF16),
            jax.ShapeDtypeStruct((bsz, t, LANES), F32),
            jax.ShapeDtypeStruct((bsz, N_EXPERTS, t), F32),
        ],
        compiler_params=_cparams("parallel", "parallel"),
        name="out_proj_router",
    )(gla_o, nat_o, swa_o, x, mod, gains, w_out, wr_hi, wr_lo)


def _excl_prefix(x):
    rows, n = x.shape
    nblk = n // LANES
    r = lax.broadcasted_iota(jnp.int32, (LANES, LANES), 0)
    c = lax.broadcasted_iota(jnp.int32, (LANES, LANES), 1)
    strict_upper = (r < c).astype(BF16)
    stacked = jnp.concatenate([x[:, j * LANES:(j + 1) * LANES] for j in range(nblk)], axis=0).astype(BF16)
    local = _dot(stacked, strict_upper)
    totals = jnp.sum(stacked.astype(F32), axis=-1, keepdims=True)
    out = []
    off = jnp.zeros((rows, 1), F32)
    for j in range(nblk):
        out.append(local[j * rows:(j + 1) * rows] + off)
        off = off + totals[j * rows:(j + 1) * rows]
    return jnp.concatenate(out, axis=1)


def _topc_kernel(aff_ref, rank_ref, rankcol_ref, *, cap):
    aff = aff_ref[0]
    bits = pltpu.bitcast(aff, jnp.int32)

    def step(i, thr):
        cand = thr | (jnp.int32(1) << (30 - i))
        cnt = jnp.sum((bits >= cand).astype(F32), axis=-1, keepdims=True)
        return jnp.where(cnt >= cap, cand, thr)

    thr = lax.fori_loop(0, 31, step, jnp.zeros((aff.shape[0], 1), jnp.int32))
    gt = bits > thr
    eq = bits == thr
    need = cap - jnp.sum(gt.astype(F32), axis=-1, keepdims=True)
    eq_before = _excl_prefix(eq.astype(F32))
    sel = gt | (eq & (eq_before < need))
    rank = jnp.where(sel, _excl_prefix(sel.astype(F32)), -1.0)
    rank_ref[0] = rank
    padded = jnp.concatenate([rank, jnp.full((LANES - rank.shape[0], rank.shape[1]), -1.0, F32)], axis=0)
    rankcol_ref[0] = padded.T


def _topc(aff_t, cap):
    bsz, e, n = aff_t.shape
    return pl.pallas_call(
        functools.partial(_topc_kernel, cap=cap),
        grid=(bsz,),
        in_specs=[pl.BlockSpec((1, e, n), lambda b: (b, 0, 0))],
        out_specs=[pl.BlockSpec((1, e, n), lambda b: (b, 0, 0)), pl.BlockSpec((1, n, LANES), lambda b: (b, 0, 0))],
        out_shape=[jax.ShapeDtypeStruct((bsz, e, n), F32), jax.ShapeDtypeStruct((bsz, n, LANES), F32)],
        compiler_params=_cparams("parallel"),
        name="expert_topc",
    )(aff_t)


def _gather_kernel(h_ref, rank_ref, *rest, cap):
    xs_ref = rest[-1]
    n = h_ref.shape[1]
    slot = lax.broadcasted_iota(jnp.int32, (cap, n), 0).astype(F32)
    h = h_ref[0]
    for e in range(N_EXPERTS):
        onehot = (slot == rank_ref[0, e:e + 1, :]).astype(BF16)
        xs_ref[e] = _dot(onehot, h).astype(BF16)


def _gather(h, rank, cap, total_rows, row_block0, dest=None):
    bsz, n, d = h.shape
    in_specs = [
        pl.BlockSpec((1, n, d), lambda b: (b, 0, 0)),
        pl.BlockSpec((1, N_EXPERTS, n), lambda b: (b, 0, 0)),
    ]
    args = [h, rank]
    aliases = {}
    if dest is not None:
        in_specs.append(pl.BlockSpec(memory_space=pl.ANY))
        args.append(dest)
        aliases = {2: 0}
    return pl.pallas_call(
        functools.partial(_gather_kernel, cap=cap),
        grid=(bsz,),
        in_specs=in_specs,
        out_specs=pl.BlockSpec((N_EXPERTS, cap, d), lambda b: (0, row_block0 + b, 0)),
        out_shape=jax.ShapeDtypeStruct((N_EXPERTS, total_rows, d), BF16),
        input_output_aliases=aliases,
        compiler_params=_cparams("parallel"),
        name="moe_gather",
    )(*args)


def _ffn_kernel(xs_ref, wg_ref, wu_ref, wd_ref, y_ref, wg_bf, wu_bf, wd_bf):
    @pl.when(pl.program_id(1) == 0)
    def _():
        wg_bf[...] = wg_ref[0, 0].astype(BF16)
        wu_bf[...] = wu_ref[0, 0].astype(BF16)
        wd_bf[...] = wd_ref[0, 0].astype(BF16)

    xs = xs_ref[0]
    hid = (_silu(_dot(xs, wg_bf[...])) * _dot(xs, wu_bf[...])).astype(BF16)
    y_ref[0] = _dot(hid, wd_bf[...]).astype(BF16)


def _ffn(xs, w_gate, w_up, w_down, layer, rb):
    _, e, d, ff = w_gate.shape
    rows = xs.shape[1]
    xspec = pl.BlockSpec((1, rb, d), lambda i, r: (i, r, 0))
    return pl.pallas_call(
        _ffn_kernel,
        grid=(e, rows // rb),
        in_specs=[
            xspec,
            pl.BlockSpec((1, 1, d, ff), lambda i, r: (layer, i, 0, 0)),
            pl.BlockSpec((1, 1, d, ff), lambda i, r: (layer, i, 0, 0)),
            pl.BlockSpec((1, 1, ff, d), lambda i, r: (layer, i, 0, 0)),
        ],
        out_specs=xspec,
        out_shape=jax.ShapeDtypeStruct(xs.shape, BF16),
        scratch_shapes=[pltpu.VMEM((d, ff), BF16), pltpu.VMEM((d, ff), BF16), pltpu.VMEM((ff, d), BF16)],
        compiler_params=_cparams("parallel", "arbitrary"),
        name="moe_ffn",
    )(xs, w_gate, w_up, w_down)


def _combine_kernel(y_ref, rankcol_ref, aff_ref, x_ref, mod_ref, gain_ref, o_ref, *, cap):
    rb = x_ref.shape[1]
    slot = lax.broadcasted_iota(jnp.int32, (rb, cap), 1).astype(F32)
    rc = rankcol_ref[0]
    af = aff_ref[0]
    acc = jnp.zeros((rb, x_ref.shape[2]), F32)
    for e in range(N_EXPERTS):
        onehot = (rc[:, e:e + 1] == slot).astype(BF16)
        acc = acc + af[:, e:e + 1] * _dot(onehot, y_ref[e])
    o_ref[0] = x_ref[0] + mod_ref[0, 5:6, :] * _rms(acc, gain_ref[3:4, :])


def _combine(y, row_block0, rankcol, aff, x, mod, gains, cap):
    bsz, n, d = x.shape
    rb = min(n, 512)
    tok = lambda b, i: (b, i, 0)
    return pl.pallas_call(
        functools.partial(_combine_kernel, cap=cap),
        grid=(bsz, n // rb),
        in_specs=[
            pl.BlockSpec((N_EXPERTS, cap, d), lambda b, i: (0, row_block0 + b, 0)),
            pl.BlockSpec((1, rb, LANES), tok),
            pl.BlockSpec((1, rb, LANES), tok),
            pl.BlockSpec((1, rb, d), tok),
            pl.BlockSpec((1, 6, d), lambda b, i: (b, 0, 0)),
            pl.BlockSpec((4, d), lambda b, i: (0, 0)),
        ],
        out_specs=pl.BlockSpec((1, rb, d), tok),
        out_shape=jax.ShapeDtypeStruct((bsz, n, d), F32),
        compiler_params=_cparams("parallel", "arbitrary"),
        name="moe_combine",
    )(y, rankcol, aff, x, mod, gains)


def _relayout_w_in(w):
    widths = (GLA_KW, GLA_VW, GLA_RANK, GLA_RANK, NAT_W, NAT_W, SWA_KW, SWA_KW, GLA_KW, GLA_VW, NAT_W, SWA_QW)
    gk, gv, gaf, gab, nk, nv, sk, sv, gq, gg, nq, sq = jnp.split(w, np.cumsum(widths)[:-1].tolist(), axis=1)
    pad = jnp.zeros((w.shape[0], LANES - 2 * GLA_RANK), w.dtype)
    sk0, sk1 = sk[:, :HEAD_DIM], sk[:, HEAD_DIM:]
    sv0, sv1 = sv[:, :HEAD_DIM], sv[:, HEAD_DIM:]
    cols = [gk, gq, gv, gg, gaf, gab, pad, nq, nk, nv, sq, sk0, sk0, sk1, sk1, sv0, sv0, sv1, sv1]
    return jnp.concatenate(cols, axis=1).astype(BF16)


def _rope_tables(t):
    half = HEAD_DIM // 4
    freqs = ROPE_BASE ** (-np.arange(half, dtype=np.float32) / half)
    pos = np.arange(t)
    ang_r = (pos // GRID_W).astype(np.float32)[:, None] * freqs
    ang_c = (pos % GRID_W).astype(np.float32)[:, None] * freqs
    cos = np.concatenate([np.cos(ang_r), np.cos(ang_r), np.cos(ang_c), np.cos(ang_c)], axis=1)
    sin = np.concatenate([-np.sin(ang_r), np.sin(ang_r), -np.sin(ang_c), np.sin(ang_c)], axis=1)
    reps = LANES // HEAD_DIM
    return jnp.asarray(np.tile(cos, (1, reps)), F32), jnp.asarray(np.tile(sin, (1, reps)), F32)


def _moe(h_list, afft_list, aff_list, x_list, mod_list, gains, w_gate, w_up, w_down, layer):
    bsz = h_list[0].shape[0]
    caps = [CAPACITY_FACTOR * h.shape[1] // N_EXPERTS for h in h_list]
    total_rows = sum(bsz * cap for cap in caps)
    ranks = [_topc(a, cap) for a, cap in zip(afft_list, caps)]
    xs = None
    row0 = 0
    blocks0 = []
    for h, r, cap in zip(h_list, ranks, caps):
        blocks0.append(row0 // cap)
        xs = _gather(h, r[0], cap, total_rows, row0 // cap, xs)
        row0 += bsz * cap
    rb = next(r for r in (768, 512, 384, 256, 128, 96, 64, 32) if total_rows % r == 0)
    y = _ffn(xs, w_gate, w_up, w_down, layer, rb)
    return [_combine(y, blk0, r[1], aff, x, mod, gains, cap)
            for blk0, r, aff, x, mod, cap in zip(blocks0, ranks, aff_list, x_list, mod_list, caps)]


def kernel(x, c, ctx, c_ctx, w_mod, b_mod, norm_gains, w_in, w_out, gla_a_up, gla_a_bias, gla_norm,
           nat_rpb, swa_sink, w_router, w_gate, w_up, w_down):
    bsz, t, d = x.shape
    m = ctx.shape[1]
    depth = w_mod.shape[0]

    cc = jnp.concatenate([c, c_ctx[None], jnp.zeros((16 - bsz - 1, d), F32)], axis=0)
    mod_all = _modulation(cc, w_mod, b_mod)
    rope_tabs = _rope_tables(t)
    zero_state = jnp.zeros((bsz, GLA_VW, GLA_KW), F32)

    xc = ctx
    for l in range(depth):
        update_ctx = l < depth - 1
        mod = mod_all[l, :bsz].reshape(bsz, 6, d)
        mod_c = jnp.broadcast_to(mod_all[l, bsz].reshape(1, 6, d), (bsz, 6, d))
        gains = norm_gains[l]
        w_in_l = _relayout_w_in(w_in[l])
        aup = jnp.zeros((LANES, 2 * GLA_KW), F32)
        aup = aup.at[0:GLA_RANK, 0:GLA_KW].set(gla_a_up[l, 0]).at[GLA_RANK:2 * GLA_RANK, GLA_KW:].set(gla_a_up[l, 1])
        abias = gla_a_bias[l].reshape(1, 2 * GLA_KW)
        gla_gain = jnp.tile(gla_norm[l], GLA_HEADS).reshape(1, GLA_VW)
        sink_tab = jnp.broadcast_to(swa_sink[l][:, None], (SWA_Q_HEADS, LANES))
        w_out_l = w_out[l].astype(BF16)
        wr = jnp.pad(w_router[l], ((0, 0), (0, LANES - N_EXPERTS)))
        wr_hi = wr.astype(BF16)
        wr_lo = (wr - wr_hi.astype(F32)).astype(BF16)

        gla_p, nat_p, swa_p = _in_proj(x, mod, gains, w_in_l, aup, abias, rope_tabs, 512)
        gla_c, nat_c, swa_c = _in_proj(xc, mod_c, gains, w_in_l, aup, abias, None, m)

        gla_co, s_f, s_b = _gla(gla_c, zero_state, zero_state, gla_gain)
        gla_o, _, _ = _gla(gla_p, s_f, s_b, gla_gain)
        nat_o = _nat(nat_p, nat_c, _nat_bias_table(nat_rpb[l], m))
        swa_o = _swa(swa_p, swa_c, sink_tab)

        x1, h2, aff, aff_t = _out_proj(gla_o, nat_o, swa_o, x, mod, gains, w_out_l, wr_hi, wr_lo, 512)
        if update_ctx:
            nat_co, swa_co = _ctx_attn(nat_c, swa_c, sink_tab)
            xc1, hc2, aff_c, aff_ct = _out_proj(gla_co, nat_co, swa_co, xc, mod_c, gains, w_out_l, wr_hi, wr_lo, m)
            x, xc = _moe([h2, hc2], [aff_t, aff_ct], [aff, aff_c], [x1, xc1], [mod, mod_c], gains,
                         w_gate, w_up, w_down, l)
        else:
            (x,) = _moe([h2], [aff_t], [aff], [x1], [mod], gains, w_gate, w_up, w_down, l)
    return x
```

```python
import functools

import jax
import jax.numpy as jnp
import numpy as np
from jax import lax
from jax.experimental import pallas as pl
from jax.experimental.pallas import tpu as pltpu

F32 = jnp.float32
BF16 = jnp.bfloat16

D_MODEL = 1024
GRID_W = 64
HEAD_DIM = 64
GLA_HEADS = 4
GLA_DK = 32
GLA_DV = 64
GLA_RANK = 16
GLA_TAU = 16.0
GLA_CHUNK = 64
NAT_HEADS = 4
NAT_WIN_ROWS = 8
NAT_WIN_COLS = 16
SWA_Q_HEADS = 8
SWA_KV_HEADS = 2
SWA_WINDOW = 128
SWA_BLOCK = 128
ROPE_BASE = 10000.0
N_EXPERTS = 16
EXPERT_FF = 1024
CAPACITY_FACTOR = 2
NORM_EPS = 1e-6

GLA_KW = GLA_HEADS * GLA_DK
GLA_VW = GLA_HEADS * GLA_DV
NAT_W = NAT_HEADS * HEAD_DIM
SWA_QW = SWA_Q_HEADS * HEAD_DIM
SWA_KW = SWA_KV_HEADS * HEAD_DIM
LANES = 128
NEG_BIG = -1e30

GLA_PROJ_W = 2 * GLA_KW + 2 * GLA_VW + LANES
GLA_OUT_W = 2 * GLA_KW + 2 * GLA_VW + 2 * GLA_KW
NAT_PROJ_W = 3 * NAT_W
SWA_PROJ_W = SWA_QW + 4 * SWA_KW
IN_PROJ_W = GLA_PROJ_W + NAT_PROJ_W + SWA_PROJ_W

VMEM_LIMIT = 56 * 1024 * 1024


def _cparams(*sem):
    return pltpu.CompilerParams(dimension_semantics=sem, vmem_limit_bytes=VMEM_LIMIT)


def _dot(a, b):
    return jnp.dot(a, b, preferred_element_type=F32)


def _dot_nt(a, b):
    return lax.dot_general(a, b, (((1,), (1,)), ((), ())), preferred_element_type=F32)


def _dot_tn(a, b):
    return lax.dot_general(a, b, (((0,), (0,)), ((), ())), preferred_element_type=F32)


def _split2(a):
    hi = a.astype(BF16)
    lo = (a - hi.astype(F32)).astype(BF16)
    return hi, lo


def _split3(a):
    hi = a.astype(BF16)
    r = a - hi.astype(F32)
    mid = r.astype(BF16)
    lo = (r - mid.astype(F32)).astype(BF16)
    return hi, mid, lo


def _dot_f32(a, b):
    ah, al = _split2(a)
    bh, bl = _split2(b)
    return _dot(ah, bh) + _dot(al, bh) + _dot(ah, bl)


def _rms(x, gain):
    return x * lax.rsqrt(jnp.mean(x * x, axis=-1, keepdims=True) + NORM_EPS) * gain


def _silu(x):
    return x * jax.nn.sigmoid(x)


def _mod_kernel(c_ref, w_ref, b_ref, o_ref):
    a = _silu(c_ref[...])
    o_ref[0] = _dot_f32(a, w_ref[0]) + b_ref[0]


def _modulation(cc, w_mod, b_mod):
    depth, d, n = w_mod.shape
    r = cc.shape[0]
    tn = 1536
    return pl.pallas_call(
        _mod_kernel,
        grid=(depth, n // tn),
        in_specs=[
            pl.BlockSpec((r, d), lambda l, j: (0, 0)),
            pl.BlockSpec((1, d, tn), lambda l, j: (l, 0, j)),
            pl.BlockSpec((1, 1, tn), lambda l, j: (l, 0, j)),
        ],
        out_specs=pl.BlockSpec((1, r, tn), lambda l, j: (l, 0, j)),
        out_shape=jax.ShapeDtypeStruct((depth, r, n), F32),
        compiler_params=_cparams("parallel", "parallel"),
        name="adaln_mod",
    )(cc, w_mod, b_mod.reshape(depth, 1, n))


def _rope_rotate(x, first_half):
    up = pltpu.roll(x, LANES - 16, 1)
    down = pltpu.roll(x, 16, 1)
    return jnp.where(first_half, up, down)


def _in_proj_kernel(*refs, rope):
    if rope:
        x_ref, mod_ref, gain_ref, w_ref, aup_ref, ab_ref, cos_ref, sin_ref, gla_ref, nat_ref, swa_ref = refs
    else:
        x_ref, mod_ref, gain_ref, w_ref, aup_ref, ab_ref, gla_ref, nat_ref, swa_ref = refs
    x = x_ref[0]
    y = _rms(x, gain_ref[0:1, :])
    h = (y * (1.0 + mod_ref[0, 1:2, :]) + mod_ref[0, 0:1, :]).astype(BF16)

    kq = 2 * GLA_KW + 2 * GLA_VW
    pg = _dot(h, w_ref[:, 0:GLA_PROJ_W])
    gla_ref[0, :, 0:kq] = pg[:, 0:kq]
    z = _dot_f32(pg[:, kq:kq + LANES], aup_ref[...]) + ab_ref[...]
    log_a = (jnp.minimum(z, 0.0) - jnp.log1p(jnp.exp(-jnp.abs(z)))) * (1.0 / GLA_TAU)
    tm = x.shape[0]
    row = lax.broadcasted_iota(jnp.int32, (tm, tm), 0)
    col = lax.broadcasted_iota(jnp.int32, (tm, tm), 1)
    same_chunk = (row // GLA_CHUNK) == (col // GLA_CHUNK)
    for j, tri in enumerate(((same_chunk & (row >= col)).astype(BF16), (same_chunk & (row <= col)).astype(BF16))):
        hi, mid, lo = _split3(log_a[:, j * GLA_KW:(j + 1) * GLA_KW])
        gla_ref[0, :, kq + j * GLA_KW:kq + (j + 1) * GLA_KW] = _dot(tri, hi) + _dot(tri, mid) + _dot(tri, lo)

    o = GLA_PROJ_W
    pn = _dot(h, w_ref[:, o:o + NAT_PROJ_W])
    scale = HEAD_DIM ** -0.5
    nat_ref[0, :, 0:NAT_W] = (pn[:, 0:NAT_W] * scale).astype(BF16)
    nat_ref[0, :, NAT_W:] = pn[:, NAT_W:].astype(BF16)

    o = GLA_PROJ_W + NAT_PROJ_W
    ps = _dot(h, w_ref[:, o:o + SWA_PROJ_W])
    n_rot = (SWA_QW + 2 * SWA_KW) // LANES
    if rope:
        cos = cos_ref[...]
        sin = sin_ref[...]
        lane = lax.broadcasted_iota(jnp.int32, cos.shape, 1)
        first_half = (lane % 32) < 16
    for j in range(SWA_PROJ_W // LANES):
        t = ps[:, j * LANES:(j + 1) * LANES]
        if rope and j < n_rot:
            t = t * cos + _rope_rotate(t, first_half) * sin
        if j < SWA_QW // LANES:
            t = t * scale
        swa_ref[0, :, j * LANES:(j + 1) * LANES] = t.astype(BF16)


def _in_proj(x, mod, gains, w, aup, abias, rope_tabs, tm):
    bsz, t, d = x.shape
    rope = rope_tabs is not None
    in_specs = [
        pl.BlockSpec((1, tm, d), lambda b, i: (b, i, 0)),
        pl.BlockSpec((1, 6, d), lambda b, i: (b, 0, 0)),
        pl.BlockSpec((4, d), lambda b, i: (0, 0)),
        pl.BlockSpec((d, IN_PROJ_W), lambda b, i: (0, 0)),
        pl.BlockSpec((LANES, 2 * GLA_KW), lambda b, i: (0, 0)),
        pl.BlockSpec((1, 2 * GLA_KW), lambda b, i: (0, 0)),
    ]
    args = [x, mod, gains, w, aup, abias]
    if rope:
        in_specs += [pl.BlockSpec((tm, LANES), lambda b, i: (i, 0))] * 2
        args += list(rope_tabs)
    return pl.pallas_call(
        functools.partial(_in_proj_kernel, rope=rope),
        grid=(bsz, t // tm),
        in_specs=in_specs,
        out_specs=[
            pl.BlockSpec((1, tm, GLA_OUT_W), lambda b, i: (b, i, 0)),
            pl.BlockSpec((1, tm, NAT_PROJ_W), lambda b, i: (b, i, 0)),
            pl.BlockSpec((1, tm, SWA_PROJ_W), lambda b, i: (b, i, 0)),
        ],
        out_shape=[
            jax.ShapeDtypeStruct((bsz, t, GLA_OUT_W), F32),
            jax.ShapeDtypeStruct((bsz, t, NAT_PROJ_W), BF16),
            jax.ShapeDtypeStruct((bsz, t, SWA_PROJ_W), BF16),
        ],
        compiler_params=_cparams("parallel", "parallel"),
        name="in_proj_rope" if rope else "in_proj",
    )(*args)


def _gla_kernel(p_ref, s0f_ref, s0b_ref, gain_ref, o_ref, sff_ref, sfb_ref, acc_ref, sf_ref, sb_ref):
    t = p_ref.shape[1]
    c = GLA_CHUNK
    nc = t // c
    qi = lax.broadcasted_iota(jnp.int32, (c, GLA_HEADS * c), 0)
    kj = lax.broadcasted_iota(jnp.int32, (c, GLA_HEADS * c), 1) % c
    lower = qi >= kj
    upper = qi <= kj
    srow = lax.broadcasted_iota(jnp.int32, (GLA_VW, GLA_KW), 0) // GLA_DV
    scol = lax.broadcasted_iota(jnp.int32, (GLA_VW, GLA_KW), 1) // GLA_DK
    head_diag = srow == scol
    q_scale = GLA_DK ** -0.5

    sf_ref[...] = s0f_ref[0]
    sb_ref[...] = s0b_ref[0]
    acc_ref[...] = jnp.zeros_like(acc_ref)

    def one_direction(base, keep, end_row, cum_off, st_ref):
        rows = pl.ds(base, c)
        k = p_ref[0, rows, 0:GLA_KW]
        q = p_ref[0, rows, GLA_KW:2 * GLA_KW] * q_scale
        vb = p_ref[0, rows, 2 * GLA_KW:2 * GLA_KW + GLA_VW].astype(BF16)
        cum = p_ref[0, rows, cum_off:cum_off + GLA_KW]
        mid = cum[c // 2:c // 2 + 1, :]
        total = cum[end_row:end_row + 1, :]
        st = st_ref[...]
        inter = _dot_nt((q * jnp.exp(cum)).astype(BF16), st.astype(BF16))
        qs = (q * jnp.exp(cum - mid)).astype(BF16)
        ks = (k * jnp.exp(mid - cum)).astype(BF16)
        s = _dot_nt(qs, _head_stack(ks, GLA_DK, GLA_HEADS))
        s = jnp.where(keep, s, 0.0).astype(BF16)
        intra = _dot(s, _head_stack(vb, GLA_DV, GLA_HEADS))
        acc_ref[rows, :] += inter + intra
        kd = (k * jnp.exp(total - cum)).astype(BF16)
        upd = _dot_tn(vb, kd)
        st_ref[...] = st * jnp.exp(total) + jnp.where(head_diag, upd, 0.0)

    def body(i, carry):
        one_direction(pl.multiple_of(i * c, c), lower, c - 1, 2 * GLA_KW + 2 * GLA_VW, sf_ref)
        one_direction(pl.multiple_of((nc - 1 - i) * c, c), upper, 0, 3 * GLA_KW + 2 * GLA_VW, sb_ref)
        return carry

    lax.fori_loop(0, nc, body, 0)
    sff_ref[0] = sf_ref[...]
    sfb_ref[0] = sb_ref[...]

    hr = lax.broadcasted_iota(jnp.int32, (GLA_VW, GLA_VW), 0) // GLA_DV
    hc = lax.broadcasted_iota(jnp.int32, (GLA_VW, GLA_VW), 1) // GLA_DV
    head_ones = (hr == hc).astype(BF16)
    blk = 256
    for j in range(t // blk):
        rows = pl.ds(j * blk, blk)
        o = acc_ref[rows, :]
        hi, mid, lo = _split3(o * o)
        ms = (_dot(hi, head_ones) + _dot(mid, head_ones) + _dot(lo, head_ones)) * (1.0 / GLA_DV)
        gate = p_ref[0, rows, 2 * GLA_KW + GLA_VW:2 * GLA_KW + 2 * GLA_VW]
        o_ref[0, rows, :] = (o * lax.rsqrt(ms + NORM_EPS) * gain_ref[...] * _silu(gate)).astype(BF16)


def _gla(p, s0f, s0b, gain):
    bsz, t, w = p.shape
    st_spec = pl.BlockSpec((1, GLA_VW, GLA_KW), lambda b: (b, 0, 0))
    st_shape = jax.ShapeDtypeStruct((bsz, GLA_VW, GLA_KW), F32)
    return pl.pallas_call(
        _gla_kernel,
        grid=(bsz,),
        in_specs=[
            pl.BlockSpec((1, t, w), lambda b: (b, 0, 0)),
            st_spec, st_spec,
            pl.BlockSpec((1, GLA_VW), lambda b: (0, 0)),
        ],
        out_specs=[pl.BlockSpec((1, t, GLA_VW), lambda b: (b, 0, 0)), st_spec, st_spec],
        out_shape=[jax.ShapeDtypeStruct((bsz, t, GLA_VW), BF16), st_shape, st_shape],
        scratch_shapes=[
            pltpu.VMEM((t, GLA_VW), F32),
            pltpu.VMEM((GLA_VW, GLA_KW), F32),
            pltpu.VMEM((GLA_VW, GLA_KW), F32),
        ],
        compiler_params=_cparams("parallel"),
        name="gla_bidir",
    )(p, s0f, s0b, gain)


def _head_stack(q, width, n):
    lane = lax.broadcasted_iota(jnp.int32, (1, q.shape[1]), 1) // width
    return jnp.concatenate([jnp.where(lane == h, q, jnp.zeros_like(q)) for h in range(n)], axis=0)


def _head_unstack(o, width, n):
    rows = o.shape[0] // n
    lane = lax.broadcasted_iota(jnp.int32, (1, o.shape[1]), 1) // width
    out = o[0:rows]
    for h in range(1, n):
        out = jnp.where(lane == h, o[h * rows:(h + 1) * rows], out)
    return out


NAT_ROWS_PER_STEP = 4


def _nat_kernel(lat_ref, ctx_ref, bias_ref, o_ref):
    n_rows = lat_ref.shape[1] // GRID_W
    kc = ctx_ref[0, :, NAT_W:2 * NAT_W]
    vc = ctx_ref[0, :, 2 * NAT_W:3 * NAT_W]
    n_loc = NAT_WIN_ROWS * GRID_W
    for i in range(NAT_ROWS_PER_STEP):
        r = pl.program_id(1) * NAT_ROWS_PER_STEP + i
        r0 = jnp.clip(r - NAT_WIN_ROWS // 2, 0, n_rows - NAT_WIN_ROWS)
        q = lat_ref[0, pl.ds(pl.multiple_of(r * GRID_W, GRID_W), GRID_W), 0:NAT_W]
        krows = pl.ds(pl.multiple_of(r0 * GRID_W, GRID_W), n_loc)
        qs = _head_stack(q, HEAD_DIM, NAT_HEADS)
        s = jnp.concatenate([_dot_nt(qs, lat_ref[0, krows, NAT_W:2 * NAT_W]), _dot_nt(qs, kc)], axis=1)
        s = s + bias_ref[r0 - r + NAT_WIN_ROWS - 1]
        p = jnp.exp(s - jnp.max(s, axis=-1, keepdims=True))
        den = jnp.sum(p, axis=-1, keepdims=True)
        p = p.astype(BF16)
        o = (_dot(p[:, 0:n_loc], lat_ref[0, krows, 2 * NAT_W:3 * NAT_W]) + _dot(p[:, n_loc:], vc)) / den
        o_ref[0, i * GRID_W:(i + 1) * GRID_W, :] = _head_unstack(o, HEAD_DIM, NAT_HEADS).astype(BF16)


def _nat(lat, ctx, bias):
    bsz, t, w = lat.shape
    m = ctx.shape[1]
    rows = NAT_ROWS_PER_STEP * GRID_W
    return pl.pallas_call(
        _nat_kernel,
        grid=(bsz, t // rows),
        in_specs=[
            pl.BlockSpec((1, t, w), lambda b, r: (b, 0, 0)),
            pl.BlockSpec((1, m, w), lambda b, r: (b, 0, 0)),
            pl.BlockSpec(bias.shape, lambda b, r: (0, 0, 0)),
        ],
        out_specs=pl.BlockSpec((1, rows, NAT_W), lambda b, r: (b, r, 0)),
        out_shape=jax.ShapeDtypeStruct((bsz, t, NAT_W), BF16),
        compiler_params=_cparams("parallel", "arbitrary"),
        name="nat_attn",
    )(lat, ctx, bias)


def _nat_bias_table(rpb, m):
    cols = np.arange(GRID_W)
    col_start = np.clip(cols - NAT_WIN_COLS // 2, 0, GRID_W - NAT_WIN_COLS)
    kc = np.arange(GRID_W)
    inside = (kc[None, :] >= col_start[:, None]) & (kc[None, :] < col_start[:, None] + NAT_WIN_COLS)
    rel = kc[None, :] - cols[:, None] + NAT_WIN_COLS - 1
    col_sel = (rel[:, :, None] == np.arange(2 * NAT_WIN_COLS - 1)) & inside[:, :, None]
    ridx = np.arange(NAT_WIN_ROWS)[:, None] + np.arange(NAT_WIN_ROWS)[None, :]
    row_sel = ridx[:, :, None] == np.arange(2 * NAT_WIN_ROWS - 1)
    tab = jnp.einsum("hab,oia,ckb->ohcik", rpb.astype(F32), jnp.asarray(row_sel, F32), jnp.asarray(col_sel, F32),
                     precision=lax.Precision.HIGHEST)
    tab = jnp.where(jnp.asarray(inside)[None, None, :, None, :], tab, NEG_BIG)
    tab = tab.reshape(NAT_WIN_ROWS, NAT_HEADS * GRID_W, NAT_WIN_ROWS * GRID_W)
    return jnp.concatenate([tab, jnp.zeros(tab.shape[:2] + (m,), F32)], axis=2)


def _sink_column(sink_ref, first_head, n_heads, rows):
    return jnp.concatenate(
        [jnp.broadcast_to(sink_ref[first_head + j:first_head + j + 1, 0:1], (rows, 1)) for j in range(n_heads)], axis=0)


def _softmax_pv_sink(s, v, sink):
    m = jnp.maximum(jnp.max(s, axis=-1, keepdims=True), sink)
    p = jnp.exp(s - m).astype(BF16)
    o = _dot(p, jnp.concatenate([v, jnp.ones_like(v)], axis=1))
    w = v.shape[1]
    return o[:, 0:w] / (o[:, w:] + jnp.exp(sink - m))


def _swa_kernel(q_ref, kp_ref, kn_ref, kx_ref, vp_ref, vn_ref, vx_ref, kc_ref, vc_ref, sink_ref, o_ref):
    n = pl.program_id(1)
    nb = pl.num_programs(1)
    blk = SWA_BLOCK
    m = kc_ref.shape[1]
    qi = lax.broadcasted_iota(jnp.int32, (blk, 3 * blk + m), 0)
    kj = lax.broadcasted_iota(jnp.int32, (blk, 3 * blk + m), 1)
    kpos = (n - 1) * blk + kj
    local_ok = (jnp.abs(kj - blk - qi) <= SWA_WINDOW) & (kpos >= 0) & (kpos < nb * blk)
    mask_add = jnp.where(local_ok | (kj >= 3 * blk), 0.0, NEG_BIG)[None]
    group = SWA_Q_HEADS // SWA_KV_HEADS
    for g in range(SWA_KV_HEADS):
        gl = pl.ds(g * LANES, LANES)
        k = jnp.concatenate([kp_ref[0, :, gl], kn_ref[0, :, gl], kx_ref[0, :, gl], kc_ref[0, :, gl]], axis=0)
        v = jnp.concatenate([vp_ref[0, :, gl], vn_ref[0, :, gl], vx_ref[0, :, gl], vc_ref[0, :, gl]], axis=0)
        qs = jnp.concatenate(
            [_head_stack(q_ref[0, :, pl.ds((g * 2 + p) * LANES, LANES)], HEAD_DIM, 2) for p in range(2)], axis=0)
        s = (_dot_nt(qs, k).reshape(group, blk, 3 * blk + m) + mask_add).reshape(group * blk, 3 * blk + m)
        o = _softmax_pv_sink(s, v, _sink_column(sink_ref, g * group, group, blk))
        for p in range(2):
            pair = _head_unstack(o[2 * p * blk:(2 * p + 2) * blk], HEAD_DIM, 2)
            o_ref[0, :, pl.ds((g * 2 + p) * LANES, LANES)] = pair.astype(BF16)


def _swa(lat, ctx, sink_tab):
    bsz, t, w = lat.shape
    m = ctx.shape[1]
    nb = t // SWA_BLOCK
    qb = SWA_QW // (2 * SWA_KW)
    kw = 2 * SWA_KW

    def spec(which, lane_blk):
        if which == "prev":
            return pl.BlockSpec((1, SWA_BLOCK, kw), lambda b, n: (b, jnp.maximum(n - 1, 0), lane_blk))
        if which == "next":
            return pl.BlockSpec((1, SWA_BLOCK, kw), lambda b, n: (b, jnp.minimum(n + 1, nb - 1), lane_blk))
        return pl.BlockSpec((1, SWA_BLOCK, kw), lambda b, n: (b, n, lane_blk))

    return pl.pallas_call(
        _swa_kernel,
        grid=(bsz, nb),
        in_specs=[
            pl.BlockSpec((1, SWA_BLOCK, SWA_QW), lambda b, n: (b, n, 0)),
            spec("prev", qb), spec("cur", qb), spec("next", qb),
            spec("prev", qb + 1), spec("cur", qb + 1), spec("next", qb + 1),
            pl.BlockSpec((1, m, kw), lambda b, n: (b, 0, qb)),
            pl.BlockSpec((1, m, kw), lambda b, n: (b, 0, qb + 1)),
            pl.BlockSpec((SWA_Q_HEADS, LANES), lambda b, n: (0, 0)),
        ],
        out_specs=pl.BlockSpec((1, SWA_BLOCK, SWA_QW), lambda b, n: (b, n, 0)),
        out_shape=jax.ShapeDtypeStruct((bsz, t, SWA_QW), BF16),
        compiler_params=_cparams("parallel", "arbitrary"),
        name="swa_attn",
    )(lat, lat, lat, lat, lat, lat, lat, ctx, ctx, sink_tab)


def _ctx_attn_kernel(nat_ref, swa_ref, sink_ref, on_ref, os_ref):
    m = nat_ref.shape[1]
    q = nat_ref[0, :, 0:NAT_W]
    k = nat_ref[0, :, NAT_W:2 * NAT_W]
    v = nat_ref[0, :, 2 * NAT_W:3 * NAT_W]
    s = _dot_nt(_head_stack(q, HEAD_DIM, NAT_HEADS), k)
    p = jnp.exp(s - jnp.max(s, axis=-1, keepdims=True))
    o = _dot(p.astype(BF16), v) / jnp.sum(p, axis=-1, keepdims=True)
    on_ref[0] = _head_unstack(o, HEAD_DIM, NAT_HEADS).astype(BF16)

    group = SWA_Q_HEADS // SWA_KV_HEADS
    for g in range(SWA_KV_HEADS):
        k = swa_ref[0, :, pl.ds(SWA_QW + g * LANES, LANES)]
        v = swa_ref[0, :, pl.ds(SWA_QW + 2 * SWA_KW + g * LANES, LANES)]
        qs = jnp.concatenate(
            [_head_stack(swa_ref[0, :, pl.ds((g * 2 + p) * LANES, LANES)], HEAD_DIM, 2) for p in range(2)], axis=0)
        o = _softmax_pv_sink(_dot_nt(qs, k), v, _sink_column(sink_ref, g * group, group, m))
        for pr in range(2):
            pair = _head_unstack(o[2 * pr * m:(2 * pr + 2) * m], HEAD_DIM, 2)
            os_ref[0, :, pl.ds((g * 2 + pr) * LANES, LANES)] = pair.astype(BF16)


def _ctx_attn(nat_c, swa_c, sink_tab):
    bsz, m, _ = nat_c.shape
    return pl.pallas_call(
        _ctx_attn_kernel,
        grid=(bsz,),
        in_specs=[
            pl.BlockSpec((1, m, NAT_PROJ_W), lambda b: (b, 0, 0)),
            pl.BlockSpec((1, m, SWA_PROJ_W), lambda b: (b, 0, 0)),
            pl.BlockSpec((SWA_Q_HEADS, LANES), lambda b: (0, 0)),
        ],
        out_specs=[
            pl.BlockSpec((1, m, NAT_W), lambda b: (b, 0, 0)),
            pl.BlockSpec((1, m, SWA_QW), lambda b: (b, 0, 0)),
        ],
        out_shape=[
            jax.ShapeDtypeStruct((bsz, m, NAT_W), BF16),
            jax.ShapeDtypeStruct((bsz, m, SWA_QW), BF16),
        ],
        compiler_params=_cparams("parallel"),
        name="ctx_attn",
    )(nat_c, swa_c, sink_tab)


def _out_proj_kernel(gla_ref, nat_ref, swa_ref, x_ref, mod_ref, gain_ref, w_ref, wrh_ref, wrl_ref,
                     x1_ref, h2_ref, aff_ref, afft_ref):
    mix = (_dot(gla_ref[0], w_ref[0:GLA_VW, :])
           + _dot(nat_ref[0], w_ref[GLA_VW:GLA_VW + NAT_W, :])
           + _dot(swa_ref[0], w_ref[GLA_VW + NAT_W:, :]))
    x1 = x_ref[0] + mod_ref[0, 2:3, :] * _rms(mix, gain_ref[1:2, :])
    x1_ref[0] = x1
    h2 = _rms(x1, gain_ref[2:3, :]) * (1.0 + mod_ref[0, 4:5, :]) + mod_ref[0, 3:4, :]
    hi, lo = _split2(h2)
    h2_ref[0] = hi
    wrh = wrh_ref[...]
    logits = _dot(hi, wrh) + _dot(lo, wrh) + _dot(hi, wrl_ref[...])
    lane = lax.broadcasted_iota(jnp.int32, logits.shape, 1)
    logits = jnp.where(lane < N_EXPERTS, logits, NEG_BIG)
    e = jnp.exp(logits - jnp.max(logits, axis=-1, keepdims=True))
    aff = e / jnp.sum(e, axis=-1, keepdims=True)
    aff_ref[0] = aff
    afft_ref[0] = aff.T[0:N_EXPERTS, :]


def _out_proj(gla_o, nat_o, swa_o, x, mod, gains, w_out, wr_hi, wr_lo, tm):
    bsz, t, d = x.shape
    tok = lambda b, i: (b, i, 0)
    const = lambda b, i: (0, 0)
    return pl.pallas_call(
        _out_proj_kernel,
        grid=(bsz, t // tm),
        in_specs=[
            pl.BlockSpec((1, tm, GLA_VW), tok),
            pl.BlockSpec((1, tm, NAT_W), tok),
            pl.BlockSpec((1, tm, SWA_QW), tok),
            pl.BlockSpec((1, tm, d), tok),
            pl.BlockSpec((1, 6, d), lambda b, i: (b, 0, 0)),
            pl.BlockSpec((4, d), const),
            pl.BlockSpec((d, d), const),
            pl.BlockSpec((d, LANES), const),
            pl.BlockSpec((d, LANES), const),
        ],
        out_specs=[
            pl.BlockSpec((1, tm, d), tok),
            pl.BlockSpec((1, tm, d), tok),
            pl.BlockSpec((1, tm, LANES), tok),
            pl.BlockSpec((1, N_EXPERTS, tm), lambda b, i: (b, 0, i)),
        ],
        out_shape=[
            jax.ShapeDtypeStruct((bsz, t, d), F32),
            jax.ShapeDtypeStruct((bsz, t, d), BF16),
            jax.ShapeDtypeStruct((bsz, t, LANES), F32),
            jax.ShapeDtypeStruct((bsz, N_EXPERTS, t), F32),
        ],
        compiler_params=_cparams("parallel", "parallel"),
        name="out_proj_router",
    )(gla_o, nat_o, swa_o, x, mod, gains, w_out, wr_hi, wr_lo)


def _excl_prefix(x):
    rows, n = x.shape
    nblk = n // LANES
    r = lax.broadcasted_iota(jnp.int32, (LANES, LANES), 0)
    c = lax.broadcasted_iota(jnp.int32, (LANES, LANES), 1)
    strict_upper = (r < c).astype(BF16)
    stacked = jnp.concatenate([x[:, j * LANES:(j + 1) * LANES] for j in range(nblk)], axis=0).astype(BF16)
    local = _dot(stacked, strict_upper)
    totals = jnp.sum(stacked.astype(F32), axis=-1, keepdims=True)
    out = []
    off = jnp.zeros((rows, 1), F32)
    for j in range(nblk):
        out.append(local[j * rows:(j + 1) * rows] + off)
        off = off + totals[j * rows:(j + 1) * rows]
    return jnp.concatenate(out, axis=1)


def _topc_kernel(aff_ref, rank_ref, rankcol_ref, *, cap):
    aff = aff_ref[0]
    bits = pltpu.bitcast(aff, jnp.int32)

    def step(i, thr):
        cand = thr | (jnp.int32(1) << (30 - i))
        cnt = jnp.sum((bits >= cand).astype(F32), axis=-1, keepdims=True)
        return jnp.where(cnt >= cap, cand, thr)

    thr = lax.fori_loop(0, 31, step, jnp.zeros((aff.shape[0], 1), jnp.int32))
    gt = bits > thr
    eq = bits == thr
    need = cap - jnp.sum(gt.astype(F32), axis=-1, keepdims=True)
    eq_before = _excl_prefix(eq.astype(F32))
    sel = gt | (eq & (eq_before < need))
    rank = jnp.where(sel, _excl_prefix(sel.astype(F32)), -1.0)
    rank_ref[0] = rank
    padded = jnp.concatenate([rank, jnp.full((LANES - rank.shape[0], rank.shape[1]), -1.0, F32)], axis=0)
    rankcol_ref[0] = padded.T


def _topc(aff_t, cap):
    bsz, e, n = aff_t.shape
    return pl.pallas_call(
        functools.partial(_topc_kernel, cap=cap),
        grid=(bsz,),
        in_specs=[pl.BlockSpec((1, e, n), lambda b: (b, 0, 0))],
        out_specs=[pl.BlockSpec((1, e, n), lambda b: (b, 0, 0)), pl.BlockSpec((1, n, LANES), lambda b: (b, 0, 0))],
        out_shape=[jax.ShapeDtypeStruct((bsz, e, n), F32), jax.ShapeDtypeStruct((bsz, n, LANES), F32)],
        compiler_params=_cparams("parallel"),
        name="expert_topc",
    )(aff_t)


def _gather_kernel(h_ref, rank_ref, xs_ref, *, cap):
    n = h_ref.shape[1]
    slot = lax.broadcasted_iota(jnp.int32, (cap, n), 0).astype(F32)
    h = h_ref[0]
    for e in range(N_EXPERTS):
        onehot = (slot == rank_ref[0, e:e + 1, :]).astype(BF16)
        xs_ref[e] = _dot(onehot, h).astype(BF16)


def _gather(h, rank, cap):
    bsz, n, d = h.shape
    return pl.pallas_call(
        functools.partial(_gather_kernel, cap=cap),
        grid=(bsz,),
        in_specs=[
            pl.BlockSpec((1, n, d), lambda b: (b, 0, 0)),
            pl.BlockSpec((1, N_EXPERTS, n), lambda b: (b, 0, 0)),
        ],
        out_specs=pl.BlockSpec((N_EXPERTS, cap, d), lambda b: (0, b, 0)),
        out_shape=jax.ShapeDtypeStruct((N_EXPERTS, bsz * cap, d), BF16),
        compiler_params=_cparams("parallel"),
        name="moe_gather",
    )(h, rank)


FFN_PARTS = 4


def _ffn_kernel(*refs, has_ctx):
    if has_ctx:
        xs_ref, xc_ref, wg_ref, wu_ref, wd_ref, y_ref, yc_ref, wg_bf, wu_bf, wd_bf = refs
    else:
        xs_ref, wg_ref, wu_ref, wd_ref, y_ref, wg_bf, wu_bf, wd_bf = refs
    s = pl.program_id(0)
    j = pl.program_id(1)
    n_exp = pl.num_programs(0) - 1
    piece = wg_ref.shape[2]

    @pl.when((s < n_exp) & (j < FFN_PARTS))
    def _():
        rows = pl.ds(pl.multiple_of(j * piece, piece), piece)
        wg_bf[s % 2, rows, :] = wg_ref[0, 0].astype(BF16)
        wu_bf[s % 2, rows, :] = wu_ref[0, 0].astype(BF16)
        wd_bf[s % 2, rows, :] = wd_ref[0, 0].astype(BF16)

    def swiglu(x_ref, o_ref):
        cur = (s + 1) % 2
        xs = x_ref[0]
        hid = (_silu(_dot(xs, wg_bf[cur])) * _dot(xs, wu_bf[cur])).astype(BF16)
        o_ref[0] = _dot(hid, wd_bf[cur]).astype(BF16)

    @pl.when((s > 0) & (j < FFN_PARTS))
    def _():
        swiglu(xs_ref, y_ref)

    if has_ctx:
        @pl.when((s > 0) & (j == FFN_PARTS))
        def _():
            swiglu(xc_ref, yc_ref)


def _ffn(xs, xc, w_gate, w_up, w_down, layer):
    _, e, d, ff = w_gate.shape
    rb = xs.shape[1] // FFN_PARTS
    has_ctx = xc is not None
    last = FFN_PARTS - 1

    def x_idx(s, j):
        return (jnp.maximum(s - 1, 0), jnp.where(s > 0, jnp.minimum(j, last), 0), 0)

    def w_idx(s, j):
        return (layer, jnp.minimum(s, e - 1), jnp.minimum(j, last), 0)

    xspec = pl.BlockSpec((1, rb, d), x_idx)
    cspecs, cargs, cshapes = [], [], []
    if has_ctx:
        cspecs = [pl.BlockSpec((1, xc.shape[1], d), lambda s, j: (jnp.maximum(s - 1, 0), 0, 0))]
        cargs = [xc]
        cshapes = [jax.ShapeDtypeStruct(xc.shape, BF16)]
    out = pl.pallas_call(
        functools.partial(_ffn_kernel, has_ctx=has_ctx),
        grid=(e + 1, FFN_PARTS + int(has_ctx)),
        in_specs=[xspec] + cspecs + [
            pl.BlockSpec((1, 1, d // FFN_PARTS, ff), w_idx),
            pl.BlockSpec((1, 1, d // FFN_PARTS, ff), w_idx),
            pl.BlockSpec((1, 1, ff // FFN_PARTS, d), w_idx),
        ],
        out_specs=[xspec] + cspecs,
        out_shape=[jax.ShapeDtypeStruct(xs.shape, BF16)] + cshapes,
        scratch_shapes=[pltpu.VMEM((2, d, ff), BF16), pltpu.VMEM((2, d, ff), BF16), pltpu.VMEM((2, ff, d), BF16)],
        compiler_params=_cparams("arbitrary", "arbitrary"),
        name="moe_ffn",
    )(xs, *cargs, w_gate, w_up, w_down)
    return out


def _combine_kernel(y_ref, rankcol_ref, aff_ref, x_ref, mod_ref, gain_ref, o_ref, *, cap):
    rb = x_ref.shape[1]
    slot = lax.broadcasted_iota(jnp.int32, (rb, cap), 1).astype(F32)
    rc = rankcol_ref[0]
    af = aff_ref[0]
    acc = jnp.zeros((rb, x_ref.shape[2]), F32)
    for e in range(N_EXPERTS):
        onehot = (rc[:, e:e + 1] == slot).astype(BF16)
        acc = acc + af[:, e:e + 1] * _dot(onehot, y_ref[e])
    o_ref[0] = x_ref[0] + mod_ref[0, 5:6, :] * _rms(acc, gain_ref[3:4, :])


def _combine(y, rankcol, aff, x, mod, gains, cap):
    bsz, n, d = x.shape
    rb = min(n, 512)
    tok = lambda b, i: (b, i, 0)
    return pl.pallas_call(
        functools.partial(_combine_kernel, cap=cap),
        grid=(bsz, n // rb),
        in_specs=[
            pl.BlockSpec((N_EXPERTS, cap, d), lambda b, i: (0, b, 0)),
            pl.BlockSpec((1, rb, LANES), tok),
            pl.BlockSpec((1, rb, LANES), tok),
            pl.BlockSpec((1, rb, d), tok),
            pl.BlockSpec((1, 6, d), lambda b, i: (b, 0, 0)),
            pl.BlockSpec((4, d), lambda b, i: (0, 0)),
        ],
        out_specs=pl.BlockSpec((1, rb, d), tok),
        out_shape=jax.ShapeDtypeStruct((bsz, n, d), F32),
        compiler_params=_cparams("parallel", "arbitrary"),
        name="moe_combine",
    )(y, rankcol, aff, x, mod, gains)


def _relayout_w_in(w):
    widths = (GLA_KW, GLA_VW, GLA_RANK, GLA_RANK, NAT_W, NAT_W, SWA_KW, SWA_KW, GLA_KW, GLA_VW, NAT_W, SWA_QW)
    gk, gv, gaf, gab, nk, nv, sk, sv, gq, gg, nq, sq = jnp.split(w, np.cumsum(widths)[:-1].tolist(), axis=1)
    pad = jnp.zeros((w.shape[0], LANES - 2 * GLA_RANK), w.dtype)
    sk0, sk1 = sk[:, :HEAD_DIM], sk[:, HEAD_DIM:]
    sv0, sv1 = sv[:, :HEAD_DIM], sv[:, HEAD_DIM:]
    cols = [gk, gq, gv, gg, gaf, gab, pad, nq, nk, nv, sq, sk0, sk0, sk1, sk1, sv0, sv0, sv1, sv1]
    return jnp.concatenate(cols, axis=1).astype(BF16)


def _rope_tables(t):
    half = HEAD_DIM // 4
    freqs = ROPE_BASE ** (-np.arange(half, dtype=np.float32) / half)
    pos = np.arange(t)
    ang_r = (pos // GRID_W).astype(np.float32)[:, None] * freqs
    ang_c = (pos % GRID_W).astype(np.float32)[:, None] * freqs
    cos = np.concatenate([np.cos(ang_r), np.cos(ang_r), np.cos(ang_c), np.cos(ang_c)], axis=1)
    sin = np.concatenate([-np.sin(ang_r), np.sin(ang_r), -np.sin(ang_c), np.sin(ang_c)], axis=1)
    reps = LANES // HEAD_DIM
    return jnp.asarray(np.tile(cos, (1, reps)), F32), jnp.asarray(np.tile(sin, (1, reps)), F32)


def _moe(h_list, afft_list, aff_list, x_list, mod_list, gains, w_gate, w_up, w_down, layer):
    caps = [CAPACITY_FACTOR * h.shape[1] // N_EXPERTS for h in h_list]
    ranks = [_topc(a, cap) for a, cap in zip(afft_list, caps)]
    xs = [_gather(h, r[0], cap) for h, r, cap in zip(h_list, ranks, caps)]
    ys = _ffn(xs[0], xs[1] if len(xs) > 1 else None, w_gate, w_up, w_down, layer)
    return [_combine(y, r[1], aff, x, mod, gains, cap)
            for y, r, aff, x, mod, cap in zip(ys, ranks, aff_list, x_list, mod_list, caps)]


def kernel(x, c, ctx, c_ctx, w_mod, b_mod, norm_gains, w_in, w_out, gla_a_up, gla_a_bias, gla_norm,
           nat_rpb, swa_sink, w_router, w_gate, w_up, w_down):
    bsz, t, d = x.shape
    m = ctx.shape[1]
    depth = w_mod.shape[0]

    cc = jnp.concatenate([c, c_ctx[None], jnp.zeros((16 - bsz - 1, d), F32)], axis=0)
    mod_all = _modulation(cc, w_mod, b_mod)
    rope_tabs = _rope_tables(t)
    zero_state = jnp.zeros((bsz, GLA_VW, GLA_KW), F32)

    xc = ctx
    for l in range(depth):
        update_ctx = l < depth - 1
        mod = mod_all[l, :bsz].reshape(bsz, 6, d)
        mod_c = jnp.broadcast_to(mod_all[l, bsz].reshape(1, 6, d), (bsz, 6, d))
        gains = norm_gains[l]
        w_in_l = _relayout_w_in(w_in[l])
        aup = jnp.zeros((LANES, 2 * GLA_KW), F32)
        aup = aup.at[0:GLA_RANK, 0:GLA_KW].set(gla_a_up[l, 0]).at[GLA_RANK:2 * GLA_RANK, GLA_KW:].set(gla_a_up[l, 1])
        abias = gla_a_bias[l].reshape(1, 2 * GLA_KW)
        gla_gain = jnp.tile(gla_norm[l], GLA_HEADS).reshape(1, GLA_VW)
        sink_tab = jnp.broadcast_to(swa_sink[l][:, None], (SWA_Q_HEADS, LANES))
        w_out_l = w_out[l].astype(BF16)
        wr = jnp.pad(w_router[l], ((0, 0), (0, LANES - N_EXPERTS)))
        wr_hi = wr.astype(BF16)
        wr_lo = (wr - wr_hi.astype(F32)).astype(BF16)

        gla_p, nat_p, swa_p = _in_proj(x, mod, gains, w_in_l, aup, abias, rope_tabs, 512)
        gla_c, nat_c, swa_c = _in_proj(xc, mod_c, gains, w_in_l, aup, abias, None, m)

        gla_co, s_f, s_b = _gla(gla_c, zero_state, zero_state, gla_gain)
        gla_o, _, _ = _gla(gla_p, s_f, s_b, gla_gain)
        nat_o = _nat(nat_p, nat_c, _nat_bias_table(nat_rpb[l], m))
        swa_o = _swa(swa_p, swa_c, sink_tab)

        x1, h2, aff, aff_t = _out_proj(gla_o, nat_o, swa_o, x, mod, gains, w_out_l, wr_hi, wr_lo, 512)
        if update_ctx:
            nat_co, swa_co = _ctx_attn(nat_c, swa_c, sink_tab)
            xc1, hc2, aff_c, aff_ct = _out_proj(gla_co, nat_co, swa_co, xc, mod_c, gains, w_out_l, wr_hi, wr_lo, m)
            x, xc = _moe([h2, hc2], [aff_t, aff_ct], [aff, aff_c], [x1, xc1], [mod, mod_c], gains,
                         w_gate, w_up, w_down, l)
        else:
            (x,) = _moe([h2], [aff_t], [aff], [x1], [mod], gains, w_gate, w_up, w_down, l)
    return x
```

```python
import functools

import jax
import jax.numpy as jnp
import numpy as np
from jax import lax
from jax.experimental import pallas as pl
from jax.experimental.pallas import tpu as pltpu

F32 = jnp.float32
BF16 = jnp.bfloat16

D_MODEL = 1024
GRID_W = 64
HEAD_DIM = 64
GLA_HEADS = 4
GLA_DK = 32
GLA_DV = 64
GLA_RANK = 16
GLA_TAU = 16.0
GLA_CHUNK = 64
NAT_HEADS = 4
NAT_WIN_ROWS = 8
NAT_WIN_COLS = 16
SWA_Q_HEADS = 8
SWA_KV_HEADS = 2
SWA_WINDOW = 128
SWA_BLOCK = 128
ROPE_BASE = 10000.0
N_EXPERTS = 16
EXPERT_FF = 1024
CAPACITY_FACTOR = 2
NORM_EPS = 1e-6

GLA_KW = GLA_HEADS * GLA_DK
GLA_VW = GLA_HEADS * GLA_DV
NAT_W = NAT_HEADS * HEAD_DIM
SWA_QW = SWA_Q_HEADS * HEAD_DIM
SWA_KW = SWA_KV_HEADS * HEAD_DIM
LANES = 128
NEG_BIG = -1e30
LOG2E = 1.4426950408889634

GLA_PROJ_W = 2 * GLA_KW + 2 * GLA_VW + LANES
GLA_OUT_W = 2 * GLA_KW + 2 * GLA_VW + 2 * GLA_KW
NAT_PROJ_W = 3 * NAT_W
SWA_PROJ_W = SWA_QW + 4 * SWA_KW
IN_PROJ_W = GLA_PROJ_W + NAT_PROJ_W + SWA_PROJ_W

VMEM_LIMIT = 56 * 1024 * 1024


def _cparams(*sem):
    return pltpu.CompilerParams(dimension_semantics=sem, vmem_limit_bytes=VMEM_LIMIT)


def _dot(a, b):
    return jnp.dot(a, b, preferred_element_type=F32)


def _dot_nt(a, b):
    return lax.dot_general(a, b, (((1,), (1,)), ((), ())), preferred_element_type=F32)


def _dot_tn(a, b):
    return lax.dot_general(a, b, (((0,), (0,)), ((), ())), preferred_element_type=F32)


def _split2(a):
    hi = a.astype(BF16)
    lo = (a - hi.astype(F32)).astype(BF16)
    return hi, lo


def _split3(a):
    hi = a.astype(BF16)
    r = a - hi.astype(F32)
    mid = r.astype(BF16)
    lo = (r - mid.astype(F32)).astype(BF16)
    return hi, mid, lo


def _dot_f32(a, b):
    ah, al = _split2(a)
    bh, bl = _split2(b)
    return _dot(ah, bh) + _dot(al, bh) + _dot(ah, bl)


def _rms(x, gain):
    return x * lax.rsqrt(jnp.mean(x * x, axis=-1, keepdims=True) + NORM_EPS) * gain


def _silu(x):
    return x * jax.nn.sigmoid(x)


def _mod_kernel(c_ref, w_ref, b_ref, o_ref):
    a = _silu(c_ref[...])
    o_ref[0] = _dot_f32(a, w_ref[0]) + b_ref[0]


def _modulation(cc, w_mod, b_mod):
    depth, d, n = w_mod.shape
    r = cc.shape[0]
    tn = 1536
    return pl.pallas_call(
        _mod_kernel,
        grid=(depth, n // tn),
        in_specs=[
            pl.BlockSpec((r, d), lambda l, j: (0, 0)),
            pl.BlockSpec((1, d, tn), lambda l, j: (l, 0, j)),
            pl.BlockSpec((1, 1, tn), lambda l, j: (l, 0, j)),
        ],
        out_specs=pl.BlockSpec((1, r, tn), lambda l, j: (l, 0, j)),
        out_shape=jax.ShapeDtypeStruct((depth, r, n), F32),
        compiler_params=_cparams("parallel", "parallel"),
        name="adaln_mod",
    )(cc, w_mod, b_mod.reshape(depth, 1, n))


def _rope_rotate(x, first_half):
    up = pltpu.roll(x, LANES - 16, 1)
    down = pltpu.roll(x, 16, 1)
    return jnp.where(first_half, up, down)


def _in_proj_kernel(*refs, rope):
    if rope:
        x_ref, mod_ref, gain_ref, w_ref, aup_ref, ab_ref, tri_ref, cos_ref, sin_ref, gla_ref, nat_ref, swa_ref = refs
    else:
        x_ref, mod_ref, gain_ref, w_ref, aup_ref, ab_ref, tri_ref, gla_ref, nat_ref, swa_ref = refs
    x = x_ref[0]
    y = _rms(x, gain_ref[0:1, :])
    h = (y * (1.0 + mod_ref[0, 1:2, :]) + mod_ref[0, 0:1, :]).astype(BF16)

    kq = 2 * GLA_KW + 2 * GLA_VW
    pg = _dot(h, w_ref[:, 0:GLA_PROJ_W])
    gla_ref[0, :, 0:kq] = pg[:, 0:kq]
    z = _dot_f32(pg[:, kq:kq + LANES], aup_ref[...]) + ab_ref[...]
    log_a = (jnp.minimum(z, 0.0) - jnp.log1p(jnp.exp(-jnp.abs(z)))) * (1.0 / GLA_TAU)
    hi, lo = _split2(log_a)
    tri = tri_ref[...]
    cum = _dot(tri, hi) + _dot(tri, lo)
    gla_ref[0, :, kq:kq + GLA_KW] = cum[:, 0:GLA_KW]
    tm = x.shape[0]
    nck = tm // GLA_CHUNK
    cb = cum[:, GLA_KW:].reshape(nck, GLA_CHUNK, GLA_KW)
    lb = log_a[:, GLA_KW:].reshape(nck, GLA_CHUNK, GLA_KW)
    from_end = cb[:, GLA_CHUNK - 1:GLA_CHUNK, :] - cb + lb
    gla_ref[0, :, kq + GLA_KW:kq + 2 * GLA_KW] = from_end.reshape(tm, GLA_KW)

    o = GLA_PROJ_W
    pn = _dot(h, w_ref[:, o:o + NAT_PROJ_W])
    scale = HEAD_DIM ** -0.5 * LOG2E
    nat_ref[0, :, 0:NAT_W] = (pn[:, 0:NAT_W] * scale).astype(BF16)
    nat_ref[0, :, NAT_W:] = pn[:, NAT_W:].astype(BF16)

    o = GLA_PROJ_W + NAT_PROJ_W
    ps = _dot(h, w_ref[:, o:o + SWA_PROJ_W])
    n_rot = (SWA_QW + 2 * SWA_KW) // LANES
    if rope:
        cos = cos_ref[...]
        sin = sin_ref[...]
        lane = lax.broadcasted_iota(jnp.int32, cos.shape, 1)
        first_half = (lane % 32) < 16
    for j in range(SWA_PROJ_W // LANES):
        t = ps[:, j * LANES:(j + 1) * LANES]
        if rope and j < n_rot:
            t = t * cos + _rope_rotate(t, first_half) * sin
        if j < SWA_QW // LANES:
            t = t * scale
        swa_ref[0, :, j * LANES:(j + 1) * LANES] = t.astype(BF16)


def _in_proj(x, mod, gains, w, aup, abias, rope_tabs, tm):
    bsz, t, d = x.shape
    rope = rope_tabs is not None
    in_specs = [
        pl.BlockSpec((1, tm, d), lambda b, i: (b, i, 0)),
        pl.BlockSpec((1, 6, d), lambda b, i: (b, 0, 0)),
        pl.BlockSpec((4, d), lambda b, i: (0, 0)),
        pl.BlockSpec((d, IN_PROJ_W), lambda b, i: (0, 0)),
        pl.BlockSpec((LANES, 2 * GLA_KW), lambda b, i: (0, 0)),
        pl.BlockSpec((1, 2 * GLA_KW), lambda b, i: (0, 0)),
        pl.BlockSpec((tm, tm), lambda b, i: (0, 0)),
    ]
    r = np.arange(tm)
    tri = (r[:, None] // GLA_CHUNK == r[None, :] // GLA_CHUNK) & (r[:, None] >= r[None, :])
    args = [x, mod, gains, w, aup, abias, jnp.asarray(tri, BF16)]
    if rope:
        in_specs += [pl.BlockSpec((tm, LANES), lambda b, i: (i, 0))] * 2
        args += list(rope_tabs)
    return pl.pallas_call(
        functools.partial(_in_proj_kernel, rope=rope),
        grid=(bsz, t // tm),
        in_specs=in_specs,
        out_specs=[
            pl.BlockSpec((1, tm, GLA_OUT_W), lambda b, i: (b, i, 0)),
            pl.BlockSpec((1, tm, NAT_PROJ_W), lambda b, i: (b, i, 0)),
            pl.BlockSpec((1, tm, SWA_PROJ_W), lambda b, i: (b, i, 0)),
        ],
        out_shape=[
            jax.ShapeDtypeStruct((bsz, t, GLA_OUT_W), F32),
            jax.ShapeDtypeStruct((bsz, t, NAT_PROJ_W), BF16),
            jax.ShapeDtypeStruct((bsz, t, SWA_PROJ_W), BF16),
        ],
        compiler_params=_cparams("parallel", "parallel"),
        name="in_proj_rope" if rope else "in_proj",
    )(*args)


def _gla_kernel(p_ref, s0f_ref, s0b_ref, gain_ref, o_ref, sff_ref, sfb_ref, acc_ref, sf_ref, sb_ref):
    t = p_ref.shape[1]
    c = GLA_CHUNK
    nc = t // c
    qi = lax.broadcasted_iota(jnp.int32, (c, GLA_HEADS * c), 0)
    kj = lax.broadcasted_iota(jnp.int32, (c, GLA_HEADS * c), 1) % c
    lower = qi >= kj
    upper = qi <= kj
    srow = lax.broadcasted_iota(jnp.int32, (GLA_VW, GLA_KW), 0) // GLA_DV
    scol = lax.broadcasted_iota(jnp.int32, (GLA_VW, GLA_KW), 1) // GLA_DK
    head_diag = srow == scol
    q_scale = GLA_DK ** -0.5

    sf_ref[...] = s0f_ref[0]
    sb_ref[...] = s0b_ref[0]
    acc_ref[...] = jnp.zeros_like(acc_ref)

    def one_direction(base, keep, end_row, cum_off, st_ref):
        rows = pl.ds(base, c)
        k = p_ref[0, rows, 0:GLA_KW]
        q = p_ref[0, rows, GLA_KW:2 * GLA_KW] * q_scale
        vb = p_ref[0, rows, 2 * GLA_KW:2 * GLA_KW + GLA_VW].astype(BF16)
        cum = p_ref[0, rows, cum_off:cum_off + GLA_KW]
        mid = cum[c // 2:c // 2 + 1, :]
        total = cum[end_row:end_row + 1, :]
        st = st_ref[...]
        inter = _dot_nt((q * jnp.exp(cum)).astype(BF16), st.astype(BF16))
        qs = (q * jnp.exp(cum - mid)).astype(BF16)
        ks = (k * jnp.exp(mid - cum)).astype(BF16)
        s = _dot_nt(qs, _head_stack(ks, GLA_DK, GLA_HEADS))
        s = jnp.where(keep, s, 0.0).astype(BF16)
        intra = _dot(s, _head_stack(vb, GLA_DV, GLA_HEADS))
        acc_ref[rows, :] += inter + intra
        kd = (k * jnp.exp(total - cum)).astype(BF16)
        upd = _dot_tn(vb, kd)
        st_ref[...] = st * jnp.exp(total) + jnp.where(head_diag, upd, 0.0)

    def body(i, carry):
        one_direction(pl.multiple_of(i * c, c), lower, c - 1, 2 * GLA_KW + 2 * GLA_VW, sf_ref)
        one_direction(pl.multiple_of((nc - 1 - i) * c, c), upper, 0, 3 * GLA_KW + 2 * GLA_VW, sb_ref)
        return carry

    lax.fori_loop(0, nc, body, 0)
    sff_ref[0] = sf_ref[...]
    sfb_ref[0] = sb_ref[...]

    hr = lax.broadcasted_iota(jnp.int32, (GLA_VW, GLA_VW), 0) // GLA_DV
    hc = lax.broadcasted_iota(jnp.int32, (GLA_VW, GLA_VW), 1) // GLA_DV
    head_ones = (hr == hc).astype(BF16)
    blk = 256
    for j in range(t // blk):
        rows = pl.ds(j * blk, blk)
        o = acc_ref[rows, :]
        hi, mid, lo = _split3(o * o)
        ms = (_dot(hi, head_ones) + _dot(mid, head_ones) + _dot(lo, head_ones)) * (1.0 / GLA_DV)
        gate = p_ref[0, rows, 2 * GLA_KW + GLA_VW:2 * GLA_KW + 2 * GLA_VW]
        o_ref[0, rows, :] = (o * lax.rsqrt(ms + NORM_EPS) * gain_ref[...] * _silu(gate)).astype(BF16)


def _gla(p, s0f, s0b, gain):
    bsz, t, w = p.shape
    st_spec = pl.BlockSpec((1, GLA_VW, GLA_KW), lambda b: (b, 0, 0))
    st_shape = jax.ShapeDtypeStruct((bsz, GLA_VW, GLA_KW), F32)
    return pl.pallas_call(
        _gla_kernel,
        grid=(bsz,),
        in_specs=[
            pl.BlockSpec((1, t, w), lambda b: (b, 0, 0)),
            st_spec, st_spec,
            pl.BlockSpec((1, GLA_VW), lambda b: (0, 0)),
        ],
        out_specs=[pl.BlockSpec((1, t, GLA_VW), lambda b: (b, 0, 0)), st_spec, st_spec],
        out_shape=[jax.ShapeDtypeStruct((bsz, t, GLA_VW), BF16), st_shape, st_shape],
        scratch_shapes=[
            pltpu.VMEM((t, GLA_VW), F32),
            pltpu.VMEM((GLA_VW, GLA_KW), F32),
            pltpu.VMEM((GLA_VW, GLA_KW), F32),
        ],
        compiler_params=_cparams("parallel"),
        name="gla_bidir",
    )(p, s0f, s0b, gain)


def _head_stack(q, width, n):
    lane = lax.broadcasted_iota(jnp.int32, (1, q.shape[1]), 1) // width
    return jnp.concatenate([jnp.where(lane == h, q, jnp.zeros_like(q)) for h in range(n)], axis=0)


def _head_unstack(o, width, n):
    rows = o.shape[0] // n
    lane = lax.broadcasted_iota(jnp.int32, (1, o.shape[1]), 1) // width
    out = o[0:rows]
    for h in range(1, n):
        out = jnp.where(lane == h, o[h * rows:(h + 1) * rows], out)
    return out


NAT_ROWS_PER_STEP = 4


def _nat_kernel(lat_ref, ctx_ref, bias_ref, o_ref):
    n_rows = lat_ref.shape[1] // GRID_W
    kc = ctx_ref[0, :, NAT_W:2 * NAT_W]
    vc = ctx_ref[0, :, 2 * NAT_W:3 * NAT_W]
    n_loc = NAT_WIN_ROWS * GRID_W
    for i in range(NAT_ROWS_PER_STEP):
        r = pl.program_id(1) * NAT_ROWS_PER_STEP + i
        r0 = jnp.clip(r - NAT_WIN_ROWS // 2, 0, n_rows - NAT_WIN_ROWS)
        q = lat_ref[0, pl.ds(pl.multiple_of(r * GRID_W, GRID_W), GRID_W), 0:NAT_W]
        krows = pl.ds(pl.multiple_of(r0 * GRID_W, GRID_W), n_loc)
        qs = _head_stack(q, HEAD_DIM, NAT_HEADS)
        s = jnp.concatenate([_dot_nt(qs, lat_ref[0, krows, NAT_W:2 * NAT_W]), _dot_nt(qs, kc)], axis=1)
        s = s + bias_ref[r0 - r + NAT_WIN_ROWS - 1]
        p = jnp.exp2((s - jnp.max(s, axis=-1, keepdims=True)).astype(BF16))
        den = jnp.sum(p, axis=-1, keepdims=True, dtype=F32)
        o = (_dot(p[:, 0:n_loc], lat_ref[0, krows, 2 * NAT_W:3 * NAT_W]) + _dot(p[:, n_loc:], vc)) / den
        o_ref[0, i * GRID_W:(i + 1) * GRID_W, :] = _head_unstack(o, HEAD_DIM, NAT_HEADS).astype(BF16)


def _nat(lat, ctx, bias):
    bsz, t, w = lat.shape
    m = ctx.shape[1]
    rows = NAT_ROWS_PER_STEP * GRID_W
    return pl.pallas_call(
        _nat_kernel,
        grid=(bsz, t // rows),
        in_specs=[
            pl.BlockSpec((1, t, w), lambda b, r: (b, 0, 0)),
            pl.BlockSpec((1, m, w), lambda b, r: (b, 0, 0)),
            pl.BlockSpec(bias.shape, lambda b, r: (0, 0, 0)),
        ],
        out_specs=pl.BlockSpec((1, rows, NAT_W), lambda b, r: (b, r, 0)),
        out_shape=jax.ShapeDtypeStruct((bsz, t, NAT_W), BF16),
        compiler_params=_cparams("parallel", "arbitrary"),
        name="nat_attn",
    )(lat, ctx, bias)


def _nat_bias_table(rpb, m):
    cols = np.arange(GRID_W)
    col_start = np.clip(cols - NAT_WIN_COLS // 2, 0, GRID_W - NAT_WIN_COLS)
    kc = np.arange(GRID_W)
    inside = (kc[None, :] >= col_start[:, None]) & (kc[None, :] < col_start[:, None] + NAT_WIN_COLS)
    rel = kc[None, :] - cols[:, None] + NAT_WIN_COLS - 1
    col_sel = (rel[:, :, None] == np.arange(2 * NAT_WIN_COLS - 1)) & inside[:, :, None]
    ridx = np.arange(NAT_WIN_ROWS)[:, None] + np.arange(NAT_WIN_ROWS)[None, :]
    row_sel = ridx[:, :, None] == np.arange(2 * NAT_WIN_ROWS - 1)
    tab = jnp.einsum("hab,oia,ckb->ohcik", rpb.astype(F32), jnp.asarray(row_sel, F32), jnp.asarray(col_sel, F32),
                     precision=lax.Precision.HIGHEST)
    tab = jnp.where(jnp.asarray(inside)[None, None, :, None, :], tab * LOG2E, NEG_BIG)
    tab = tab.reshape(NAT_WIN_ROWS, NAT_HEADS * GRID_W, NAT_WIN_ROWS * GRID_W)
    return jnp.concatenate([tab, jnp.zeros(tab.shape[:2] + (m,), F32)], axis=2)


def _sink_column(sink_ref, first_head, n_heads, rows):
    return jnp.concatenate(
        [jnp.broadcast_to(sink_ref[first_head + j:first_head + j + 1, 0:1], (rows, 1)) for j in range(n_heads)], axis=0)


def _softmax_pv_sink(s, v, sink):
    m = jnp.maximum(jnp.max(s, axis=-1, keepdims=True), sink)
    p = jnp.exp2((s - m).astype(BF16))
    o = _dot(p, jnp.concatenate([v, jnp.ones_like(v)], axis=1))
    w = v.shape[1]
    return o[:, 0:w] / (o[:, w:] + jnp.exp2(sink - m))


def _swa_kernel(q_ref, kp_ref, kn_ref, kx_ref, vp_ref, vn_ref, vx_ref, kc_ref, vc_ref, sink_ref, o_ref):
    n = pl.program_id(1)
    nb = pl.num_programs(1)
    blk = SWA_BLOCK
    m = kc_ref.shape[1]
    qi = lax.broadcasted_iota(jnp.int32, (blk, blk), 0)
    kj = lax.broadcasted_iota(jnp.int32, (blk, blk), 1)
    mask_prev = jnp.where((kj >= qi) & (n > 0), 0.0, NEG_BIG)[None]
    mask_next = jnp.where((kj <= qi) & (n < nb - 1), 0.0, NEG_BIG)[None]
    group = SWA_Q_HEADS // SWA_KV_HEADS
    for g in range(SWA_KV_HEADS):
        gl = pl.ds(g * LANES, LANES)
        k = jnp.concatenate([kp_ref[0, :, gl], kn_ref[0, :, gl], kx_ref[0, :, gl], kc_ref[0, :, gl]], axis=0)
        v = jnp.concatenate([vp_ref[0, :, gl], vn_ref[0, :, gl], vx_ref[0, :, gl], vc_ref[0, :, gl]], axis=0)
        qs = jnp.concatenate(
            [_head_stack(q_ref[0, :, pl.ds((g * 2 + p) * LANES, LANES)], HEAD_DIM, 2) for p in range(2)], axis=0)
        s = _dot_nt(qs, k).reshape(group, blk, 3 * blk + m)
        s = jnp.concatenate([s[:, :, 0:blk] + mask_prev, s[:, :, blk:2 * blk], s[:, :, 2 * blk:3 * blk] + mask_next,
                             s[:, :, 3 * blk:]], axis=2).reshape(group * blk, 3 * blk + m)
        o = _softmax_pv_sink(s, v, _sink_column(sink_ref, g * group, group, blk))
        for p in range(2):
            pair = _head_unstack(o[2 * p * blk:(2 * p + 2) * blk], HEAD_DIM, 2)
            o_ref[0, :, pl.ds((g * 2 + p) * LANES, LANES)] = pair.astype(BF16)


def _swa(lat, ctx, sink_tab):
    bsz, t, w = lat.shape
    m = ctx.shape[1]
    nb = t // SWA_BLOCK
    qb = SWA_QW // (2 * SWA_KW)
    kw = 2 * SWA_KW

    def spec(which, lane_blk):
        if which == "prev":
            return pl.BlockSpec((1, SWA_BLOCK, kw), lambda b, n: (b, jnp.maximum(n - 1, 0), lane_blk))
        if which == "next":
            return pl.BlockSpec((1, SWA_BLOCK, kw), lambda b, n: (b, jnp.minimum(n + 1, nb - 1), lane_blk))
        return pl.BlockSpec((1, SWA_BLOCK, kw), lambda b, n: (b, n, lane_blk))

    return pl.pallas_call(
        _swa_kernel,
        grid=(bsz, nb),
        in_specs=[
            pl.BlockSpec((1, SWA_BLOCK, SWA_QW), lambda b, n: (b, n, 0)),
            spec("prev", qb), spec("cur", qb), spec("next", qb),
            spec("prev", qb + 1), spec("cur", qb + 1), spec("next", qb + 1),
            pl.BlockSpec((1, m, kw), lambda b, n: (b, 0, qb)),
            pl.BlockSpec((1, m, kw), lambda b, n: (b, 0, qb + 1)),
            pl.BlockSpec((SWA_Q_HEADS, LANES), lambda b, n: (0, 0)),
        ],
        out_specs=pl.BlockSpec((1, SWA_BLOCK, SWA_QW), lambda b, n: (b, n, 0)),
        out_shape=jax.ShapeDtypeStruct((bsz, t, SWA_QW), BF16),
        compiler_params=_cparams("parallel", "arbitrary"),
        name="swa_attn",
    )(lat, lat, lat, lat, lat, lat, lat, ctx, ctx, sink_tab)


def _ctx_attn_kernel(nat_ref, swa_ref, sink_ref, on_ref, os_ref):
    m = nat_ref.shape[1]
    q = nat_ref[0, :, 0:NAT_W]
    k = nat_ref[0, :, NAT_W:2 * NAT_W]
    v = nat_ref[0, :, 2 * NAT_W:3 * NAT_W]
    s = _dot_nt(_head_stack(q, HEAD_DIM, NAT_HEADS), k)
    p = jnp.exp2((s - jnp.max(s, axis=-1, keepdims=True)).astype(BF16))
    o = _dot(p, v) / jnp.sum(p, axis=-1, keepdims=True, dtype=F32)
    on_ref[0] = _head_unstack(o, HEAD_DIM, NAT_HEADS).astype(BF16)

    group = SWA_Q_HEADS // SWA_KV_HEADS
    for g in range(SWA_KV_HEADS):
        k = swa_ref[0, :, pl.ds(SWA_QW + g * LANES, LANES)]
        v = swa_ref[0, :, pl.ds(SWA_QW + 2 * SWA_KW + g * LANES, LANES)]
        qs = jnp.concatenate(
            [_head_stack(swa_ref[0, :, pl.ds((g * 2 + p) * LANES, LANES)], HEAD_DIM, 2) for p in range(2)], axis=0)
        o = _softmax_pv_sink(_dot_nt(qs, k), v, _sink_column(sink_ref, g * group, group, m))
        for pr in range(2):
            pair = _head_unstack(o[2 * pr * m:(2 * pr + 2) * m], HEAD_DIM, 2)
            os_ref[0, :, pl.ds((g * 2 + pr) * LANES, LANES)] = pair.astype(BF16)


def _ctx_attn(nat_c, swa_c, sink_tab):
    bsz, m, _ = nat_c.shape
    return pl.pallas_call(
        _ctx_attn_kernel,
        grid=(bsz,),
        in_specs=[
            pl.BlockSpec((1, m, NAT_PROJ_W), lambda b: (b, 0, 0)),
            pl.BlockSpec((1, m, SWA_PROJ_W), lambda b: (b, 0, 0)),
            pl.BlockSpec((SWA_Q_HEADS, LANES), lambda b: (0, 0)),
        ],
        out_specs=[
            pl.BlockSpec((1, m, NAT_W), lambda b: (b, 0, 0)),
            pl.BlockSpec((1, m, SWA_QW), lambda b: (b, 0, 0)),
        ],
        out_shape=[
            jax.ShapeDtypeStruct((bsz, m, NAT_W), BF16),
            jax.ShapeDtypeStruct((bsz, m, SWA_QW), BF16),
        ],
        compiler_params=_cparams("parallel"),
        name="ctx_attn",
    )(nat_c, swa_c, sink_tab)


def _out_proj_kernel(gla_ref, nat_ref, swa_ref, x_ref, mod_ref, gain_ref, w_ref, wr_ref,
                     x1_ref, h2_ref, aff_ref, afft_ref):
    tm = x_ref.shape[1]
    half = tm // 2
    for part in range(2):
        rows = pl.ds(part * half, half)
        mix = (_dot(gla_ref[0, rows, :], w_ref[0:GLA_VW, :])
               + _dot(nat_ref[0, rows, :], w_ref[GLA_VW:GLA_VW + NAT_W, :])
               + _dot(swa_ref[0, rows, :], w_ref[GLA_VW + NAT_W:, :]))
        x1 = x_ref[0, rows, :] + mod_ref[0, 2:3, :] * _rms(mix, gain_ref[1:2, :])
        x1_ref[0, rows, :] = x1
        h2 = (_rms(x1, gain_ref[2:3, :]) * (1.0 + mod_ref[0, 4:5, :]) + mod_ref[0, 3:4, :]).astype(BF16)
        h2_ref[0, rows, :] = h2
        logits = _dot(h2, wr_ref[...])
        lane = lax.broadcasted_iota(jnp.int32, logits.shape, 1)
        logits = jnp.where(lane < N_EXPERTS, logits, NEG_BIG)
        e = jnp.exp(logits - jnp.max(logits, axis=-1, keepdims=True))
        aff = e / jnp.sum(e, axis=-1, keepdims=True)
        aff_ref[0, rows, :] = aff
        afft_ref[0, :, rows] = aff.T[0:N_EXPERTS, :]


def _out_proj(gla_o, nat_o, swa_o, x, mod, gains, w_out, wr, tm):
    bsz, t, d = x.shape
    tok = lambda b, i: (b, i, 0)
    const = lambda b, i: (0, 0)
    return pl.pallas_call(
        _out_proj_kernel,
        grid=(bsz, t // tm),
        in_specs=[
            pl.BlockSpec((1, tm, GLA_VW), tok),
            pl.BlockSpec((1, tm, NAT_W), tok),
            pl.BlockSpec((1, tm, SWA_QW), tok),
            pl.BlockSpec((1, tm, d), tok),
            pl.BlockSpec((1, 6, d), lambda b, i: (b, 0, 0)),
            pl.BlockSpec((4, d), const),
            pl.BlockSpec((d, d), const),
            pl.BlockSpec((d, LANES), const),
        ],
        out_specs=[
            pl.BlockSpec((1, tm, d), tok),
            pl.BlockSpec((1, tm, d), tok),
            pl.BlockSpec((1, tm, LANES), tok),
            pl.BlockSpec((1, N_EXPERTS, tm), lambda b, i: (b, 0, i)),
        ],
        out_shape=[
            jax.ShapeDtypeStruct((bsz, t, d), F32),
            jax.ShapeDtypeStruct((bsz, t, d), BF16),
            jax.ShapeDtypeStruct((bsz, t, LANES), F32),
            jax.ShapeDtypeStruct((bsz, N_EXPERTS, t), F32),
        ],
        compiler_params=_cparams("parallel", "parallel"),
        name="out_proj_router",
    )(gla_o, nat_o, swa_o, x, mod, gains, w_out, wr)


def _excl_prefix(x):
    rows, n = x.shape
    nblk = n // LANES
    r = lax.broadcasted_iota(jnp.int32, (LANES, LANES), 0)
    c = lax.broadcasted_iota(jnp.int32, (LANES, LANES), 1)
    strict_upper = (r < c).astype(BF16)
    stacked = jnp.concatenate([x[:, j * LANES:(j + 1) * LANES] for j in range(nblk)], axis=0).astype(BF16)
    local = _dot(stacked, strict_upper)
    totals = jnp.sum(stacked.astype(F32), axis=-1, keepdims=True)
    out = []
    off = jnp.zeros((rows, 1), F32)
    for j in range(nblk):
        out.append(local[j * rows:(j + 1) * rows] + off)
        off = off + totals[j * rows:(j + 1) * rows]
    return jnp.concatenate(out, axis=1)


def _topc_kernel(aff_ref, rank_ref, rankcol_ref, *, cap):
    aff = aff_ref[0]
    bits = pltpu.bitcast(aff, jnp.int32)

    def step(i, thr):
        cand = thr | (jnp.int32(1) << (30 - i))
        cnt = jnp.sum((bits >= cand).astype(F32), axis=-1, keepdims=True)
        return jnp.where(cnt >= cap, cand, thr)

    thr = lax.fori_loop(0, 31, step, jnp.zeros((aff.shape[0], 1), jnp.int32))
    gt = bits > thr
    eq = bits == thr
    need = cap - jnp.sum(gt.astype(F32), axis=-1, keepdims=True)
    eq_before = _excl_prefix(eq.astype(F32))
    sel = gt | (eq & (eq_before < need))
    rank = jnp.where(sel, _excl_prefix(sel.astype(F32)), -1.0)
    rank_ref[0] = rank
    padded = jnp.concatenate([rank, jnp.full((LANES - rank.shape[0], rank.shape[1]), -1.0, F32)], axis=0)
    rankcol_ref[0] = padded.T


def _topc(aff_t, cap):
    bsz, e, n = aff_t.shape
    return pl.pallas_call(
        functools.partial(_topc_kernel, cap=cap),
        grid=(bsz,),
        in_specs=[pl.BlockSpec((1, e, n), lambda b: (b, 0, 0))],
        out_specs=[pl.BlockSpec((1, e, n), lambda b: (b, 0, 0)), pl.BlockSpec((1, n, LANES), lambda b: (b, 0, 0))],
        out_shape=[jax.ShapeDtypeStruct((bsz, e, n), F32), jax.ShapeDtypeStruct((bsz, n, LANES), F32)],
        compiler_params=_cparams("parallel"),
        name="expert_topc",
    )(aff_t)


def _gather_kernel(h_ref, rank_ref, xs_ref, *, cap):
    n = h_ref.shape[1]
    slot = lax.broadcasted_iota(jnp.int32, (cap, n), 0).astype(F32)
    h = h_ref[0]
    for e in range(N_EXPERTS):
        onehot = (slot == rank_ref[0, e:e + 1, :]).astype(BF16)
        xs_ref[e] = _dot(onehot, h).astype(BF16)


def _gather(h, rank, cap):
    bsz, n, d = h.shape
    return pl.pallas_call(
        functools.partial(_gather_kernel, cap=cap),
        grid=(bsz,),
        in_specs=[
            pl.BlockSpec((1, n, d), lambda b: (b, 0, 0)),
            pl.BlockSpec((1, N_EXPERTS, n), lambda b: (b, 0, 0)),
        ],
        out_specs=pl.BlockSpec((N_EXPERTS, cap, d), lambda b: (0, b, 0)),
        out_shape=jax.ShapeDtypeStruct((N_EXPERTS, bsz * cap, d), BF16),
        compiler_params=_cparams("parallel"),
        name="moe_gather",
    )(h, rank)


FFN_PARTS = 4


def _ffn_kernel(*refs, has_ctx):
    if has_ctx:
        xs_ref, xc_ref, wg_ref, wu_ref, wd_ref, y_ref, yc_ref, wg_bf, wu_bf, wd_bf = refs
    else:
        xs_ref, wg_ref, wu_ref, wd_ref, y_ref, wg_bf, wu_bf, wd_bf = refs
    s = pl.program_id(0)
    j = pl.program_id(1)
    n_exp = pl.num_programs(0) - 1
    piece = wg_ref.shape[2]

    @pl.when((s < n_exp) & (j < FFN_PARTS))
    def _():
        rows = pl.ds(pl.multiple_of(j * piece, piece), piece)
        wg_bf[s % 2, rows, :] = wg_ref[0, 0].astype(BF16)
        wu_bf[s % 2, rows, :] = wu_ref[0, 0].astype(BF16)
        wd_bf[s % 2, rows, :] = wd_ref[0, 0].astype(BF16)

    def swiglu(x_ref, o_ref):
        cur = (s + 1) % 2
        xs = x_ref[0]
        hid = (_silu(_dot(xs, wg_bf[cur])) * _dot(xs, wu_bf[cur])).astype(BF16)
        o_ref[0] = _dot(hid, wd_bf[cur]).astype(BF16)

    @pl.when((s > 0) & (j < FFN_PARTS))
    def _():
        swiglu(xs_ref, y_ref)

    if has_ctx:
        @pl.when((s > 0) & (j == FFN_PARTS))
        def _():
            swiglu(xc_ref, yc_ref)


def _ffn(xs, xc, w_gate, w_up, w_down, layer):
    _, e, d, ff = w_gate.shape
    rb = xs.shape[1] // FFN_PARTS
    has_ctx = xc is not None
    last = FFN_PARTS - 1

    def x_idx(s, j):
        return (jnp.maximum(s - 1, 0), jnp.where(s > 0, jnp.minimum(j, last), 0), 0)

    def w_idx(s, j):
        return (layer, jnp.minimum(s, e - 1), jnp.minimum(j, last), 0)

    xspec = pl.BlockSpec((1, rb, d), x_idx)
    cspecs, cargs, cshapes = [], [], []
    if has_ctx:
        cspecs = [pl.BlockSpec((1, xc.shape[1], d), lambda s, j: (jnp.maximum(s - 1, 0), 0, 0))]
        cargs = [xc]
        cshapes = [jax.ShapeDtypeStruct(xc.shape, BF16)]
    out = pl.pallas_call(
        functools.partial(_ffn_kernel, has_ctx=has_ctx),
        grid=(e + 1, FFN_PARTS + int(has_ctx)),
        in_specs=[xspec] + cspecs + [
            pl.BlockSpec((1, 1, d // FFN_PARTS, ff), w_idx),
            pl.BlockSpec((1, 1, d // FFN_PARTS, ff), w_idx),
            pl.BlockSpec((1, 1, ff // FFN_PARTS, d), w_idx),
        ],
        out_specs=[xspec] + cspecs,
        out_shape=[jax.ShapeDtypeStruct(xs.shape, BF16)] + cshapes,
        scratch_shapes=[pltpu.VMEM((2, d, ff), BF16), pltpu.VMEM((2, d, ff), BF16), pltpu.VMEM((2, ff, d), BF16)],
        compiler_params=_cparams("arbitrary", "arbitrary"),
        name="moe_ffn",
    )(xs, *cargs, w_gate, w_up, w_down)
    return out


def _combine_kernel(y_ref, rankcol_ref, aff_ref, x_ref, mod_ref, gain_ref, o_ref, *, cap):
    rb = x_ref.shape[1]
    slot = lax.broadcasted_iota(jnp.int32, (rb, cap), 1).astype(F32)
    rc = rankcol_ref[0]
    af = aff_ref[0]
    acc = jnp.zeros((rb, x_ref.shape[2]), F32)
    for e in range(N_EXPERTS):
        onehot = (rc[:, e:e + 1] == slot).astype(BF16)
        acc = acc + af[:, e:e + 1] * _dot(onehot, y_ref[e])
    o_ref[0] = x_ref[0] + mod_ref[0, 5:6, :] * _rms(acc, gain_ref[3:4, :])


def _combine(y, rankcol, aff, x, mod, gains, cap):
    bsz, n, d = x.shape
    rb = min(n, 512)
    tok = lambda b, i: (b, i, 0)
    return pl.pallas_call(
        functools.partial(_combine_kernel, cap=cap),
        grid=(bsz, n // rb),
        in_specs=[
            pl.BlockSpec((N_EXPERTS, cap, d), lambda b, i: (0, b, 0)),
            pl.BlockSpec((1, rb, LANES), tok),
            pl.BlockSpec((1, rb, LANES), tok),
            pl.BlockSpec((1, rb, d), tok),
            pl.BlockSpec((1, 6, d), lambda b, i: (b, 0, 0)),
            pl.BlockSpec((4, d), lambda b, i: (0, 0)),
        ],
        out_specs=pl.BlockSpec((1, rb, d), tok),
        out_shape=jax.ShapeDtypeStruct((bsz, n, d), F32),
        compiler_params=_cparams("parallel", "arbitrary"),
        name="moe_combine",
    )(y, rankcol, aff, x, mod, gains)


def _relayout_w_in(w):
    widths = (GLA_KW, GLA_VW, GLA_RANK, GLA_RANK, NAT_W, NAT_W, SWA_KW, SWA_KW, GLA_KW, GLA_VW, NAT_W, SWA_QW)
    gk, gv, gaf, gab, nk, nv, sk, sv, gq, gg, nq, sq = jnp.split(w, np.cumsum(widths)[:-1].tolist(), axis=1)
    pad = jnp.zeros((w.shape[0], LANES - 2 * GLA_RANK), w.dtype)
    sk0, sk1 = sk[:, :HEAD_DIM], sk[:, HEAD_DIM:]
    sv0, sv1 = sv[:, :HEAD_DIM], sv[:, HEAD_DIM:]
    cols = [gk, gq, gv, gg, gaf, gab, pad, nq, nk, nv, sq, sk0, sk0, sk1, sk1, sv0, sv0, sv1, sv1]
    return jnp.concatenate(cols, axis=1).astype(BF16)


def _rope_tables(t):
    half = HEAD_DIM // 4
    freqs = ROPE_BASE ** (-np.arange(half, dtype=np.float32) / half)
    pos = np.arange(t)
    ang_r = (pos // GRID_W).astype(np.float32)[:, None] * freqs
    ang_c = (pos % GRID_W).astype(np.float32)[:, None] * freqs
    cos = np.concatenate([np.cos(ang_r), np.cos(ang_r), np.cos(ang_c), np.cos(ang_c)], axis=1)
    sin = np.concatenate([-np.sin(ang_r), np.sin(ang_r), -np.sin(ang_c), np.sin(ang_c)], axis=1)
    reps = LANES // HEAD_DIM
    return jnp.asarray(np.tile(cos, (1, reps)), F32), jnp.asarray(np.tile(sin, (1, reps)), F32)


def _moe(h_list, afft_list, aff_list, x_list, mod_list, gains, w_gate, w_up, w_down, layer):
    caps = [CAPACITY_FACTOR * h.shape[1] // N_EXPERTS for h in h_list]
    ranks = [_topc(a, cap) for a, cap in zip(afft_list, caps)]
    xs = [_gather(h, r[0], cap) for h, r, cap in zip(h_list, ranks, caps)]
    ys = _ffn(xs[0], xs[1] if len(xs) > 1 else None, w_gate, w_up, w_down, layer)
    return [_combine(y, r[1], aff, x, mod, gains, cap)
            for y, r, aff, x, mod, cap in zip(ys, ranks, aff_list, x_list, mod_list, caps)]


def kernel(x, c, ctx, c_ctx, w_mod, b_mod, norm_gains, w_in, w_out, gla_a_up, gla_a_bias, gla_norm,
           nat_rpb, swa_sink, w_router, w_gate, w_up, w_down):
    bsz, t, d = x.shape
    m = ctx.shape[1]
    depth = w_mod.shape[0]

    cc = jnp.concatenate([c, c_ctx[None], jnp.zeros((16 - bsz - 1, d), F32)], axis=0)
    mod_all = _modulation(cc, w_mod, b_mod)
    rope_tabs = _rope_tables(t)
    zero_state = jnp.zeros((bsz, GLA_VW, GLA_KW), F32)

    xc = ctx
    for l in range(depth):
        update_ctx = l < depth - 1
        mod = mod_all[l, :bsz].reshape(bsz, 6, d)
        mod_c = jnp.broadcast_to(mod_all[l, bsz].reshape(1, 6, d), (bsz, 6, d))
        gains = norm_gains[l]
        w_in_l = _relayout_w_in(w_in[l])
        aup = jnp.zeros((LANES, 2 * GLA_KW), F32)
        aup = aup.at[0:GLA_RANK, 0:GLA_KW].set(gla_a_up[l, 0]).at[GLA_RANK:2 * GLA_RANK, GLA_KW:].set(gla_a_up[l, 1])
        abias = gla_a_bias[l].reshape(1, 2 * GLA_KW)
        gla_gain = jnp.tile(gla_norm[l], GLA_HEADS).reshape(1, GLA_VW)
        sink_tab = jnp.broadcast_to(swa_sink[l][:, None] * LOG2E, (SWA_Q_HEADS, LANES))
        w_out_l = w_out[l].astype(BF16)
        wr = jnp.pad(w_router[l], ((0, 0), (0, LANES - N_EXPERTS))).astype(BF16)

        gla_p, nat_p, swa_p = _in_proj(x, mod, gains, w_in_l, aup, abias, rope_tabs, 512)
        gla_c, nat_c, swa_c = _in_proj(xc, mod_c, gains, w_in_l, aup, abias, None, m)

        gla_co, s_f, s_b = _gla(gla_c, zero_state, zero_state, gla_gain)
        gla_o, _, _ = _gla(gla_p, s_f, s_b, gla_gain)
        nat_o = _nat(nat_p, nat_c, _nat_bias_table(nat_rpb[l], m))
        swa_o = _swa(swa_p, swa_c, sink_tab)

        x1, h2, aff, aff_t = _out_proj(gla_o, nat_o, swa_o, x, mod, gains, w_out_l, wr, 512)
        if update_ctx:
            nat_co, swa_co = _ctx_attn(nat_c, swa_c, sink_tab)
            xc1, hc2, aff_c, aff_ct = _out_proj(gla_co, nat_co, swa_co, xc, mod_c, gains, w_out_l, wr, m)
            x, xc = _moe([h2, hc2], [aff_t, aff_ct], [aff, aff_c], [x1, xc1], [mod, mod_c], gains,
                         w_gate, w_up, w_down, l)
        else:
            (x,) = _moe([h2], [aff_t], [aff], [x1], [mod], gains, w_gate, w_up, w_down, l)
    return x
```

```python
import functools

import jax
import jax.numpy as jnp
import numpy as np
from jax import lax
from jax.experimental import pallas as pl
from jax.experimental.pallas import tpu as pltpu

F32 = jnp.float32
BF16 = jnp.bfloat16

D_MODEL = 1024
GRID_W = 64
HEAD_DIM = 64
GLA_HEADS = 4
GLA_DK = 32
GLA_DV = 64
GLA_RANK = 16
GLA_TAU = 16.0
GLA_CHUNK = 64
NAT_HEADS = 4
NAT_WIN_ROWS = 8
NAT_WIN_COLS = 16
SWA_Q_HEADS = 8
SWA_KV_HEADS = 2
SWA_WINDOW = 128
SWA_BLOCK = 128
ROPE_BASE = 10000.0
N_EXPERTS = 16
EXPERT_FF = 1024
CAPACITY_FACTOR = 2
NORM_EPS = 1e-6

GLA_KW = GLA_HEADS * GLA_DK
GLA_VW = GLA_HEADS * GLA_DV
NAT_W = NAT_HEADS * HEAD_DIM
SWA_QW = SWA_Q_HEADS * HEAD_DIM
SWA_KW = SWA_KV_HEADS * HEAD_DIM
LANES = 128
NEG_BIG = -1e30
LOG2E = 1.4426950408889634

GLA_PROJ_W = 2 * GLA_KW + 2 * GLA_VW + LANES
GLA_OUT_W = 2 * GLA_KW + 2 * GLA_VW + 2 * GLA_KW
NAT_PROJ_W = 3 * NAT_W
SWA_PROJ_W = SWA_QW + 4 * SWA_KW
IN_PROJ_W = GLA_PROJ_W + NAT_PROJ_W + SWA_PROJ_W

VMEM_LIMIT = 56 * 1024 * 1024


def _cparams(*sem):
    return pltpu.CompilerParams(dimension_semantics=sem, vmem_limit_bytes=VMEM_LIMIT)


def _dot(a, b):
    return jnp.dot(a, b, preferred_element_type=F32)


def _dot_nt(a, b):
    return lax.dot_general(a, b, (((1,), (1,)), ((), ())), preferred_element_type=F32)


def _dot_tn(a, b):
    return lax.dot_general(a, b, (((0,), (0,)), ((), ())), preferred_element_type=F32)


def _split2(a):
    hi = a.astype(BF16)
    lo = (a - hi.astype(F32)).astype(BF16)
    return hi, lo


def _split3(a):
    hi = a.astype(BF16)
    r = a - hi.astype(F32)
    mid = r.astype(BF16)
    lo = (r - mid.astype(F32)).astype(BF16)
    return hi, mid, lo


def _dot_f32(a, b):
    ah, al = _split2(a)
    bh, bl = _split2(b)
    return _dot(ah, bh) + _dot(al, bh) + _dot(ah, bl)


def _rms(x, gain):
    return x * lax.rsqrt(jnp.mean(x * x, axis=-1, keepdims=True) + NORM_EPS) * gain


def _silu(x):
    return x * jax.nn.sigmoid(x)


def _mod_kernel(c_ref, w_ref, b_ref, o_ref):
    a = _silu(c_ref[...])
    o_ref[0] = _dot_f32(a, w_ref[0]) + b_ref[0]


def _modulation(cc, w_mod, b_mod):
    depth, d, n = w_mod.shape
    r = cc.shape[0]
    tn = 1536
    return pl.pallas_call(
        _mod_kernel,
        grid=(depth, n // tn),
        in_specs=[
            pl.BlockSpec((r, d), lambda l, j: (0, 0)),
            pl.BlockSpec((1, d, tn), lambda l, j: (l, 0, j)),
            pl.BlockSpec((1, 1, tn), lambda l, j: (l, 0, j)),
        ],
        out_specs=pl.BlockSpec((1, r, tn), lambda l, j: (l, 0, j)),
        out_shape=jax.ShapeDtypeStruct((depth, r, n), F32),
        compiler_params=_cparams("parallel", "parallel"),
        name="adaln_mod",
    )(cc, w_mod, b_mod.reshape(depth, 1, n))


def _rope_rotate(x, first_half):
    up = pltpu.roll(x, LANES - 16, 1)
    down = pltpu.roll(x, 16, 1)
    return jnp.where(first_half, up, down)


def _in_proj_kernel(*refs, rope):
    if rope:
        x_ref, mod_ref, gain_ref, w_ref, aup_ref, ab_ref, tri_ref, cos_ref, sin_ref, gla_ref, nat_ref, swa_ref = refs
    else:
        x_ref, mod_ref, gain_ref, w_ref, aup_ref, ab_ref, tri_ref, gla_ref, nat_ref, swa_ref = refs
    x = x_ref[0]
    y = _rms(x, gain_ref[0:1, :])
    h = (y * (1.0 + mod_ref[0, 1:2, :]) + mod_ref[0, 0:1, :]).astype(BF16)

    kq = 2 * GLA_KW + 2 * GLA_VW
    pg = _dot(h, w_ref[:, 0:GLA_PROJ_W])
    gla_ref[0, :, 0:kq] = pg[:, 0:kq]
    z = _dot_f32(pg[:, kq:kq + LANES], aup_ref[...]) + ab_ref[...]
    log_a = (jnp.minimum(z, 0.0) - jnp.log1p(jnp.exp(-jnp.abs(z)))) * (1.0 / GLA_TAU)
    hi, lo = _split2(log_a)
    tri = tri_ref[...]
    cum = _dot(tri, hi) + _dot(tri, lo)
    gla_ref[0, :, kq:kq + GLA_KW] = cum[:, 0:GLA_KW]
    tm = x.shape[0]
    nck = tm // GLA_CHUNK
    cb = cum[:, GLA_KW:].reshape(nck, GLA_CHUNK, GLA_KW)
    lb = log_a[:, GLA_KW:].reshape(nck, GLA_CHUNK, GLA_KW)
    from_end = cb[:, GLA_CHUNK - 1:GLA_CHUNK, :] - cb + lb
    gla_ref[0, :, kq + GLA_KW:kq + 2 * GLA_KW] = from_end.reshape(tm, GLA_KW)

    o = GLA_PROJ_W
    pn = _dot(h, w_ref[:, o:o + NAT_PROJ_W])
    scale = HEAD_DIM ** -0.5 * LOG2E
    nat_ref[0, :, 0:NAT_W] = (pn[:, 0:NAT_W] * scale).astype(BF16)
    nat_ref[0, :, NAT_W:] = pn[:, NAT_W:].astype(BF16)

    o = GLA_PROJ_W + NAT_PROJ_W
    ps = _dot(h, w_ref[:, o:o + SWA_PROJ_W])
    n_rot = (SWA_QW + 2 * SWA_KW) // LANES
    if rope:
        cos = cos_ref[...]
        sin = sin_ref[...]
        lane = lax.broadcasted_iota(jnp.int32, cos.shape, 1)
        first_half = (lane % 32) < 16
    for j in range(SWA_PROJ_W // LANES):
        t = ps[:, j * LANES:(j + 1) * LANES]
        if rope and j < n_rot:
            t = t * cos + _rope_rotate(t, first_half) * sin
        if j < SWA_QW // LANES:
            t = t * scale
        swa_ref[0, :, j * LANES:(j + 1) * LANES] = t.astype(BF16)


def _in_proj(x, mod, gains, w, aup, abias, rope_tabs, tm):
    bsz, t, d = x.shape
    rope = rope_tabs is not None
    in_specs = [
        pl.BlockSpec((1, tm, d), lambda b, i: (b, i, 0)),
        pl.BlockSpec((1, 6, d), lambda b, i: (b, 0, 0)),
        pl.BlockSpec((4, d), lambda b, i: (0, 0)),
        pl.BlockSpec((d, IN_PROJ_W), lambda b, i: (0, 0)),
        pl.BlockSpec((LANES, 2 * GLA_KW), lambda b, i: (0, 0)),
        pl.BlockSpec((1, 2 * GLA_KW), lambda b, i: (0, 0)),
        pl.BlockSpec((tm, tm), lambda b, i: (0, 0)),
    ]
    r = np.arange(tm)
    tri = (r[:, None] // GLA_CHUNK == r[None, :] // GLA_CHUNK) & (r[:, None] >= r[None, :])
    args = [x, mod, gains, w, aup, abias, jnp.asarray(tri, BF16)]
    if rope:
        in_specs += [pl.BlockSpec((tm, LANES), lambda b, i: (i, 0))] * 2
        args += list(rope_tabs)
    return pl.pallas_call(
        functools.partial(_in_proj_kernel, rope=rope),
        grid=(bsz, t // tm),
        in_specs=in_specs,
        out_specs=[
            pl.BlockSpec((1, tm, GLA_OUT_W), lambda b, i: (b, i, 0)),
            pl.BlockSpec((1, tm, NAT_PROJ_W), lambda b, i: (b, i, 0)),
            pl.BlockSpec((1, tm, SWA_PROJ_W), lambda b, i: (b, i, 0)),
        ],
        out_shape=[
            jax.ShapeDtypeStruct((bsz, t, GLA_OUT_W), F32),
            jax.ShapeDtypeStruct((bsz, t, NAT_PROJ_W), BF16),
            jax.ShapeDtypeStruct((bsz, t, SWA_PROJ_W), BF16),
        ],
        compiler_params=_cparams("parallel", "parallel"),
        name="in_proj_rope" if rope else "in_proj",
    )(*args)


def _gla_kernel(p_ref, s0f_ref, s0b_ref, gain_ref, o_ref, sff_ref, sfb_ref, acc_ref, sf_ref, sb_ref):
    t = p_ref.shape[1]
    c = GLA_CHUNK
    nc = t // c
    qi = lax.broadcasted_iota(jnp.int32, (c, GLA_HEADS * c), 0)
    kj = lax.broadcasted_iota(jnp.int32, (c, GLA_HEADS * c), 1) % c
    lower = qi >= kj
    upper = qi <= kj
    srow = lax.broadcasted_iota(jnp.int32, (GLA_VW, GLA_KW), 0) // GLA_DV
    scol = lax.broadcasted_iota(jnp.int32, (GLA_VW, GLA_KW), 1) // GLA_DK
    head_diag = srow == scol
    q_scale = GLA_DK ** -0.5

    sf_ref[...] = s0f_ref[0]
    sb_ref[...] = s0b_ref[0]
    acc_ref[...] = jnp.zeros_like(acc_ref)

    def one_direction(base, keep, end_row, cum_off, st_ref):
        rows = pl.ds(base, c)
        k = p_ref[0, rows, 0:GLA_KW]
        q = p_ref[0, rows, GLA_KW:2 * GLA_KW] * q_scale
        vb = p_ref[0, rows, 2 * GLA_KW:2 * GLA_KW + GLA_VW].astype(BF16)
        cum = p_ref[0, rows, cum_off:cum_off + GLA_KW]
        mid = cum[c // 2:c // 2 + 1, :]
        total = cum[end_row:end_row + 1, :]
        st = st_ref[...]
        inter = _dot_nt((q * jnp.exp(cum)).astype(BF16), st.astype(BF16))
        qs = (q * jnp.exp(cum - mid)).astype(BF16)
        ks = (k * jnp.exp(mid - cum)).astype(BF16)
        s = _dot_nt(qs, _head_stack(ks, GLA_DK, GLA_HEADS))
        s = jnp.where(keep, s, 0.0).astype(BF16)
        intra = _dot(s, _head_stack(vb, GLA_DV, GLA_HEADS))
        acc_ref[rows, :] += inter + intra
        kd = (k * jnp.exp(total - cum)).astype(BF16)
        upd = _dot_tn(vb, kd)
        st_ref[...] = st * jnp.exp(total) + jnp.where(head_diag, upd, 0.0)

    def body(i, carry):
        one_direction(pl.multiple_of(i * c, c), lower, c - 1, 2 * GLA_KW + 2 * GLA_VW, sf_ref)
        one_direction(pl.multiple_of((nc - 1 - i) * c, c), upper, 0, 3 * GLA_KW + 2 * GLA_VW, sb_ref)
        return carry

    lax.fori_loop(0, nc, body, 0, unroll=2)
    sff_ref[0] = sf_ref[...]
    sfb_ref[0] = sb_ref[...]

    hr = lax.broadcasted_iota(jnp.int32, (GLA_VW, GLA_VW), 0) // GLA_DV
    hc = lax.broadcasted_iota(jnp.int32, (GLA_VW, GLA_VW), 1) // GLA_DV
    head_ones = (hr == hc).astype(BF16)
    blk = 256
    for j in range(t // blk):
        rows = pl.ds(j * blk, blk)
        o = acc_ref[rows, :]
        hi, mid, lo = _split3(o * o)
        ms = (_dot(hi, head_ones) + _dot(mid, head_ones) + _dot(lo, head_ones)) * (1.0 / GLA_DV)
        gate = p_ref[0, rows, 2 * GLA_KW + GLA_VW:2 * GLA_KW + 2 * GLA_VW]
        o_ref[0, rows, :] = (o * lax.rsqrt(ms + NORM_EPS) * gain_ref[...] * _silu(gate)).astype(BF16)


def _gla(p, s0f, s0b, gain):
    bsz, t, w = p.shape
    st_spec = pl.BlockSpec((1, GLA_VW, GLA_KW), lambda b: (b, 0, 0))
    st_shape = jax.ShapeDtypeStruct((bsz, GLA_VW, GLA_KW), F32)
    return pl.pallas_call(
        _gla_kernel,
        grid=(bsz,),
        in_specs=[
            pl.BlockSpec((1, t, w), lambda b: (b, 0, 0)),
            st_spec, st_spec,
            pl.BlockSpec((1, GLA_VW), lambda b: (0, 0)),
        ],
        out_specs=[pl.BlockSpec((1, t, GLA_VW), lambda b: (b, 0, 0)), st_spec, st_spec],
        out_shape=[jax.ShapeDtypeStruct((bsz, t, GLA_VW), BF16), st_shape, st_shape],
        scratch_shapes=[
            pltpu.VMEM((t, GLA_VW), F32),
            pltpu.VMEM((GLA_VW, GLA_KW), F32),
            pltpu.VMEM((GLA_VW, GLA_KW), F32),
        ],
        compiler_params=_cparams("parallel"),
        name="gla_bidir",
    )(p, s0f, s0b, gain)


def _head_stack(q, width, n):
    lane = lax.broadcasted_iota(jnp.int32, (1, q.shape[1]), 1) // width
    return jnp.concatenate([jnp.where(lane == h, q, jnp.zeros_like(q)) for h in range(n)], axis=0)


def _head_unstack(o, width, n):
    rows = o.shape[0] // n
    lane = lax.broadcasted_iota(jnp.int32, (1, o.shape[1]), 1) // width
    out = o[0:rows]
    for h in range(1, n):
        out = jnp.where(lane == h, o[h * rows:(h + 1) * rows], out)
    return out


NAT_ROWS_PER_STEP = 4


def _nat_kernel(lat_ref, ctx_ref, bias_ref, o_ref):
    n_rows = lat_ref.shape[1] // GRID_W
    kc = ctx_ref[0, :, NAT_W:2 * NAT_W]
    vc = ctx_ref[0, :, 2 * NAT_W:3 * NAT_W]
    n_loc = NAT_WIN_ROWS * GRID_W
    for i in range(NAT_ROWS_PER_STEP):
        r = pl.program_id(1) * NAT_ROWS_PER_STEP + i
        r0 = jnp.clip(r - NAT_WIN_ROWS // 2, 0, n_rows - NAT_WIN_ROWS)
        q = lat_ref[0, pl.ds(pl.multiple_of(r * GRID_W, GRID_W), GRID_W), 0:NAT_W]
        krows = pl.ds(pl.multiple_of(r0 * GRID_W, GRID_W), n_loc)
        qs = _head_stack(q, HEAD_DIM, NAT_HEADS)
        s = jnp.concatenate([_dot_nt(qs, lat_ref[0, krows, NAT_W:2 * NAT_W]), _dot_nt(qs, kc)], axis=1)
        s = s + bias_ref[r0 - r + NAT_WIN_ROWS - 1]
        p = jnp.exp2((s - jnp.max(s, axis=-1, keepdims=True)).astype(BF16))
        den = jnp.sum(p, axis=-1, keepdims=True, dtype=F32)
        o = (_dot(p[:, 0:n_loc], lat_ref[0, krows, 2 * NAT_W:3 * NAT_W]) + _dot(p[:, n_loc:], vc)) / den
        o_ref[0, i * GRID_W:(i + 1) * GRID_W, :] = _head_unstack(o, HEAD_DIM, NAT_HEADS).astype(BF16)


def _nat(lat, ctx, bias):
    bsz, t, w = lat.shape
    m = ctx.shape[1]
    rows = NAT_ROWS_PER_STEP * GRID_W
    return pl.pallas_call(
        _nat_kernel,
        grid=(bsz, t // rows),
        in_specs=[
            pl.BlockSpec((1, t, w), lambda b, r: (b, 0, 0)),
            pl.BlockSpec((1, m, w), lambda b, r: (b, 0, 0)),
            pl.BlockSpec(bias.shape, lambda b, r: (0, 0, 0)),
        ],
        out_specs=pl.BlockSpec((1, rows, NAT_W), lambda b, r: (b, r, 0)),
        out_shape=jax.ShapeDtypeStruct((bsz, t, NAT_W), BF16),
        compiler_params=_cparams("parallel", "arbitrary"),
        name="nat_attn",
    )(lat, ctx, bias)


def _nat_bias_table(rpb, m):
    cols = np.arange(GRID_W)
    col_start = np.clip(cols - NAT_WIN_COLS // 2, 0, GRID_W - NAT_WIN_COLS)
    kc = np.arange(GRID_W)
    inside = (kc[None, :] >= col_start[:, None]) & (kc[None, :] < col_start[:, None] + NAT_WIN_COLS)
    rel = kc[None, :] - cols[:, None] + NAT_WIN_COLS - 1
    col_sel = (rel[:, :, None] == np.arange(2 * NAT_WIN_COLS - 1)) & inside[:, :, None]
    ridx = np.arange(NAT_WIN_ROWS)[:, None] + np.arange(NAT_WIN_ROWS)[None, :]
    row_sel = ridx[:, :, None] == np.arange(2 * NAT_WIN_ROWS - 1)
    tab = jnp.einsum("hab,oia,ckb->ohcik", rpb.astype(F32), jnp.asarray(row_sel, F32), jnp.asarray(col_sel, F32),
                     precision=lax.Precision.HIGHEST)
    tab = jnp.where(jnp.asarray(inside)[None, None, :, None, :], tab * LOG2E, NEG_BIG)
    tab = tab.reshape(NAT_WIN_ROWS, NAT_HEADS * GRID_W, NAT_WIN_ROWS * GRID_W)
    return jnp.concatenate([tab, jnp.zeros(tab.shape[:2] + (m,), F32)], axis=2)


def _sink_column(sink_ref, first_head, n_heads, rows):
    return jnp.concatenate(
        [jnp.broadcast_to(sink_ref[first_head + j:first_head + j + 1, 0:1], (rows, 1)) for j in range(n_heads)], axis=0)


def _softmax_pv_sink(s, v, sink):
    m = jnp.maximum(jnp.max(s, axis=-1, keepdims=True), sink)
    p = jnp.exp2((s - m).astype(BF16))
    o = _dot(p, jnp.concatenate([v, jnp.ones_like(v)], axis=1))
    w = v.shape[1]
    return o[:, 0:w] / (o[:, w:] + jnp.exp2(sink - m))


SWA_BLOCKS_PER_STEP = 4


def _swa_kernel(lat_ref, ctx_ref, sink_ref, o_ref):
    t = lat_ref.shape[1]
    m = ctx_ref.shape[1]
    blk = SWA_BLOCK
    n_loc = 3 * blk
    qi = lax.broadcasted_iota(jnp.int32, (blk, n_loc), 0)
    kj = lax.broadcasted_iota(jnp.int32, (blk, n_loc), 1)
    group = SWA_Q_HEADS // SWA_KV_HEADS
    for a in range(SWA_BLOCKS_PER_STEP):
        n = pl.program_id(1) * SWA_BLOCKS_PER_STEP + a
        base = pl.multiple_of(jnp.clip((n - 1) * blk, 0, t - n_loc), blk)
        qrows = pl.ds(pl.multiple_of(n * blk, blk), blk)
        krows = pl.ds(base, n_loc)
        mask_add = jnp.where(jnp.abs(base + kj - n * blk - qi) <= SWA_WINDOW, 0.0, NEG_BIG)[None]
        for g in range(SWA_KV_HEADS):
            kcol = pl.ds(SWA_QW + g * LANES, LANES)
            vcol = pl.ds(SWA_QW + 2 * SWA_KW + g * LANES, LANES)
            k = jnp.concatenate([lat_ref[0, krows, kcol], ctx_ref[0, :, kcol]], axis=0)
            v = jnp.concatenate([lat_ref[0, krows, vcol], ctx_ref[0, :, vcol]], axis=0)
            qs = jnp.concatenate(
                [_head_stack(lat_ref[0, qrows, pl.ds((g * 2 + p) * LANES, LANES)], HEAD_DIM, 2) for p in range(2)],
                axis=0)
            s = _dot_nt(qs, k).reshape(group, blk, n_loc + m)
            s = jnp.concatenate([s[:, :, 0:n_loc] + mask_add, s[:, :, n_loc:]], axis=2).reshape(group * blk, n_loc + m)
            o = _softmax_pv_sink(s, v, _sink_column(sink_ref, g * group, group, blk))
            for p in range(2):
                pair = _head_unstack(o[2 * p * blk:(2 * p + 2) * blk], HEAD_DIM, 2)
                o_ref[0, a * blk:(a + 1) * blk, pl.ds((g * 2 + p) * LANES, LANES)] = pair.astype(BF16)


def _swa(lat, ctx, sink_tab):
    bsz, t, w = lat.shape
    m = ctx.shape[1]
    rows = SWA_BLOCKS_PER_STEP * SWA_BLOCK
    return pl.pallas_call(
        _swa_kernel,
        grid=(bsz, t // rows),
        in_specs=[
            pl.BlockSpec((1, t, w), lambda b, n: (b, 0, 0)),
            pl.BlockSpec((1, m, w), lambda b, n: (b, 0, 0)),
            pl.BlockSpec((SWA_Q_HEADS, LANES), lambda b, n: (0, 0)),
        ],
        out_specs=pl.BlockSpec((1, rows, SWA_QW), lambda b, n: (b, n, 0)),
        out_shape=jax.ShapeDtypeStruct((bsz, t, SWA_QW), BF16),
        compiler_params=_cparams("parallel", "arbitrary"),
        name="swa_attn",
    )(lat, ctx, sink_tab)


def _ctx_attn_kernel(nat_ref, swa_ref, sink_ref, on_ref, os_ref):
    m = nat_ref.shape[1]
    q = nat_ref[0, :, 0:NAT_W]
    k = nat_ref[0, :, NAT_W:2 * NAT_W]
    v = nat_ref[0, :, 2 * NAT_W:3 * NAT_W]
    s = _dot_nt(_head_stack(q, HEAD_DIM, NAT_HEADS), k)
    p = jnp.exp2((s - jnp.max(s, axis=-1, keepdims=True)).astype(BF16))
    o = _dot(p, v) / jnp.sum(p, axis=-1, keepdims=True, dtype=F32)
    on_ref[0] = _head_unstack(o, HEAD_DIM, NAT_HEADS).astype(BF16)

    group = SWA_Q_HEADS // SWA_KV_HEADS
    for g in range(SWA_KV_HEADS):
        k = swa_ref[0, :, pl.ds(SWA_QW + g * LANES, LANES)]
        v = swa_ref[0, :, pl.ds(SWA_QW + 2 * SWA_KW + g * LANES, LANES)]
        qs = jnp.concatenate(
            [_head_stack(swa_ref[0, :, pl.ds((g * 2 + p) * LANES, LANES)], HEAD_DIM, 2) for p in range(2)], axis=0)
        o = _softmax_pv_sink(_dot_nt(qs, k), v, _sink_column(sink_ref, g * group, group, m))
        for pr in range(2):
            pair = _head_unstack(o[2 * pr * m:(2 * pr + 2) * m], HEAD_DIM, 2)
            os_ref[0, :, pl.ds((g * 2 + pr) * LANES, LANES)] = pair.astype(BF16)


def _ctx_attn(nat_c, swa_c, sink_tab):
    bsz, m, _ = nat_c.shape
    return pl.pallas_call(
        _ctx_attn_kernel,
        grid=(bsz,),
        in_specs=[
            pl.BlockSpec((1, m, NAT_PROJ_W), lambda b: (b, 0, 0)),
            pl.BlockSpec((1, m, SWA_PROJ_W), lambda b: (b, 0, 0)),
            pl.BlockSpec((SWA_Q_HEADS, LANES), lambda b: (0, 0)),
        ],
        out_specs=[
            pl.BlockSpec((1, m, NAT_W), lambda b: (b, 0, 0)),
            pl.BlockSpec((1, m, SWA_QW), lambda b: (b, 0, 0)),
        ],
        out_shape=[
            jax.ShapeDtypeStruct((bsz, m, NAT_W), BF16),
            jax.ShapeDtypeStruct((bsz, m, SWA_QW), BF16),
        ],
        compiler_params=_cparams("parallel"),
        name="ctx_attn",
    )(nat_c, swa_c, sink_tab)


def _out_proj_kernel(gla_ref, nat_ref, swa_ref, x_ref, mod_ref, gain_ref, w_ref, wr_ref,
                     x1_ref, h2_ref, aff_ref, afft_ref):
    tm = x_ref.shape[1]
    half = tm // 2
    for part in range(2):
        rows = pl.ds(part * half, half)
        mix = (_dot(gla_ref[0, rows, :], w_ref[0:GLA_VW, :])
               + _dot(nat_ref[0, rows, :], w_ref[GLA_VW:GLA_VW + NAT_W, :])
               + _dot(swa_ref[0, rows, :], w_ref[GLA_VW + NAT_W:, :]))
        x1 = x_ref[0, rows, :] + mod_ref[0, 2:3, :] * _rms(mix, gain_ref[1:2, :])
        x1_ref[0, rows, :] = x1
        h2 = (_rms(x1, gain_ref[2:3, :]) * (1.0 + mod_ref[0, 4:5, :]) + mod_ref[0, 3:4, :]).astype(BF16)
        h2_ref[0, rows, :] = h2
        logits = _dot(h2, wr_ref[...])
        lane = lax.broadcasted_iota(jnp.int32, logits.shape, 1)
        logits = jnp.where(lane < N_EXPERTS, logits, NEG_BIG)
        e = jnp.exp(logits - jnp.max(logits, axis=-1, keepdims=True))
        aff = e / jnp.sum(e, axis=-1, keepdims=True)
        aff_ref[0, rows, :] = aff
        afft_ref[0, :, rows] = aff.T[0:N_EXPERTS, :]


def _out_proj(gla_o, nat_o, swa_o, x, mod, gains, w_out, wr, tm):
    bsz, t, d = x.shape
    tok = lambda b, i: (b, i, 0)
    const = lambda b, i: (0, 0)
    return pl.pallas_call(
        _out_proj_kernel,
        grid=(bsz, t // tm),
        in_specs=[
            pl.BlockSpec((1, tm, GLA_VW), tok),
            pl.BlockSpec((1, tm, NAT_W), tok),
            pl.BlockSpec((1, tm, SWA_QW), tok),
            pl.BlockSpec((1, tm, d), tok),
            pl.BlockSpec((1, 6, d), lambda b, i: (b, 0, 0)),
            pl.BlockSpec((4, d), const),
            pl.BlockSpec((d, d), const),
            pl.BlockSpec((d, LANES), const),
        ],
        out_specs=[
            pl.BlockSpec((1, tm, d), tok),
            pl.BlockSpec((1, tm, d), tok),
            pl.BlockSpec((1, tm, LANES), tok),
            pl.BlockSpec((1, N_EXPERTS, tm), lambda b, i: (b, 0, i)),
        ],
        out_shape=[
            jax.ShapeDtypeStruct((bsz, t, d), F32),
            jax.ShapeDtypeStruct((bsz, t, d), BF16),
            jax.ShapeDtypeStruct((bsz, t, LANES), F32),
            jax.ShapeDtypeStruct((bsz, N_EXPERTS, t), F32),
        ],
        compiler_params=_cparams("parallel", "parallel"),
        name="out_proj_router",
    )(gla_o, nat_o, swa_o, x, mod, gains, w_out, wr)


def _excl_prefix(x):
    rows, n = x.shape
    nblk = n // LANES
    r = lax.broadcasted_iota(jnp.int32, (LANES, LANES), 0)
    c = lax.broadcasted_iota(jnp.int32, (LANES, LANES), 1)
    strict_upper = (r < c).astype(BF16)
    stacked = jnp.concatenate([x[:, j * LANES:(j + 1) * LANES] for j in range(nblk)], axis=0).astype(BF16)
    local = _dot(stacked, strict_upper)
    totals = jnp.sum(stacked.astype(F32), axis=-1, keepdims=True)
    out = []
    off = jnp.zeros((rows, 1), F32)
    for j in range(nblk):
        out.append(local[j * rows:(j + 1) * rows] + off)
        off = off + totals[j * rows:(j + 1) * rows]
    return jnp.concatenate(out, axis=1)


def _topc_kernel(aff_ref, rank_ref, rankcol_ref, *, cap):
    aff = aff_ref[...]
    bits = pltpu.bitcast(aff, jnp.int32)

    def step(i, thr):
        cand = thr | (jnp.int32(1) << (30 - i))
        cnt = jnp.sum((bits >= cand).astype(F32), axis=-1, keepdims=True)
        return jnp.where(cnt >= cap, cand, thr)

    thr = lax.fori_loop(0, 31, step, jnp.zeros((aff.shape[0], 1), jnp.int32))
    gt = bits > thr
    eq = bits == thr
    need = cap - jnp.sum(gt.astype(F32), axis=-1, keepdims=True)
    eq_before = _excl_prefix(eq.astype(F32))
    sel = gt | (eq & (eq_before < need))
    rank = jnp.where(sel, _excl_prefix(sel.astype(F32)), -1.0)
    rank_ref[...] = rank
    n = rank.shape[1]
    fill = jnp.full((LANES - N_EXPERTS, n), -1.0, F32)
    for b in range(rankcol_ref.shape[0]):
        rankcol_ref[b] = jnp.concatenate([rank[b * N_EXPERTS:(b + 1) * N_EXPERTS], fill], axis=0).T


def _topc(aff_t, cap):
    bsz, e, n = aff_t.shape
    rank, rankcol = pl.pallas_call(
        functools.partial(_topc_kernel, cap=cap),
        grid=(1,),
        in_specs=[pl.BlockSpec((bsz * e, n), lambda i: (0, 0))],
        out_specs=[pl.BlockSpec((bsz * e, n), lambda i: (0, 0)), pl.BlockSpec((bsz, n, LANES), lambda i: (0, 0, 0))],
        out_shape=[jax.ShapeDtypeStruct((bsz * e, n), F32), jax.ShapeDtypeStruct((bsz, n, LANES), F32)],
        compiler_params=_cparams("arbitrary"),
        name="expert_topc",
    )(aff_t.reshape(bsz * e, n))
    return rank.reshape(bsz, e, n), rankcol


def _gather_kernel(h_ref, rank_ref, xs_ref, *, cap):
    n = h_ref.shape[1]
    slot = lax.broadcasted_iota(jnp.int32, (cap, n), 0).astype(F32)
    h = h_ref[0]
    for e in range(N_EXPERTS):
        onehot = (slot == rank_ref[0, e:e + 1, :]).astype(BF16)
        xs_ref[e] = _dot(onehot, h).astype(BF16)


def _gather(h, rank, cap):
    bsz, n, d = h.shape
    return pl.pallas_call(
        functools.partial(_gather_kernel, cap=cap),
        grid=(bsz,),
        in_specs=[
            pl.BlockSpec((1, n, d), lambda b: (b, 0, 0)),
            pl.BlockSpec((1, N_EXPERTS, n), lambda b: (b, 0, 0)),
        ],
        out_specs=pl.BlockSpec((N_EXPERTS, cap, d), lambda b: (0, b, 0)),
        out_shape=jax.ShapeDtypeStruct((N_EXPERTS, bsz * cap, d), BF16),
        compiler_params=_cparams("parallel"),
        name="moe_gather",
    )(h, rank)


FFN_PARTS = 4


def _ffn_kernel(*refs, has_ctx):
    if has_ctx:
        xs_ref, xc_ref, wg_ref, wu_ref, wd_ref, y_ref, yc_ref, wg_bf, wu_bf, wd_bf = refs
    else:
        xs_ref, wg_ref, wu_ref, wd_ref, y_ref, wg_bf, wu_bf, wd_bf = refs
    s = pl.program_id(0)
    j = pl.program_id(1)
    n_exp = pl.num_programs(0) - 1
    piece = wg_ref.shape[2]

    @pl.when((s < n_exp) & (j < FFN_PARTS))
    def _():
        rows = pl.ds(pl.multiple_of(j * piece, piece), piece)
        wg_bf[s % 2, rows, :] = wg_ref[0, 0].astype(BF16)
        wu_bf[s % 2, rows, :] = wu_ref[0, 0].astype(BF16)
        wd_bf[s % 2, rows, :] = wd_ref[0, 0].astype(BF16)

    def swiglu(x_ref, o_ref):
        cur = (s + 1) % 2
        xs = x_ref[0]
        hid = (_silu(_dot(xs, wg_bf[cur])) * _dot(xs, wu_bf[cur])).astype(BF16)
        o_ref[0] = _dot(hid, wd_bf[cur]).astype(BF16)

    @pl.when((s > 0) & (j < FFN_PARTS))
    def _():
        swiglu(xs_ref, y_ref)

    if has_ctx:
        @pl.when((s > 0) & (j == FFN_PARTS))
        def _():
            swiglu(xc_ref, yc_ref)


def _ffn(xs, xc, w_gate, w_up, w_down, layer):
    _, e, d, ff = w_gate.shape
    rb = xs.shape[1] // FFN_PARTS
    has_ctx = xc is not None
    last = FFN_PARTS - 1

    def x_idx(s, j):
        return (jnp.maximum(s - 1, 0), jnp.where(s > 0, jnp.minimum(j, last), 0), 0)

    def w_idx(s, j):
        return (layer, jnp.minimum(s, e - 1), jnp.minimum(j, last), 0)

    xspec = pl.BlockSpec((1, rb, d), x_idx)
    cspecs, cargs, cshapes = [], [], []
    if has_ctx:
        cspecs = [pl.BlockSpec((1, xc.shape[1], d), lambda s, j: (jnp.maximum(s - 1, 0), 0, 0))]
        cargs = [xc]
        cshapes = [jax.ShapeDtypeStruct(xc.shape, BF16)]
    out = pl.pallas_call(
        functools.partial(_ffn_kernel, has_ctx=has_ctx),
        grid=(e + 1, FFN_PARTS + int(has_ctx)),
        in_specs=[xspec] + cspecs + [
            pl.BlockSpec((1, 1, d // FFN_PARTS, ff), w_idx),
            pl.BlockSpec((1, 1, d // FFN_PARTS, ff), w_idx),
            pl.BlockSpec((1, 1, ff // FFN_PARTS, d), w_idx),
        ],
        out_specs=[xspec] + cspecs,
        out_shape=[jax.ShapeDtypeStruct(xs.shape, BF16)] + cshapes,
        scratch_shapes=[pltpu.VMEM((2, d, ff), BF16), pltpu.VMEM((2, d, ff), BF16), pltpu.VMEM((2, ff, d), BF16)],
        compiler_params=_cparams("arbitrary", "arbitrary"),
        name="moe_ffn",
    )(xs, *cargs, w_gate, w_up, w_down)
    return out


def _combine_kernel(y_ref, rankcol_ref, aff_ref, x_ref, mod_ref, gain_ref, o_ref, *, cap):
    rb = x_ref.shape[1]
    slot = lax.broadcasted_iota(jnp.int32, (rb, cap), 1).astype(F32)
    rc = rankcol_ref[0]
    af = aff_ref[0]
    acc = jnp.zeros((rb, x_ref.shape[2]), F32)
    for e in range(N_EXPERTS):
        onehot = (rc[:, e:e + 1] == slot).astype(BF16)
        acc = acc + af[:, e:e + 1] * _dot(onehot, y_ref[e])
    o_ref[0] = x_ref[0] + mod_ref[0, 5:6, :] * _rms(acc, gain_ref[3:4, :])


def _combine(y, rankcol, aff, x, mod, gains, cap):
    bsz, n, d = x.shape
    rb = min(n, 512)
    tok = lambda b, i: (b, i, 0)
    return pl.pallas_call(
        functools.partial(_combine_kernel, cap=cap),
        grid=(bsz, n // rb),
        in_specs=[
            pl.BlockSpec((N_EXPERTS, cap, d), lambda b, i: (0, b, 0)),
            pl.BlockSpec((1, rb, LANES), tok),
            pl.BlockSpec((1, rb, LANES), tok),
            pl.BlockSpec((1, rb, d), tok),
            pl.BlockSpec((1, 6, d), lambda b, i: (b, 0, 0)),
            pl.BlockSpec((4, d), lambda b, i: (0, 0)),
        ],
        out_specs=pl.BlockSpec((1, rb, d), tok),
        out_shape=jax.ShapeDtypeStruct((bsz, n, d), F32),
        compiler_params=_cparams("parallel", "arbitrary"),
        name="moe_combine",
    )(y, rankcol, aff, x, mod, gains)


def _relayout_w_in(w):
    widths = (GLA_KW, GLA_VW, GLA_RANK, GLA_RANK, NAT_W, NAT_W, SWA_KW, SWA_KW, GLA_KW, GLA_VW, NAT_W, SWA_QW)
    gk, gv, gaf, gab, nk, nv, sk, sv, gq, gg, nq, sq = jnp.split(w, np.cumsum(widths)[:-1].tolist(), axis=1)
    pad = jnp.zeros((w.shape[0], LANES - 2 * GLA_RANK), w.dtype)
    sk0, sk1 = sk[:, :HEAD_DIM], sk[:, HEAD_DIM:]
    sv0, sv1 = sv[:, :HEAD_DIM], sv[:, HEAD_DIM:]
    cols = [gk, gq, gv, gg, gaf, gab, pad, nq, nk, nv, sq, sk0, sk0, sk1, sk1, sv0, sv0, sv1, sv1]
    return jnp.concatenate(cols, axis=1).astype(BF16)


def _rope_tables(t):
    half = HEAD_DIM // 4
    freqs = ROPE_BASE ** (-np.arange(half, dtype=np.float32) / half)
    pos = np.arange(t)
    ang_r = (pos // GRID_W).astype(np.float32)[:, None] * freqs
    ang_c = (pos % GRID_W).astype(np.float32)[:, None] * freqs
    cos = np.concatenate([np.cos(ang_r), np.cos(ang_r), np.cos(ang_c), np.cos(ang_c)], axis=1)
    sin = np.concatenate([-np.sin(ang_r), np.sin(ang_r), -np.sin(ang_c), np.sin(ang_c)], axis=1)
    reps = LANES // HEAD_DIM
    return jnp.asarray(np.tile(cos, (1, reps)), F32), jnp.asarray(np.tile(sin, (1, reps)), F32)


def _moe(h_list, afft_list, aff_list, x_list, mod_list, gains, w_gate, w_up, w_down, layer):
    caps = [CAPACITY_FACTOR * h.shape[1] // N_EXPERTS for h in h_list]
    ranks = [_topc(a, cap) for a, cap in zip(afft_list, caps)]
    xs = [_gather(h, r[0], cap) for h, r, cap in zip(h_list, ranks, caps)]
    ys = _ffn(xs[0], xs[1] if len(xs) > 1 else None, w_gate, w_up, w_down, layer)
    return [_combine(y, r[1], aff, x, mod, gains, cap)
            for y, r, aff, x, mod, cap in zip(ys, ranks, aff_list, x_list, mod_list, caps)]


def kernel(x, c, ctx, c_ctx, w_mod, b_mod, norm_gains, w_in, w_out, gla_a_up, gla_a_bias, gla_norm,
           nat_rpb, swa_sink, w_router, w_gate, w_up, w_down):
    bsz, t, d = x.shape
    m = ctx.shape[1]
    depth = w_mod.shape[0]

    cc = jnp.concatenate([c, c_ctx[None], jnp.zeros((16 - bsz - 1, d), F32)], axis=0)
    mod_all = _modulation(cc, w_mod, b_mod)
    rope_tabs = _rope_tables(t)
    zero_state = jnp.zeros((bsz, GLA_VW, GLA_KW), F32)

    xc = ctx
    for l in range(depth):
        update_ctx = l < depth - 1
        mod = mod_all[l, :bsz].reshape(bsz, 6, d)
        mod_c = jnp.broadcast_to(mod_all[l, bsz].reshape(1, 6, d), (bsz, 6, d))
        gains = norm_gains[l]
        w_in_l = _relayout_w_in(w_in[l])
        aup = jnp.zeros((LANES, 2 * GLA_KW), F32)
        aup = aup.at[0:GLA_RANK, 0:GLA_KW].set(gla_a_up[l, 0]).at[GLA_RANK:2 * GLA_RANK, GLA_KW:].set(gla_a_up[l, 1])
        abias = gla_a_bias[l].reshape(1, 2 * GLA_KW)
        gla_gain = jnp.tile(gla_norm[l], GLA_HEADS).reshape(1, GLA_VW)
        sink_tab = jnp.broadcast_to(swa_sink[l][:, None] * LOG2E, (SWA_Q_HEADS, LANES))
        w_out_l = w_out[l].astype(BF16)
        wr = jnp.pad(w_router[l], ((0, 0), (0, LANES - N_EXPERTS))).astype(BF16)

        gla_p, nat_p, swa_p = _in_proj(x, mod, gains, w_in_l, aup, abias, rope_tabs, 512)
        gla_c, nat_c, swa_c = _in_proj(xc, mod_c, gains, w_in_l, aup, abias, None, m)

        gla_co, s_f, s_b = _gla(gla_c, zero_state, zero_state, gla_gain)
        gla_o, _, _ = _gla(gla_p, s_f, s_b, gla_gain)
        nat_o = _nat(nat_p, nat_c, _nat_bias_table(nat_rpb[l], m))
        swa_o = _swa(swa_p, swa_c, sink_tab)

        x1, h2, aff, aff_t = _out_proj(gla_o, nat_o, swa_o, x, mod, gains, w_out_l, wr, 512)
        if update_ctx:
            nat_co, swa_co = _ctx_attn(nat_c, swa_c, sink_tab)
            xc1, hc2, aff_c, aff_ct = _out_proj(gla_co, nat_co, swa_co, xc, mod_c, gains, w_out_l, wr, m)
            x, xc = _moe([h2, hc2], [aff_t, aff_ct], [aff, aff_c], [x1, xc1], [mod, mod_c], gains,
                         w_gate, w_up, w_down, l)
        else:
            (x,) = _moe([h2], [aff_t], [aff], [x1], [mod], gains, w_gate, w_up, w_down, l)
    return x
```

```python
import functools

import jax
import jax.numpy as jnp
import numpy as np
from jax import lax
from jax.experimental import pallas as pl
from jax.experimental.pallas import tpu as pltpu

F32 = jnp.float32
BF16 = jnp.bfloat16

D_MODEL = 1024
GRID_W = 64
HEAD_DIM = 64
GLA_HEADS = 4
GLA_DK = 32
GLA_DV = 64
GLA_RANK = 16
GLA_TAU = 16.0
GLA_CHUNK = 64
NAT_HEADS = 4
NAT_WIN_ROWS = 8
NAT_WIN_COLS = 16
SWA_Q_HEADS = 8
SWA_KV_HEADS = 2
SWA_WINDOW = 128
SWA_BLOCK = 128
ROPE_BASE = 10000.0
N_EXPERTS = 16
EXPERT_FF = 1024
CAPACITY_FACTOR = 2
NORM_EPS = 1e-6

GLA_KW = GLA_HEADS * GLA_DK
GLA_VW = GLA_HEADS * GLA_DV
NAT_W = NAT_HEADS * HEAD_DIM
SWA_QW = SWA_Q_HEADS * HEAD_DIM
SWA_KW = SWA_KV_HEADS * HEAD_DIM
LANES = 128
NEG_BIG = -1e30
LOG2E = 1.4426950408889634

GLA_PROJ_W = 2 * GLA_KW + 2 * GLA_VW + LANES
GLA_OUT_W = 2 * GLA_KW + 2 * GLA_VW + 2 * GLA_KW
NAT_PROJ_W = 3 * NAT_W
SWA_PROJ_W = SWA_QW + 4 * SWA_KW
IN_PROJ_W = GLA_PROJ_W + NAT_PROJ_W + SWA_PROJ_W

VMEM_LIMIT = 56 * 1024 * 1024


def _cparams(*sem):
    return pltpu.CompilerParams(dimension_semantics=sem, vmem_limit_bytes=VMEM_LIMIT)


def _dot(a, b):
    return jnp.dot(a, b, preferred_element_type=F32)


def _dot_nt(a, b):
    return lax.dot_general(a, b, (((1,), (1,)), ((), ())), preferred_element_type=F32)


def _dot_tn(a, b):
    return lax.dot_general(a, b, (((0,), (0,)), ((), ())), preferred_element_type=F32)


def _split2(a):
    hi = a.astype(BF16)
    lo = (a - hi.astype(F32)).astype(BF16)
    return hi, lo


def _split3(a):
    hi = a.astype(BF16)
    r = a - hi.astype(F32)
    mid = r.astype(BF16)
    lo = (r - mid.astype(F32)).astype(BF16)
    return hi, mid, lo


def _dot_f32(a, b):
    ah, al = _split2(a)
    bh, bl = _split2(b)
    return _dot(ah, bh) + _dot(al, bh) + _dot(ah, bl)


def _rms(x, gain):
    return x * lax.rsqrt(jnp.mean(x * x, axis=-1, keepdims=True) + NORM_EPS) * gain


def _silu(x):
    return x * jax.nn.sigmoid(x)


def _mod_kernel(c_ref, w_ref, b_ref, o_ref):
    a = _silu(c_ref[...])
    o_ref[0] = _dot_f32(a, w_ref[0]) + b_ref[0]


def _modulation(cc, w_mod, b_mod):
    depth, d, n = w_mod.shape
    r = cc.shape[0]
    tn = 1536
    return pl.pallas_call(
        _mod_kernel,
        grid=(depth, n // tn),
        in_specs=[
            pl.BlockSpec((r, d), lambda l, j: (0, 0)),
            pl.BlockSpec((1, d, tn), lambda l, j: (l, 0, j)),
            pl.BlockSpec((1, 1, tn), lambda l, j: (l, 0, j)),
        ],
        out_specs=pl.BlockSpec((1, r, tn), lambda l, j: (l, 0, j)),
        out_shape=jax.ShapeDtypeStruct((depth, r, n), F32),
        compiler_params=_cparams("parallel", "parallel"),
        name="adaln_mod",
    )(cc, w_mod, b_mod.reshape(depth, 1, n))


def _rope_rotate(x, first_half):
    up = pltpu.roll(x, LANES - 16, 1)
    down = pltpu.roll(x, 16, 1)
    return jnp.where(first_half, up, down)


def _in_proj_kernel(*refs, rope):
    if rope:
        x_ref, mod_ref, gain_ref, w_ref, aup_ref, ab_ref, tri_ref, cos_ref, sin_ref, gla_ref, nat_ref, swa_ref = refs
    else:
        x_ref, mod_ref, gain_ref, w_ref, aup_ref, ab_ref, tri_ref, gla_ref, nat_ref, swa_ref = refs
    x = x_ref[0]
    y = _rms(x, gain_ref[0:1, :])
    h = (y * (1.0 + mod_ref[0, 1:2, :]) + mod_ref[0, 0:1, :]).astype(BF16)

    kq = 2 * GLA_KW + 2 * GLA_VW
    pg = _dot(h, w_ref[:, 0:GLA_PROJ_W])
    gla_ref[0, :, 0:kq] = pg[:, 0:kq]
    z = _dot_f32(pg[:, kq:kq + LANES], aup_ref[...]) + ab_ref[...]
    log_a = (jnp.minimum(z, 0.0) - jnp.log1p(jnp.exp(-jnp.abs(z)))) * (1.0 / GLA_TAU)
    hi, lo = _split2(log_a)
    tri = tri_ref[...]
    cum = _dot(tri, hi) + _dot(tri, lo)
    gla_ref[0, :, kq:kq + GLA_KW] = cum[:, 0:GLA_KW]
    tm = x.shape[0]
    nck = tm // GLA_CHUNK
    cb = cum[:, GLA_KW:].reshape(nck, GLA_CHUNK, GLA_KW)
    lb = log_a[:, GLA_KW:].reshape(nck, GLA_CHUNK, GLA_KW)
    from_end = cb[:, GLA_CHUNK - 1:GLA_CHUNK, :] - cb + lb
    gla_ref[0, :, kq + GLA_KW:kq + 2 * GLA_KW] = from_end.reshape(tm, GLA_KW)

    o = GLA_PROJ_W
    pn = _dot(h, w_ref[:, o:o + NAT_PROJ_W])
    scale = HEAD_DIM ** -0.5 * LOG2E
    nat_ref[0, :, 0:NAT_W] = (pn[:, 0:NAT_W] * scale).astype(BF16)
    nat_ref[0, :, NAT_W:] = pn[:, NAT_W:].astype(BF16)

    o = GLA_PROJ_W + NAT_PROJ_W
    ps = _dot(h, w_ref[:, o:o + SWA_PROJ_W])
    n_rot = (SWA_QW + 2 * SWA_KW) // LANES
    if rope:
        cos = cos_ref[...]
        sin = sin_ref[...]
        lane = lax.broadcasted_iota(jnp.int32, cos.shape, 1)
        first_half = (lane % 32) < 16
    for j in range(SWA_PROJ_W // LANES):
        t = ps[:, j * LANES:(j + 1) * LANES]
        if rope and j < n_rot:
            t = t * cos + _rope_rotate(t, first_half) * sin
        if j < SWA_QW // LANES:
            t = t * scale
        swa_ref[0, :, j * LANES:(j + 1) * LANES] = t.astype(BF16)


def _in_proj(x, mod, gains, w, aup, abias, rope_tabs, tm):
    bsz, t, d = x.shape
    rope = rope_tabs is not None
    in_specs = [
        pl.BlockSpec((1, tm, d), lambda b, i: (b, i, 0)),
        pl.BlockSpec((1, 6, d), lambda b, i: (b, 0, 0)),
        pl.BlockSpec((4, d), lambda b, i: (0, 0)),
        pl.BlockSpec((d, IN_PROJ_W), lambda b, i: (0, 0)),
        pl.BlockSpec((LANES, 2 * GLA_KW), lambda b, i: (0, 0)),
        pl.BlockSpec((1, 2 * GLA_KW), lambda b, i: (0, 0)),
        pl.BlockSpec((tm, tm), lambda b, i: (0, 0)),
    ]
    r = np.arange(tm)
    tri = (r[:, None] // GLA_CHUNK == r[None, :] // GLA_CHUNK) & (r[:, None] >= r[None, :])
    args = [x, mod, gains, w, aup, abias, jnp.asarray(tri, BF16)]
    if rope:
        in_specs += [pl.BlockSpec((tm, LANES), lambda b, i: (i, 0))] * 2
        args += list(rope_tabs)
    return pl.pallas_call(
        functools.partial(_in_proj_kernel, rope=rope),
        grid=(bsz, t // tm),
        in_specs=in_specs,
        out_specs=[
            pl.BlockSpec((1, tm, GLA_OUT_W), lambda b, i: (b, i, 0)),
            pl.BlockSpec((1, tm, NAT_PROJ_W), lambda b, i: (b, i, 0)),
            pl.BlockSpec((1, tm, SWA_PROJ_W), lambda b, i: (b, i, 0)),
        ],
        out_shape=[
            jax.ShapeDtypeStruct((bsz, t, GLA_OUT_W), F32),
            jax.ShapeDtypeStruct((bsz, t, NAT_PROJ_W), BF16),
            jax.ShapeDtypeStruct((bsz, t, SWA_PROJ_W), BF16),
        ],
        compiler_params=_cparams("parallel", "parallel"),
        name="in_proj_rope" if rope else "in_proj",
    )(*args)


def _gla_kernel(p_ref, s0f_ref, s0b_ref, gain_ref, o_ref, sff_ref, sfb_ref, acc_ref, sf_ref, sb_ref):
    t = p_ref.shape[1]
    c = GLA_CHUNK
    nc = t // c
    qi = lax.broadcasted_iota(jnp.int32, (c, GLA_HEADS * c), 0)
    kj = lax.broadcasted_iota(jnp.int32, (c, GLA_HEADS * c), 1) % c
    lower = qi >= kj
    upper = qi <= kj
    srow = lax.broadcasted_iota(jnp.int32, (GLA_VW, GLA_KW), 0) // GLA_DV
    scol = lax.broadcasted_iota(jnp.int32, (GLA_VW, GLA_KW), 1) // GLA_DK
    head_diag = srow == scol
    q_scale = GLA_DK ** -0.5

    sf_ref[...] = s0f_ref[0]
    sb_ref[...] = s0b_ref[0]
    acc_ref[...] = jnp.zeros_like(acc_ref)

    def one_direction(base, keep, end_row, cum_off, st_ref):
        rows = pl.ds(base, c)
        k = p_ref[0, rows, 0:GLA_KW]
        q = p_ref[0, rows, GLA_KW:2 * GLA_KW] * q_scale
        vb = p_ref[0, rows, 2 * GLA_KW:2 * GLA_KW + GLA_VW].astype(BF16)
        cum = p_ref[0, rows, cum_off:cum_off + GLA_KW]
        mid = cum[c // 2:c // 2 + 1, :]
        total = cum[end_row:end_row + 1, :]
        st = st_ref[...]
        inter = _dot_nt((q * jnp.exp(cum)).astype(BF16), st.astype(BF16))
        qs = (q * jnp.exp(cum - mid)).astype(BF16)
        ks = (k * jnp.exp(mid - cum)).astype(BF16)
        s = _dot_nt(qs, _head_stack(ks, GLA_DK, GLA_HEADS))
        s = jnp.where(keep, s, 0.0).astype(BF16)
        intra = _dot(s, _head_stack(vb, GLA_DV, GLA_HEADS))
        acc_ref[rows, :] += inter + intra
        kd = (k * jnp.exp(total - cum)).astype(BF16)
        upd = _dot_tn(vb, kd)
        st_ref[...] = st * jnp.exp(total) + jnp.where(head_diag, upd, 0.0)

    def body(i, carry):
        one_direction(pl.multiple_of(i * c, c), lower, c - 1, 2 * GLA_KW + 2 * GLA_VW, sf_ref)
        one_direction(pl.multiple_of((nc - 1 - i) * c, c), upper, 0, 3 * GLA_KW + 2 * GLA_VW, sb_ref)
        return carry

    lax.fori_loop(0, nc, body, 0, unroll=2)
    sff_ref[0] = sf_ref[...]
    sfb_ref[0] = sb_ref[...]

    hr = lax.broadcasted_iota(jnp.int32, (GLA_VW, GLA_VW), 0) // GLA_DV
    hc = lax.broadcasted_iota(jnp.int32, (GLA_VW, GLA_VW), 1) // GLA_DV
    head_ones = (hr == hc).astype(BF16)
    blk = 256
    for j in range(t // blk):
        rows = pl.ds(j * blk, blk)
        o = acc_ref[rows, :]
        hi, mid, lo = _split3(o * o)
        ms = (_dot(hi, head_ones) + _dot(mid, head_ones) + _dot(lo, head_ones)) * (1.0 / GLA_DV)
        gate = p_ref[0, rows, 2 * GLA_KW + GLA_VW:2 * GLA_KW + 2 * GLA_VW]
        o_ref[0, rows, :] = (o * lax.rsqrt(ms + NORM_EPS) * gain_ref[...] * _silu(gate)).astype(BF16)


def _gla(p, s0f, s0b, gain):
    bsz, t, w = p.shape
    st_spec = pl.BlockSpec((1, GLA_VW, GLA_KW), lambda b: (b, 0, 0))
    st_shape = jax.ShapeDtypeStruct((bsz, GLA_VW, GLA_KW), F32)
    return pl.pallas_call(
        _gla_kernel,
        grid=(bsz,),
        in_specs=[
            pl.BlockSpec((1, t, w), lambda b: (b, 0, 0)),
            st_spec, st_spec,
            pl.BlockSpec((1, GLA_VW), lambda b: (0, 0)),
        ],
        out_specs=[pl.BlockSpec((1, t, GLA_VW), lambda b: (b, 0, 0)), st_spec, st_spec],
        out_shape=[jax.ShapeDtypeStruct((bsz, t, GLA_VW), BF16), st_shape, st_shape],
        scratch_shapes=[
            pltpu.VMEM((t, GLA_VW), F32),
            pltpu.VMEM((GLA_VW, GLA_KW), F32),
            pltpu.VMEM((GLA_VW, GLA_KW), F32),
        ],
        compiler_params=_cparams("parallel"),
        name="gla_bidir",
    )(p, s0f, s0b, gain)


def _head_stack(q, width, n):
    lane = lax.broadcasted_iota(jnp.int32, (1, q.shape[1]), 1) // width
    return jnp.concatenate([jnp.where(lane == h, q, jnp.zeros_like(q)) for h in range(n)], axis=0)


def _head_unstack(o, width, n):
    rows = o.shape[0] // n
    lane = lax.broadcasted_iota(jnp.int32, (1, o.shape[1]), 1) // width
    out = o[0:rows]
    for h in range(1, n):
        out = jnp.where(lane == h, o[h * rows:(h + 1) * rows], out)
    return out


NAT_ROWS_PER_STEP = 4


def _nat_kernel(lat_ref, ctx_ref, bias_ref, o_ref):
    n_rows = lat_ref.shape[1] // GRID_W
    kc = ctx_ref[0, :, NAT_W:2 * NAT_W]
    vc = ctx_ref[0, :, 2 * NAT_W:3 * NAT_W]
    n_loc = NAT_WIN_ROWS * GRID_W
    for i in range(NAT_ROWS_PER_STEP):
        r = pl.program_id(1) * NAT_ROWS_PER_STEP + i
        r0 = jnp.clip(r - NAT_WIN_ROWS // 2, 0, n_rows - NAT_WIN_ROWS)
        q = lat_ref[0, pl.ds(pl.multiple_of(r * GRID_W, GRID_W), GRID_W), 0:NAT_W]
        krows = pl.ds(pl.multiple_of(r0 * GRID_W, GRID_W), n_loc)
        qs = _head_stack(q, HEAD_DIM, NAT_HEADS)
        s = jnp.concatenate([_dot_nt(qs, lat_ref[0, krows, NAT_W:2 * NAT_W]), _dot_nt(qs, kc)], axis=1)
        s = s + bias_ref[r0 - r + NAT_WIN_ROWS - 1]
        p = jnp.exp2((s - jnp.max(s, axis=-1, keepdims=True)).astype(BF16))
        den = jnp.sum(p, axis=-1, keepdims=True, dtype=F32)
        o = (_dot(p[:, 0:n_loc], lat_ref[0, krows, 2 * NAT_W:3 * NAT_W]) + _dot(p[:, n_loc:], vc)) / den
        o_ref[0, i * GRID_W:(i + 1) * GRID_W, :] = _head_unstack(o, HEAD_DIM, NAT_HEADS).astype(BF16)


def _nat(lat, ctx, bias):
    bsz, t, w = lat.shape
    m = ctx.shape[1]
    rows = NAT_ROWS_PER_STEP * GRID_W
    return pl.pallas_call(
        _nat_kernel,
        grid=(bsz, t // rows),
        in_specs=[
            pl.BlockSpec((1, t, w), lambda b, r: (b, 0, 0)),
            pl.BlockSpec((1, m, w), lambda b, r: (b, 0, 0)),
            pl.BlockSpec(bias.shape, lambda b, r: (0, 0, 0)),
        ],
        out_specs=pl.BlockSpec((1, rows, NAT_W), lambda b, r: (b, r, 0)),
        out_shape=jax.ShapeDtypeStruct((bsz, t, NAT_W), BF16),
        compiler_params=_cparams("parallel", "arbitrary"),
        name="nat_attn",
    )(lat, ctx, bias)


def _nat_bias_table(rpb, m):
    cols = np.arange(GRID_W)
    col_start = np.clip(cols - NAT_WIN_COLS // 2, 0, GRID_W - NAT_WIN_COLS)
    kc = np.arange(GRID_W)
    inside = (kc[None, :] >= col_start[:, None]) & (kc[None, :] < col_start[:, None] + NAT_WIN_COLS)
    rel = kc[None, :] - cols[:, None] + NAT_WIN_COLS - 1
    col_sel = (rel[:, :, None] == np.arange(2 * NAT_WIN_COLS - 1)) & inside[:, :, None]
    ridx = np.arange(NAT_WIN_ROWS)[:, None] + np.arange(NAT_WIN_ROWS)[None, :]
    row_sel = ridx[:, :, None] == np.arange(2 * NAT_WIN_ROWS - 1)
    tab = jnp.einsum("hab,oia,ckb->ohcik", rpb.astype(F32), jnp.asarray(row_sel, F32), jnp.asarray(col_sel, F32),
                     precision=lax.Precision.HIGHEST)
    tab = jnp.where(jnp.asarray(inside)[None, None, :, None, :], tab * LOG2E, NEG_BIG)
    tab = tab.reshape(NAT_WIN_ROWS, NAT_HEADS * GRID_W, NAT_WIN_ROWS * GRID_W)
    return jnp.concatenate([tab, jnp.zeros(tab.shape[:2] + (m,), F32)], axis=2)


def _sink_column(sink_ref, first_head, n_heads, rows):
    return jnp.concatenate(
        [jnp.broadcast_to(sink_ref[first_head + j:first_head + j + 1, 0:1], (rows, 1)) for j in range(n_heads)], axis=0)


def _softmax_pv_sink(s, v, sink):
    m = jnp.maximum(jnp.max(s, axis=-1, keepdims=True), sink)
    p = jnp.exp2((s - m).astype(BF16))
    o = _dot(p, jnp.concatenate([v, jnp.ones_like(v)], axis=1))
    w = v.shape[1]
    return o[:, 0:w] / (o[:, w:] + jnp.exp2(sink - m))


SWA_BLOCKS_PER_STEP = 4


def _swa_kernel(lat_ref, ctx_ref, sink_ref, o_ref):
    t = lat_ref.shape[1]
    m = ctx_ref.shape[1]
    blk = SWA_BLOCK
    n_loc = 3 * blk
    qi = lax.broadcasted_iota(jnp.int32, (blk, n_loc), 0)
    kj = lax.broadcasted_iota(jnp.int32, (blk, n_loc), 1)
    group = SWA_Q_HEADS // SWA_KV_HEADS
    for a in range(SWA_BLOCKS_PER_STEP):
        n = pl.program_id(1) * SWA_BLOCKS_PER_STEP + a
        base = pl.multiple_of(jnp.clip((n - 1) * blk, 0, t - n_loc), blk)
        qrows = pl.ds(pl.multiple_of(n * blk, blk), blk)
        krows = pl.ds(base, n_loc)
        mask_add = jnp.where(jnp.abs(base + kj - n * blk - qi) <= SWA_WINDOW, 0.0, NEG_BIG)[None]
        for g in range(SWA_KV_HEADS):
            kcol = pl.ds(SWA_QW + g * LANES, LANES)
            vcol = pl.ds(SWA_QW + 2 * SWA_KW + g * LANES, LANES)
            k = jnp.concatenate([lat_ref[0, krows, kcol], ctx_ref[0, :, kcol]], axis=0)
            v = jnp.concatenate([lat_ref[0, krows, vcol], ctx_ref[0, :, vcol]], axis=0)
            qs = jnp.concatenate(
                [_head_stack(lat_ref[0, qrows, pl.ds((g * 2 + p) * LANES, LANES)], HEAD_DIM, 2) for p in range(2)],
                axis=0)
            s = _dot_nt(qs, k).reshape(group, blk, n_loc + m)
            s = jnp.concatenate([s[:, :, 0:n_loc] + mask_add, s[:, :, n_loc:]], axis=2).reshape(group * blk, n_loc + m)
            o = _softmax_pv_sink(s, v, _sink_column(sink_ref, g * group, group, blk))
            for p in range(2):
                pair = _head_unstack(o[2 * p * blk:(2 * p + 2) * blk], HEAD_DIM, 2)
                o_ref[0, a * blk:(a + 1) * blk, pl.ds((g * 2 + p) * LANES, LANES)] = pair.astype(BF16)


def _swa(lat, ctx, sink_tab):
    bsz, t, w = lat.shape
    m = ctx.shape[1]
    rows = SWA_BLOCKS_PER_STEP * SWA_BLOCK
    return pl.pallas_call(
        _swa_kernel,
        grid=(bsz, t // rows),
        in_specs=[
            pl.BlockSpec((1, t, w), lambda b, n: (b, 0, 0)),
            pl.BlockSpec((1, m, w), lambda b, n: (b, 0, 0)),
            pl.BlockSpec((SWA_Q_HEADS, LANES), lambda b, n: (0, 0)),
        ],
        out_specs=pl.BlockSpec((1, rows, SWA_QW), lambda b, n: (b, n, 0)),
        out_shape=jax.ShapeDtypeStruct((bsz, t, SWA_QW), BF16),
        compiler_params=_cparams("parallel", "arbitrary"),
        name="swa_attn",
    )(lat, ctx, sink_tab)


def _ctx_attn_kernel(nat_ref, swa_ref, sink_ref, on_ref, os_ref):
    m = nat_ref.shape[1]
    q = nat_ref[0, :, 0:NAT_W]
    k = nat_ref[0, :, NAT_W:2 * NAT_W]
    v = nat_ref[0, :, 2 * NAT_W:3 * NAT_W]
    s = _dot_nt(_head_stack(q, HEAD_DIM, NAT_HEADS), k)
    p = jnp.exp2((s - jnp.max(s, axis=-1, keepdims=True)).astype(BF16))
    o = _dot(p, v) / jnp.sum(p, axis=-1, keepdims=True, dtype=F32)
    on_ref[0] = _head_unstack(o, HEAD_DIM, NAT_HEADS).astype(BF16)

    group = SWA_Q_HEADS // SWA_KV_HEADS
    for g in range(SWA_KV_HEADS):
        k = swa_ref[0, :, pl.ds(SWA_QW + g * LANES, LANES)]
        v = swa_ref[0, :, pl.ds(SWA_QW + 2 * SWA_KW + g * LANES, LANES)]
        qs = jnp.concatenate(
            [_head_stack(swa_ref[0, :, pl.ds((g * 2 + p) * LANES, LANES)], HEAD_DIM, 2) for p in range(2)], axis=0)
        o = _softmax_pv_sink(_dot_nt(qs, k), v, _sink_column(sink_ref, g * group, group, m))
        for pr in range(2):
            pair = _head_unstack(o[2 * pr * m:(2 * pr + 2) * m], HEAD_DIM, 2)
            os_ref[0, :, pl.ds((g * 2 + pr) * LANES, LANES)] = pair.astype(BF16)


def _ctx_attn(nat_c, swa_c, sink_tab):
    bsz, m, _ = nat_c.shape
    return pl.pallas_call(
        _ctx_attn_kernel,
        grid=(bsz,),
        in_specs=[
            pl.BlockSpec((1, m, NAT_PROJ_W), lambda b: (b, 0, 0)),
            pl.BlockSpec((1, m, SWA_PROJ_W), lambda b: (b, 0, 0)),
            pl.BlockSpec((SWA_Q_HEADS, LANES), lambda b: (0, 0)),
        ],
        out_specs=[
            pl.BlockSpec((1, m, NAT_W), lambda b: (b, 0, 0)),
            pl.BlockSpec((1, m, SWA_QW), lambda b: (b, 0, 0)),
        ],
        out_shape=[
            jax.ShapeDtypeStruct((bsz, m, NAT_W), BF16),
            jax.ShapeDtypeStruct((bsz, m, SWA_QW), BF16),
        ],
        compiler_params=_cparams("parallel"),
        name="ctx_attn",
    )(nat_c, swa_c, sink_tab)


def _out_proj_kernel(gla_ref, nat_ref, swa_ref, x_ref, mod_ref, gain_ref, w_ref, wr_ref,
                     x1_ref, h2_ref, aff_ref, afft_ref):
    tm = x_ref.shape[1]
    half = tm // 2
    for part in range(2):
        rows = pl.ds(part * half, half)
        mix = (_dot(gla_ref[0, rows, :], w_ref[0:GLA_VW, :])
               + _dot(nat_ref[0, rows, :], w_ref[GLA_VW:GLA_VW + NAT_W, :])
               + _dot(swa_ref[0, rows, :], w_ref[GLA_VW + NAT_W:, :]))
        x1 = x_ref[0, rows, :] + mod_ref[0, 2:3, :] * _rms(mix, gain_ref[1:2, :])
        x1_ref[0, rows, :] = x1
        h2 = (_rms(x1, gain_ref[2:3, :]) * (1.0 + mod_ref[0, 4:5, :]) + mod_ref[0, 3:4, :]).astype(BF16)
        h2_ref[0, rows, :] = h2
        logits = _dot(h2, wr_ref[...])
        lane = lax.broadcasted_iota(jnp.int32, logits.shape, 1)
        logits = jnp.where(lane < N_EXPERTS, logits, NEG_BIG)
        e = jnp.exp(logits - jnp.max(logits, axis=-1, keepdims=True))
        aff = e / jnp.sum(e, axis=-1, keepdims=True)
        aff_ref[0, rows, :] = aff
        afft_ref[0, :, rows] = aff.T[0:N_EXPERTS, :]


def _out_proj(gla_o, nat_o, swa_o, x, mod, gains, w_out, wr, tm):
    bsz, t, d = x.shape
    tok = lambda b, i: (b, i, 0)
    const = lambda b, i: (0, 0)
    return pl.pallas_call(
        _out_proj_kernel,
        grid=(bsz, t // tm),
        in_specs=[
            pl.BlockSpec((1, tm, GLA_VW), tok),
            pl.BlockSpec((1, tm, NAT_W), tok),
            pl.BlockSpec((1, tm, SWA_QW), tok),
            pl.BlockSpec((1, tm, d), tok),
            pl.BlockSpec((1, 6, d), lambda b, i: (b, 0, 0)),
            pl.BlockSpec((4, d), const),
            pl.BlockSpec((d, d), const),
            pl.BlockSpec((d, LANES), const),
        ],
        out_specs=[
            pl.BlockSpec((1, tm, d), tok),
            pl.BlockSpec((1, tm, d), tok),
            pl.BlockSpec((1, tm, LANES), tok),
            pl.BlockSpec((1, N_EXPERTS, tm), lambda b, i: (b, 0, i)),
        ],
        out_shape=[
            jax.ShapeDtypeStruct((bsz, t, d), F32),
            jax.ShapeDtypeStruct((bsz, t, d), BF16),
            jax.ShapeDtypeStruct((bsz, t, LANES), F32),
            jax.ShapeDtypeStruct((bsz, N_EXPERTS, t), F32),
        ],
        compiler_params=_cparams("parallel", "parallel"),
        name="out_proj_router",
    )(gla_o, nat_o, swa_o, x, mod, gains, w_out, wr)


def _excl_prefix(x):
    rows, n = x.shape
    nblk = n // LANES
    r = lax.broadcasted_iota(jnp.int32, (LANES, LANES), 0)
    c = lax.broadcasted_iota(jnp.int32, (LANES, LANES), 1)
    strict_upper = (r < c).astype(BF16)
    stacked = jnp.concatenate([x[:, j * LANES:(j + 1) * LANES] for j in range(nblk)], axis=0).astype(BF16)
    local = _dot(stacked, strict_upper)
    totals = jnp.sum(stacked.astype(F32), axis=-1, keepdims=True)
    out = []
    offs = [jnp.zeros((rows, 1), F32)]
    for j in range(nblk):
        out.append(local[j * rows:(j + 1) * rows] + offs[-1])
        offs.append(offs[-1] + totals[j * rows:(j + 1) * rows])
    return jnp.concatenate(out, axis=1), offs


def _topc_kernel(aff_ref, rank_ref, rankcol_ref, cnt_ref, *, cap):
    aff = aff_ref[...]

    def step(i, thr):
        cand = thr | (jnp.int32(1) << (30 - i))
        cnt = jnp.sum((aff >= pltpu.bitcast(cand, F32)).astype(F32), axis=-1, keepdims=True)
        return jnp.where(cnt >= cap, cand, thr)

    thr = lax.fori_loop(0, 31, step, jnp.zeros((aff.shape[0], 1), jnp.int32))
    thr = pltpu.bitcast(thr, F32)
    gt = aff > thr
    eq = aff == thr
    need = cap - jnp.sum(gt.astype(F32), axis=-1, keepdims=True)
    eq_before, _ = _excl_prefix(eq.astype(F32))
    sel = gt | (eq & (eq_before < need))
    sel_before, offs = _excl_prefix(sel.astype(F32))
    rank = jnp.where(sel, sel_before, -1.0)
    rank_ref[...] = rank
    lane = lax.broadcasted_iota(jnp.int32, (rank.shape[0], LANES), 1)
    cnt = jnp.zeros((rank.shape[0], LANES), F32)
    for j, off in enumerate(offs):
        cnt = jnp.where(lane == j, off, cnt)
    cnt_ref[...] = cnt.astype(jnp.int32)
    n = rank.shape[1]
    fill = jnp.full((LANES - N_EXPERTS, n), -1.0, F32)
    for b in range(rankcol_ref.shape[0]):
        rankcol_ref[b] = jnp.concatenate([rank[b * N_EXPERTS:(b + 1) * N_EXPERTS], fill], axis=0).T


def _topc(aff_t, cap):
    bsz, e, n = aff_t.shape
    rows = bsz * e
    rank, rankcol, cnt = pl.pallas_call(
        functools.partial(_topc_kernel, cap=cap),
        grid=(1,),
        in_specs=[pl.BlockSpec((rows, n), lambda i: (0, 0))],
        out_specs=[pl.BlockSpec((rows, n), lambda i: (0, 0)), pl.BlockSpec((bsz, n, LANES), lambda i: (0, 0, 0)),
                   pl.BlockSpec((rows, LANES), lambda i: (0, 0))],
        out_shape=[jax.ShapeDtypeStruct((rows, n), F32), jax.ShapeDtypeStruct((bsz, n, LANES), F32),
                   jax.ShapeDtypeStruct((rows, LANES), jnp.int32)],
        compiler_params=_cparams("arbitrary"),
        name="expert_topc",
    )(aff_t.reshape(rows, n))
    return rank.reshape(bsz, e, n), rankcol, cnt.reshape(bsz, e, LANES)


def _gather_kernel(h_ref, rank_ref, xs_ref, *, cap):
    n = h_ref.shape[1]
    slot = lax.broadcasted_iota(jnp.int32, (cap, n), 0).astype(F32)
    h = h_ref[0]
    for e in range(N_EXPERTS):
        onehot = (slot == rank_ref[0, e:e + 1, :]).astype(BF16)
        xs_ref[e] = _dot(onehot, h).astype(BF16)


def _gather(h, rank, cap):
    bsz, n, d = h.shape
    return pl.pallas_call(
        functools.partial(_gather_kernel, cap=cap),
        grid=(bsz,),
        in_specs=[
            pl.BlockSpec((1, n, d), lambda b: (b, 0, 0)),
            pl.BlockSpec((1, N_EXPERTS, n), lambda b: (b, 0, 0)),
        ],
        out_specs=pl.BlockSpec((N_EXPERTS, cap, d), lambda b: (0, b, 0)),
        out_shape=jax.ShapeDtypeStruct((N_EXPERTS, bsz * cap, d), BF16),
        compiler_params=_cparams("parallel"),
        name="moe_gather",
    )(h, rank)


GATHER_TILE = 256
GATHER_WIN = 5
GATHER_SLOTS = 128


def _gather_win_kernel(cnt_ref, h_ref, rank_ref, xs_ref, *, cap):
    b = pl.program_id(0)
    n = h_ref.shape[1]
    nt = n // GATHER_TILE
    win = GATHER_WIN * GATHER_TILE
    slot_w = lax.broadcasted_iota(jnp.int32, (GATHER_SLOTS, win), 0).astype(F32).astype(BF16)
    slot_f = lax.broadcasted_iota(jnp.int32, (cap, n), 0).astype(F32).astype(BF16)
    one = jnp.ones((), BF16)
    zero = jnp.zeros((), BF16)

    n_sb = cap // GATHER_SLOTS
    group = 2

    def per_group(g, carry):
        starts, fits = [], []
        for j in range(group):
            base = (g * group + j) * (nt + 1)
            c = [cnt_ref[b, base + kt] for kt in range(nt + 1)]
            for sb in range(n_sb):
                lo_slot = sb * GATHER_SLOTS
                first = sum((c[kt + 1] <= lo_slot).astype(jnp.int32) for kt in range(nt))
                last = sum((c[kt] < lo_slot + GATHER_SLOTS).astype(jnp.int32) for kt in range(nt))
                fits.append(last - first <= GATHER_WIN)
                starts.append(jnp.minimum(first, nt - GATHER_WIN))
        all_fit = functools.reduce(jnp.logical_and, fits)

        @pl.when(all_fit)
        def _():
            for j in range(group):
                e = g * group + j
                for sb in range(n_sb):
                    start = starts[j * n_sb + sb]
                    rk = jnp.concatenate([rank_ref[0, e, pl.ds(start + i, 1), :] for i in range(GATHER_WIN)], axis=1)
                    onehot = jnp.where(slot_w == (rk - sb * GATHER_SLOTS).astype(BF16), one, zero)
                    hw = h_ref[0, pl.ds(pl.multiple_of(start * GATHER_TILE, GATHER_TILE), win), :]
                    xs_ref[e, pl.ds(sb * GATHER_SLOTS, GATHER_SLOTS), :] = _dot(onehot, hw).astype(BF16)

        @pl.when(jnp.logical_not(all_fit))
        def _():
            for j in range(group):
                e = g * group + j
                rk = jnp.concatenate([rank_ref[0, e, kt:kt + 1, :] for kt in range(nt)], axis=1).astype(BF16)
                onehot = jnp.where(slot_f == rk, one, zero)
                xs_ref[e] = _dot(onehot, h_ref[0]).astype(BF16)
        return carry

    lax.fori_loop(0, N_EXPERTS // group, per_group, 0)


def _gather_win(h, rank, cnt, cap):
    bsz, n, d = h.shape
    nt = n // GATHER_TILE
    step = GATHER_TILE // LANES
    bounds = cnt[:, :, 0:nt * step + 1:step].reshape(bsz, N_EXPERTS * (nt + 1))
    return pl.pallas_call(
        functools.partial(_gather_win_kernel, cap=cap),
        grid_spec=pltpu.PrefetchScalarGridSpec(
            num_scalar_prefetch=1,
            grid=(bsz,),
            in_specs=[
                pl.BlockSpec((1, n, d), lambda b, c: (b, 0, 0)),
                pl.BlockSpec((1, N_EXPERTS, nt, GATHER_TILE), lambda b, c: (b, 0, 0, 0)),
            ],
            out_specs=pl.BlockSpec((N_EXPERTS, cap, d), lambda b, c: (0, b, 0)),
        ),
        out_shape=jax.ShapeDtypeStruct((N_EXPERTS, bsz * cap, d), BF16),
        compiler_params=_cparams("parallel"),
        name="moe_gather_win",
    )(bounds, h, rank.reshape(bsz, N_EXPERTS, nt, GATHER_TILE))


FFN_PARTS = 4


def _ffn_kernel(*refs, has_ctx):
    if has_ctx:
        xs_ref, xc_ref, wg_ref, wu_ref, wd_ref, y_ref, yc_ref, wg_bf, wu_bf, wd_bf = refs
    else:
        xs_ref, wg_ref, wu_ref, wd_ref, y_ref, wg_bf, wu_bf, wd_bf = refs
    s = pl.program_id(0)
    j = pl.program_id(1)
    n_exp = pl.num_programs(0) - 1
    piece = wg_ref.shape[2]

    @pl.when((s < n_exp) & (j < FFN_PARTS))
    def _():
        rows = pl.ds(pl.multiple_of(j * piece, piece), piece)
        wg_bf[s % 2, rows, :] = wg_ref[0, 0].astype(BF16)
        wu_bf[s % 2, rows, :] = wu_ref[0, 0].astype(BF16)
        wd_bf[s % 2, rows, :] = wd_ref[0, 0].astype(BF16)

    def swiglu(x_ref, o_ref):
        cur = (s + 1) % 2
        xs = x_ref[0]
        hid = (_silu(_dot(xs, wg_bf[cur])) * _dot(xs, wu_bf[cur])).astype(BF16)
        o_ref[0] = _dot(hid, wd_bf[cur]).astype(BF16)

    @pl.when((s > 0) & (j < FFN_PARTS))
    def _():
        swiglu(xs_ref, y_ref)

    if has_ctx:
        @pl.when((s > 0) & (j == FFN_PARTS))
        def _():
            swiglu(xc_ref, yc_ref)


def _ffn(xs, xc, w_gate, w_up, w_down, layer):
    _, e, d, ff = w_gate.shape
    rb = xs.shape[1] // FFN_PARTS
    has_ctx = xc is not None
    last = FFN_PARTS - 1

    def x_idx(s, j):
        return (jnp.maximum(s - 1, 0), jnp.where(s > 0, jnp.minimum(j, last), 0), 0)

    def w_idx(s, j):
        return (layer, jnp.minimum(s, e - 1), jnp.minimum(j, last), 0)

    xspec = pl.BlockSpec((1, rb, d), x_idx)
    cspecs, cargs, cshapes = [], [], []
    if has_ctx:
        cspecs = [pl.BlockSpec((1, xc.shape[1], d), lambda s, j: (jnp.maximum(s - 1, 0), 0, 0))]
        cargs = [xc]
        cshapes = [jax.ShapeDtypeStruct(xc.shape, BF16)]
    out = pl.pallas_call(
        functools.partial(_ffn_kernel, has_ctx=has_ctx),
        grid=(e + 1, FFN_PARTS + int(has_ctx)),
        in_specs=[xspec] + cspecs + [
            pl.BlockSpec((1, 1, d // FFN_PARTS, ff), w_idx),
            pl.BlockSpec((1, 1, d // FFN_PARTS, ff), w_idx),
            pl.BlockSpec((1, 1, ff // FFN_PARTS, d), w_idx),
        ],
        out_specs=[xspec] + cspecs,
        out_shape=[jax.ShapeDtypeStruct(xs.shape, BF16)] + cshapes,
        scratch_shapes=[pltpu.VMEM((2, d, ff), BF16), pltpu.VMEM((2, d, ff), BF16), pltpu.VMEM((2, ff, d), BF16)],
        compiler_params=_cparams("arbitrary", "arbitrary"),
        name="moe_ffn",
    )(xs, *cargs, w_gate, w_up, w_down)
    return out


def _combine_kernel(y_ref, rankcol_ref, aff_ref, x_ref, mod_ref, gain_ref, o_ref, *, cap):
    rb = x_ref.shape[1]
    slot = lax.broadcasted_iota(jnp.int32, (rb, cap), 1).astype(F32)
    rc = rankcol_ref[0]
    af = aff_ref[0]
    acc = jnp.zeros((rb, x_ref.shape[2]), F32)
    for e in range(N_EXPERTS):
        onehot = (rc[:, e:e + 1] == slot).astype(BF16)
        acc = acc + af[:, e:e + 1] * _dot(onehot, y_ref[e])
    o_ref[0] = x_ref[0] + mod_ref[0, 5:6, :] * _rms(acc, gain_ref[3:4, :])


def _combine(y, rankcol, aff, x, mod, gains, cap):
    bsz, n, d = x.shape
    rb = min(n, 512)
    tok = lambda b, i: (b, i, 0)
    return pl.pallas_call(
        functools.partial(_combine_kernel, cap=cap),
        grid=(bsz, n // rb),
        in_specs=[
            pl.BlockSpec((N_EXPERTS, cap, d), lambda b, i: (0, b, 0)),
            pl.BlockSpec((1, rb, LANES), tok),
            pl.BlockSpec((1, rb, LANES), tok),
            pl.BlockSpec((1, rb, d), tok),
            pl.BlockSpec((1, 6, d), lambda b, i: (b, 0, 0)),
            pl.BlockSpec((4, d), lambda b, i: (0, 0)),
        ],
        out_specs=pl.BlockSpec((1, rb, d), tok),
        out_shape=jax.ShapeDtypeStruct((bsz, n, d), F32),
        compiler_params=_cparams("parallel", "arbitrary"),
        name="moe_combine",
    )(y, rankcol, aff, x, mod, gains)


COMBINE_ROWS = 512
COMBINE_WIN = 128


def _combine_pair_kernel(cnt_ref, y_ref, rankcol_ref, aff_ref, x_ref, mod_ref, gain_ref, o_ref, extra_ref, *, cap):
    b = pl.program_id(0)
    i = pl.program_id(1)
    nrb = pl.num_programs(1)
    rb = x_ref.shape[1]
    slot_w = lax.broadcasted_iota(jnp.int32, (rb, COMBINE_WIN), 1).astype(F32)
    rc = rankcol_ref[0]
    af = aff_ref[0]
    los, unfit = [], []
    for e in range(N_EXPERTS):
        c0 = cnt_ref[b, e * (nrb + 1) + i]
        c1 = cnt_ref[b, e * (nrb + 1) + i + 1]
        lo = jnp.minimum((c0 // 16) * 16, cap - COMBINE_WIN)
        los.append(pl.multiple_of(lo, 16))
        unfit.append(c1 - lo > COMBINE_WIN)
    acc = jnp.zeros((rb, x_ref.shape[2]), F32)
    for e0 in range(0, N_EXPERTS, 2):
        hot, rows = [], []
        for e in (e0, e0 + 1):
            hot.append(jnp.where(rc[:, e:e + 1] - los[e].astype(F32) == slot_w, af[:, e:e + 1], 0.0).astype(BF16))
            rows.append(y_ref[e, pl.ds(los[e], COMBINE_WIN), :])
        acc = acc + _dot(jnp.concatenate(hot, axis=1), jnp.concatenate(rows, axis=0))

    def finish(total):
        o_ref[0] = x_ref[0] + mod_ref[0, 5:6, :] * _rms(total, gain_ref[3:4, :])

    any_unfit = functools.reduce(jnp.logical_or, unfit)

    @pl.when(jnp.logical_not(any_unfit))
    def _():
        finish(acc)

    @pl.when(any_unfit)
    def _():
        extra_ref[...] = jnp.zeros_like(extra_ref)
        slot_f = lax.broadcasted_iota(jnp.int32, (rb, cap), 1)
        for e in range(N_EXPERTS):
            @pl.when(unfit[e])
            def _():
                outside = (slot_f < los[e]) | (slot_f >= los[e] + COMBINE_WIN)
                hot = jnp.where((rc[:, e:e + 1] == slot_f.astype(F32)) & outside, af[:, e:e + 1], 0.0).astype(BF16)
                extra_ref[...] += _dot(hot, y_ref[e])
        finish(acc + extra_ref[...])


def _combine_pair(y, rankcol, aff, cnt, x, mod, gains, cap):
    bsz, n, d = x.shape
    rb = COMBINE_ROWS
    nrb = n // rb
    step = rb // LANES
    bounds = cnt[:, :, 0:nrb * step + 1:step].reshape(bsz, N_EXPERTS * (nrb + 1))
    tok = lambda b, i, c: (b, i, 0)
    return pl.pallas_call(
        functools.partial(_combine_pair_kernel, cap=cap),
        grid_spec=pltpu.PrefetchScalarGridSpec(
            num_scalar_prefetch=1,
            grid=(bsz, nrb),
            in_specs=[
                pl.BlockSpec((N_EXPERTS, cap, d), lambda b, i, c: (0, b, 0)),
                pl.BlockSpec((1, rb, LANES), tok),
                pl.BlockSpec((1, rb, LANES), tok),
                pl.BlockSpec((1, rb, d), tok),
                pl.BlockSpec((1, 6, d), lambda b, i, c: (b, 0, 0)),
                pl.BlockSpec((4, d), lambda b, i, c: (0, 0)),
            ],
            out_specs=pl.BlockSpec((1, rb, d), tok),
            scratch_shapes=[pltpu.VMEM((rb, d), F32)],
        ),
        out_shape=jax.ShapeDtypeStruct((bsz, n, d), F32),
        compiler_params=_cparams("parallel", "arbitrary"),
        name="moe_combine_pair",
    )(bounds, y, rankcol, aff, x, mod, gains)


def _relayout_w_in(w):
    widths = (GLA_KW, GLA_VW, GLA_RANK, GLA_RANK, NAT_W, NAT_W, SWA_KW, SWA_KW, GLA_KW, GLA_VW, NAT_W, SWA_QW)
    gk, gv, gaf, gab, nk, nv, sk, sv, gq, gg, nq, sq = jnp.split(w, np.cumsum(widths)[:-1].tolist(), axis=1)
    pad = jnp.zeros((w.shape[0], LANES - 2 * GLA_RANK), w.dtype)
    sk0, sk1 = sk[:, :HEAD_DIM], sk[:, HEAD_DIM:]
    sv0, sv1 = sv[:, :HEAD_DIM], sv[:, HEAD_DIM:]
    cols = [gk, gq, gv, gg, gaf, gab, pad, nq, nk, nv, sq, sk0, sk0, sk1, sk1, sv0, sv0, sv1, sv1]
    return jnp.concatenate(cols, axis=1).astype(BF16)


def _rope_tables(t):
    half = HEAD_DIM // 4
    freqs = ROPE_BASE ** (-np.arange(half, dtype=np.float32) / half)
    pos = np.arange(t)
    ang_r = (pos // GRID_W).astype(np.float32)[:, None] * freqs
    ang_c = (pos % GRID_W).astype(np.float32)[:, None] * freqs
    cos = np.concatenate([np.cos(ang_r), np.cos(ang_r), np.cos(ang_c), np.cos(ang_c)], axis=1)
    sin = np.concatenate([-np.sin(ang_r), np.sin(ang_r), -np.sin(ang_c), np.sin(ang_c)], axis=1)
    reps = LANES // HEAD_DIM
    return jnp.asarray(np.tile(cos, (1, reps)), F32), jnp.asarray(np.tile(sin, (1, reps)), F32)


def _moe(h_list, afft_list, aff_list, x_list, mod_list, gains, w_gate, w_up, w_down, layer):
    caps = [CAPACITY_FACTOR * h.shape[1] // N_EXPERTS for h in h_list]
    ranks = [_topc(a, cap) for a, cap in zip(afft_list, caps)]
    long = [h.shape[1] >= GATHER_WIN * GATHER_TILE and h.shape[1] % COMBINE_ROWS == 0 and cap >= 2 * COMBINE_WIN
            for h, cap in zip(h_list, caps)]
    xs = [_gather_win(h, r[0], r[2], cap) if lg else _gather(h, r[0], cap)
          for h, r, cap, lg in zip(h_list, ranks, caps, long)]
    ys = _ffn(xs[0], xs[1] if len(xs) > 1 else None, w_gate, w_up, w_down, layer)
    return [_combine_pair(y, r[1], aff, r[2], x, mod, gains, cap) if lg else _combine(y, r[1], aff, x, mod, gains, cap)
            for y, r, aff, x, mod, cap, lg in zip(ys, ranks, aff_list, x_list, mod_list, caps, long)]


def kernel(x, c, ctx, c_ctx, w_mod, b_mod, norm_gains, w_in, w_out, gla_a_up, gla_a_bias, gla_norm,
           nat_rpb, swa_sink, w_router, w_gate, w_up, w_down):
    bsz, t, d = x.shape
    m = ctx.shape[1]
    depth = w_mod.shape[0]

    cc = jnp.concatenate([c, c_ctx[None], jnp.zeros((16 - bsz - 1, d), F32)], axis=0)
    mod_all = _modulation(cc, w_mod, b_mod)
    rope_tabs = _rope_tables(t)
    zero_state = jnp.zeros((bsz, GLA_VW, GLA_KW), F32)

    xc = ctx
    for l in range(depth):
        update_ctx = l < depth - 1
        mod = mod_all[l, :bsz].reshape(bsz, 6, d)
        mod_c = jnp.broadcast_to(mod_all[l, bsz].reshape(1, 6, d), (bsz, 6, d))
        gains = norm_gains[l]
        w_in_l = _relayout_w_in(w_in[l])
        aup = jnp.zeros((LANES, 2 * GLA_KW), F32)
        aup = aup.at[0:GLA_RANK, 0:GLA_KW].set(gla_a_up[l, 0]).at[GLA_RANK:2 * GLA_RANK, GLA_KW:].set(gla_a_up[l, 1])
        abias = gla_a_bias[l].reshape(1, 2 * GLA_KW)
        gla_gain = jnp.tile(gla_norm[l], GLA_HEADS).reshape(1, GLA_VW)
        sink_tab = jnp.broadcast_to(swa_sink[l][:, None] * LOG2E, (SWA_Q_HEADS, LANES))
        w_out_l = w_out[l].astype(BF16)
        wr = jnp.pad(w_router[l], ((0, 0), (0, LANES - N_EXPERTS))).astype(BF16)

        gla_p, nat_p, swa_p = _in_proj(x, mod, gains, w_in_l, aup, abias, rope_tabs, 512)
        gla_c, nat_c, swa_c = _in_proj(xc, mod_c, gains, w_in_l, aup, abias, None, m)

        gla_co, s_f, s_b = _gla(gla_c, zero_state, zero_state, gla_gain)
        gla_o, _, _ = _gla(gla_p, s_f, s_b, gla_gain)
        nat_o = _nat(nat_p, nat_c, _nat_bias_table(nat_rpb[l], m))
        swa_o = _swa(swa_p, swa_c, sink_tab)

        x1, h2, aff, aff_t = _out_proj(gla_o, nat_o, swa_o, x, mod, gains, w_out_l, wr, 512)
        if update_ctx:
            nat_co, swa_co = _ctx_attn(nat_c, swa_c, sink_tab)
            xc1, hc2, aff_c, aff_ct = _out_proj(gla_co, nat_co, swa_co, xc, mod_c, gains, w_out_l, wr, m)
            x, xc = _moe([h2, hc2], [aff_t, aff_ct], [aff, aff_c], [x1, xc1], [mod, mod_c], gains,
                         w_gate, w_up, w_down, l)
        else:
            (x,) = _moe([h2], [aff_t], [aff], [x1], [mod], gains, w_gate, w_up, w_down, l)
    return x
```

```python
import functools

import jax
import jax.numpy as jnp
import numpy as np
from jax import lax
from jax.experimental import pallas as pl
from jax.experimental.pallas import tpu as pltpu

F32 = jnp.float32
BF16 = jnp.bfloat16

D_MODEL = 1024
GRID_W = 64
HEAD_DIM = 64
GLA_HEADS = 4
GLA_DK = 32
GLA_DV = 64
GLA_RANK = 16
GLA_TAU = 16.0
GLA_CHUNK = 64
NAT_HEADS = 4
NAT_WIN_ROWS = 8
NAT_WIN_COLS = 16
SWA_Q_HEADS = 8
SWA_KV_HEADS = 2
SWA_WINDOW = 128
SWA_BLOCK = 128
ROPE_BASE = 10000.0
N_EXPERTS = 16
EXPERT_FF = 1024
CAPACITY_FACTOR = 2
NORM_EPS = 1e-6

GLA_KW = GLA_HEADS * GLA_DK
GLA_VW = GLA_HEADS * GLA_DV
NAT_W = NAT_HEADS * HEAD_DIM
SWA_QW = SWA_Q_HEADS * HEAD_DIM
SWA_KW = SWA_KV_HEADS * HEAD_DIM
LANES = 128
NEG_BIG = -1e30
LOG2E = 1.4426950408889634

GLA_PROJ_W = 2 * GLA_KW + 2 * GLA_VW + LANES
GLA_OUT_W = 2 * GLA_KW + 2 * GLA_VW + 2 * GLA_KW
NAT_PROJ_W = 3 * NAT_W
SWA_PROJ_W = SWA_QW + 4 * SWA_KW
IN_PROJ_W = GLA_PROJ_W + NAT_PROJ_W + SWA_PROJ_W

VMEM_LIMIT = 56 * 1024 * 1024


def _cparams(*sem):
    return pltpu.CompilerParams(dimension_semantics=sem, vmem_limit_bytes=VMEM_LIMIT)


def _dot(a, b):
    return jnp.dot(a, b, preferred_element_type=F32)


def _dot_nt(a, b):
    return lax.dot_general(a, b, (((1,), (1,)), ((), ())), preferred_element_type=F32)


def _dot_tn(a, b):
    return lax.dot_general(a, b, (((0,), (0,)), ((), ())), preferred_element_type=F32)


def _split2(a):
    hi = a.astype(BF16)
    lo = (a - hi.astype(F32)).astype(BF16)
    return hi, lo


def _split3(a):
    hi = a.astype(BF16)
    r = a - hi.astype(F32)
    mid = r.astype(BF16)
    lo = (r - mid.astype(F32)).astype(BF16)
    return hi, mid, lo


def _dot_f32(a, b):
    ah, al = _split2(a)
    bh, bl = _split2(b)
    return _dot(ah, bh) + _dot(al, bh) + _dot(ah, bl)


def _rms(x, gain):
    return x * lax.rsqrt(jnp.mean(x * x, axis=-1, keepdims=True) + NORM_EPS) * gain


def _silu(x):
    return x * jax.nn.sigmoid(x)


def _mod_kernel(c_ref, w_ref, b_ref, o_ref):
    a = _silu(c_ref[...])
    o_ref[0] = _dot_f32(a, w_ref[0]) + b_ref[0]


def _modulation(cc, w_mod, b_mod):
    depth, d, n = w_mod.shape
    r = cc.shape[0]
    tn = 1536
    return pl.pallas_call(
        _mod_kernel,
        grid=(depth, n // tn),
        in_specs=[
            pl.BlockSpec((r, d), lambda l, j: (0, 0)),
            pl.BlockSpec((1, d, tn), lambda l, j: (l, 0, j)),
            pl.BlockSpec((1, 1, tn), lambda l, j: (l, 0, j)),
        ],
        out_specs=pl.BlockSpec((1, r, tn), lambda l, j: (l, 0, j)),
        out_shape=jax.ShapeDtypeStruct((depth, r, n), F32),
        compiler_params=_cparams("parallel", "parallel"),
        name="adaln_mod",
    )(cc, w_mod, b_mod.reshape(depth, 1, n))


def _rope_rotate(x, first_half):
    up = pltpu.roll(x, LANES - 16, 1)
    down = pltpu.roll(x, 16, 1)
    return jnp.where(first_half, up, down)


IN_PROJ_PARTS = 1
CUMSUM_ROWS = 512


def _in_proj_kernel(*refs, rope):
    if rope:
        x_ref, mod_ref, gain_ref, w_ref, aup_ref, ab_ref, tri_ref, cos_ref, sin_ref, gla_ref, nat_ref, swa_ref = refs
    else:
        x_ref, mod_ref, gain_ref, w_ref, aup_ref, ab_ref, tri_ref, gla_ref, nat_ref, swa_ref = refs
    tm = x_ref.shape[1]
    pm = tm // IN_PROJ_PARTS
    kq = 2 * GLA_KW + 2 * GLA_VW
    scale = HEAD_DIM ** -0.5 * LOG2E
    n_rot = (SWA_QW + 2 * SWA_KW) // LANES
    nck = pm // GLA_CHUNK
    tri = tri_ref[...]
    for part in range(IN_PROJ_PARTS):
        rows = pl.ds(part * pm, pm)
        x = x_ref[0, rows, :]
        y = _rms(x, gain_ref[0:1, :])
        h = (y * (1.0 + mod_ref[0, 1:2, :]) + mod_ref[0, 0:1, :]).astype(BF16)

        pg = _dot(h, w_ref[:, 0:GLA_PROJ_W])
        gla_ref[0, rows, 0:kq] = pg[:, 0:kq]
        z = _dot_f32(pg[:, kq:kq + LANES], aup_ref[...]) + ab_ref[...]
        log_a = (jnp.minimum(z, 0.0) - jnp.log1p(jnp.exp(-jnp.abs(z)))) * (1.0 / GLA_TAU)
        hi, lo = _split2(log_a)
        cr = tri.shape[0]
        cum = jnp.concatenate([_dot(tri, hi[i * cr:(i + 1) * cr]) + _dot(tri, lo[i * cr:(i + 1) * cr])
                               for i in range(pm // cr)], axis=0)
        gla_ref[0, rows, kq:kq + GLA_KW] = cum[:, 0:GLA_KW]
        cb = cum[:, GLA_KW:].reshape(nck, GLA_CHUNK, GLA_KW)
        lb = log_a[:, GLA_KW:].reshape(nck, GLA_CHUNK, GLA_KW)
        from_end = cb[:, GLA_CHUNK - 1:GLA_CHUNK, :] - cb + lb
        gla_ref[0, rows, kq + GLA_KW:kq + 2 * GLA_KW] = from_end.reshape(pm, GLA_KW)

        o = GLA_PROJ_W
        pn = _dot(h, w_ref[:, o:o + NAT_PROJ_W])
        nat_ref[0, rows, 0:NAT_W] = (pn[:, 0:NAT_W] * scale).astype(BF16)
        nat_ref[0, rows, NAT_W:] = pn[:, NAT_W:].astype(BF16)

        o = GLA_PROJ_W + NAT_PROJ_W
        ps = _dot(h, w_ref[:, o:o + SWA_PROJ_W])
        if rope:
            cos = cos_ref[rows, :]
            sin = sin_ref[rows, :]
            lane = lax.broadcasted_iota(jnp.int32, cos.shape, 1)
            first_half = (lane % 32) < 16
        for j in range(SWA_PROJ_W // LANES):
            t = ps[:, j * LANES:(j + 1) * LANES]
            if rope and j < n_rot:
                t = t * cos + _rope_rotate(t, first_half) * sin
            if j < SWA_QW // LANES:
                t = t * scale
            swa_ref[0, rows, j * LANES:(j + 1) * LANES] = t.astype(BF16)


def _in_proj(x, mod, gains, w, aup, abias, rope_tabs, tm):
    bsz, t, d = x.shape
    rope = rope_tabs is not None
    cum_rows = min(tm // IN_PROJ_PARTS, CUMSUM_ROWS)
    in_specs = [
        pl.BlockSpec((1, tm, d), lambda b, i: (b, i, 0)),
        pl.BlockSpec((1, 6, d), lambda b, i: (b, 0, 0)),
        pl.BlockSpec((4, d), lambda b, i: (0, 0)),
        pl.BlockSpec((d, IN_PROJ_W), lambda b, i: (0, 0)),
        pl.BlockSpec((LANES, 2 * GLA_KW), lambda b, i: (0, 0)),
        pl.BlockSpec((1, 2 * GLA_KW), lambda b, i: (0, 0)),
        pl.BlockSpec((cum_rows, cum_rows), lambda b, i: (0, 0)),
    ]
    r = np.arange(cum_rows)
    tri = (r[:, None] // GLA_CHUNK == r[None, :] // GLA_CHUNK) & (r[:, None] >= r[None, :])
    args = [x, mod, gains, w, aup, abias, jnp.asarray(tri, BF16)]
    if rope:
        in_specs += [pl.BlockSpec((tm, LANES), lambda b, i: (i, 0))] * 2
        args += list(rope_tabs)
    return pl.pallas_call(
        functools.partial(_in_proj_kernel, rope=rope),
        grid=(bsz, t // tm),
        in_specs=in_specs,
        out_specs=[
            pl.BlockSpec((1, tm, GLA_OUT_W), lambda b, i: (b, i, 0)),
            pl.BlockSpec((1, tm, NAT_PROJ_W), lambda b, i: (b, i, 0)),
            pl.BlockSpec((1, tm, SWA_PROJ_W), lambda b, i: (b, i, 0)),
        ],
        out_shape=[
            jax.ShapeDtypeStruct((bsz, t, GLA_OUT_W), F32),
            jax.ShapeDtypeStruct((bsz, t, NAT_PROJ_W), BF16),
            jax.ShapeDtypeStruct((bsz, t, SWA_PROJ_W), BF16),
        ],
        compiler_params=_cparams("parallel", "parallel"),
        name="in_proj_rope" if rope else "in_proj",
    )(*args)


def _gla_kernel(p_ref, s0f_ref, s0b_ref, gain_ref, o_ref, sff_ref, sfb_ref, acc_ref, sf_ref, sb_ref):
    t = p_ref.shape[1]
    c = GLA_CHUNK
    nc = t // c
    qi = lax.broadcasted_iota(jnp.int32, (c, GLA_HEADS * c), 0)
    kj = lax.broadcasted_iota(jnp.int32, (c, GLA_HEADS * c), 1) % c
    lower = qi >= kj
    upper = qi <= kj
    srow = lax.broadcasted_iota(jnp.int32, (GLA_VW, GLA_KW), 0) // GLA_DV
    scol = lax.broadcasted_iota(jnp.int32, (GLA_VW, GLA_KW), 1) // GLA_DK
    head_diag = srow == scol
    q_scale = GLA_DK ** -0.5

    sf_ref[...] = s0f_ref[0]
    sb_ref[...] = s0b_ref[0]
    acc_ref[...] = jnp.zeros_like(acc_ref)

    def one_direction(base, keep, end_row, cum_off, st_ref):
        rows = pl.ds(base, c)
        k = p_ref[0, rows, 0:GLA_KW]
        q = p_ref[0, rows, GLA_KW:2 * GLA_KW] * q_scale
        vb = p_ref[0, rows, 2 * GLA_KW:2 * GLA_KW + GLA_VW].astype(BF16)
        cum = p_ref[0, rows, cum_off:cum_off + GLA_KW]
        mid = cum[c // 2:c // 2 + 1, :]
        total = cum[end_row:end_row + 1, :]
        st = st_ref[...]
        inter = _dot_nt((q * jnp.exp(cum)).astype(BF16), st.astype(BF16))
        qs = (q * jnp.exp(cum - mid)).astype(BF16)
        ks = (k * jnp.exp(mid - cum)).astype(BF16)
        s = _dot_nt(qs, _head_stack(ks, GLA_DK, GLA_HEADS))
        s = jnp.where(keep, s, 0.0).astype(BF16)
        intra = _dot(s, _head_stack(vb, GLA_DV, GLA_HEADS))
        acc_ref[rows, :] += inter + intra
        kd = (k * jnp.exp(total - cum)).astype(BF16)
        upd = _dot_tn(vb, kd)
        st_ref[...] = st * jnp.exp(total) + jnp.where(head_diag, upd, 0.0)

    def body(i, carry):
        one_direction(pl.multiple_of(i * c, c), lower, c - 1, 2 * GLA_KW + 2 * GLA_VW, sf_ref)
        one_direction(pl.multiple_of((nc - 1 - i) * c, c), upper, 0, 3 * GLA_KW + 2 * GLA_VW, sb_ref)
        return carry

    lax.fori_loop(0, nc, body, 0, unroll=4)
    sff_ref[0] = sf_ref[...]
    sfb_ref[0] = sb_ref[...]

    hr = lax.broadcasted_iota(jnp.int32, (GLA_VW, GLA_VW), 0) // GLA_DV
    hc = lax.broadcasted_iota(jnp.int32, (GLA_VW, GLA_VW), 1) // GLA_DV
    head_ones = (hr == hc).astype(BF16)
    blk = 256
    for j in range(t // blk):
        rows = pl.ds(j * blk, blk)
        o = acc_ref[rows, :]
        hi, mid, lo = _split3(o * o)
        ms = (_dot(hi, head_ones) + _dot(mid, head_ones) + _dot(lo, head_ones)) * (1.0 / GLA_DV)
        gate = p_ref[0, rows, 2 * GLA_KW + GLA_VW:2 * GLA_KW + 2 * GLA_VW]
        o_ref[0, rows, :] = (o * lax.rsqrt(ms + NORM_EPS) * gain_ref[...] * _silu(gate)).astype(BF16)


def _gla(p, s0f, s0b, gain):
    bsz, t, w = p.shape
    st_spec = pl.BlockSpec((1, GLA_VW, GLA_KW), lambda b: (b, 0, 0))
    st_shape = jax.ShapeDtypeStruct((bsz, GLA_VW, GLA_KW), F32)
    return pl.pallas_call(
        _gla_kernel,
        grid=(bsz,),
        in_specs=[
            pl.BlockSpec((1, t, w), lambda b: (b, 0, 0)),
            st_spec, st_spec,
            pl.BlockSpec((1, GLA_VW), lambda b: (0, 0)),
        ],
        out_specs=[pl.BlockSpec((1, t, GLA_VW), lambda b: (b, 0, 0)), st_spec, st_spec],
        out_shape=[jax.ShapeDtypeStruct((bsz, t, GLA_VW), BF16), st_shape, st_shape],
        scratch_shapes=[
            pltpu.VMEM((t, GLA_VW), F32),
            pltpu.VMEM((GLA_VW, GLA_KW), F32),
            pltpu.VMEM((GLA_VW, GLA_KW), F32),
        ],
        compiler_params=_cparams("parallel"),
        name="gla_bidir",
    )(p, s0f, s0b, gain)


def _head_stack(q, width, n):
    lane = lax.broadcasted_iota(jnp.int32, (1, q.shape[1]), 1) // width
    return jnp.concatenate([jnp.where(lane == h, q, jnp.zeros_like(q)) for h in range(n)], axis=0)


def _head_unstack(o, width, n):
    rows = o.shape[0] // n
    lane = lax.broadcasted_iota(jnp.int32, (1, o.shape[1]), 1) // width
    out = o[0:rows]
    for h in range(1, n):
        out = jnp.where(lane == h, o[h * rows:(h + 1) * rows], out)
    return out


NAT_ROWS_PER_STEP = 16


def _nat_kernel(lat_ref, ctx_ref, bias_ref, o_ref):
    n_rows = lat_ref.shape[1] // GRID_W
    kc = ctx_ref[0, :, NAT_W:2 * NAT_W]
    vc = ctx_ref[0, :, 2 * NAT_W:3 * NAT_W]
    n_loc = NAT_WIN_ROWS * GRID_W
    for i in range(NAT_ROWS_PER_STEP):
        r = pl.program_id(1) * NAT_ROWS_PER_STEP + i
        r0 = jnp.clip(r - NAT_WIN_ROWS // 2, 0, n_rows - NAT_WIN_ROWS)
        q = lat_ref[0, pl.ds(pl.multiple_of(r * GRID_W, GRID_W), GRID_W), 0:NAT_W]
        krows = pl.ds(pl.multiple_of(r0 * GRID_W, GRID_W), n_loc)
        qs = _head_stack(q, HEAD_DIM, NAT_HEADS)
        s = jnp.concatenate([_dot_nt(qs, lat_ref[0, krows, NAT_W:2 * NAT_W]), _dot_nt(qs, kc)], axis=1)
        s = s + bias_ref[r0 - r + NAT_WIN_ROWS - 1]
        p = jnp.exp2((s - jnp.max(s, axis=-1, keepdims=True)).astype(BF16))
        den = jnp.sum(p, axis=-1, keepdims=True, dtype=F32)
        o = (_dot(p[:, 0:n_loc], lat_ref[0, krows, 2 * NAT_W:3 * NAT_W]) + _dot(p[:, n_loc:], vc)) / den
        o_ref[0, i * GRID_W:(i + 1) * GRID_W, :] = _head_unstack(o, HEAD_DIM, NAT_HEADS).astype(BF16)


def _nat(lat, ctx, bias):
    bsz, t, w = lat.shape
    m = ctx.shape[1]
    rows = NAT_ROWS_PER_STEP * GRID_W
    return pl.pallas_call(
        _nat_kernel,
        grid=(bsz, t // rows),
        in_specs=[
            pl.BlockSpec((1, t, w), lambda b, r: (b, 0, 0)),
            pl.BlockSpec((1, m, w), lambda b, r: (b, 0, 0)),
            pl.BlockSpec(bias.shape, lambda b, r: (0, 0, 0)),
        ],
        out_specs=pl.BlockSpec((1, rows, NAT_W), lambda b, r: (b, r, 0)),
        out_shape=jax.ShapeDtypeStruct((bsz, t, NAT_W), BF16),
        compiler_params=_cparams("parallel", "arbitrary"),
        name="nat_attn",
    )(lat, ctx, bias)


def _nat_bias_table(rpb, m):
    cols = np.arange(GRID_W)
    col_start = np.clip(cols - NAT_WIN_COLS // 2, 0, GRID_W - NAT_WIN_COLS)
    kc = np.arange(GRID_W)
    inside = (kc[None, :] >= col_start[:, None]) & (kc[None, :] < col_start[:, None] + NAT_WIN_COLS)
    rel = kc[None, :] - cols[:, None] + NAT_WIN_COLS - 1
    col_sel = (rel[:, :, None] == np.arange(2 * NAT_WIN_COLS - 1)) & inside[:, :, None]
    ridx = np.arange(NAT_WIN_ROWS)[:, None] + np.arange(NAT_WIN_ROWS)[None, :]
    row_sel = ridx[:, :, None] == np.arange(2 * NAT_WIN_ROWS - 1)
    tab = jnp.einsum("hab,oia,ckb->ohcik", rpb.astype(F32), jnp.asarray(row_sel, F32), jnp.asarray(col_sel, F32),
                     precision=lax.Precision.HIGHEST)
    tab = jnp.where(jnp.asarray(inside)[None, None, :, None, :], tab * LOG2E, NEG_BIG)
    tab = tab.reshape(NAT_WIN_ROWS, NAT_HEADS * GRID_W, NAT_WIN_ROWS * GRID_W)
    return jnp.concatenate([tab, jnp.zeros(tab.shape[:2] + (m,), F32)], axis=2)


def _sink_column(sink_ref, first_head, n_heads, rows):
    return jnp.concatenate(
        [jnp.broadcast_to(sink_ref[first_head + j:first_head + j + 1, 0:1], (rows, 1)) for j in range(n_heads)], axis=0)


def _softmax_pv_sink(s, v, sink):
    m = jnp.maximum(jnp.max(s, axis=-1, keepdims=True), sink)
    p = jnp.exp2((s - m).astype(BF16))
    o = _dot(p, jnp.concatenate([v, jnp.ones_like(v)], axis=1))
    w = v.shape[1]
    return o[:, 0:w] / (o[:, w:] + jnp.exp2(sink - m))


SWA_BLOCKS_PER_STEP = 4


def _swa_kernel(lat_ref, ctx_ref, sink_ref, o_ref):
    t = lat_ref.shape[1]
    m = ctx_ref.shape[1]
    blk = SWA_BLOCK
    n_loc = 3 * blk
    qi = lax.broadcasted_iota(jnp.int32, (blk, n_loc), 0)
    kj = lax.broadcasted_iota(jnp.int32, (blk, n_loc), 1)
    group = SWA_Q_HEADS // SWA_KV_HEADS
    for a in range(SWA_BLOCKS_PER_STEP):
        n = pl.program_id(1) * SWA_BLOCKS_PER_STEP + a
        base = pl.multiple_of(jnp.clip((n - 1) * blk, 0, t - n_loc), blk)
        qrows = pl.ds(pl.multiple_of(n * blk, blk), blk)
        krows = pl.ds(base, n_loc)
        mask_add = jnp.where(jnp.abs(base + kj - n * blk - qi) <= SWA_WINDOW, 0.0, NEG_BIG)[None]
        for g in range(SWA_KV_HEADS):
            kcol = pl.ds(SWA_QW + g * LANES, LANES)
            vcol = pl.ds(SWA_QW + 2 * SWA_KW + g * LANES, LANES)
            k = jnp.concatenate([lat_ref[0, krows, kcol], ctx_ref[0, :, kcol]], axis=0)
            v = jnp.concatenate([lat_ref[0, krows, vcol], ctx_ref[0, :, vcol]], axis=0)
            qs = jnp.concatenate(
                [_head_stack(lat_ref[0, qrows, pl.ds((g * 2 + p) * LANES, LANES)], HEAD_DIM, 2) for p in range(2)],
                axis=0)
            s = _dot_nt(qs, k).reshape(group, blk, n_loc + m)
            s = jnp.concatenate([s[:, :, 0:n_loc] + mask_add, s[:, :, n_loc:]], axis=2).reshape(group * blk, n_loc + m)
            o = _softmax_pv_sink(s, v, _sink_column(sink_ref, g * group, group, blk))
            for p in range(2):
                pair = _head_unstack(o[2 * p * blk:(2 * p + 2) * blk], HEAD_DIM, 2)
                o_ref[0, a * blk:(a + 1) * blk, pl.ds((g * 2 + p) * LANES, LANES)] = pair.astype(BF16)


def _swa(lat, ctx, sink_tab):
    bsz, t, w = lat.shape
    m = ctx.shape[1]
    rows = SWA_BLOCKS_PER_STEP * SWA_BLOCK
    return pl.pallas_call(
        _swa_kernel,
        grid=(bsz, t // rows),
        in_specs=[
            pl.BlockSpec((1, t, w), lambda b, n: (b, 0, 0)),
            pl.BlockSpec((1, m, w), lambda b, n: (b, 0, 0)),
            pl.BlockSpec((SWA_Q_HEADS, LANES), lambda b, n: (0, 0)),
        ],
        out_specs=pl.BlockSpec((1, rows, SWA_QW), lambda b, n: (b, n, 0)),
        out_shape=jax.ShapeDtypeStruct((bsz, t, SWA_QW), BF16),
        compiler_params=_cparams("parallel", "arbitrary"),
        name="swa_attn",
    )(lat, ctx, sink_tab)


def _ctx_attn_kernel(nat_ref, swa_ref, sink_ref, on_ref, os_ref):
    m = nat_ref.shape[1]
    q = nat_ref[0, :, 0:NAT_W]
    k = nat_ref[0, :, NAT_W:2 * NAT_W]
    v = nat_ref[0, :, 2 * NAT_W:3 * NAT_W]
    s = _dot_nt(_head_stack(q, HEAD_DIM, NAT_HEADS), k)
    p = jnp.exp2((s - jnp.max(s, axis=-1, keepdims=True)).astype(BF16))
    o = _dot(p, v) / jnp.sum(p, axis=-1, keepdims=True, dtype=F32)
    on_ref[0] = _head_unstack(o, HEAD_DIM, NAT_HEADS).astype(BF16)

    group = SWA_Q_HEADS // SWA_KV_HEADS
    for g in range(SWA_KV_HEADS):
        k = swa_ref[0, :, pl.ds(SWA_QW + g * LANES, LANES)]
        v = swa_ref[0, :, pl.ds(SWA_QW + 2 * SWA_KW + g * LANES, LANES)]
        qs = jnp.concatenate(
            [_head_stack(swa_ref[0, :, pl.ds((g * 2 + p) * LANES, LANES)], HEAD_DIM, 2) for p in range(2)], axis=0)
        o = _softmax_pv_sink(_dot_nt(qs, k), v, _sink_column(sink_ref, g * group, group, m))
        for pr in range(2):
            pair = _head_unstack(o[2 * pr * m:(2 * pr + 2) * m], HEAD_DIM, 2)
            os_ref[0, :, pl.ds((g * 2 + pr) * LANES, LANES)] = pair.astype(BF16)


def _ctx_attn(nat_c, swa_c, sink_tab):
    bsz, m, _ = nat_c.shape
    return pl.pallas_call(
        _ctx_attn_kernel,
        grid=(bsz,),
        in_specs=[
            pl.BlockSpec((1, m, NAT_PROJ_W), lambda b: (b, 0, 0)),
            pl.BlockSpec((1, m, SWA_PROJ_W), lambda b: (b, 0, 0)),
            pl.BlockSpec((SWA_Q_HEADS, LANES), lambda b: (0, 0)),
        ],
        out_specs=[
            pl.BlockSpec((1, m, NAT_W), lambda b: (b, 0, 0)),
            pl.BlockSpec((1, m, SWA_QW), lambda b: (b, 0, 0)),
        ],
        out_shape=[
            jax.ShapeDtypeStruct((bsz, m, NAT_W), BF16),
            jax.ShapeDtypeStruct((bsz, m, SWA_QW), BF16),
        ],
        compiler_params=_cparams("parallel"),
        name="ctx_attn",
    )(nat_c, swa_c, sink_tab)


OUT_PROJ_PARTS = 2


def _out_proj_kernel(gla_ref, nat_ref, swa_ref, x_ref, mod_ref, gain_ref, w_ref, wr_ref,
                     x1_ref, h2_ref, aff_ref, afft_ref):
    tm = x_ref.shape[1]
    half = tm // OUT_PROJ_PARTS
    for part in range(OUT_PROJ_PARTS):
        rows = pl.ds(part * half, half)
        mix = (_dot(gla_ref[0, rows, :], w_ref[0:GLA_VW, :])
               + _dot(nat_ref[0, rows, :], w_ref[GLA_VW:GLA_VW + NAT_W, :])
               + _dot(swa_ref[0, rows, :], w_ref[GLA_VW + NAT_W:, :]))
        x1 = x_ref[0, rows, :] + mod_ref[0, 2:3, :] * _rms(mix, gain_ref[1:2, :])
        x1_ref[0, rows, :] = x1
        h2 = (_rms(x1, gain_ref[2:3, :]) * (1.0 + mod_ref[0, 4:5, :]) + mod_ref[0, 3:4, :]).astype(BF16)
        h2_ref[0, rows, :] = h2
        logits = _dot(h2, wr_ref[...])
        lane = lax.broadcasted_iota(jnp.int32, logits.shape, 1)
        logits = jnp.where(lane < N_EXPERTS, logits, NEG_BIG)
        e = jnp.exp(logits - jnp.max(logits, axis=-1, keepdims=True))
        aff = e / jnp.sum(e, axis=-1, keepdims=True)
        aff_ref[0, rows, :] = aff
        afft_ref[0, :, rows] = aff.T[0:N_EXPERTS, :]


def _out_proj(gla_o, nat_o, swa_o, x, mod, gains, w_out, wr, tm):
    bsz, t, d = x.shape
    tok = lambda b, i: (b, i, 0)
    const = lambda b, i: (0, 0)
    return pl.pallas_call(
        _out_proj_kernel,
        grid=(bsz, t // tm),
        in_specs=[
            pl.BlockSpec((1, tm, GLA_VW), tok),
            pl.BlockSpec((1, tm, NAT_W), tok),
            pl.BlockSpec((1, tm, SWA_QW), tok),
            pl.BlockSpec((1, tm, d), tok),
            pl.BlockSpec((1, 6, d), lambda b, i: (b, 0, 0)),
            pl.BlockSpec((4, d), const),
            pl.BlockSpec((d, d), const),
            pl.BlockSpec((d, LANES), const),
        ],
        out_specs=[
            pl.BlockSpec((1, tm, d), tok),
            pl.BlockSpec((1, tm, d), tok),
            pl.BlockSpec((1, tm, LANES), tok),
            pl.BlockSpec((1, N_EXPERTS, tm), lambda b, i: (b, 0, i)),
        ],
        out_shape=[
            jax.ShapeDtypeStruct((bsz, t, d), F32),
            jax.ShapeDtypeStruct((bsz, t, d), BF16),
            jax.ShapeDtypeStruct((bsz, t, LANES), F32),
            jax.ShapeDtypeStruct((bsz, N_EXPERTS, t), F32),
        ],
        compiler_params=_cparams("parallel", "parallel"),
        name="out_proj_router",
    )(gla_o, nat_o, swa_o, x, mod, gains, w_out, wr)


def _excl_prefix(x):
    rows, n = x.shape
    nblk = n // LANES
    r = lax.broadcasted_iota(jnp.int32, (LANES, LANES), 0)
    c = lax.broadcasted_iota(jnp.int32, (LANES, LANES), 1)
    strict_upper = (r < c).astype(BF16)
    stacked = jnp.concatenate([x[:, j * LANES:(j + 1) * LANES] for j in range(nblk)], axis=0).astype(BF16)
    local = _dot(stacked, strict_upper)
    totals = jnp.sum(stacked.astype(F32), axis=-1, keepdims=True)
    out = []
    offs = [jnp.zeros((rows, 1), F32)]
    for j in range(nblk):
        out.append(local[j * rows:(j + 1) * rows] + offs[-1])
        offs.append(offs[-1] + totals[j * rows:(j + 1) * rows])
    return jnp.concatenate(out, axis=1), offs


def _topc_kernel(aff_ref, rank_ref, rankcol_ref, cnt_ref, *, cap):
    aff = aff_ref[...]

    def step(i, thr):
        cand = thr | (jnp.int32(1) << (30 - i))
        cnt = jnp.sum((aff >= pltpu.bitcast(cand, F32)).astype(F32), axis=-1, keepdims=True)
        return jnp.where(cnt >= cap, cand, thr)

    thr = lax.fori_loop(0, 31, step, jnp.zeros((aff.shape[0], 1), jnp.int32))
    thr = pltpu.bitcast(thr, F32)
    gt = aff > thr
    eq = aff == thr
    need = cap - jnp.sum(gt.astype(F32), axis=-1, keepdims=True)
    eq_before, _ = _excl_prefix(eq.astype(F32))
    sel = gt | (eq & (eq_before < need))
    sel_before, offs = _excl_prefix(sel.astype(F32))
    rank = jnp.where(sel, sel_before, -1.0)
    rank_ref[...] = rank
    lane = lax.broadcasted_iota(jnp.int32, (rank.shape[0], LANES), 1)
    cnt = jnp.zeros((rank.shape[0], LANES), F32)
    for j, off in enumerate(offs):
        cnt = jnp.where(lane == j, off, cnt)
    cnt_ref[...] = cnt.astype(jnp.int32)
    n = rank.shape[1]
    fill = jnp.full((LANES - N_EXPERTS, n), -1.0, F32)
    for b in range(rankcol_ref.shape[0]):
        rankcol_ref[b] = jnp.concatenate([rank[b * N_EXPERTS:(b + 1) * N_EXPERTS], fill], axis=0).T


def _topc(aff_t, cap):
    bsz, e, n = aff_t.shape
    rows = bsz * e
    rank, rankcol, cnt = pl.pallas_call(
        functools.partial(_topc_kernel, cap=cap),
        grid=(1,),
        in_specs=[pl.BlockSpec((rows, n), lambda i: (0, 0))],
        out_specs=[pl.BlockSpec((rows, n), lambda i: (0, 0)), pl.BlockSpec((bsz, n, LANES), lambda i: (0, 0, 0)),
                   pl.BlockSpec((rows, LANES), lambda i: (0, 0))],
        out_shape=[jax.ShapeDtypeStruct((rows, n), F32), jax.ShapeDtypeStruct((bsz, n, LANES), F32),
                   jax.ShapeDtypeStruct((rows, LANES), jnp.int32)],
        compiler_params=_cparams("arbitrary"),
        name="expert_topc",
    )(aff_t.reshape(rows, n))
    return rank.reshape(bsz, e, n), rankcol, cnt.reshape(bsz, e, LANES)


def _gather_kernel(h_ref, rank_ref, xs_ref, *, cap):
    n = h_ref.shape[1]
    slot = lax.broadcasted_iota(jnp.int32, (cap, n), 0).astype(F32)
    h = h_ref[0]
    for e in range(N_EXPERTS):
        onehot = (slot == rank_ref[0, e:e + 1, :]).astype(BF16)
        xs_ref[e] = _dot(onehot, h).astype(BF16)


def _gather(h, rank, cap):
    bsz, n, d = h.shape
    return pl.pallas_call(
        functools.partial(_gather_kernel, cap=cap),
        grid=(bsz,),
        in_specs=[
            pl.BlockSpec((1, n, d), lambda b: (b, 0, 0)),
            pl.BlockSpec((1, N_EXPERTS, n), lambda b: (b, 0, 0)),
        ],
        out_specs=pl.BlockSpec((N_EXPERTS, cap, d), lambda b: (0, b, 0)),
        out_shape=jax.ShapeDtypeStruct((N_EXPERTS, bsz * cap, d), BF16),
        compiler_params=_cparams("parallel"),
        name="moe_gather",
    )(h, rank)


GATHER_TILE = 256
GATHER_WIN = 5
GATHER_SLOTS = 128


def _gather_win_kernel(cnt_ref, h_ref, rank_ref, xs_ref, *, cap):
    b = pl.program_id(0)
    n = h_ref.shape[1]
    nt = n // GATHER_TILE
    win = GATHER_WIN * GATHER_TILE
    slot_w = lax.broadcasted_iota(jnp.int32, (GATHER_SLOTS, win), 0).astype(F32).astype(BF16)
    slot_f = lax.broadcasted_iota(jnp.int32, (cap, n), 0).astype(F32).astype(BF16)
    one = jnp.ones((), BF16)
    zero = jnp.zeros((), BF16)

    n_sb = cap // GATHER_SLOTS
    group = 2

    def per_group(g, carry):
        starts, fits = [], []
        for j in range(group):
            base = (g * group + j) * (nt + 1)
            c = [cnt_ref[b, base + kt] for kt in range(nt + 1)]
            for sb in range(n_sb):
                lo_slot = sb * GATHER_SLOTS
                first = sum((c[kt + 1] <= lo_slot).astype(jnp.int32) for kt in range(nt))
                last = sum((c[kt] < lo_slot + GATHER_SLOTS).astype(jnp.int32) for kt in range(nt))
                fits.append(last - first <= GATHER_WIN)
                starts.append(jnp.minimum(first, nt - GATHER_WIN))
        all_fit = functools.reduce(jnp.logical_and, fits)

        @pl.when(all_fit)
        def _():
            for j in range(group):
                e = g * group + j
                for sb in range(n_sb):
                    start = starts[j * n_sb + sb]
                    rk = jnp.concatenate([rank_ref[0, e, pl.ds(start + i, 1), :] for i in range(GATHER_WIN)], axis=1)
                    onehot = jnp.where(slot_w == (rk - sb * GATHER_SLOTS).astype(BF16), one, zero)
                    hw = h_ref[0, pl.ds(pl.multiple_of(start * GATHER_TILE, GATHER_TILE), win), :]
                    xs_ref[e, pl.ds(sb * GATHER_SLOTS, GATHER_SLOTS), :] = _dot(onehot, hw).astype(BF16)

        @pl.when(jnp.logical_not(all_fit))
        def _():
            for j in range(group):
                e = g * group + j
                rk = jnp.concatenate([rank_ref[0, e, kt:kt + 1, :] for kt in range(nt)], axis=1).astype(BF16)
                onehot = jnp.where(slot_f == rk, one, zero)
                xs_ref[e] = _dot(onehot, h_ref[0]).astype(BF16)
        return carry

    lax.fori_loop(0, N_EXPERTS // group, per_group, 0)


def _gather_win(h, rank, cnt, cap):
    bsz, n, d = h.shape
    nt = n // GATHER_TILE
    step = GATHER_TILE // LANES
    bounds = cnt[:, :, 0:nt * step + 1:step].reshape(bsz, N_EXPERTS * (nt + 1))
    return pl.pallas_call(
        functools.partial(_gather_win_kernel, cap=cap),
        grid_spec=pltpu.PrefetchScalarGridSpec(
            num_scalar_prefetch=1,
            grid=(bsz,),
            in_specs=[
                pl.BlockSpec((1, n, d), lambda b, c: (b, 0, 0)),
                pl.BlockSpec((1, N_EXPERTS, nt, GATHER_TILE), lambda b, c: (b, 0, 0, 0)),
            ],
            out_specs=pl.BlockSpec((N_EXPERTS, cap, d), lambda b, c: (0, b, 0)),
        ),
        out_shape=jax.ShapeDtypeStruct((N_EXPERTS, bsz * cap, d), BF16),
        compiler_params=_cparams("parallel"),
        name="moe_gather_win",
    )(bounds, h, rank.reshape(bsz, N_EXPERTS, nt, GATHER_TILE))


FFN_PARTS = 4


def _ffn_kernel(*refs, has_ctx):
    if has_ctx:
        xs_ref, xc_ref, wg_ref, wu_ref, wd_ref, y_ref, yc_ref, wg_bf, wu_bf, wd_bf = refs
    else:
        xs_ref, wg_ref, wu_ref, wd_ref, y_ref, wg_bf, wu_bf, wd_bf = refs
    s = pl.program_id(0)
    j = pl.program_id(1)
    n_exp = pl.num_programs(0) - 1
    piece = wg_ref.shape[2]

    @pl.when((s < n_exp) & (j < FFN_PARTS))
    def _():
        rows = pl.ds(pl.multiple_of(j * piece, piece), piece)
        wg_bf[s % 2, rows, :] = wg_ref[0, 0].astype(BF16)
        wu_bf[s % 2, rows, :] = wu_ref[0, 0].astype(BF16)
        wd_bf[s % 2, rows, :] = wd_ref[0, 0].astype(BF16)

    def swiglu(x_ref, o_ref):
        cur = (s + 1) % 2
        xs = x_ref[0]
        hid = (_silu(_dot(xs, wg_bf[cur])) * _dot(xs, wu_bf[cur])).astype(BF16)
        o_ref[0] = _dot(hid, wd_bf[cur]).astype(BF16)

    @pl.when((s > 0) & (j < FFN_PARTS))
    def _():
        swiglu(xs_ref, y_ref)

    if has_ctx:
        @pl.when((s > 0) & (j == FFN_PARTS))
        def _():
            swiglu(xc_ref, yc_ref)


def _ffn(xs, xc, w_gate, w_up, w_down, layer):
    _, e, d, ff = w_gate.shape
    rb = xs.shape[1] // FFN_PARTS
    has_ctx = xc is not None
    last = FFN_PARTS - 1

    def x_idx(s, j):
        return (jnp.maximum(s - 1, 0), jnp.where(s > 0, jnp.minimum(j, last), 0), 0)

    def w_idx(s, j):
        return (layer, jnp.minimum(s, e - 1), jnp.minimum(j, last), 0)

    xspec = pl.BlockSpec((1, rb, d), x_idx)
    cspecs, cargs, cshapes = [], [], []
    if has_ctx:
        cspecs = [pl.BlockSpec((1, xc.shape[1], d), lambda s, j: (jnp.maximum(s - 1, 0), 0, 0))]
        cargs = [xc]
        cshapes = [jax.ShapeDtypeStruct(xc.shape, BF16)]
    out = pl.pallas_call(
        functools.partial(_ffn_kernel, has_ctx=has_ctx),
        grid=(e + 1, FFN_PARTS + int(has_ctx)),
        in_specs=[xspec] + cspecs + [
            pl.BlockSpec((1, 1, d // FFN_PARTS, ff), w_idx),
            pl.BlockSpec((1, 1, d // FFN_PARTS, ff), w_idx),
            pl.BlockSpec((1, 1, ff // FFN_PARTS, d), w_idx),
        ],
        out_specs=[xspec] + cspecs,
        out_shape=[jax.ShapeDtypeStruct(xs.shape, BF16)] + cshapes,
        scratch_shapes=[pltpu.VMEM((2, d, ff), BF16), pltpu.VMEM((2, d, ff), BF16), pltpu.VMEM((2, ff, d), BF16)],
        compiler_params=_cparams("arbitrary", "arbitrary"),
        name="moe_ffn",
    )(xs, *cargs, w_gate, w_up, w_down)
    return out


def _combine_kernel(y_ref, rankcol_ref, aff_ref, x_ref, mod_ref, gain_ref, o_ref, *, cap):
    rb = x_ref.shape[1]
    slot = lax.broadcasted_iota(jnp.int32, (rb, cap), 1).astype(F32)
    rc = rankcol_ref[0]
    af = aff_ref[0]
    acc = jnp.zeros((rb, x_ref.shape[2]), F32)
    for e in range(N_EXPERTS):
        onehot = (rc[:, e:e + 1] == slot).astype(BF16)
        acc = acc + af[:, e:e + 1] * _dot(onehot, y_ref[e])
    o_ref[0] = x_ref[0] + mod_ref[0, 5:6, :] * _rms(acc, gain_ref[3:4, :])


def _combine(y, rankcol, aff, x, mod, gains, cap):
    bsz, n, d = x.shape
    rb = min(n, 512)
    tok = lambda b, i: (b, i, 0)
    return pl.pallas_call(
        functools.partial(_combine_kernel, cap=cap),
        grid=(bsz, n // rb),
        in_specs=[
            pl.BlockSpec((N_EXPERTS, cap, d), lambda b, i: (0, b, 0)),
            pl.BlockSpec((1, rb, LANES), tok),
            pl.BlockSpec((1, rb, LANES), tok),
            pl.BlockSpec((1, rb, d), tok),
            pl.BlockSpec((1, 6, d), lambda b, i: (b, 0, 0)),
            pl.BlockSpec((4, d), lambda b, i: (0, 0)),
        ],
        out_specs=pl.BlockSpec((1, rb, d), tok),
        out_shape=jax.ShapeDtypeStruct((bsz, n, d), F32),
        compiler_params=_cparams("parallel", "arbitrary"),
        name="moe_combine",
    )(y, rankcol, aff, x, mod, gains)


COMBINE_ROWS = 512
COMBINE_WIN = 128


def _combine_pair_kernel(cnt_ref, y_ref, rankcol_ref, aff_ref, x_ref, mod_ref, gain_ref, o_ref, extra_ref, *, cap):
    b = pl.program_id(0)
    i = pl.program_id(1)
    nrb = pl.num_programs(1)
    rb = x_ref.shape[1]
    slot_w = lax.broadcasted_iota(jnp.int32, (rb, COMBINE_WIN), 1).astype(F32)
    rc = rankcol_ref[0]
    af = aff_ref[0]
    los, unfit = [], []
    for e in range(N_EXPERTS):
        c0 = cnt_ref[b, e * (nrb + 1) + i]
        c1 = cnt_ref[b, e * (nrb + 1) + i + 1]
        lo = jnp.minimum((c0 // 16) * 16, cap - COMBINE_WIN)
        los.append(pl.multiple_of(lo, 16))
        unfit.append(c1 - lo > COMBINE_WIN)
    acc = jnp.zeros((rb, x_ref.shape[2]), F32)
    for e0 in range(0, N_EXPERTS, 2):
        hot, rows = [], []
        for e in (e0, e0 + 1):
            hot.append(jnp.where(rc[:, e:e + 1] - los[e].astype(F32) == slot_w, af[:, e:e + 1], 0.0).astype(BF16))
            rows.append(y_ref[e, pl.ds(los[e], COMBINE_WIN), :])
        acc = acc + _dot(jnp.concatenate(hot, axis=1), jnp.concatenate(rows, axis=0))

    def finish(total):
        o_ref[0] = x_ref[0] + mod_ref[0, 5:6, :] * _rms(total, gain_ref[3:4, :])

    any_unfit = functools.reduce(jnp.logical_or, unfit)

    @pl.when(jnp.logical_not(any_unfit))
    def _():
        finish(acc)

    @pl.when(any_unfit)
    def _():
        extra_ref[...] = jnp.zeros_like(extra_ref)
        slot_f = lax.broadcasted_iota(jnp.int32, (rb, cap), 1)
        for e in range(N_EXPERTS):
            @pl.when(unfit[e])
            def _():
                outside = (slot_f < los[e]) | (slot_f >= los[e] + COMBINE_WIN)
                hot = jnp.where((rc[:, e:e + 1] == slot_f.astype(F32)) & outside, af[:, e:e + 1], 0.0).astype(BF16)
                extra_ref[...] += _dot(hot, y_ref[e])
        finish(acc + extra_ref[...])


def _combine_pair(y, rankcol, aff, cnt, x, mod, gains, cap):
    bsz, n, d = x.shape
    rb = COMBINE_ROWS
    nrb = n // rb
    step = rb // LANES
    bounds = cnt[:, :, 0:nrb * step + 1:step].reshape(bsz, N_EXPERTS * (nrb + 1))
    tok = lambda b, i, c: (b, i, 0)
    return pl.pallas_call(
        functools.partial(_combine_pair_kernel, cap=cap),
        grid_spec=pltpu.PrefetchScalarGridSpec(
            num_scalar_prefetch=1,
            grid=(bsz, nrb),
            in_specs=[
                pl.BlockSpec((N_EXPERTS, cap, d), lambda b, i, c: (0, b, 0)),
                pl.BlockSpec((1, rb, LANES), tok),
                pl.BlockSpec((1, rb, LANES), tok),
                pl.BlockSpec((1, rb, d), tok),
                pl.BlockSpec((1, 6, d), lambda b, i, c: (b, 0, 0)),
                pl.BlockSpec((4, d), lambda b, i, c: (0, 0)),
            ],
            out_specs=pl.BlockSpec((1, rb, d), tok),
            scratch_shapes=[pltpu.VMEM((rb, d), F32)],
        ),
        out_shape=jax.ShapeDtypeStruct((bsz, n, d), F32),
        compiler_params=_cparams("parallel", "arbitrary"),
        name="moe_combine_pair",
    )(bounds, y, rankcol, aff, x, mod, gains)


def _relayout_w_in(w):
    widths = (GLA_KW, GLA_VW, GLA_RANK, GLA_RANK, NAT_W, NAT_W, SWA_KW, SWA_KW, GLA_KW, GLA_VW, NAT_W, SWA_QW)
    gk, gv, gaf, gab, nk, nv, sk, sv, gq, gg, nq, sq = jnp.split(w, np.cumsum(widths)[:-1].tolist(), axis=1)
    pad = jnp.zeros((w.shape[0], LANES - 2 * GLA_RANK), w.dtype)
    sk0, sk1 = sk[:, :HEAD_DIM], sk[:, HEAD_DIM:]
    sv0, sv1 = sv[:, :HEAD_DIM], sv[:, HEAD_DIM:]
    cols = [gk, gq, gv, gg, gaf, gab, pad, nq, nk, nv, sq, sk0, sk0, sk1, sk1, sv0, sv0, sv1, sv1]
    return jnp.concatenate(cols, axis=1).astype(BF16)


def _rope_tables(t):
    half = HEAD_DIM // 4
    freqs = ROPE_BASE ** (-np.arange(half, dtype=np.float32) / half)
    pos = np.arange(t)
    ang_r = (pos // GRID_W).astype(np.float32)[:, None] * freqs
    ang_c = (pos % GRID_W).astype(np.float32)[:, None] * freqs
    cos = np.concatenate([np.cos(ang_r), np.cos(ang_r), np.cos(ang_c), np.cos(ang_c)], axis=1)
    sin = np.concatenate([-np.sin(ang_r), np.sin(ang_r), -np.sin(ang_c), np.sin(ang_c)], axis=1)
    reps = LANES // HEAD_DIM
    return jnp.asarray(np.tile(cos, (1, reps)), F32), jnp.asarray(np.tile(sin, (1, reps)), F32)


def _moe(h_list, afft_list, aff_list, x_list, mod_list, gains, w_gate, w_up, w_down, layer):
    caps = [CAPACITY_FACTOR * h.shape[1] // N_EXPERTS for h in h_list]
    ranks = [_topc(a, cap) for a, cap in zip(afft_list, caps)]
    long = [h.shape[1] >= GATHER_WIN * GATHER_TILE and h.shape[1] % COMBINE_ROWS == 0 and cap >= 2 * COMBINE_WIN
            for h, cap in zip(h_list, caps)]
    xs = [_gather_win(h, r[0], r[2], cap) if lg else _gather(h, r[0], cap)
          for h, r, cap, lg in zip(h_list, ranks, caps, long)]
    ys = _ffn(xs[0], xs[1] if len(xs) > 1 else None, w_gate, w_up, w_down, layer)
    return [_combine_pair(y, r[1], aff, r[2], x, mod, gains, cap) if lg else _combine(y, r[1], aff, x, mod, gains, cap)
            for y, r, aff, x, mod, cap, lg in zip(ys, ranks, aff_list, x_list, mod_list, caps, long)]


def kernel(x, c, ctx, c_ctx, w_mod, b_mod, norm_gains, w_in, w_out, gla_a_up, gla_a_bias, gla_norm,
           nat_rpb, swa_sink, w_router, w_gate, w_up, w_down):
    bsz, t, d = x.shape
    m = ctx.shape[1]
    depth = w_mod.shape[0]

    cc = jnp.concatenate([c, c_ctx[None], jnp.zeros((16 - bsz - 1, d), F32)], axis=0)
    mod_all = _modulation(cc, w_mod, b_mod)
    rope_tabs = _rope_tables(t)
    zero_state = jnp.zeros((bsz, GLA_VW, GLA_KW), F32)

    xc = ctx
    for l in range(depth):
        update_ctx = l < depth - 1
        mod = mod_all[l, :bsz].reshape(bsz, 6, d)
        mod_c = jnp.broadcast_to(mod_all[l, bsz].reshape(1, 6, d), (bsz, 6, d))
        gains = norm_gains[l]
        w_in_l = _relayout_w_in(w_in[l])
        aup = jnp.zeros((LANES, 2 * GLA_KW), F32)
        aup = aup.at[0:GLA_RANK, 0:GLA_KW].set(gla_a_up[l, 0]).at[GLA_RANK:2 * GLA_RANK, GLA_KW:].set(gla_a_up[l, 1])
        abias = gla_a_bias[l].reshape(1, 2 * GLA_KW)
        gla_gain = jnp.tile(gla_norm[l], GLA_HEADS).reshape(1, GLA_VW)
        sink_tab = jnp.broadcast_to(swa_sink[l][:, None] * LOG2E, (SWA_Q_HEADS, LANES))
        w_out_l = w_out[l].astype(BF16)
        wr = jnp.pad(w_router[l], ((0, 0), (0, LANES - N_EXPERTS))).astype(BF16)

        gla_p, nat_p, swa_p = _in_proj(x, mod, gains, w_in_l, aup, abias, rope_tabs, 1024)
        gla_c, nat_c, swa_c = _in_proj(xc, mod_c, gains, w_in_l, aup, abias, None, m)

        gla_co, s_f, s_b = _gla(gla_c, zero_state, zero_state, gla_gain)
        gla_o, _, _ = _gla(gla_p, s_f, s_b, gla_gain)
        nat_o = _nat(nat_p, nat_c, _nat_bias_table(nat_rpb[l], m))
        swa_o = _swa(swa_p, swa_c, sink_tab)

        x1, h2, aff, aff_t = _out_proj(gla_o, nat_o, swa_o, x, mod, gains, w_out_l, wr, 1024)
        if update_ctx:
            nat_co, swa_co = _ctx_attn(nat_c, swa_c, sink_tab)
            xc1, hc2, aff_c, aff_ct = _out_proj(gla_co, nat_co, swa_co, xc, mod_c, gains, w_out_l, wr, m)
            x, xc = _moe([h2, hc2], [aff_t, aff_ct], [aff, aff_c], [x1, xc1], [mod, mod_c], gains,
                         w_gate, w_up, w_down, l)
        else:
            (x,) = _moe([h2], [aff_t], [aff], [x1], [mod], gains, w_gate, w_up, w_down, l)
    return x
```

```python
import functools

import jax
import jax.numpy as jnp
import numpy as np
from jax import lax
from jax.experimental import pallas as pl
from jax.experimental.pallas import tpu as pltpu

F32 = jnp.float32
BF16 = jnp.bfloat16

D_MODEL = 1024
GRID_W = 64
HEAD_DIM = 64
GLA_HEADS = 4
GLA_DK = 32
GLA_DV = 64
GLA_RANK = 16
GLA_TAU = 16.0
GLA_CHUNK = 64
NAT_HEADS = 4
NAT_WIN_ROWS = 8
NAT_WIN_COLS = 16
SWA_Q_HEADS = 8
SWA_KV_HEADS = 2
SWA_WINDOW = 128
SWA_BLOCK = 128
ROPE_BASE = 10000.0
N_EXPERTS = 16
EXPERT_FF = 1024
CAPACITY_FACTOR = 2
NORM_EPS = 1e-6

GLA_KW = GLA_HEADS * GLA_DK
GLA_VW = GLA_HEADS * GLA_DV
NAT_W = NAT_HEADS * HEAD_DIM
SWA_QW = SWA_Q_HEADS * HEAD_DIM
SWA_KW = SWA_KV_HEADS * HEAD_DIM
LANES = 128
NEG_BIG = -1e30
LOG2E = 1.4426950408889634

GLA_PROJ_W = 2 * GLA_KW + 2 * GLA_VW + LANES
GLA_OUT_W = 2 * GLA_KW + 2 * GLA_VW + 2 * GLA_KW
NAT_PROJ_W = 3 * NAT_W
SWA_PROJ_W = SWA_QW + 4 * SWA_KW
IN_PROJ_W = GLA_PROJ_W + NAT_PROJ_W + SWA_PROJ_W

VMEM_LIMIT = 56 * 1024 * 1024


def _cparams(*sem):
    return pltpu.CompilerParams(dimension_semantics=sem, vmem_limit_bytes=VMEM_LIMIT)


def _dot(a, b):
    return jnp.dot(a, b, preferred_element_type=F32)


def _dot_nt(a, b):
    return lax.dot_general(a, b, (((1,), (1,)), ((), ())), preferred_element_type=F32)


def _dot_tn(a, b):
    return lax.dot_general(a, b, (((0,), (0,)), ((), ())), preferred_element_type=F32)


def _split2(a):
    hi = a.astype(BF16)
    lo = (a - hi.astype(F32)).astype(BF16)
    return hi, lo


def _split3(a):
    hi = a.astype(BF16)
    r = a - hi.astype(F32)
    mid = r.astype(BF16)
    lo = (r - mid.astype(F32)).astype(BF16)
    return hi, mid, lo


def _dot_f32(a, b):
    ah, al = _split2(a)
    bh, bl = _split2(b)
    return _dot(ah, bh) + _dot(al, bh) + _dot(ah, bl)


def _rms(x, gain):
    return x * lax.rsqrt(jnp.mean(x * x, axis=-1, keepdims=True) + NORM_EPS) * gain


def _silu(x):
    return x * jax.nn.sigmoid(x)


def _mod_kernel(c_ref, w_ref, b_ref, o_ref):
    a = _silu(c_ref[...])
    o_ref[0] = _dot_f32(a, w_ref[0]) + b_ref[0]


def _modulation(cc, w_mod, b_mod):
    depth, d, n = w_mod.shape
    r = cc.shape[0]
    tn = 1536
    return pl.pallas_call(
        _mod_kernel,
        grid=(depth, n // tn),
        in_specs=[
            pl.BlockSpec((r, d), lambda l, j: (0, 0)),
            pl.BlockSpec((1, d, tn), lambda l, j: (l, 0, j)),
            pl.BlockSpec((1, 1, tn), lambda l, j: (l, 0, j)),
        ],
        out_specs=pl.BlockSpec((1, r, tn), lambda l, j: (l, 0, j)),
        out_shape=jax.ShapeDtypeStruct((depth, r, n), F32),
        compiler_params=_cparams("parallel", "parallel"),
        name="adaln_mod",
    )(cc, w_mod, b_mod.reshape(depth, 1, n))


def _rope_rotate(x, first_half):
    up = pltpu.roll(x, LANES - 16, 1)
    down = pltpu.roll(x, 16, 1)
    return jnp.where(first_half, up, down)


IN_PROJ_PARTS = 1
CUMSUM_ROWS = 512


def _in_proj_kernel(*refs, rope):
    if rope:
        x_ref, mod_ref, gain_ref, w_ref, aup_ref, ab_ref, tri_ref, cos_ref, sin_ref, gla_ref, nat_ref, swa_ref = refs
    else:
        x_ref, mod_ref, gain_ref, w_ref, aup_ref, ab_ref, tri_ref, gla_ref, nat_ref, swa_ref = refs
    tm = x_ref.shape[1]
    pm = tm // IN_PROJ_PARTS
    kq = 2 * GLA_KW + 2 * GLA_VW
    scale = HEAD_DIM ** -0.5 * LOG2E
    n_rot = (SWA_QW + 2 * SWA_KW) // LANES
    nck = pm // GLA_CHUNK
    tri = tri_ref[...]
    for part in range(IN_PROJ_PARTS):
        rows = pl.ds(part * pm, pm)
        x = x_ref[0, rows, :]
        y = _rms(x, gain_ref[0:1, :])
        h = (y * (1.0 + mod_ref[0, 1:2, :]) + mod_ref[0, 0:1, :]).astype(BF16)

        pg = _dot(h, w_ref[:, 0:GLA_PROJ_W])
        pn = _dot(h, w_ref[:, GLA_PROJ_W:GLA_PROJ_W + NAT_PROJ_W])
        ps = _dot(h, w_ref[:, GLA_PROJ_W + NAT_PROJ_W:])

        gla_ref[0, rows, 0:kq] = pg[:, 0:kq]
        z = _dot_f32(pg[:, kq:kq + LANES], aup_ref[...]) + ab_ref[...]
        log_a = (jnp.minimum(z, 0.0) - jnp.log1p(jnp.exp(-jnp.abs(z)))) * (1.0 / GLA_TAU)
        hi, lo = _split2(log_a)
        cr = tri.shape[0]
        cum = jnp.concatenate([_dot(tri, hi[i * cr:(i + 1) * cr]) + _dot(tri, lo[i * cr:(i + 1) * cr])
                               for i in range(pm // cr)], axis=0)
        gla_ref[0, rows, kq:kq + GLA_KW] = cum[:, 0:GLA_KW]
        cb = cum[:, GLA_KW:].reshape(nck, GLA_CHUNK, GLA_KW)
        lb = log_a[:, GLA_KW:].reshape(nck, GLA_CHUNK, GLA_KW)
        from_end = cb[:, GLA_CHUNK - 1:GLA_CHUNK, :] - cb + lb
        gla_ref[0, rows, kq + GLA_KW:kq + 2 * GLA_KW] = from_end.reshape(pm, GLA_KW)

        nat_ref[0, rows, 0:NAT_W] = (pn[:, 0:NAT_W] * scale).astype(BF16)
        nat_ref[0, rows, NAT_W:] = pn[:, NAT_W:].astype(BF16)

        if rope:
            cos = cos_ref[rows, :]
            sin = sin_ref[rows, :]
            lane = lax.broadcasted_iota(jnp.int32, cos.shape, 1)
            first_half = (lane % 32) < 16
        for j in range(SWA_PROJ_W // LANES):
            t = ps[:, j * LANES:(j + 1) * LANES]
            if rope and j < n_rot:
                t = t * cos + _rope_rotate(t, first_half) * sin
            if j < SWA_QW // LANES:
                t = t * scale
            swa_ref[0, rows, j * LANES:(j + 1) * LANES] = t.astype(BF16)


def _in_proj(x, mod, gains, w, aup, abias, rope_tabs, tm):
    bsz, t, d = x.shape
    rope = rope_tabs is not None
    cum_rows = min(tm // IN_PROJ_PARTS, CUMSUM_ROWS)
    in_specs = [
        pl.BlockSpec((1, tm, d), lambda b, i: (b, i, 0)),
        pl.BlockSpec((1, 6, d), lambda b, i: (b, 0, 0)),
        pl.BlockSpec((4, d), lambda b, i: (0, 0)),
        pl.BlockSpec((d, IN_PROJ_W), lambda b, i: (0, 0)),
        pl.BlockSpec((LANES, 2 * GLA_KW), lambda b, i: (0, 0)),
        pl.BlockSpec((1, 2 * GLA_KW), lambda b, i: (0, 0)),
        pl.BlockSpec((cum_rows, cum_rows), lambda b, i: (0, 0)),
    ]
    r = np.arange(cum_rows)
    tri = (r[:, None] // GLA_CHUNK == r[None, :] // GLA_CHUNK) & (r[:, None] >= r[None, :])
    args = [x, mod, gains, w, aup, abias, jnp.asarray(tri, BF16)]
    if rope:
        in_specs += [pl.BlockSpec((tm, LANES), lambda b, i: (i, 0))] * 2
        args += list(rope_tabs)
    return pl.pallas_call(
        functools.partial(_in_proj_kernel, rope=rope),
        grid=(bsz, t // tm),
        in_specs=in_specs,
        out_specs=[
            pl.BlockSpec((1, tm, GLA_OUT_W), lambda b, i: (b, i, 0)),
            pl.BlockSpec((1, tm, NAT_PROJ_W), lambda b, i: (b, i, 0)),
            pl.BlockSpec((1, tm, SWA_PROJ_W), lambda b, i: (b, i, 0)),
        ],
        out_shape=[
            jax.ShapeDtypeStruct((bsz, t, GLA_OUT_W), F32),
            jax.ShapeDtypeStruct((bsz, t, NAT_PROJ_W), BF16),
            jax.ShapeDtypeStruct((bsz, t, SWA_PROJ_W), BF16),
        ],
        compiler_params=_cparams("parallel", "parallel"),
        name="in_proj_rope" if rope else "in_proj",
    )(*args)


def _gla_kernel(p_ref, s0f_ref, s0b_ref, gain_ref, o_ref, sff_ref, sfb_ref, acc_ref, sf_ref, sb_ref):
    t = p_ref.shape[1]
    c = GLA_CHUNK
    nc = t // c
    qi = lax.broadcasted_iota(jnp.int32, (c, GLA_HEADS * c), 0)
    kj = lax.broadcasted_iota(jnp.int32, (c, GLA_HEADS * c), 1) % c
    lower = qi >= kj
    upper = qi <= kj
    srow = lax.broadcasted_iota(jnp.int32, (GLA_VW, GLA_KW), 0) // GLA_DV
    scol = lax.broadcasted_iota(jnp.int32, (GLA_VW, GLA_KW), 1) // GLA_DK
    head_diag = srow == scol
    q_scale = GLA_DK ** -0.5

    sf_ref[...] = s0f_ref[0]
    sb_ref[...] = s0b_ref[0]
    acc_ref[...] = jnp.zeros_like(acc_ref)

    def state_free(base, end_row, cum_off):
        rows = pl.ds(base, c)
        k = p_ref[0, rows, 0:GLA_KW]
        q = p_ref[0, rows, GLA_KW:2 * GLA_KW] * q_scale
        vb = p_ref[0, rows, 2 * GLA_KW:2 * GLA_KW + GLA_VW].astype(BF16)
        cum = p_ref[0, rows, cum_off:cum_off + GLA_KW]
        mid = cum[c // 2:c // 2 + 1, :]
        total = cum[end_row:end_row + 1, :]
        qs = (q * jnp.exp(cum - mid)).astype(BF16)
        ks = (k * jnp.exp(mid - cum)).astype(BF16)
        s = _dot_nt(qs, _head_stack(ks, GLA_DK, GLA_HEADS))
        kd = (k * jnp.exp(total - cum)).astype(BF16)
        upd = _dot_tn(vb, kd)
        qe = (q * jnp.exp(cum)).astype(BF16)
        return rows, s, upd, qe, vb, jnp.exp(total)

    def with_state(keep, st_ref, rows, s, upd, qe, vb, decay):
        st = st_ref[...]
        inter = _dot_nt(qe, st.astype(BF16))
        s = jnp.where(keep, s, 0.0).astype(BF16)
        intra = _dot(s, _head_stack(vb, GLA_DV, GLA_HEADS))
        acc_ref[rows, :] += inter + intra
        st_ref[...] = st * decay + jnp.where(head_diag, upd, 0.0)

    def body(i, carry):
        fwd = state_free(pl.multiple_of(i * c, c), c - 1, 2 * GLA_KW + 2 * GLA_VW)
        bwd = state_free(pl.multiple_of((nc - 1 - i) * c, c), 0, 3 * GLA_KW + 2 * GLA_VW)
        with_state(lower, sf_ref, *fwd)
        with_state(upper, sb_ref, *bwd)
        return carry

    lax.fori_loop(0, nc, body, 0, unroll=4)
    sff_ref[0] = sf_ref[...]
    sfb_ref[0] = sb_ref[...]

    hr = lax.broadcasted_iota(jnp.int32, (GLA_VW, GLA_VW), 0) // GLA_DV
    hc = lax.broadcasted_iota(jnp.int32, (GLA_VW, GLA_VW), 1) // GLA_DV
    head_ones = (hr == hc).astype(BF16)
    blk = 256

    def mean_square(j):
        o = acc_ref[pl.ds(j * blk, blk), :]
        hi, mid, lo = _split3(o * o)
        return o, (_dot(hi, head_ones) + _dot(mid, head_ones) + _dot(lo, head_ones)) * (1.0 / GLA_DV)

    nxt = mean_square(0)
    for j in range(t // blk):
        rows = pl.ds(j * blk, blk)
        o, ms = nxt
        if j + 1 < t // blk:
            nxt = mean_square(j + 1)
        gate = p_ref[0, rows, 2 * GLA_KW + GLA_VW:2 * GLA_KW + 2 * GLA_VW]
        o_ref[0, rows, :] = (o * lax.rsqrt(ms + NORM_EPS) * gain_ref[...] * _silu(gate)).astype(BF16)


def _gla(p, s0f, s0b, gain):
    bsz, t, w = p.shape
    st_spec = pl.BlockSpec((1, GLA_VW, GLA_KW), lambda b: (b, 0, 0))
    st_shape = jax.ShapeDtypeStruct((bsz, GLA_VW, GLA_KW), F32)
    return pl.pallas_call(
        _gla_kernel,
        grid=(bsz,),
        in_specs=[
            pl.BlockSpec((1, t, w), lambda b: (b, 0, 0)),
            st_spec, st_spec,
            pl.BlockSpec((1, GLA_VW), lambda b: (0, 0)),
        ],
        out_specs=[pl.BlockSpec((1, t, GLA_VW), lambda b: (b, 0, 0)), st_spec, st_spec],
        out_shape=[jax.ShapeDtypeStruct((bsz, t, GLA_VW), BF16), st_shape, st_shape],
        scratch_shapes=[
            pltpu.VMEM((t, GLA_VW), F32),
            pltpu.VMEM((GLA_VW, GLA_KW), F32),
            pltpu.VMEM((GLA_VW, GLA_KW), F32),
        ],
        compiler_params=_cparams("parallel"),
        name="gla_bidir",
    )(p, s0f, s0b, gain)


def _head_stack(q, width, n):
    lane = lax.broadcasted_iota(jnp.int32, (1, q.shape[1]), 1) // width
    return jnp.concatenate([jnp.where(lane == h, q, jnp.zeros_like(q)) for h in range(n)], axis=0)


def _head_unstack(o, width, n):
    rows = o.shape[0] // n
    lane = lax.broadcasted_iota(jnp.int32, (1, o.shape[1]), 1) // width
    out = o[0:rows]
    for h in range(1, n):
        out = jnp.where(lane == h, o[h * rows:(h + 1) * rows], out)
    return out


NAT_ROWS_PER_STEP = 16


def _nat_kernel(lat_ref, ctx_ref, bias_ref, o_ref):
    n_rows = lat_ref.shape[1] // GRID_W
    kc = ctx_ref[0, :, NAT_W:2 * NAT_W]
    vc = ctx_ref[0, :, 2 * NAT_W:3 * NAT_W]
    n_loc = NAT_WIN_ROWS * GRID_W

    def scores(i):
        r = pl.program_id(1) * NAT_ROWS_PER_STEP + i
        r0 = jnp.clip(r - NAT_WIN_ROWS // 2, 0, n_rows - NAT_WIN_ROWS)
        q = lat_ref[0, pl.ds(pl.multiple_of(r * GRID_W, GRID_W), GRID_W), 0:NAT_W]
        krows = pl.ds(pl.multiple_of(r0 * GRID_W, GRID_W), n_loc)
        qs = _head_stack(q, HEAD_DIM, NAT_HEADS)
        s = jnp.concatenate([_dot_nt(qs, lat_ref[0, krows, NAT_W:2 * NAT_W]), _dot_nt(qs, kc)], axis=1)
        return s + bias_ref[r0 - r + NAT_WIN_ROWS - 1], krows

    def finish(i, s, krows):
        p = jnp.exp2((s - jnp.max(s, axis=-1, keepdims=True)).astype(BF16))
        den = jnp.sum(p, axis=-1, keepdims=True, dtype=F32)
        o = (_dot(p[:, 0:n_loc], lat_ref[0, krows, 2 * NAT_W:3 * NAT_W]) + _dot(p[:, n_loc:], vc)) / den
        o_ref[0, i * GRID_W:(i + 1) * GRID_W, :] = _head_unstack(o, HEAD_DIM, NAT_HEADS).astype(BF16)

    nxt = scores(0)
    for i in range(NAT_ROWS_PER_STEP):
        cur = nxt
        if i + 1 < NAT_ROWS_PER_STEP:
            nxt = scores(i + 1)
        finish(i, *cur)


def _nat(lat, ctx, bias):
    bsz, t, w = lat.shape
    m = ctx.shape[1]
    rows = NAT_ROWS_PER_STEP * GRID_W
    return pl.pallas_call(
        _nat_kernel,
        grid=(bsz, t // rows),
        in_specs=[
            pl.BlockSpec((1, t, w), lambda b, r: (b, 0, 0)),
            pl.BlockSpec((1, m, w), lambda b, r: (b, 0, 0)),
            pl.BlockSpec(bias.shape, lambda b, r: (0, 0, 0)),
        ],
        out_specs=pl.BlockSpec((1, rows, NAT_W), lambda b, r: (b, r, 0)),
        out_shape=jax.ShapeDtypeStruct((bsz, t, NAT_W), BF16),
        compiler_params=_cparams("parallel", "arbitrary"),
        name="nat_attn",
    )(lat, ctx, bias)


def _nat_bias_table(rpb, m):
    cols = np.arange(GRID_W)
    col_start = np.clip(cols - NAT_WIN_COLS // 2, 0, GRID_W - NAT_WIN_COLS)
    kc = np.arange(GRID_W)
    inside = (kc[None, :] >= col_start[:, None]) & (kc[None, :] < col_start[:, None] + NAT_WIN_COLS)
    rel = kc[None, :] - cols[:, None] + NAT_WIN_COLS - 1
    col_sel = (rel[:, :, None] == np.arange(2 * NAT_WIN_COLS - 1)) & inside[:, :, None]
    ridx = np.arange(NAT_WIN_ROWS)[:, None] + np.arange(NAT_WIN_ROWS)[None, :]
    row_sel = ridx[:, :, None] == np.arange(2 * NAT_WIN_ROWS - 1)
    tab = jnp.einsum("hab,oia,ckb->ohcik", rpb.astype(F32), jnp.asarray(row_sel, F32), jnp.asarray(col_sel, F32),
                     precision=lax.Precision.HIGHEST)
    tab = jnp.where(jnp.asarray(inside)[None, None, :, None, :], tab * LOG2E, NEG_BIG)
    tab = tab.reshape(NAT_WIN_ROWS, NAT_HEADS * GRID_W, NAT_WIN_ROWS * GRID_W)
    return jnp.concatenate([tab, jnp.zeros(tab.shape[:2] + (m,), F32)], axis=2)


def _sink_column(sink_ref, first_head, n_heads, rows):
    return jnp.concatenate(
        [jnp.broadcast_to(sink_ref[first_head + j:first_head + j + 1, 0:1], (rows, 1)) for j in range(n_heads)], axis=0)


def _softmax_pv_sink(s, v, sink):
    m = jnp.maximum(jnp.max(s, axis=-1, keepdims=True), sink)
    p = jnp.exp2((s - m).astype(BF16))
    o = _dot(p, jnp.concatenate([v, jnp.ones_like(v)], axis=1))
    w = v.shape[1]
    return o[:, 0:w] / (o[:, w:] + jnp.exp2(sink - m))


SWA_BLOCKS_PER_STEP = 4


def _swa_kernel(lat_ref, ctx_ref, sink_ref, o_ref):
    t = lat_ref.shape[1]
    m = ctx_ref.shape[1]
    blk = SWA_BLOCK
    n_loc = 3 * blk
    qi = lax.broadcasted_iota(jnp.int32, (blk, n_loc), 0)
    kj = lax.broadcasted_iota(jnp.int32, (blk, n_loc), 1)
    group = SWA_Q_HEADS // SWA_KV_HEADS
    chains = [(a, g) for a in range(SWA_BLOCKS_PER_STEP) for g in range(SWA_KV_HEADS)]

    def scores(a, g):
        n = pl.program_id(1) * SWA_BLOCKS_PER_STEP + a
        base = pl.multiple_of(jnp.clip((n - 1) * blk, 0, t - n_loc), blk)
        qrows = pl.ds(pl.multiple_of(n * blk, blk), blk)
        krows = pl.ds(base, n_loc)
        mask_add = jnp.where(jnp.abs(base + kj - n * blk - qi) <= SWA_WINDOW, 0.0, NEG_BIG)[None]
        kcol = pl.ds(SWA_QW + g * LANES, LANES)
        k = jnp.concatenate([lat_ref[0, krows, kcol], ctx_ref[0, :, kcol]], axis=0)
        qs = jnp.concatenate(
            [_head_stack(lat_ref[0, qrows, pl.ds((g * 2 + p) * LANES, LANES)], HEAD_DIM, 2) for p in range(2)], axis=0)
        s = _dot_nt(qs, k).reshape(group, blk, n_loc + m)
        s = jnp.concatenate([s[:, :, 0:n_loc] + mask_add, s[:, :, n_loc:]], axis=2).reshape(group * blk, n_loc + m)
        return s, krows

    def finish(a, g, s, krows):
        vcol = pl.ds(SWA_QW + 2 * SWA_KW + g * LANES, LANES)
        v = jnp.concatenate([lat_ref[0, krows, vcol], ctx_ref[0, :, vcol]], axis=0)
        o = _softmax_pv_sink(s, v, _sink_column(sink_ref, g * group, group, blk))
        for p in range(2):
            pair = _head_unstack(o[2 * p * blk:(2 * p + 2) * blk], HEAD_DIM, 2)
            o_ref[0, a * blk:(a + 1) * blk, pl.ds((g * 2 + p) * LANES, LANES)] = pair.astype(BF16)

    nxt = scores(*chains[0])
    for i, (a, g) in enumerate(chains):
        cur = nxt
        if i + 1 < len(chains):
            nxt = scores(*chains[i + 1])
        finish(a, g, *cur)


def _swa(lat, ctx, sink_tab):
    bsz, t, w = lat.shape
    m = ctx.shape[1]
    rows = SWA_BLOCKS_PER_STEP * SWA_BLOCK
    return pl.pallas_call(
        _swa_kernel,
        grid=(bsz, t // rows),
        in_specs=[
            pl.BlockSpec((1, t, w), lambda b, n: (b, 0, 0)),
            pl.BlockSpec((1, m, w), lambda b, n: (b, 0, 0)),
            pl.BlockSpec((SWA_Q_HEADS, LANES), lambda b, n: (0, 0)),
        ],
        out_specs=pl.BlockSpec((1, rows, SWA_QW), lambda b, n: (b, n, 0)),
        out_shape=jax.ShapeDtypeStruct((bsz, t, SWA_QW), BF16),
        compiler_params=_cparams("parallel", "arbitrary"),
        name="swa_attn",
    )(lat, ctx, sink_tab)


def _ctx_attn_kernel(nat_ref, swa_ref, sink_ref, on_ref, os_ref):
    m = nat_ref.shape[1]
    q = nat_ref[0, :, 0:NAT_W]
    k = nat_ref[0, :, NAT_W:2 * NAT_W]
    v = nat_ref[0, :, 2 * NAT_W:3 * NAT_W]
    group = SWA_Q_HEADS // SWA_KV_HEADS

    def swa_scores(g):
        k = swa_ref[0, :, pl.ds(SWA_QW + g * LANES, LANES)]
        qs = jnp.concatenate(
            [_head_stack(swa_ref[0, :, pl.ds((g * 2 + p) * LANES, LANES)], HEAD_DIM, 2) for p in range(2)], axis=0)
        return _dot_nt(qs, k)

    s = _dot_nt(_head_stack(q, HEAD_DIM, NAT_HEADS), k)
    s_swa = [swa_scores(g) for g in range(SWA_KV_HEADS)]
    p = jnp.exp2((s - jnp.max(s, axis=-1, keepdims=True)).astype(BF16))
    o = _dot(p, v) / jnp.sum(p, axis=-1, keepdims=True, dtype=F32)
    on_ref[0] = _head_unstack(o, HEAD_DIM, NAT_HEADS).astype(BF16)

    for g in range(SWA_KV_HEADS):
        v = swa_ref[0, :, pl.ds(SWA_QW + 2 * SWA_KW + g * LANES, LANES)]
        o = _softmax_pv_sink(s_swa[g], v, _sink_column(sink_ref, g * group, group, m))
        for pr in range(2):
            pair = _head_unstack(o[2 * pr * m:(2 * pr + 2) * m], HEAD_DIM, 2)
            os_ref[0, :, pl.ds((g * 2 + pr) * LANES, LANES)] = pair.astype(BF16)


def _ctx_attn(nat_c, swa_c, sink_tab):
    bsz, m, _ = nat_c.shape
    return pl.pallas_call(
        _ctx_attn_kernel,
        grid=(bsz,),
        in_specs=[
            pl.BlockSpec((1, m, NAT_PROJ_W), lambda b: (b, 0, 0)),
            pl.BlockSpec((1, m, SWA_PROJ_W), lambda b: (b, 0, 0)),
            pl.BlockSpec((SWA_Q_HEADS, LANES), lambda b: (0, 0)),
        ],
        out_specs=[
            pl.BlockSpec((1, m, NAT_W), lambda b: (b, 0, 0)),
            pl.BlockSpec((1, m, SWA_QW), lambda b: (b, 0, 0)),
        ],
        out_shape=[
            jax.ShapeDtypeStruct((bsz, m, NAT_W), BF16),
            jax.ShapeDtypeStruct((bsz, m, SWA_QW), BF16),
        ],
        compiler_params=_cparams("parallel"),
        name="ctx_attn",
    )(nat_c, swa_c, sink_tab)


OUT_PROJ_PARTS = 2


def _out_proj_kernel(gla_ref, nat_ref, swa_ref, x_ref, mod_ref, gain_ref, w_ref, wr_ref,
                     x1_ref, h2_ref, aff_ref, afft_ref):
    tm = x_ref.shape[1]
    half = tm // OUT_PROJ_PARTS

    def project(part):
        rows = pl.ds(part * half, half)
        return (_dot(gla_ref[0, rows, :], w_ref[0:GLA_VW, :])
                + _dot(nat_ref[0, rows, :], w_ref[GLA_VW:GLA_VW + NAT_W, :])
                + _dot(swa_ref[0, rows, :], w_ref[GLA_VW + NAT_W:, :]))

    nxt = project(0)
    for part in range(OUT_PROJ_PARTS):
        rows = pl.ds(part * half, half)
        mix = nxt
        if part + 1 < OUT_PROJ_PARTS:
            nxt = project(part + 1)
        x1 = x_ref[0, rows, :] + mod_ref[0, 2:3, :] * _rms(mix, gain_ref[1:2, :])
        x1_ref[0, rows, :] = x1
        h2 = (_rms(x1, gain_ref[2:3, :]) * (1.0 + mod_ref[0, 4:5, :]) + mod_ref[0, 3:4, :]).astype(BF16)
        h2_ref[0, rows, :] = h2
        logits = _dot(h2, wr_ref[...])
        lane = lax.broadcasted_iota(jnp.int32, logits.shape, 1)
        logits = jnp.where(lane < N_EXPERTS, logits, NEG_BIG)
        e = jnp.exp(logits - jnp.max(logits, axis=-1, keepdims=True))
        aff = e / jnp.sum(e, axis=-1, keepdims=True)
        aff_ref[0, rows, :] = aff
        afft_ref[0, :, rows] = aff.T[0:N_EXPERTS, :]


def _out_proj(gla_o, nat_o, swa_o, x, mod, gains, w_out, wr, tm):
    bsz, t, d = x.shape
    tok = lambda b, i: (b, i, 0)
    const = lambda b, i: (0, 0)
    return pl.pallas_call(
        _out_proj_kernel,
        grid=(bsz, t // tm),
        in_specs=[
            pl.BlockSpec((1, tm, GLA_VW), tok),
            pl.BlockSpec((1, tm, NAT_W), tok),
            pl.BlockSpec((1, tm, SWA_QW), tok),
            pl.BlockSpec((1, tm, d), tok),
            pl.BlockSpec((1, 6, d), lambda b, i: (b, 0, 0)),
            pl.BlockSpec((4, d), const),
            pl.BlockSpec((d, d), const),
            pl.BlockSpec((d, LANES), const),
        ],
        out_specs=[
            pl.BlockSpec((1, tm, d), tok),
            pl.BlockSpec((1, tm, d), tok),
            pl.BlockSpec((1, tm, LANES), tok),
            pl.BlockSpec((1, N_EXPERTS, tm), lambda b, i: (b, 0, i)),
        ],
        out_shape=[
            jax.ShapeDtypeStruct((bsz, t, d), F32),
            jax.ShapeDtypeStruct((bsz, t, d), BF16),
            jax.ShapeDtypeStruct((bsz, t, LANES), F32),
            jax.ShapeDtypeStruct((bsz, N_EXPERTS, t), F32),
        ],
        compiler_params=_cparams("parallel", "parallel"),
        name="out_proj_router",
    )(gla_o, nat_o, swa_o, x, mod, gains, w_out, wr)


def _excl_prefix(x):
    rows, n = x.shape
    nblk = n // LANES
    r = lax.broadcasted_iota(jnp.int32, (LANES, LANES), 0)
    c = lax.broadcasted_iota(jnp.int32, (LANES, LANES), 1)
    strict_upper = (r < c).astype(BF16)
    stacked = jnp.concatenate([x[:, j * LANES:(j + 1) * LANES] for j in range(nblk)], axis=0).astype(BF16)
    local = _dot(stacked, strict_upper)
    totals = jnp.sum(stacked.astype(F32), axis=-1, keepdims=True)
    out = []
    offs = [jnp.zeros((rows, 1), F32)]
    for j in range(nblk):
        out.append(local[j * rows:(j + 1) * rows] + offs[-1])
        offs.append(offs[-1] + totals[j * rows:(j + 1) * rows])
    return jnp.concatenate(out, axis=1), offs


def _topc_kernel(aff_ref, rank_ref, rankcol_ref, cnt_ref, *, cap):
    aff = aff_ref[...]

    def step(i, thr):
        cand = thr | (jnp.int32(1) << (30 - i))
        cnt = jnp.sum((aff >= pltpu.bitcast(cand, F32)).astype(F32), axis=-1, keepdims=True)
        return jnp.where(cnt >= cap, cand, thr)

    thr = lax.fori_loop(0, 31, step, jnp.zeros((aff.shape[0], 1), jnp.int32))
    thr = pltpu.bitcast(thr, F32)
    gt = aff > thr
    eq = aff == thr
    need = cap - jnp.sum(gt.astype(F32), axis=-1, keepdims=True)
    eq_before, _ = _excl_prefix(eq.astype(F32))
    sel = gt | (eq & (eq_before < need))
    sel_before, offs = _excl_prefix(sel.astype(F32))
    rank = jnp.where(sel, sel_before, -1.0)
    rank_ref[...] = rank
    lane = lax.broadcasted_iota(jnp.int32, (rank.shape[0], LANES), 1)
    cnt = jnp.zeros((rank.shape[0], LANES), F32)
    for j, off in enumerate(offs):
        cnt = jnp.where(lane == j, off, cnt)
    cnt_ref[...] = cnt.astype(jnp.int32)
    n = rank.shape[1]
    fill = jnp.full((LANES - N_EXPERTS, n), -1.0, F32)
    for b in range(rankcol_ref.shape[0]):
        rankcol_ref[b] = jnp.concatenate([rank[b * N_EXPERTS:(b + 1) * N_EXPERTS], fill], axis=0).T


def _topc(aff_t, cap):
    bsz, e, n = aff_t.shape
    rows = bsz * e
    rank, rankcol, cnt = pl.pallas_call(
        functools.partial(_topc_kernel, cap=cap),
        grid=(1,),
        in_specs=[pl.BlockSpec((rows, n), lambda i: (0, 0))],
        out_specs=[pl.BlockSpec((rows, n), lambda i: (0, 0)), pl.BlockSpec((bsz, n, LANES), lambda i: (0, 0, 0)),
                   pl.BlockSpec((rows, LANES), lambda i: (0, 0))],
        out_shape=[jax.ShapeDtypeStruct((rows, n), F32), jax.ShapeDtypeStruct((bsz, n, LANES), F32),
                   jax.ShapeDtypeStruct((rows, LANES), jnp.int32)],
        compiler_params=_cparams("arbitrary"),
        name="expert_topc",
    )(aff_t.reshape(rows, n))
    return rank.reshape(bsz, e, n), rankcol, cnt.reshape(bsz, e, LANES)


def _gather_kernel(h_ref, rank_ref, xs_ref, *, cap):
    n = h_ref.shape[1]
    slot = lax.broadcasted_iota(jnp.int32, (cap, n), 0).astype(F32)
    h = h_ref[0]
    for e in range(N_EXPERTS):
        onehot = (slot == rank_ref[0, e:e + 1, :]).astype(BF16)
        xs_ref[e] = _dot(onehot, h).astype(BF16)


def _gather(h, rank, cap):
    bsz, n, d = h.shape
    return pl.pallas_call(
        functools.partial(_gather_kernel, cap=cap),
        grid=(bsz,),
        in_specs=[
            pl.BlockSpec((1, n, d), lambda b: (b, 0, 0)),
            pl.BlockSpec((1, N_EXPERTS, n), lambda b: (b, 0, 0)),
        ],
        out_specs=pl.BlockSpec((N_EXPERTS, cap, d), lambda b: (0, b, 0)),
        out_shape=jax.ShapeDtypeStruct((N_EXPERTS, bsz * cap, d), BF16),
        compiler_params=_cparams("parallel"),
        name="moe_gather",
    )(h, rank)


GATHER_TILE = 256
GATHER_WIN = 5
GATHER_SLOTS = 128


def _gather_win_kernel(cnt_ref, h_ref, rank_ref, xs_ref, *, cap):
    b = pl.program_id(0)
    n = h_ref.shape[1]
    nt = n // GATHER_TILE
    win = GATHER_WIN * GATHER_TILE
    slot_w = lax.broadcasted_iota(jnp.int32, (GATHER_SLOTS, win), 0).astype(F32).astype(BF16)
    slot_f = lax.broadcasted_iota(jnp.int32, (cap, n), 0).astype(F32).astype(BF16)
    one = jnp.ones((), BF16)
    zero = jnp.zeros((), BF16)

    n_sb = cap // GATHER_SLOTS
    group = 2

    def per_group(g, carry):
        starts, fits = [], []
        for j in range(group):
            base = (g * group + j) * (nt + 1)
            c = [cnt_ref[b, base + kt] for kt in range(nt + 1)]
            for sb in range(n_sb):
                lo_slot = sb * GATHER_SLOTS
                first = sum((c[kt + 1] <= lo_slot).astype(jnp.int32) for kt in range(nt))
                last = sum((c[kt] < lo_slot + GATHER_SLOTS).astype(jnp.int32) for kt in range(nt))
                fits.append(last - first <= GATHER_WIN)
                starts.append(jnp.minimum(first, nt - GATHER_WIN))
        all_fit = functools.reduce(jnp.logical_and, fits)

        @pl.when(all_fit)
        def _():
            for j in range(group):
                e = g * group + j
                for sb in range(n_sb):
                    start = starts[j * n_sb + sb]
                    rk = jnp.concatenate([rank_ref[0, e, pl.ds(start + i, 1), :] for i in range(GATHER_WIN)], axis=1)
                    onehot = jnp.where(slot_w == (rk - sb * GATHER_SLOTS).astype(BF16), one, zero)
                    hw = h_ref[0, pl.ds(pl.multiple_of(start * GATHER_TILE, GATHER_TILE), win), :]
                    xs_ref[e, pl.ds(sb * GATHER_SLOTS, GATHER_SLOTS), :] = _dot(onehot, hw).astype(BF16)

        @pl.when(jnp.logical_not(all_fit))
        def _():
            for j in range(group):
                e = g * group + j
                rk = jnp.concatenate([rank_ref[0, e, kt:kt + 1, :] for kt in range(nt)], axis=1).astype(BF16)
                onehot = jnp.where(slot_f == rk, one, zero)
                xs_ref[e] = _dot(onehot, h_ref[0]).astype(BF16)
        return carry

    lax.fori_loop(0, N_EXPERTS // group, per_group, 0)


def _gather_win(h, rank, cnt, cap):
    bsz, n, d = h.shape
    nt = n // GATHER_TILE
    step = GATHER_TILE // LANES
    bounds = cnt[:, :, 0:nt * step + 1:step].reshape(bsz, N_EXPERTS * (nt + 1))
    return pl.pallas_call(
        functools.partial(_gather_win_kernel, cap=cap),
        grid_spec=pltpu.PrefetchScalarGridSpec(
            num_scalar_prefetch=1,
            grid=(bsz,),
            in_specs=[
                pl.BlockSpec((1, n, d), lambda b, c: (b, 0, 0)),
                pl.BlockSpec((1, N_EXPERTS, nt, GATHER_TILE), lambda b, c: (b, 0, 0, 0)),
            ],
            out_specs=pl.BlockSpec((N_EXPERTS, cap, d), lambda b, c: (0, b, 0)),
        ),
        out_shape=jax.ShapeDtypeStruct((N_EXPERTS, bsz * cap, d), BF16),
        compiler_params=_cparams("parallel"),
        name="moe_gather_win",
    )(bounds, h, rank.reshape(bsz, N_EXPERTS, nt, GATHER_TILE))


FFN_PARTS = 4


def _ffn_kernel(*refs, has_ctx):
    if has_ctx:
        xs_ref, xc_ref, wg_ref, wu_ref, wd_ref, y_ref, yc_ref, wg_bf, wu_bf, wd_bf = refs
    else:
        xs_ref, wg_ref, wu_ref, wd_ref, y_ref, wg_bf, wu_bf, wd_bf = refs
    s = pl.program_id(0)
    j = pl.program_id(1)
    n_exp = pl.num_programs(0) - 1
    piece = wg_ref.shape[2]

    @pl.when((s < n_exp) & (j < FFN_PARTS))
    def _():
        rows = pl.ds(pl.multiple_of(j * piece, piece), piece)
        wg_bf[s % 2, rows, :] = wg_ref[0, 0].astype(BF16)
        wu_bf[s % 2, rows, :] = wu_ref[0, 0].astype(BF16)
        wd_bf[s % 2, rows, :] = wd_ref[0, 0].astype(BF16)

    def swiglu(x_ref, o_ref):
        cur = (s + 1) % 2
        xs = x_ref[0]
        hid = (_silu(_dot(xs, wg_bf[cur])) * _dot(xs, wu_bf[cur])).astype(BF16)
        o_ref[0] = _dot(hid, wd_bf[cur]).astype(BF16)

    @pl.when((s > 0) & (j < FFN_PARTS))
    def _():
        swiglu(xs_ref, y_ref)

    if has_ctx:
        @pl.when((s > 0) & (j == FFN_PARTS))
        def _():
            swiglu(xc_ref, yc_ref)


def _ffn(xs, xc, w_gate, w_up, w_down, layer):
    _, e, d, ff = w_gate.shape
    rb = xs.shape[1] // FFN_PARTS
    has_ctx = xc is not None
    last = FFN_PARTS - 1

    def x_idx(s, j):
        return (jnp.maximum(s - 1, 0), jnp.where(s > 0, jnp.minimum(j, last), 0), 0)

    def w_idx(s, j):
        return (layer, jnp.minimum(s, e - 1), jnp.minimum(j, last), 0)

    xspec = pl.BlockSpec((1, rb, d), x_idx)
    cspecs, cargs, cshapes = [], [], []
    if has_ctx:
        cspecs = [pl.BlockSpec((1, xc.shape[1], d), lambda s, j: (jnp.maximum(s - 1, 0), 0, 0))]
        cargs = [xc]
        cshapes = [jax.ShapeDtypeStruct(xc.shape, BF16)]
    out = pl.pallas_call(
        functools.partial(_ffn_kernel, has_ctx=has_ctx),
        grid=(e + 1, FFN_PARTS + int(has_ctx)),
        in_specs=[xspec] + cspecs + [
            pl.BlockSpec((1, 1, d // FFN_PARTS, ff), w_idx),
            pl.BlockSpec((1, 1, d // FFN_PARTS, ff), w_idx),
            pl.BlockSpec((1, 1, ff // FFN_PARTS, d), w_idx),
        ],
        out_specs=[xspec] + cspecs,
        out_shape=[jax.ShapeDtypeStruct(xs.shape, BF16)] + cshapes,
        scratch_shapes=[pltpu.VMEM((2, d, ff), BF16), pltpu.VMEM((2, d, ff), BF16), pltpu.VMEM((2, ff, d), BF16)],
        compiler_params=_cparams("arbitrary", "arbitrary"),
        name="moe_ffn",
    )(xs, *cargs, w_gate, w_up, w_down)
    return out


def _combine_kernel(y_ref, rankcol_ref, aff_ref, x_ref, mod_ref, gain_ref, o_ref, *, cap):
    rb = x_ref.shape[1]
    slot = lax.broadcasted_iota(jnp.int32, (rb, cap), 1).astype(F32)
    rc = rankcol_ref[0]
    af = aff_ref[0]
    acc = jnp.zeros((rb, x_ref.shape[2]), F32)
    for e in range(N_EXPERTS):
        onehot = (rc[:, e:e + 1] == slot).astype(BF16)
        acc = acc + af[:, e:e + 1] * _dot(onehot, y_ref[e])
    o_ref[0] = x_ref[0] + mod_ref[0, 5:6, :] * _rms(acc, gain_ref[3:4, :])


def _combine(y, rankcol, aff, x, mod, gains, cap):
    bsz, n, d = x.shape
    rb = min(n, 512)
    tok = lambda b, i: (b, i, 0)
    return pl.pallas_call(
        functools.partial(_combine_kernel, cap=cap),
        grid=(bsz, n // rb),
        in_specs=[
            pl.BlockSpec((N_EXPERTS, cap, d), lambda b, i: (0, b, 0)),
            pl.BlockSpec((1, rb, LANES), tok),
            pl.BlockSpec((1, rb, LANES), tok),
            pl.BlockSpec((1, rb, d), tok),
            pl.BlockSpec((1, 6, d), lambda b, i: (b, 0, 0)),
            pl.BlockSpec((4, d), lambda b, i: (0, 0)),
        ],
        out_specs=pl.BlockSpec((1, rb, d), tok),
        out_shape=jax.ShapeDtypeStruct((bsz, n, d), F32),
        compiler_params=_cparams("parallel", "arbitrary"),
        name="moe_combine",
    )(y, rankcol, aff, x, mod, gains)


COMBINE_ROWS = 512
COMBINE_WIN = 128


def _combine_pair_kernel(cnt_ref, y_ref, rankcol_ref, aff_ref, x_ref, mod_ref, gain_ref, o_ref, extra_ref, *, cap):
    b = pl.program_id(0)
    i = pl.program_id(1)
    nrb = pl.num_programs(1)
    rb = x_ref.shape[1]
    slot_w = lax.broadcasted_iota(jnp.int32, (rb, COMBINE_WIN), 1).astype(F32)
    rc = rankcol_ref[0]
    af = aff_ref[0]
    los, unfit = [], []
    for e in range(N_EXPERTS):
        c0 = cnt_ref[b, e * (nrb + 1) + i]
        c1 = cnt_ref[b, e * (nrb + 1) + i + 1]
        lo = jnp.minimum((c0 // 16) * 16, cap - COMBINE_WIN)
        los.append(pl.multiple_of(lo, 16))
        unfit.append(c1 - lo > COMBINE_WIN)
    def operands(e0):
        hot, rows = [], []
        for e in (e0, e0 + 1):
            hot.append(jnp.where(rc[:, e:e + 1] - los[e].astype(F32) == slot_w, af[:, e:e + 1], 0.0).astype(BF16))
            rows.append(y_ref[e, pl.ds(los[e], COMBINE_WIN), :])
        return jnp.concatenate(hot, axis=1), jnp.concatenate(rows, axis=0)

    acc = jnp.zeros((rb, x_ref.shape[2]), F32)
    nxt = operands(0)
    for e0 in range(0, N_EXPERTS, 2):
        cur = nxt
        if e0 + 2 < N_EXPERTS:
            nxt = operands(e0 + 2)
        acc = acc + _dot(*cur)

    def finish(total):
        o_ref[0] = x_ref[0] + mod_ref[0, 5:6, :] * _rms(total, gain_ref[3:4, :])

    any_unfit = functools.reduce(jnp.logical_or, unfit)

    @pl.when(jnp.logical_not(any_unfit))
    def _():
        finish(acc)

    @pl.when(any_unfit)
    def _():
        extra_ref[...] = jnp.zeros_like(extra_ref)
        slot_f = lax.broadcasted_iota(jnp.int32, (rb, cap), 1)
        for e in range(N_EXPERTS):
            @pl.when(unfit[e])
            def _():
                outside = (slot_f < los[e]) | (slot_f >= los[e] + COMBINE_WIN)
                hot = jnp.where((rc[:, e:e + 1] == slot_f.astype(F32)) & outside, af[:, e:e + 1], 0.0).astype(BF16)
                extra_ref[...] += _dot(hot, y_ref[e])
        finish(acc + extra_ref[...])


def _combine_pair(y, rankcol, aff, cnt, x, mod, gains, cap):
    bsz, n, d = x.shape
    rb = COMBINE_ROWS
    nrb = n // rb
    step = rb // LANES
    bounds = cnt[:, :, 0:nrb * step + 1:step].reshape(bsz, N_EXPERTS * (nrb + 1))
    tok = lambda b, i, c: (b, i, 0)
    return pl.pallas_call(
        functools.partial(_combine_pair_kernel, cap=cap),
        grid_spec=pltpu.PrefetchScalarGridSpec(
            num_scalar_prefetch=1,
            grid=(bsz, nrb),
            in_specs=[
                pl.BlockSpec((N_EXPERTS, cap, d), lambda b, i, c: (0, b, 0)),
                pl.BlockSpec((1, rb, LANES), tok),
                pl.BlockSpec((1, rb, LANES), tok),
                pl.BlockSpec((1, rb, d), tok),
                pl.BlockSpec((1, 6, d), lambda b, i, c: (b, 0, 0)),
                pl.BlockSpec((4, d), lambda b, i, c: (0, 0)),
            ],
            out_specs=pl.BlockSpec((1, rb, d), tok),
            scratch_shapes=[pltpu.VMEM((rb, d), F32)],
        ),
        out_shape=jax.ShapeDtypeStruct((bsz, n, d), F32),
        compiler_params=_cparams("parallel", "arbitrary"),
        name="moe_combine_pair",
    )(bounds, y, rankcol, aff, x, mod, gains)


def _relayout_w_in(w):
    widths = (GLA_KW, GLA_VW, GLA_RANK, GLA_RANK, NAT_W, NAT_W, SWA_KW, SWA_KW, GLA_KW, GLA_VW, NAT_W, SWA_QW)
    gk, gv, gaf, gab, nk, nv, sk, sv, gq, gg, nq, sq = jnp.split(w, np.cumsum(widths)[:-1].tolist(), axis=1)
    pad = jnp.zeros((w.shape[0], LANES - 2 * GLA_RANK), w.dtype)
    sk0, sk1 = sk[:, :HEAD_DIM], sk[:, HEAD_DIM:]
    sv0, sv1 = sv[:, :HEAD_DIM], sv[:, HEAD_DIM:]
    cols = [gk, gq, gv, gg, gaf, gab, pad, nq, nk, nv, sq, sk0, sk0, sk1, sk1, sv0, sv0, sv1, sv1]
    return jnp.concatenate(cols, axis=1).astype(BF16)


def _rope_tables(t):
    half = HEAD_DIM // 4
    freqs = ROPE_BASE ** (-np.arange(half, dtype=np.float32) / half)
    pos = np.arange(t)
    ang_r = (pos // GRID_W).astype(np.float32)[:, None] * freqs
    ang_c = (pos % GRID_W).astype(np.float32)[:, None] * freqs
    cos = np.concatenate([np.cos(ang_r), np.cos(ang_r), np.cos(ang_c), np.cos(ang_c)], axis=1)
    sin = np.concatenate([-np.sin(ang_r), np.sin(ang_r), -np.sin(ang_c), np.sin(ang_c)], axis=1)
    reps = LANES // HEAD_DIM
    return jnp.asarray(np.tile(cos, (1, reps)), F32), jnp.asarray(np.tile(sin, (1, reps)), F32)


def _moe(h_list, afft_list, aff_list, x_list, mod_list, gains, w_gate, w_up, w_down, layer):
    caps = [CAPACITY_FACTOR * h.shape[1] // N_EXPERTS for h in h_list]
    ranks = [_topc(a, cap) for a, cap in zip(afft_list, caps)]
    long = [h.shape[1] >= GATHER_WIN * GATHER_TILE and h.shape[1] % COMBINE_ROWS == 0 and cap >= 2 * COMBINE_WIN
            for h, cap in zip(h_list, caps)]
    xs = [_gather_win(h, r[0], r[2], cap) if lg else _gather(h, r[0], cap)
          for h, r, cap, lg in zip(h_list, ranks, caps, long)]
    ys = _ffn(xs[0], xs[1] if len(xs) > 1 else None, w_gate, w_up, w_down, layer)
    return [_combine_pair(y, r[1], aff, r[2], x, mod, gains, cap) if lg else _combine(y, r[1], aff, x, mod, gains, cap)
            for y, r, aff, x, mod, cap, lg in zip(ys, ranks, aff_list, x_list, mod_list, caps, long)]


def kernel(x, c, ctx, c_ctx, w_mod, b_mod, norm_gains, w_in, w_out, gla_a_up, gla_a_bias, gla_norm,
           nat_rpb, swa_sink, w_router, w_gate, w_up, w_down):
    bsz, t, d = x.shape
    m = ctx.shape[1]
    depth = w_mod.shape[0]

    cc = jnp.concatenate([c, c_ctx[None], jnp.zeros((16 - bsz - 1, d), F32)], axis=0)
    mod_all = _modulation(cc, w_mod, b_mod)
    rope_tabs = _rope_tables(t)
    zero_state = jnp.zeros((bsz, GLA_VW, GLA_KW), F32)

    xc = ctx
    for l in range(depth):
        update_ctx = l < depth - 1
        mod = mod_all[l, :bsz].reshape(bsz, 6, d)
        mod_c = jnp.broadcast_to(mod_all[l, bsz].reshape(1, 6, d), (bsz, 6, d))
        gains = norm_gains[l]
        w_in_l = _relayout_w_in(w_in[l])
        aup = jnp.zeros((LANES, 2 * GLA_KW), F32)
        aup = aup.at[0:GLA_RANK, 0:GLA_KW].set(gla_a_up[l, 0]).at[GLA_RANK:2 * GLA_RANK, GLA_KW:].set(gla_a_up[l, 1])
        abias = gla_a_bias[l].reshape(1, 2 * GLA_KW)
        gla_gain = jnp.tile(gla_norm[l], GLA_HEADS).reshape(1, GLA_VW)
        sink_tab = jnp.broadcast_to(swa_sink[l][:, None] * LOG2E, (SWA_Q_HEADS, LANES))
        w_out_l = w_out[l].astype(BF16)
        wr = jnp.pad(w_router[l], ((0, 0), (0, LANES - N_EXPERTS))).astype(BF16)

        gla_p, nat_p, swa_p = _in_proj(x, mod, gains, w_in_l, aup, abias, rope_tabs, 1024)
        gla_c, nat_c, swa_c = _in_proj(xc, mod_c, gains, w_in_l, aup, abias, None, m)

        gla_co, s_f, s_b = _gla(gla_c, zero_state, zero_state, gla_gain)
        gla_o, _, _ = _gla(gla_p, s_f, s_b, gla_gain)
        nat_o = _nat(nat_p, nat_c, _nat_bias_table(nat_rpb[l], m))
        swa_o = _swa(swa_p, swa_c, sink_tab)

        x1, h2, aff, aff_t = _out_proj(gla_o, nat_o, swa_o, x, mod, gains, w_out_l, wr, 1024)
        if update_ctx:
            nat_co, swa_co = _ctx_attn(nat_c, swa_c, sink_tab)
            xc1, hc2, aff_c, aff_ct = _out_proj(gla_co, nat_co, swa_co, xc, mod_c, gains, w_out_l, wr, m)
            x, xc = _moe([h2, hc2], [aff_t, aff_ct], [aff, aff_c], [x1, xc1], [mod, mod_c], gains,
                         w_gate, w_up, w_down, l)
        else:
            (x,) = _moe([h2], [aff_t], [aff], [x1], [mod], gains, w_gate, w_up, w_down, l)
    return x
```

```python
import functools

import jax
import jax.numpy as jnp
import numpy as np
from jax import lax
from jax.experimental import pallas as pl
from jax.experimental.pallas import tpu as pltpu

F32 = jnp.float32
BF16 = jnp.bfloat16

D_MODEL = 1024
GRID_W = 64
HEAD_DIM = 64
GLA_HEADS = 4
GLA_DK = 32
GLA_DV = 64
GLA_RANK = 16
GLA_TAU = 16.0
GLA_CHUNK = 64
NAT_HEADS = 4
NAT_WIN_ROWS = 8
NAT_WIN_COLS = 16
SWA_Q_HEADS = 8
SWA_KV_HEADS = 2
SWA_WINDOW = 128
SWA_BLOCK = 128
ROPE_BASE = 10000.0
N_EXPERTS = 16
EXPERT_FF = 1024
CAPACITY_FACTOR = 2
NORM_EPS = 1e-6

GLA_KW = GLA_HEADS * GLA_DK
GLA_VW = GLA_HEADS * GLA_DV
NAT_W = NAT_HEADS * HEAD_DIM
SWA_QW = SWA_Q_HEADS * HEAD_DIM
SWA_KW = SWA_KV_HEADS * HEAD_DIM
LANES = 128
NEG_BIG = -1e30
LOG2E = 1.4426950408889634

GLA_OUT_W = 2 * GLA_KW + 2 * GLA_VW + 2 * GLA_KW
NAT_PROJ_W = 3 * NAT_W
SWA_PROJ_W = SWA_QW + 4 * SWA_KW
W_GK = 0
W_GV = W_GK + GLA_KW
W_GA = W_GV + GLA_VW
W_NK = W_GA + LANES
W_SK = W_NK + 2 * NAT_W
W_SV = W_SK + SWA_KW
W_Q0 = W_SV + SWA_KW
W_GQ = W_Q0
W_GG = W_GQ + GLA_KW
W_NQ = W_GG + GLA_VW
W_SQ = W_NQ + NAT_W
IN_PROJ_W = W_SQ + SWA_QW

VMEM_LIMIT = 56 * 1024 * 1024


def _cparams(*sem):
    return pltpu.CompilerParams(dimension_semantics=sem, vmem_limit_bytes=VMEM_LIMIT)


def _dot(a, b):
    return jnp.dot(a, b, preferred_element_type=F32)


def _dot_nt(a, b):
    return lax.dot_general(a, b, (((1,), (1,)), ((), ())), preferred_element_type=F32)


def _dot_tn(a, b):
    return lax.dot_general(a, b, (((0,), (0,)), ((), ())), preferred_element_type=F32)


def _split2(a):
    hi = a.astype(BF16)
    lo = (a - hi.astype(F32)).astype(BF16)
    return hi, lo


def _split3(a):
    hi = a.astype(BF16)
    r = a - hi.astype(F32)
    mid = r.astype(BF16)
    lo = (r - mid.astype(F32)).astype(BF16)
    return hi, mid, lo


def _dot_f32(a, b):
    ah, al = _split2(a)
    bh, bl = _split2(b)
    return _dot(ah, bh) + _dot(al, bh) + _dot(ah, bl)


def _rms(x, gain):
    return x * lax.rsqrt(jnp.mean(x * x, axis=-1, keepdims=True) + NORM_EPS) * gain


def _silu(x):
    return x * jax.nn.sigmoid(x)


def _mod_kernel(c_ref, w_ref, b_ref, o_ref):
    a = _silu(c_ref[...])
    o_ref[0] = _dot_f32(a, w_ref[0]) + b_ref[0]


def _modulation(cc, w_mod, b_mod):
    depth, d, n = w_mod.shape
    r = cc.shape[0]
    tn = 1536
    return pl.pallas_call(
        _mod_kernel,
        grid=(depth, n // tn),
        in_specs=[
            pl.BlockSpec((r, d), lambda l, j: (0, 0)),
            pl.BlockSpec((1, d, tn), lambda l, j: (l, 0, j)),
            pl.BlockSpec((1, 1, tn), lambda l, j: (l, 0, j)),
        ],
        out_specs=pl.BlockSpec((1, r, tn), lambda l, j: (l, 0, j)),
        out_shape=jax.ShapeDtypeStruct((depth, r, n), F32),
        compiler_params=_cparams("parallel", "parallel"),
        name="adaln_mod",
    )(cc, w_mod, b_mod.reshape(depth, 1, n))


def _rope_rotate(x, first_half):
    up = pltpu.roll(x, LANES - 16, 1)
    down = pltpu.roll(x, 16, 1)
    return jnp.where(first_half, up, down)


IN_PROJ_PARTS = 1
CUMSUM_ROWS = 512


def _in_proj_kernel(*refs, rope):
    if rope:
        x_ref, mod_ref, gain_ref, w_ref, aup_ref, ab_ref, tri_ref, cos_ref, sin_ref, gla_ref, nat_ref, swa_ref = refs
    else:
        x_ref, mod_ref, gain_ref, w_ref, aup_ref, ab_ref, tri_ref, gla_ref, nat_ref, swa_ref = refs
    tm = x_ref.shape[1]
    pm = tm // IN_PROJ_PARTS
    kq = 2 * GLA_KW + 2 * GLA_VW
    scale = HEAD_DIM ** -0.5 * LOG2E
    nck = pm // GLA_CHUNK
    tri = tri_ref[...]
    for part in range(IN_PROJ_PARTS):
        rows = pl.ds(part * pm, pm)
        x = x_ref[0, rows, :]
        y = _rms(x, gain_ref[0:1, :])
        h = (y * (1.0 + mod_ref[0, 1:2, :]) + mod_ref[0, 0:1, :]).astype(BF16)

        pkv = _dot(h, w_ref[:, 0:W_Q0])
        pq = _dot(h, w_ref[:, W_Q0:])

        gla_ref[0, rows, 0:GLA_KW] = pkv[:, W_GK:W_GK + GLA_KW]
        gla_ref[0, rows, GLA_KW:2 * GLA_KW] = pq[:, W_GQ - W_Q0:W_GQ - W_Q0 + GLA_KW]
        gla_ref[0, rows, 2 * GLA_KW:2 * GLA_KW + GLA_VW] = pkv[:, W_GV:W_GV + GLA_VW]
        gla_ref[0, rows, 2 * GLA_KW + GLA_VW:kq] = pq[:, W_GG - W_Q0:W_GG - W_Q0 + GLA_VW]
        z = _dot_f32(pkv[:, W_GA:W_GA + LANES], aup_ref[...]) + ab_ref[...]
        log_a = (jnp.minimum(z, 0.0) - jnp.log1p(jnp.exp(-jnp.abs(z)))) * (1.0 / GLA_TAU)
        hi, lo = _split2(log_a)
        cr = tri.shape[0]
        cum = jnp.concatenate([_dot(tri, hi[i * cr:(i + 1) * cr]) + _dot(tri, lo[i * cr:(i + 1) * cr])
                               for i in range(pm // cr)], axis=0)
        gla_ref[0, rows, kq:kq + GLA_KW] = cum[:, 0:GLA_KW]
        cb = cum[:, GLA_KW:].reshape(nck, GLA_CHUNK, GLA_KW)
        lb = log_a[:, GLA_KW:].reshape(nck, GLA_CHUNK, GLA_KW)
        from_end = cb[:, GLA_CHUNK - 1:GLA_CHUNK, :] - cb + lb
        gla_ref[0, rows, kq + GLA_KW:kq + 2 * GLA_KW] = from_end.reshape(pm, GLA_KW)

        nat_ref[0, rows, 0:NAT_W] = (pq[:, W_NQ - W_Q0:W_NQ - W_Q0 + NAT_W] * scale).astype(BF16)
        nat_ref[0, rows, NAT_W:] = pkv[:, W_NK:W_NK + 2 * NAT_W].astype(BF16)

        if rope:
            cos = cos_ref[rows, :]
            sin = sin_ref[rows, :]
        lane = lax.broadcasted_iota(jnp.int32, (pm, LANES), 1)
        first_half = (lane % 32) < 16
        low_head = lane < HEAD_DIM

        def rotary(t):
            return t * cos + _rope_rotate(t, first_half) * sin if rope else t

        for j in range(SWA_QW // LANES):
            t = rotary(pq[:, W_SQ - W_Q0 + j * LANES:W_SQ - W_Q0 + (j + 1) * LANES]) * scale
            swa_ref[0, rows, j * LANES:(j + 1) * LANES] = t.astype(BF16)
        for j, t in enumerate((rotary(pkv[:, W_SK:W_SK + SWA_KW]), pkv[:, W_SV:W_SV + SWA_KW])):
            other = pltpu.roll(t, HEAD_DIM, 1)
            o = SWA_QW + 2 * j * SWA_KW
            swa_ref[0, rows, o:o + LANES] = jnp.where(low_head, t, other).astype(BF16)
            swa_ref[0, rows, o + LANES:o + 2 * LANES] = jnp.where(low_head, other, t).astype(BF16)


def _in_proj(x, mod, gains, w, aup, abias, rope_tabs, tm):
    bsz, t, d = x.shape
    rope = rope_tabs is not None
    cum_rows = min(tm // IN_PROJ_PARTS, CUMSUM_ROWS)
    in_specs = [
        pl.BlockSpec((1, tm, d), lambda b, i: (b, i, 0)),
        pl.BlockSpec((1, 6, d), lambda b, i: (b, 0, 0)),
        pl.BlockSpec((4, d), lambda b, i: (0, 0)),
        pl.BlockSpec((d, IN_PROJ_W), lambda b, i: (0, 0)),
        pl.BlockSpec((LANES, 2 * GLA_KW), lambda b, i: (0, 0)),
        pl.BlockSpec((1, 2 * GLA_KW), lambda b, i: (0, 0)),
        pl.BlockSpec((cum_rows, cum_rows), lambda b, i: (0, 0)),
    ]
    r = np.arange(cum_rows)
    tri = (r[:, None] // GLA_CHUNK == r[None, :] // GLA_CHUNK) & (r[:, None] >= r[None, :])
    args = [x, mod, gains, w, aup, abias, jnp.asarray(tri, BF16)]
    if rope:
        in_specs += [pl.BlockSpec((tm, LANES), lambda b, i: (i, 0))] * 2
        args += list(rope_tabs)
    return pl.pallas_call(
        functools.partial(_in_proj_kernel, rope=rope),
        grid=(bsz, t // tm),
        in_specs=in_specs,
        out_specs=[
            pl.BlockSpec((1, tm, GLA_OUT_W), lambda b, i: (b, i, 0)),
            pl.BlockSpec((1, tm, NAT_PROJ_W), lambda b, i: (b, i, 0)),
            pl.BlockSpec((1, tm, SWA_PROJ_W), lambda b, i: (b, i, 0)),
        ],
        out_shape=[
            jax.ShapeDtypeStruct((bsz, t, GLA_OUT_W), F32),
            jax.ShapeDtypeStruct((bsz, t, NAT_PROJ_W), BF16),
            jax.ShapeDtypeStruct((bsz, t, SWA_PROJ_W), BF16),
        ],
        compiler_params=_cparams("parallel", "parallel"),
        name="in_proj_rope" if rope else "in_proj",
    )(*args)


def _gla_kernel(p_ref, s0f_ref, s0b_ref, gain_ref, o_ref, sff_ref, sfb_ref, acc_ref, sf_ref, sb_ref):
    t = p_ref.shape[1]
    c = GLA_CHUNK
    nc = t // c
    qi = lax.broadcasted_iota(jnp.int32, (c, GLA_HEADS * c), 0)
    kj = lax.broadcasted_iota(jnp.int32, (c, GLA_HEADS * c), 1) % c
    lower = qi >= kj
    upper = qi <= kj
    srow = lax.broadcasted_iota(jnp.int32, (GLA_VW, GLA_KW), 0) // GLA_DV
    scol = lax.broadcasted_iota(jnp.int32, (GLA_VW, GLA_KW), 1) // GLA_DK
    head_diag = srow == scol
    q_scale = GLA_DK ** -0.5

    sf_ref[...] = s0f_ref[0]
    sb_ref[...] = s0b_ref[0]
    acc_ref[...] = jnp.zeros_like(acc_ref)

    def state_free(base, end_row, cum_off):
        rows = pl.ds(base, c)
        k = p_ref[0, rows, 0:GLA_KW]
        q = p_ref[0, rows, GLA_KW:2 * GLA_KW] * q_scale
        vb = p_ref[0, rows, 2 * GLA_KW:2 * GLA_KW + GLA_VW].astype(BF16)
        cum = p_ref[0, rows, cum_off:cum_off + GLA_KW]
        mid = cum[c // 2:c // 2 + 1, :]
        total = cum[end_row:end_row + 1, :]
        qs = (q * jnp.exp(cum - mid)).astype(BF16)
        ks = (k * jnp.exp(mid - cum)).astype(BF16)
        s = _dot_nt(qs, _head_stack(ks, GLA_DK, GLA_HEADS))
        kd = (k * jnp.exp(total - cum)).astype(BF16)
        upd = _dot_tn(vb, kd)
        qe = (q * jnp.exp(cum)).astype(BF16)
        return rows, s, upd, qe, vb, jnp.exp(total)

    def with_state(keep, st_ref, rows, s, upd, qe, vb, decay):
        st = st_ref[...]
        inter = _dot_nt(qe, st.astype(BF16))
        s = jnp.where(keep, s, 0.0).astype(BF16)
        intra = _dot(s, _head_stack(vb, GLA_DV, GLA_HEADS))
        acc_ref[rows, :] += inter + intra
        st_ref[...] = st * decay + jnp.where(head_diag, upd, 0.0)

    def body(i, carry):
        fwd = state_free(pl.multiple_of(i * c, c), c - 1, 2 * GLA_KW + 2 * GLA_VW)
        bwd = state_free(pl.multiple_of((nc - 1 - i) * c, c), 0, 3 * GLA_KW + 2 * GLA_VW)
        with_state(lower, sf_ref, *fwd)
        with_state(upper, sb_ref, *bwd)
        return carry

    lax.fori_loop(0, nc, body, 0, unroll=4)
    sff_ref[0] = sf_ref[...]
    sfb_ref[0] = sb_ref[...]

    hr = lax.broadcasted_iota(jnp.int32, (GLA_VW, GLA_VW), 0) // GLA_DV
    hc = lax.broadcasted_iota(jnp.int32, (GLA_VW, GLA_VW), 1) // GLA_DV
    head_ones = (hr == hc).astype(BF16)
    blk = 256

    def mean_square(j):
        o = acc_ref[pl.ds(j * blk, blk), :]
        hi, mid, lo = _split3(o * o)
        return o, (_dot(hi, head_ones) + _dot(mid, head_ones) + _dot(lo, head_ones)) * (1.0 / GLA_DV)

    nxt = mean_square(0)
    for j in range(t // blk):
        rows = pl.ds(j * blk, blk)
        o, ms = nxt
        if j + 1 < t // blk:
            nxt = mean_square(j + 1)
        gate = p_ref[0, rows, 2 * GLA_KW + GLA_VW:2 * GLA_KW + 2 * GLA_VW]
        o_ref[0, rows, :] = (o * lax.rsqrt(ms + NORM_EPS) * gain_ref[...] * _silu(gate)).astype(BF16)


def _gla(p, s0f, s0b, gain):
    bsz, t, w = p.shape
    st_spec = pl.BlockSpec((1, GLA_VW, GLA_KW), lambda b: (b, 0, 0))
    st_shape = jax.ShapeDtypeStruct((bsz, GLA_VW, GLA_KW), F32)
    return pl.pallas_call(
        _gla_kernel,
        grid=(bsz,),
        in_specs=[
            pl.BlockSpec((1, t, w), lambda b: (b, 0, 0)),
            st_spec, st_spec,
            pl.BlockSpec((1, GLA_VW), lambda b: (0, 0)),
        ],
        out_specs=[pl.BlockSpec((1, t, GLA_VW), lambda b: (b, 0, 0)), st_spec, st_spec],
        out_shape=[jax.ShapeDtypeStruct((bsz, t, GLA_VW), BF16), st_shape, st_shape],
        scratch_shapes=[
            pltpu.VMEM((t, GLA_VW), F32),
            pltpu.VMEM((GLA_VW, GLA_KW), F32),
            pltpu.VMEM((GLA_VW, GLA_KW), F32),
        ],
        compiler_params=_cparams("parallel"),
        name="gla_bidir",
    )(p, s0f, s0b, gain)


def _head_stack(q, width, n):
    lane = lax.broadcasted_iota(jnp.int32, (1, q.shape[1]), 1) // width
    return jnp.concatenate([jnp.where(lane == h, q, jnp.zeros_like(q)) for h in range(n)], axis=0)


def _head_unstack(o, width, n):
    rows = o.shape[0] // n
    lane = lax.broadcasted_iota(jnp.int32, (1, o.shape[1]), 1) // width
    out = o[0:rows]
    for h in range(1, n):
        out = jnp.where(lane == h, o[h * rows:(h + 1) * rows], out)
    return out


NAT_ROWS_PER_STEP = 16


def _nat_kernel(lat_ref, ctx_ref, bias_ref, o_ref):
    n_rows = lat_ref.shape[1] // GRID_W
    kc = ctx_ref[0, :, NAT_W:2 * NAT_W]
    vc = ctx_ref[0, :, 2 * NAT_W:3 * NAT_W]
    n_loc = NAT_WIN_ROWS * GRID_W

    def scores(i):
        r = pl.program_id(1) * NAT_ROWS_PER_STEP + i
        r0 = jnp.clip(r - NAT_WIN_ROWS // 2, 0, n_rows - NAT_WIN_ROWS)
        q = lat_ref[0, pl.ds(pl.multiple_of(r * GRID_W, GRID_W), GRID_W), 0:NAT_W]
        krows = pl.ds(pl.multiple_of(r0 * GRID_W, GRID_W), n_loc)
        qs = _head_stack(q, HEAD_DIM, NAT_HEADS)
        first = r0 - r + NAT_WIN_ROWS - 1
        bias = jnp.concatenate([bias_ref[first + 2 * j] for j in range(NAT_WIN_ROWS // 2)], axis=1)
        s = jnp.concatenate([_dot_nt(qs, lat_ref[0, krows, NAT_W:2 * NAT_W]) + bias, _dot_nt(qs, kc)], axis=1)
        return s, krows

    def finish(i, s, krows):
        p = jnp.exp2((s - jnp.max(s, axis=-1, keepdims=True)).astype(BF16))
        den = jnp.sum(p, axis=-1, keepdims=True, dtype=F32)
        o = (_dot(p[:, 0:n_loc], lat_ref[0, krows, 2 * NAT_W:3 * NAT_W]) + _dot(p[:, n_loc:], vc)) / den
        o_ref[0, i * GRID_W:(i + 1) * GRID_W, :] = _head_unstack(o, HEAD_DIM, NAT_HEADS).astype(BF16)

    nxt = scores(0)
    for i in range(NAT_ROWS_PER_STEP):
        cur = nxt
        if i + 1 < NAT_ROWS_PER_STEP:
            nxt = scores(i + 1)
        finish(i, *cur)


def _nat(lat, ctx, bias):
    bsz, t, w = lat.shape
    m = ctx.shape[1]
    rows = NAT_ROWS_PER_STEP * GRID_W
    return pl.pallas_call(
        _nat_kernel,
        grid=(bsz, t // rows),
        in_specs=[
            pl.BlockSpec((1, t, w), lambda b, r: (b, 0, 0)),
            pl.BlockSpec((1, m, w), lambda b, r: (b, 0, 0)),
            pl.BlockSpec(bias.shape, lambda b, r: (0, 0, 0)),
        ],
        out_specs=pl.BlockSpec((1, rows, NAT_W), lambda b, r: (b, r, 0)),
        out_shape=jax.ShapeDtypeStruct((bsz, t, NAT_W), BF16),
        compiler_params=_cparams("parallel", "arbitrary"),
        name="nat_attn",
    )(lat, ctx, bias)


def _nat_bias_table(rpb):
    cols = np.arange(GRID_W)
    col_start = np.clip(cols - NAT_WIN_COLS // 2, 0, GRID_W - NAT_WIN_COLS)
    kc = np.arange(GRID_W)
    inside = (kc[None, :] >= col_start[:, None]) & (kc[None, :] < col_start[:, None] + NAT_WIN_COLS)
    rel = kc[None, :] - cols[:, None] + NAT_WIN_COLS - 1
    col_sel = (rel[:, :, None] == np.arange(2 * NAT_WIN_COLS - 1)) & inside[:, :, None]
    tab = jnp.einsum("hab,ckb->ahck", rpb.astype(F32), jnp.asarray(col_sel, F32), precision=lax.Precision.HIGHEST)
    tab = jnp.where(jnp.asarray(inside)[None, None], tab * LOG2E, NEG_BIG)
    tab = tab.reshape(2 * NAT_WIN_ROWS - 1, NAT_HEADS * GRID_W, GRID_W)
    return jnp.concatenate([tab[:-1], tab[1:]], axis=2)


def _sink_column(sink_ref, first_head, n_heads, rows):
    return jnp.concatenate(
        [jnp.broadcast_to(sink_ref[first_head + j:first_head + j + 1, 0:1], (rows, 1)) for j in range(n_heads)], axis=0)


def _softmax_pv_sink(s, v, sink):
    m = jnp.maximum(jnp.max(s, axis=-1, keepdims=True), sink)
    p = jnp.exp2((s - m).astype(BF16))
    o = _dot(p, jnp.concatenate([v, jnp.ones_like(v)], axis=1))
    w = v.shape[1]
    return o[:, 0:w] / (o[:, w:] + jnp.exp2(sink - m))


SWA_BLOCKS_PER_STEP = 4


def _swa_kernel(lat_ref, ctx_ref, sink_ref, o_ref):
    t = lat_ref.shape[1]
    m = ctx_ref.shape[1]
    blk = SWA_BLOCK
    n_loc = 3 * blk
    qi = lax.broadcasted_iota(jnp.int32, (blk, n_loc), 0)
    kj = lax.broadcasted_iota(jnp.int32, (blk, n_loc), 1)
    group = SWA_Q_HEADS // SWA_KV_HEADS
    chains = [(a, g) for a in range(SWA_BLOCKS_PER_STEP) for g in range(SWA_KV_HEADS)]

    def scores(a, g):
        n = pl.program_id(1) * SWA_BLOCKS_PER_STEP + a
        base = pl.multiple_of(jnp.clip((n - 1) * blk, 0, t - n_loc), blk)
        qrows = pl.ds(pl.multiple_of(n * blk, blk), blk)
        krows = pl.ds(base, n_loc)
        mask_add = jnp.where(jnp.abs(base + kj - n * blk - qi) <= SWA_WINDOW, 0.0, NEG_BIG)[None]
        kcol = pl.ds(SWA_QW + g * LANES, LANES)
        k = jnp.concatenate([lat_ref[0, krows, kcol], ctx_ref[0, :, kcol]], axis=0)
        qs = jnp.concatenate(
            [_head_stack(lat_ref[0, qrows, pl.ds((g * 2 + p) * LANES, LANES)], HEAD_DIM, 2) for p in range(2)], axis=0)
        s = _dot_nt(qs, k).reshape(group, blk, n_loc + m)
        s = jnp.concatenate([s[:, :, 0:n_loc] + mask_add, s[:, :, n_loc:]], axis=2).reshape(group * blk, n_loc + m)
        return s, krows

    def finish(a, g, s, krows):
        vcol = pl.ds(SWA_QW + 2 * SWA_KW + g * LANES, LANES)
        v = jnp.concatenate([lat_ref[0, krows, vcol], ctx_ref[0, :, vcol]], axis=0)
        o = _softmax_pv_sink(s, v, _sink_column(sink_ref, g * group, group, blk))
        for p in range(2):
            pair = _head_unstack(o[2 * p * blk:(2 * p + 2) * blk], HEAD_DIM, 2)
            o_ref[0, a * blk:(a + 1) * blk, pl.ds((g * 2 + p) * LANES, LANES)] = pair.astype(BF16)

    nxt = scores(*chains[0])
    for i, (a, g) in enumerate(chains):
        cur = nxt
        if i + 1 < len(chains):
            nxt = scores(*chains[i + 1])
        finish(a, g, *cur)


def _swa(lat, ctx, sink_tab):
    bsz, t, w = lat.shape
    m = ctx.shape[1]
    rows = SWA_BLOCKS_PER_STEP * SWA_BLOCK
    return pl.pallas_call(
        _swa_kernel,
        grid=(bsz, t // rows),
        in_specs=[
            pl.BlockSpec((1, t, w), lambda b, n: (b, 0, 0)),
            pl.BlockSpec((1, m, w), lambda b, n: (b, 0, 0)),
            pl.BlockSpec((SWA_Q_HEADS, LANES), lambda b, n: (0, 0)),
        ],
        out_specs=pl.BlockSpec((1, rows, SWA_QW), lambda b, n: (b, n, 0)),
        out_shape=jax.ShapeDtypeStruct((bsz, t, SWA_QW), BF16),
        compiler_params=_cparams("parallel", "arbitrary"),
        name="swa_attn",
    )(lat, ctx, sink_tab)


def _ctx_attn_kernel(nat_ref, swa_ref, sink_ref, on_ref, os_ref):
    m = nat_ref.shape[1]
    q = nat_ref[0, :, 0:NAT_W]
    k = nat_ref[0, :, NAT_W:2 * NAT_W]
    v = nat_ref[0, :, 2 * NAT_W:3 * NAT_W]
    group = SWA_Q_HEADS // SWA_KV_HEADS

    def swa_scores(g):
        k = swa_ref[0, :, pl.ds(SWA_QW + g * LANES, LANES)]
        qs = jnp.concatenate(
            [_head_stack(swa_ref[0, :, pl.ds((g * 2 + p) * LANES, LANES)], HEAD_DIM, 2) for p in range(2)], axis=0)
        return _dot_nt(qs, k)

    s = _dot_nt(_head_stack(q, HEAD_DIM, NAT_HEADS), k)
    s_swa = [swa_scores(g) for g in range(SWA_KV_HEADS)]
    p = jnp.exp2((s - jnp.max(s, axis=-1, keepdims=True)).astype(BF16))
    o = _dot(p, v) / jnp.sum(p, axis=-1, keepdims=True, dtype=F32)
    on_ref[0] = _head_unstack(o, HEAD_DIM, NAT_HEADS).astype(BF16)

    for g in range(SWA_KV_HEADS):
        v = swa_ref[0, :, pl.ds(SWA_QW + 2 * SWA_KW + g * LANES, LANES)]
        o = _softmax_pv_sink(s_swa[g], v, _sink_column(sink_ref, g * group, group, m))
        for pr in range(2):
            pair = _head_unstack(o[2 * pr * m:(2 * pr + 2) * m], HEAD_DIM, 2)
            os_ref[0, :, pl.ds((g * 2 + pr) * LANES, LANES)] = pair.astype(BF16)


def _ctx_attn(nat_c, swa_c, sink_tab):
    bsz, m, _ = nat_c.shape
    return pl.pallas_call(
        _ctx_attn_kernel,
        grid=(bsz,),
        in_specs=[
            pl.BlockSpec((1, m, NAT_PROJ_W), lambda b: (b, 0, 0)),
            pl.BlockSpec((1, m, SWA_PROJ_W), lambda b: (b, 0, 0)),
            pl.BlockSpec((SWA_Q_HEADS, LANES), lambda b: (0, 0)),
        ],
        out_specs=[
            pl.BlockSpec((1, m, NAT_W), lambda b: (b, 0, 0)),
            pl.BlockSpec((1, m, SWA_QW), lambda b: (b, 0, 0)),
        ],
        out_shape=[
            jax.ShapeDtypeStruct((bsz, m, NAT_W), BF16),
            jax.ShapeDtypeStruct((bsz, m, SWA_QW), BF16),
        ],
        compiler_params=_cparams("parallel"),
        name="ctx_attn",
    )(nat_c, swa_c, sink_tab)


OUT_PROJ_PARTS = 2


def _out_proj_kernel(gla_ref, nat_ref, swa_ref, x_ref, mod_ref, gain_ref, w_ref, wr_ref,
                     x1_ref, h2_ref, aff_ref, afft_ref):
    tm = x_ref.shape[1]
    half = tm // OUT_PROJ_PARTS

    def project(part):
        rows = pl.ds(part * half, half)
        return (_dot(gla_ref[0, rows, :], w_ref[0:GLA_VW, :])
                + _dot(nat_ref[0, rows, :], w_ref[GLA_VW:GLA_VW + NAT_W, :])
                + _dot(swa_ref[0, rows, :], w_ref[GLA_VW + NAT_W:, :]))

    nxt = project(0)
    for part in range(OUT_PROJ_PARTS):
        rows = pl.ds(part * half, half)
        mix = nxt
        if part + 1 < OUT_PROJ_PARTS:
            nxt = project(part + 1)
        x1 = x_ref[0, rows, :] + mod_ref[0, 2:3, :] * _rms(mix, gain_ref[1:2, :])
        x1_ref[0, rows, :] = x1
        h2 = (_rms(x1, gain_ref[2:3, :]) * (1.0 + mod_ref[0, 4:5, :]) + mod_ref[0, 3:4, :]).astype(BF16)
        h2_ref[0, rows, :] = h2
        logits = _dot(h2, wr_ref[...])
        lane = lax.broadcasted_iota(jnp.int32, logits.shape, 1)
        logits = jnp.where(lane < N_EXPERTS, logits, NEG_BIG)
        e = jnp.exp(logits - jnp.max(logits, axis=-1, keepdims=True))
        aff = e / jnp.sum(e, axis=-1, keepdims=True)
        aff_ref[0, rows, :] = aff
        afft_ref[0, :, rows] = aff.T[0:N_EXPERTS, :]


def _out_proj(gla_o, nat_o, swa_o, x, mod, gains, w_out, wr, tm):
    bsz, t, d = x.shape
    tok = lambda b, i: (b, i, 0)
    const = lambda b, i: (0, 0)
    return pl.pallas_call(
        _out_proj_kernel,
        grid=(bsz, t // tm),
        in_specs=[
            pl.BlockSpec((1, tm, GLA_VW), tok),
            pl.BlockSpec((1, tm, NAT_W), tok),
            pl.BlockSpec((1, tm, SWA_QW), tok),
            pl.BlockSpec((1, tm, d), tok),
            pl.BlockSpec((1, 6, d), lambda b, i: (b, 0, 0)),
            pl.BlockSpec((4, d), const),
            pl.BlockSpec((d, d), const),
            pl.BlockSpec((d, LANES), const),
        ],
        out_specs=[
            pl.BlockSpec((1, tm, d), tok),
            pl.BlockSpec((1, tm, d), tok),
            pl.BlockSpec((1, tm, LANES), tok),
            pl.BlockSpec((1, N_EXPERTS, tm), lambda b, i: (b, 0, i)),
        ],
        out_shape=[
            jax.ShapeDtypeStruct((bsz, t, d), F32),
            jax.ShapeDtypeStruct((bsz, t, d), BF16),
            jax.ShapeDtypeStruct((bsz, t, LANES), F32),
            jax.ShapeDtypeStruct((bsz, N_EXPERTS, t), F32),
        ],
        compiler_params=_cparams("parallel", "parallel"),
        name="out_proj_router",
    )(gla_o, nat_o, swa_o, x, mod, gains, w_out, wr)


def _excl_prefix(x):
    rows, n = x.shape
    nblk = n // LANES
    r = lax.broadcasted_iota(jnp.int32, (LANES, LANES), 0)
    c = lax.broadcasted_iota(jnp.int32, (LANES, LANES), 1)
    strict_upper = (r < c).astype(BF16)
    stacked = jnp.concatenate([x[:, j * LANES:(j + 1) * LANES] for j in range(nblk)], axis=0).astype(BF16)
    local = _dot(stacked, strict_upper)
    totals = jnp.sum(stacked.astype(F32), axis=-1, keepdims=True)
    out = []
    offs = [jnp.zeros((rows, 1), F32)]
    for j in range(nblk):
        out.append(local[j * rows:(j + 1) * rows] + offs[-1])
        offs.append(offs[-1] + totals[j * rows:(j + 1) * rows])
    return jnp.concatenate(out, axis=1), offs


def _topc_kernel(aff_ref, rank_ref, rankcol_ref, cnt_ref, *, cap):
    aff = aff_ref[...]

    def step(i, thr):
        cand = thr | (jnp.int32(1) << (30 - i))
        cnt = jnp.sum((aff >= pltpu.bitcast(cand, F32)).astype(F32), axis=-1, keepdims=True)
        return jnp.where(cnt >= cap, cand, thr)

    thr = lax.fori_loop(0, 31, step, jnp.zeros((aff.shape[0], 1), jnp.int32))
    thr = pltpu.bitcast(thr, F32)
    gt = aff > thr
    eq = aff == thr
    need = cap - jnp.sum(gt.astype(F32), axis=-1, keepdims=True)
    eq_before, _ = _excl_prefix(eq.astype(F32))
    sel = gt | (eq & (eq_before < need))
    sel_before, offs = _excl_prefix(sel.astype(F32))
    rank = jnp.where(sel, sel_before, -1.0)
    rank_ref[...] = rank
    lane = lax.broadcasted_iota(jnp.int32, (rank.shape[0], LANES), 1)
    cnt = jnp.zeros((rank.shape[0], LANES), F32)
    for j, off in enumerate(offs):
        cnt = jnp.where(lane == j, off, cnt)
    cnt_ref[...] = cnt.astype(jnp.int32)
    n = rank.shape[1]
    fill = jnp.full((LANES - N_EXPERTS, n), -1.0, F32)
    for b in range(rankcol_ref.shape[0]):
        rankcol_ref[b] = jnp.concatenate([rank[b * N_EXPERTS:(b + 1) * N_EXPERTS], fill], axis=0).T


def _topc(aff_t, cap):
    bsz, e, n = aff_t.shape
    rows = bsz * e
    rank, rankcol, cnt = pl.pallas_call(
        functools.partial(_topc_kernel, cap=cap),
        grid=(1,),
        in_specs=[pl.BlockSpec((rows, n), lambda i: (0, 0))],
        out_specs=[pl.BlockSpec((rows, n), lambda i: (0, 0)), pl.BlockSpec((bsz, n, LANES), lambda i: (0, 0, 0)),
                   pl.BlockSpec((rows, LANES), lambda i: (0, 0))],
        out_shape=[jax.ShapeDtypeStruct((rows, n), F32), jax.ShapeDtypeStruct((bsz, n, LANES), F32),
                   jax.ShapeDtypeStruct((rows, LANES), jnp.int32)],
        compiler_params=_cparams("arbitrary"),
        name="expert_topc",
    )(aff_t.reshape(rows, n))
    return rank.reshape(bsz, e, n), rankcol, cnt.reshape(bsz, e, LANES)


def _gather_kernel(h_ref, rank_ref, xs_ref, *, cap):
    n = h_ref.shape[1]
    slot = lax.broadcasted_iota(jnp.int32, (cap, n), 0).astype(F32)
    h = h_ref[0]
    for e in range(N_EXPERTS):
        onehot = (slot == rank_ref[0, e:e + 1, :]).astype(BF16)
        xs_ref[e] = _dot(onehot, h).astype(BF16)


def _gather(h, rank, cap):
    bsz, n, d = h.shape
    return pl.pallas_call(
        functools.partial(_gather_kernel, cap=cap),
        grid=(bsz,),
        in_specs=[
            pl.BlockSpec((1, n, d), lambda b: (b, 0, 0)),
            pl.BlockSpec((1, N_EXPERTS, n), lambda b: (b, 0, 0)),
        ],
        out_specs=pl.BlockSpec((N_EXPERTS, cap, d), lambda b: (0, b, 0)),
        out_shape=jax.ShapeDtypeStruct((N_EXPERTS, bsz * cap, d), BF16),
        compiler_params=_cparams("parallel"),
        name="moe_gather",
    )(h, rank)


GATHER_TILE = 256
GATHER_WIN = 5
GATHER_SLOTS = 128


def _gather_win_kernel(cnt_ref, h_ref, rank_ref, xs_ref, *, cap):
    b = pl.program_id(0)
    n = h_ref.shape[1]
    nt = n // GATHER_TILE
    win = GATHER_WIN * GATHER_TILE
    slot_w = lax.broadcasted_iota(jnp.int32, (GATHER_SLOTS, win), 0).astype(F32).astype(BF16)
    slot_f = lax.broadcasted_iota(jnp.int32, (cap, n), 0).astype(F32).astype(BF16)
    one = jnp.ones((), BF16)
    zero = jnp.zeros((), BF16)

    n_sb = cap // GATHER_SLOTS
    group = 2

    def per_group(g, carry):
        starts, fits = [], []
        for j in range(group):
            base = (g * group + j) * (nt + 1)
            c = [cnt_ref[b, base + kt] for kt in range(nt + 1)]
            for sb in range(n_sb):
                lo_slot = sb * GATHER_SLOTS
                first = sum((c[kt + 1] <= lo_slot).astype(jnp.int32) for kt in range(nt))
                last = sum((c[kt] < lo_slot + GATHER_SLOTS).astype(jnp.int32) for kt in range(nt))
                fits.append(last - first <= GATHER_WIN)
                starts.append(jnp.minimum(first, nt - GATHER_WIN))
        all_fit = functools.reduce(jnp.logical_and, fits)

        @pl.when(all_fit)
        def _():
            for j in range(group):
                e = g * group + j
                for sb in range(n_sb):
                    start = starts[j * n_sb + sb]
                    rk = jnp.concatenate([rank_ref[0, e, pl.ds(start + i, 1), :] for i in range(GATHER_WIN)], axis=1)
                    onehot = jnp.where(slot_w == (rk - sb * GATHER_SLOTS).astype(BF16), one, zero)
                    hw = h_ref[0, pl.ds(pl.multiple_of(start * GATHER_TILE, GATHER_TILE), win), :]
                    xs_ref[e, pl.ds(sb * GATHER_SLOTS, GATHER_SLOTS), :] = _dot(onehot, hw).astype(BF16)

        @pl.when(jnp.logical_not(all_fit))
        def _():
            for j in range(group):
                e = g * group + j
                rk = jnp.concatenate([rank_ref[0, e, kt:kt + 1, :] for kt in range(nt)], axis=1).astype(BF16)
                onehot = jnp.where(slot_f == rk, one, zero)
                xs_ref[e] = _dot(onehot, h_ref[0]).astype(BF16)
        return carry

    lax.fori_loop(0, N_EXPERTS // group, per_group, 0)


def _gather_win(h, rank, cnt, cap):
    bsz, n, d = h.shape
    nt = n // GATHER_TILE
    step = GATHER_TILE // LANES
    bounds = cnt[:, :, 0:nt * step + 1:step].reshape(bsz, N_EXPERTS * (nt + 1))
    return pl.pallas_call(
        functools.partial(_gather_win_kernel, cap=cap),
        grid_spec=pltpu.PrefetchScalarGridSpec(
            num_scalar_prefetch=1,
            grid=(bsz,),
            in_specs=[
                pl.BlockSpec((1, n, d), lambda b, c: (b, 0, 0)),
                pl.BlockSpec((1, N_EXPERTS, nt, GATHER_TILE), lambda b, c: (b, 0, 0, 0)),
            ],
            out_specs=pl.BlockSpec((N_EXPERTS, cap, d), lambda b, c: (0, b, 0)),
        ),
        out_shape=jax.ShapeDtypeStruct((N_EXPERTS, bsz * cap, d), BF16),
        compiler_params=_cparams("parallel"),
        name="moe_gather_win",
    )(bounds, h, rank.reshape(bsz, N_EXPERTS, nt, GATHER_TILE))


FFN_PARTS = 4


def _ffn_kernel(*refs, has_ctx):
    if has_ctx:
        xs_ref, xc_ref, wg_ref, wu_ref, wd_ref, y_ref, yc_ref, wg_bf, wu_bf, wd_bf = refs
    else:
        xs_ref, wg_ref, wu_ref, wd_ref, y_ref, wg_bf, wu_bf, wd_bf = refs
    s = pl.program_id(0)
    j = pl.program_id(1)
    n_exp = pl.num_programs(0) - 1
    piece = wg_ref.shape[2]

    @pl.when((s < n_exp) & (j < FFN_PARTS))
    def _():
        rows = pl.ds(pl.multiple_of(j * piece, piece), piece)
        wg_bf[s % 2, rows, :] = wg_ref[0, 0].astype(BF16)
        wu_bf[s % 2, rows, :] = wu_ref[0, 0].astype(BF16)
        wd_bf[s % 2, rows, :] = wd_ref[0, 0].astype(BF16)

    def swiglu(x_ref, o_ref):
        cur = (s + 1) % 2
        xs = x_ref[0]
        hid = (_silu(_dot(xs, wg_bf[cur])) * _dot(xs, wu_bf[cur])).astype(BF16)
        o_ref[0] = _dot(hid, wd_bf[cur]).astype(BF16)

    @pl.when((s > 0) & (j < FFN_PARTS))
    def _():
        swiglu(xs_ref, y_ref)

    if has_ctx:
        @pl.when((s > 0) & (j == FFN_PARTS))
        def _():
            swiglu(xc_ref, yc_ref)


def _ffn(xs, xc, w_gate, w_up, w_down, layer):
    _, e, d, ff = w_gate.shape
    rb = xs.shape[1] // FFN_PARTS
    has_ctx = xc is not None
    last = FFN_PARTS - 1

    def x_idx(s, j):
        return (jnp.maximum(s - 1, 0), jnp.where(s > 0, jnp.minimum(j, last), 0), 0)

    def w_idx(s, j):
        return (layer, jnp.minimum(s, e - 1), jnp.minimum(j, last), 0)

    xspec = pl.BlockSpec((1, rb, d), x_idx)
    cspecs, cargs, cshapes = [], [], []
    if has_ctx:
        cspecs = [pl.BlockSpec((1, xc.shape[1], d), lambda s, j: (jnp.maximum(s - 1, 0), 0, 0))]
        cargs = [xc]
        cshapes = [jax.ShapeDtypeStruct(xc.shape, BF16)]
    out = pl.pallas_call(
        functools.partial(_ffn_kernel, has_ctx=has_ctx),
        grid=(e + 1, FFN_PARTS + int(has_ctx)),
        in_specs=[xspec] + cspecs + [
            pl.BlockSpec((1, 1, d // FFN_PARTS, ff), w_idx),
            pl.BlockSpec((1, 1, d // FFN_PARTS, ff), w_idx),
            pl.BlockSpec((1, 1, ff // FFN_PARTS, d), w_idx),
        ],
        out_specs=[xspec] + cspecs,
        out_shape=[jax.ShapeDtypeStruct(xs.shape, BF16)] + cshapes,
        scratch_shapes=[pltpu.VMEM((2, d, ff), BF16), pltpu.VMEM((2, d, ff), BF16), pltpu.VMEM((2, ff, d), BF16)],
        compiler_params=_cparams("arbitrary", "arbitrary"),
        name="moe_ffn",
    )(xs, *cargs, w_gate, w_up, w_down)
    return out


def _combine_kernel(y_ref, rankcol_ref, aff_ref, x_ref, mod_ref, gain_ref, o_ref, *, cap):
    rb = x_ref.shape[1]
    slot = lax.broadcasted_iota(jnp.int32, (rb, cap), 1).astype(F32)
    rc = rankcol_ref[0]
    af = aff_ref[0]
    acc = jnp.zeros((rb, x_ref.shape[2]), F32)
    for e in range(N_EXPERTS):
        onehot = (rc[:, e:e + 1] == slot).astype(BF16)
        acc = acc + af[:, e:e + 1] * _dot(onehot, y_ref[e])
    o_ref[0] = x_ref[0] + mod_ref[0, 5:6, :] * _rms(acc, gain_ref[3:4, :])


def _combine(y, rankcol, aff, x, mod, gains, cap):
    bsz, n, d = x.shape
    rb = min(n, 512)
    tok = lambda b, i: (b, i, 0)
    return pl.pallas_call(
        functools.partial(_combine_kernel, cap=cap),
        grid=(bsz, n // rb),
        in_specs=[
            pl.BlockSpec((N_EXPERTS, cap, d), lambda b, i: (0, b, 0)),
            pl.BlockSpec((1, rb, LANES), tok),
            pl.BlockSpec((1, rb, LANES), tok),
            pl.BlockSpec((1, rb, d), tok),
            pl.BlockSpec((1, 6, d), lambda b, i: (b, 0, 0)),
            pl.BlockSpec((4, d), lambda b, i: (0, 0)),
        ],
        out_specs=pl.BlockSpec((1, rb, d), tok),
        out_shape=jax.ShapeDtypeStruct((bsz, n, d), F32),
        compiler_params=_cparams("parallel", "arbitrary"),
        name="moe_combine",
    )(y, rankcol, aff, x, mod, gains)


COMBINE_ROWS = 512
COMBINE_WIN = 128


def _combine_pair_kernel(cnt_ref, y_ref, rankcol_ref, aff_ref, x_ref, mod_ref, gain_ref, o_ref, extra_ref, *, cap):
    b = pl.program_id(0)
    i = pl.program_id(1)
    nrb = pl.num_programs(1)
    rb = x_ref.shape[1]
    slot_w = lax.broadcasted_iota(jnp.int32, (rb, COMBINE_WIN), 1).astype(F32)
    rc = rankcol_ref[0]
    af = aff_ref[0]
    los, unfit = [], []
    for e in range(N_EXPERTS):
        c0 = cnt_ref[b, e * (nrb + 1) + i]
        c1 = cnt_ref[b, e * (nrb + 1) + i + 1]
        lo = jnp.minimum((c0 // 16) * 16, cap - COMBINE_WIN)
        los.append(pl.multiple_of(lo, 16))
        unfit.append(c1 - lo > COMBINE_WIN)
    def operands(e0):
        hot, rows = [], []
        for e in (e0, e0 + 1):
            hot.append(jnp.where(rc[:, e:e + 1] - los[e].astype(F32) == slot_w, af[:, e:e + 1], 0.0).astype(BF16))
            rows.append(y_ref[e, pl.ds(los[e], COMBINE_WIN), :])
        return jnp.concatenate(hot, axis=1), jnp.concatenate(rows, axis=0)

    acc = jnp.zeros((rb, x_ref.shape[2]), F32)
    nxt = operands(0)
    for e0 in range(0, N_EXPERTS, 2):
        cur = nxt
        if e0 + 2 < N_EXPERTS:
            nxt = operands(e0 + 2)
        acc = acc + _dot(*cur)

    def finish(total):
        o_ref[0] = x_ref[0] + mod_ref[0, 5:6, :] * _rms(total, gain_ref[3:4, :])

    any_unfit = functools.reduce(jnp.logical_or, unfit)

    @pl.when(jnp.logical_not(any_unfit))
    def _():
        finish(acc)

    @pl.when(any_unfit)
    def _():
        extra_ref[...] = jnp.zeros_like(extra_ref)
        slot_f = lax.broadcasted_iota(jnp.int32, (rb, cap), 1)
        for e in range(N_EXPERTS):
            @pl.when(unfit[e])
            def _():
                outside = (slot_f < los[e]) | (slot_f >= los[e] + COMBINE_WIN)
                hot = jnp.where((rc[:, e:e + 1] == slot_f.astype(F32)) & outside, af[:, e:e + 1], 0.0).astype(BF16)
                extra_ref[...] += _dot(hot, y_ref[e])
        finish(acc + extra_ref[...])


def _combine_pair(y, rankcol, aff, cnt, x, mod, gains, cap):
    bsz, n, d = x.shape
    rb = COMBINE_ROWS
    nrb = n // rb
    step = rb // LANES
    bounds = cnt[:, :, 0:nrb * step + 1:step].reshape(bsz, N_EXPERTS * (nrb + 1))
    tok = lambda b, i, c: (b, i, 0)
    return pl.pallas_call(
        functools.partial(_combine_pair_kernel, cap=cap),
        grid_spec=pltpu.PrefetchScalarGridSpec(
            num_scalar_prefetch=1,
            grid=(bsz, nrb),
            in_specs=[
                pl.BlockSpec((N_EXPERTS, cap, d), lambda b, i, c: (0, b, 0)),
                pl.BlockSpec((1, rb, LANES), tok),
                pl.BlockSpec((1, rb, LANES), tok),
                pl.BlockSpec((1, rb, d), tok),
                pl.BlockSpec((1, 6, d), lambda b, i, c: (b, 0, 0)),
                pl.BlockSpec((4, d), lambda b, i, c: (0, 0)),
            ],
            out_specs=pl.BlockSpec((1, rb, d), tok),
            scratch_shapes=[pltpu.VMEM((rb, d), F32)],
        ),
        out_shape=jax.ShapeDtypeStruct((bsz, n, d), F32),
        compiler_params=_cparams("parallel", "arbitrary"),
        name="moe_combine_pair",
    )(bounds, y, rankcol, aff, x, mod, gains)


def _relayout_w_in(w):
    cut = GLA_KW + GLA_VW + 2 * GLA_RANK
    pad = jnp.zeros((w.shape[0], LANES - 2 * GLA_RANK), w.dtype)
    return jnp.concatenate([w[:, :cut], pad, w[:, cut:]], axis=1).astype(BF16)


def _rope_tables(t):
    half = HEAD_DIM // 4
    freqs = ROPE_BASE ** (-np.arange(half, dtype=np.float32) / half)
    pos = np.arange(t)
    ang_r = (pos // GRID_W).astype(np.float32)[:, None] * freqs
    ang_c = (pos % GRID_W).astype(np.float32)[:, None] * freqs
    cos = np.concatenate([np.cos(ang_r), np.cos(ang_r), np.cos(ang_c), np.cos(ang_c)], axis=1)
    sin = np.concatenate([-np.sin(ang_r), np.sin(ang_r), -np.sin(ang_c), np.sin(ang_c)], axis=1)
    reps = LANES // HEAD_DIM
    return jnp.asarray(np.tile(cos, (1, reps)), F32), jnp.asarray(np.tile(sin, (1, reps)), F32)


def _moe(h_list, afft_list, aff_list, x_list, mod_list, gains, w_gate, w_up, w_down, layer):
    caps = [CAPACITY_FACTOR * h.shape[1] // N_EXPERTS for h in h_list]
    ranks = [_topc(a, cap) for a, cap in zip(afft_list, caps)]
    long = [h.shape[1] >= GATHER_WIN * GATHER_TILE and h.shape[1] % COMBINE_ROWS == 0 and cap >= 2 * COMBINE_WIN
            for h, cap in zip(h_list, caps)]
    xs = [_gather_win(h, r[0], r[2], cap) if lg else _gather(h, r[0], cap)
          for h, r, cap, lg in zip(h_list, ranks, caps, long)]
    ys = _ffn(xs[0], xs[1] if len(xs) > 1 else None, w_gate, w_up, w_down, layer)
    return [_combine_pair(y, r[1], aff, r[2], x, mod, gains, cap) if lg else _combine(y, r[1], aff, x, mod, gains, cap)
            for y, r, aff, x, mod, cap, lg in zip(ys, ranks, aff_list, x_list, mod_list, caps, long)]


def kernel(x, c, ctx, c_ctx, w_mod, b_mod, norm_gains, w_in, w_out, gla_a_up, gla_a_bias, gla_norm,
           nat_rpb, swa_sink, w_router, w_gate, w_up, w_down):
    bsz, t, d = x.shape
    m = ctx.shape[1]
    depth = w_mod.shape[0]

    cc = jnp.concatenate([c, c_ctx[None], jnp.zeros((16 - bsz - 1, d), F32)], axis=0)
    mod_all = _modulation(cc, w_mod, b_mod)
    rope_tabs = _rope_tables(t)
    zero_state = jnp.zeros((bsz, GLA_VW, GLA_KW), F32)

    xc = ctx
    for l in range(depth):
        update_ctx = l < depth - 1
        mod = mod_all[l, :bsz].reshape(bsz, 6, d)
        mod_c = jnp.broadcast_to(mod_all[l, bsz].reshape(1, 6, d), (bsz, 6, d))
        gains = norm_gains[l]
        w_in_l = _relayout_w_in(w_in[l])
        aup = jnp.zeros((LANES, 2 * GLA_KW), F32)
        aup = aup.at[0:GLA_RANK, 0:GLA_KW].set(gla_a_up[l, 0]).at[GLA_RANK:2 * GLA_RANK, GLA_KW:].set(gla_a_up[l, 1])
        abias = gla_a_bias[l].reshape(1, 2 * GLA_KW)
        gla_gain = jnp.tile(gla_norm[l], GLA_HEADS).reshape(1, GLA_VW)
        sink_tab = jnp.broadcast_to(swa_sink[l][:, None] * LOG2E, (SWA_Q_HEADS, LANES))
        w_out_l = w_out[l].astype(BF16)
        wr = jnp.pad(w_router[l], ((0, 0), (0, LANES - N_EXPERTS))).astype(BF16)

        gla_p, nat_p, swa_p = _in_proj(x, mod, gains, w_in_l, aup, abias, rope_tabs, 1024)
        gla_c, nat_c, swa_c = _in_proj(xc, mod_c, gains, w_in_l, aup, abias, None, m)

        gla_co, s_f, s_b = _gla(gla_c, zero_state, zero_state, gla_gain)
        gla_o, _, _ = _gla(gla_p, s_f, s_b, gla_gain)
        nat_o = _nat(nat_p, nat_c, _nat_bias_table(nat_rpb[l]))
        swa_o = _swa(swa_p, swa_c, sink_tab)

        x1, h2, aff, aff_t = _out_proj(gla_o, nat_o, swa_o, x, mod, gains, w_out_l, wr, 1024)
        if update_ctx:
            nat_co, swa_co = _ctx_attn(nat_c, swa_c, sink_tab)
            xc1, hc2, aff_c, aff_ct = _out_proj(gla_co, nat_co, swa_co, xc, mod_c, gains, w_out_l, wr, m)
            x, xc = _moe([h2, hc2], [aff_t, aff_ct], [aff, aff_c], [x1, xc1], [mod, mod_c], gains,
                         w_gate, w_up, w_down, l)
        else:
            (x,) = _moe([h2], [aff_t], [aff], [x1], [mod], gains, w_gate, w_up, w_down, l)
    return x
```

```python
import functools

import jax
import jax.numpy as jnp
import numpy as np
from jax import lax
from jax.experimental import pallas as pl
from jax.experimental.pallas import tpu as pltpu

F32 = jnp.float32
BF16 = jnp.bfloat16

D_MODEL = 1024
GRID_W = 64
HEAD_DIM = 64
GLA_HEADS = 4
GLA_DK = 32
GLA_DV = 64
GLA_RANK = 16
GLA_TAU = 16.0
GLA_CHUNK = 64
NAT_HEADS = 4
NAT_WIN_ROWS = 8
NAT_WIN_COLS = 16
SWA_Q_HEADS = 8
SWA_KV_HEADS = 2
SWA_WINDOW = 128
SWA_BLOCK = 128
ROPE_BASE = 10000.0
N_EXPERTS = 16
EXPERT_FF = 1024
CAPACITY_FACTOR = 2
NORM_EPS = 1e-6

GLA_KW = GLA_HEADS * GLA_DK
GLA_VW = GLA_HEADS * GLA_DV
NAT_W = NAT_HEADS * HEAD_DIM
SWA_QW = SWA_Q_HEADS * HEAD_DIM
SWA_KW = SWA_KV_HEADS * HEAD_DIM
LANES = 128
NEG_BIG = -1e30
LOG2E = 1.4426950408889634

GLA_OUT_W = 2 * GLA_KW + 2 * GLA_VW + 2 * GLA_KW
NAT_PROJ_W = 3 * NAT_W
SWA_PROJ_W = SWA_QW + 4 * SWA_KW
W_GK = 0
W_GV = W_GK + GLA_KW
W_GA = W_GV + GLA_VW
W_NK = W_GA + LANES
W_SK = W_NK + 2 * NAT_W
W_SV = W_SK + SWA_KW
W_Q0 = W_SV + SWA_KW
W_GQ = W_Q0
W_GG = W_GQ + GLA_KW
W_NQ = W_GG + GLA_VW
W_SQ = W_NQ + NAT_W
IN_PROJ_W = W_SQ + SWA_QW

VMEM_LIMIT = 56 * 1024 * 1024


def _cparams(*sem):
    return pltpu.CompilerParams(dimension_semantics=sem, vmem_limit_bytes=VMEM_LIMIT)


def _dot(a, b):
    return jnp.dot(a, b, preferred_element_type=F32)


def _dot_nt(a, b):
    return lax.dot_general(a, b, (((1,), (1,)), ((), ())), preferred_element_type=F32)


def _dot_tn(a, b):
    return lax.dot_general(a, b, (((0,), (0,)), ((), ())), preferred_element_type=F32)


def _split2(a):
    hi = a.astype(BF16)
    lo = (a - hi.astype(F32)).astype(BF16)
    return hi, lo


def _split3(a):
    hi = a.astype(BF16)
    r = a - hi.astype(F32)
    mid = r.astype(BF16)
    lo = (r - mid.astype(F32)).astype(BF16)
    return hi, mid, lo


def _dot_f32(a, b):
    ah, al = _split2(a)
    bh, bl = _split2(b)
    return _dot(ah, bh) + _dot(al, bh) + _dot(ah, bl)


def _rms(x, gain):
    return x * lax.rsqrt(jnp.mean(x * x, axis=-1, keepdims=True) + NORM_EPS) * gain


def _silu(x):
    return x * jax.nn.sigmoid(x)


def _mod_kernel(c_ref, w_ref, b_ref, o_ref):
    a = _silu(c_ref[...])
    o_ref[0] = _dot_f32(a, w_ref[0]) + b_ref[0]


def _modulation(cc, w_mod, b_mod):
    depth, d, n = w_mod.shape
    r = cc.shape[0]
    tn = 1536
    return pl.pallas_call(
        _mod_kernel,
        grid=(depth, n // tn),
        in_specs=[
            pl.BlockSpec((r, d), lambda l, j: (0, 0)),
            pl.BlockSpec((1, d, tn), lambda l, j: (l, 0, j)),
            pl.BlockSpec((1, 1, tn), lambda l, j: (l, 0, j)),
        ],
        out_specs=pl.BlockSpec((1, r, tn), lambda l, j: (l, 0, j)),
        out_shape=jax.ShapeDtypeStruct((depth, r, n), F32),
        compiler_params=_cparams("parallel", "parallel"),
        name="adaln_mod",
    )(cc, w_mod, b_mod.reshape(depth, 1, n))


def _rope_rotate(x, first_half):
    up = pltpu.roll(x, LANES - 16, 1)
    down = pltpu.roll(x, 16, 1)
    return jnp.where(first_half, up, down)


IN_PROJ_PARTS = 1
CUMSUM_ROWS = 512


def _in_proj_kernel(*refs, rope):
    if rope:
        x_ref, mod_ref, gain_ref, w_ref, aup_ref, ab_ref, tri_ref, cos_ref, sin_ref, gla_ref, nat_ref, swa_ref = refs
    else:
        x_ref, mod_ref, gain_ref, w_ref, aup_ref, ab_ref, tri_ref, gla_ref, nat_ref, swa_ref = refs
    tm = x_ref.shape[1]
    pm = tm // IN_PROJ_PARTS
    kq = 2 * GLA_KW + 2 * GLA_VW
    scale = HEAD_DIM ** -0.5 * LOG2E
    nck = pm // GLA_CHUNK
    tri = tri_ref[...]
    for part in range(IN_PROJ_PARTS):
        rows = pl.ds(part * pm, pm)
        x = x_ref[0, rows, :]
        y = _rms(x, gain_ref[0:1, :])
        h = (y * (1.0 + mod_ref[0, 1:2, :]) + mod_ref[0, 0:1, :]).astype(BF16)

        pkv = _dot(h, w_ref[0, :, 0:W_Q0])
        pq = _dot(h, w_ref[0, :, W_Q0:])

        gla_ref[0, rows, 0:GLA_KW] = pkv[:, W_GK:W_GK + GLA_KW]
        gla_ref[0, rows, GLA_KW:2 * GLA_KW] = pq[:, W_GQ - W_Q0:W_GQ - W_Q0 + GLA_KW]
        gla_ref[0, rows, 2 * GLA_KW:2 * GLA_KW + GLA_VW] = pkv[:, W_GV:W_GV + GLA_VW]
        gla_ref[0, rows, 2 * GLA_KW + GLA_VW:kq] = pq[:, W_GG - W_Q0:W_GG - W_Q0 + GLA_VW]
        z = _dot_f32(pkv[:, W_GA:W_GA + LANES], aup_ref[...]) + ab_ref[...]
        log_a = (jnp.minimum(z, 0.0) - jnp.log1p(jnp.exp(-jnp.abs(z)))) * (1.0 / GLA_TAU)
        hi, lo = _split2(log_a)
        cr = tri.shape[0]
        cum = jnp.concatenate([_dot(tri, hi[i * cr:(i + 1) * cr]) + _dot(tri, lo[i * cr:(i + 1) * cr])
                               for i in range(pm // cr)], axis=0)
        gla_ref[0, rows, kq:kq + GLA_KW] = cum[:, 0:GLA_KW]
        cb = cum[:, GLA_KW:].reshape(nck, GLA_CHUNK, GLA_KW)
        lb = log_a[:, GLA_KW:].reshape(nck, GLA_CHUNK, GLA_KW)
        from_end = cb[:, GLA_CHUNK - 1:GLA_CHUNK, :] - cb + lb
        gla_ref[0, rows, kq + GLA_KW:kq + 2 * GLA_KW] = from_end.reshape(pm, GLA_KW)

        nat_ref[0, rows, 0:NAT_W] = (pq[:, W_NQ - W_Q0:W_NQ - W_Q0 + NAT_W] * scale).astype(BF16)
        nat_ref[0, rows, NAT_W:] = pkv[:, W_NK:W_NK + 2 * NAT_W].astype(BF16)

        if rope:
            cos = cos_ref[rows, :]
            sin = sin_ref[rows, :]
        lane = lax.broadcasted_iota(jnp.int32, (pm, LANES), 1)
        first_half = (lane % 32) < 16
        low_head = lane < HEAD_DIM

        def rotary(t):
            return t * cos + _rope_rotate(t, first_half) * sin if rope else t

        for j in range(SWA_QW // LANES):
            t = rotary(pq[:, W_SQ - W_Q0 + j * LANES:W_SQ - W_Q0 + (j + 1) * LANES]) * scale
            swa_ref[0, rows, j * LANES:(j + 1) * LANES] = t.astype(BF16)
        for j, t in enumerate((rotary(pkv[:, W_SK:W_SK + SWA_KW]), pkv[:, W_SV:W_SV + SWA_KW])):
            other = pltpu.roll(t, HEAD_DIM, 1)
            o = SWA_QW + 2 * j * SWA_KW
            swa_ref[0, rows, o:o + LANES] = jnp.where(low_head, t, other).astype(BF16)
            swa_ref[0, rows, o + LANES:o + 2 * LANES] = jnp.where(low_head, other, t).astype(BF16)


def _in_proj(x, mod, gains, w, layer, aup, abias, rope_tabs, tm):
    bsz, t, d = x.shape
    rope = rope_tabs is not None
    cum_rows = min(tm // IN_PROJ_PARTS, CUMSUM_ROWS)
    in_specs = [
        pl.BlockSpec((1, tm, d), lambda b, i: (b, i, 0)),
        pl.BlockSpec((1, 6, d), lambda b, i: (b, 0, 0)),
        pl.BlockSpec((4, d), lambda b, i: (0, 0)),
        pl.BlockSpec((1, d, IN_PROJ_W), lambda b, i: (layer, 0, 0)),
        pl.BlockSpec((LANES, 2 * GLA_KW), lambda b, i: (0, 0)),
        pl.BlockSpec((1, 2 * GLA_KW), lambda b, i: (0, 0)),
        pl.BlockSpec((cum_rows, cum_rows), lambda b, i: (0, 0)),
    ]
    r = np.arange(cum_rows)
    tri = (r[:, None] // GLA_CHUNK == r[None, :] // GLA_CHUNK) & (r[:, None] >= r[None, :])
    args = [x, mod, gains, w, aup, abias, jnp.asarray(tri, BF16)]
    if rope:
        in_specs += [pl.BlockSpec((tm, LANES), lambda b, i: (i, 0))] * 2
        args += list(rope_tabs)
    return pl.pallas_call(
        functools.partial(_in_proj_kernel, rope=rope),
        grid=(bsz, t // tm),
        in_specs=in_specs,
        out_specs=[
            pl.BlockSpec((1, tm, GLA_OUT_W), lambda b, i: (b, i, 0)),
            pl.BlockSpec((1, tm, NAT_PROJ_W), lambda b, i: (b, i, 0)),
            pl.BlockSpec((1, tm, SWA_PROJ_W), lambda b, i: (b, i, 0)),
        ],
        out_shape=[
            jax.ShapeDtypeStruct((bsz, t, GLA_OUT_W), F32),
            jax.ShapeDtypeStruct((bsz, t, NAT_PROJ_W), BF16),
            jax.ShapeDtypeStruct((bsz, t, SWA_PROJ_W), BF16),
        ],
        compiler_params=_cparams("parallel", "parallel"),
        name="in_proj_rope" if rope else "in_proj",
    )(*args)


def _gla_kernel(p_ref, s0f_ref, s0b_ref, gain_ref, o_ref, sff_ref, sfb_ref, acc_ref, sf_ref, sb_ref):
    t = p_ref.shape[1]
    c = GLA_CHUNK
    nc = t // c
    qi = lax.broadcasted_iota(jnp.int32, (c, GLA_HEADS * c), 0)
    kj = lax.broadcasted_iota(jnp.int32, (c, GLA_HEADS * c), 1) % c
    lower = qi >= kj
    upper = qi <= kj
    srow = lax.broadcasted_iota(jnp.int32, (GLA_VW, GLA_KW), 0) // GLA_DV
    scol = lax.broadcasted_iota(jnp.int32, (GLA_VW, GLA_KW), 1) // GLA_DK
    head_diag = srow == scol
    q_scale = GLA_DK ** -0.5

    sf_ref[...] = s0f_ref[0]
    sb_ref[...] = s0b_ref[0]
    acc_ref[...] = jnp.zeros_like(acc_ref)

    def state_free(base, end_row, cum_off):
        rows = pl.ds(base, c)
        k = p_ref[0, rows, 0:GLA_KW]
        q = p_ref[0, rows, GLA_KW:2 * GLA_KW] * q_scale
        vb = p_ref[0, rows, 2 * GLA_KW:2 * GLA_KW + GLA_VW].astype(BF16)
        cum = p_ref[0, rows, cum_off:cum_off + GLA_KW]
        mid = cum[c // 2:c // 2 + 1, :]
        total = cum[end_row:end_row + 1, :]
        qs = (q * jnp.exp(cum - mid)).astype(BF16)
        ks = (k * jnp.exp(mid - cum)).astype(BF16)
        s = _dot_nt(qs, _head_stack(ks, GLA_DK, GLA_HEADS))
        kd = (k * jnp.exp(total - cum)).astype(BF16)
        upd = _dot_tn(vb, kd)
        qe = (q * jnp.exp(cum)).astype(BF16)
        return rows, s, upd, qe, vb, jnp.exp(total)

    def with_state(keep, st_ref, rows, s, upd, qe, vb, decay):
        st = st_ref[...]
        inter = _dot_nt(qe, st.astype(BF16))
        s = jnp.where(keep, s, 0.0).astype(BF16)
        intra = _dot(s, _head_stack(vb, GLA_DV, GLA_HEADS))
        acc_ref[rows, :] += inter + intra
        st_ref[...] = st * decay + jnp.where(head_diag, upd, 0.0)

    def body(i, carry):
        fwd = state_free(pl.multiple_of(i * c, c), c - 1, 2 * GLA_KW + 2 * GLA_VW)
        bwd = state_free(pl.multiple_of((nc - 1 - i) * c, c), 0, 3 * GLA_KW + 2 * GLA_VW)
        with_state(lower, sf_ref, *fwd)
        with_state(upper, sb_ref, *bwd)
        return carry

    lax.fori_loop(0, nc, body, 0, unroll=4)
    sff_ref[0] = sf_ref[...]
    sfb_ref[0] = sb_ref[...]

    hr = lax.broadcasted_iota(jnp.int32, (GLA_VW, GLA_VW), 0) // GLA_DV
    hc = lax.broadcasted_iota(jnp.int32, (GLA_VW, GLA_VW), 1) // GLA_DV
    head_ones = (hr == hc).astype(BF16)
    blk = 256

    def mean_square(j):
        o = acc_ref[pl.ds(j * blk, blk), :]
        hi, mid, lo = _split3(o * o)
        return o, (_dot(hi, head_ones) + _dot(mid, head_ones) + _dot(lo, head_ones)) * (1.0 / GLA_DV)

    nxt = mean_square(0)
    for j in range(t // blk):
        rows = pl.ds(j * blk, blk)
        o, ms = nxt
        if j + 1 < t // blk:
            nxt = mean_square(j + 1)
        gate = p_ref[0, rows, 2 * GLA_KW + GLA_VW:2 * GLA_KW + 2 * GLA_VW]
        o_ref[0, rows, :] = (o * lax.rsqrt(ms + NORM_EPS) * gain_ref[...] * _silu(gate)).astype(BF16)


def _gla(p, s0f, s0b, gain):
    bsz, t, w = p.shape
    st_spec = pl.BlockSpec((1, GLA_VW, GLA_KW), lambda b: (b, 0, 0))
    st_shape = jax.ShapeDtypeStruct((bsz, GLA_VW, GLA_KW), F32)
    return pl.pallas_call(
        _gla_kernel,
        grid=(bsz,),
        in_specs=[
            pl.BlockSpec((1, t, w), lambda b: (b, 0, 0)),
            st_spec, st_spec,
            pl.BlockSpec((1, GLA_VW), lambda b: (0, 0)),
        ],
        out_specs=[pl.BlockSpec((1, t, GLA_VW), lambda b: (b, 0, 0)), st_spec, st_spec],
        out_shape=[jax.ShapeDtypeStruct((bsz, t, GLA_VW), BF16), st_shape, st_shape],
        scratch_shapes=[
            pltpu.VMEM((t, GLA_VW), F32),
            pltpu.VMEM((GLA_VW, GLA_KW), F32),
            pltpu.VMEM((GLA_VW, GLA_KW), F32),
        ],
        compiler_params=_cparams("parallel"),
        name="gla_bidir",
    )(p, s0f, s0b, gain)


def _head_stack(q, width, n):
    lane = lax.broadcasted_iota(jnp.int32, (1, q.shape[1]), 1) // width
    return jnp.concatenate([jnp.where(lane == h, q, jnp.zeros_like(q)) for h in range(n)], axis=0)


def _head_unstack(o, width, n):
    rows = o.shape[0] // n
    lane = lax.broadcasted_iota(jnp.int32, (1, o.shape[1]), 1) // width
    out = o[0:rows]
    for h in range(1, n):
        out = jnp.where(lane == h, o[h * rows:(h + 1) * rows], out)
    return out


NAT_ROWS_PER_STEP = 16


def _nat_kernel(lat_ref, ctx_ref, bias_ref, o_ref):
    n_rows = lat_ref.shape[1] // GRID_W
    kc = ctx_ref[0, :, NAT_W:2 * NAT_W]
    vc = ctx_ref[0, :, 2 * NAT_W:3 * NAT_W]
    n_loc = NAT_WIN_ROWS * GRID_W

    def scores(i):
        r = pl.program_id(1) * NAT_ROWS_PER_STEP + i
        r0 = jnp.clip(r - NAT_WIN_ROWS // 2, 0, n_rows - NAT_WIN_ROWS)
        q = lat_ref[0, pl.ds(pl.multiple_of(r * GRID_W, GRID_W), GRID_W), 0:NAT_W]
        krows = pl.ds(pl.multiple_of(r0 * GRID_W, GRID_W), n_loc)
        qs = _head_stack(q, HEAD_DIM, NAT_HEADS)
        first = r0 - r + NAT_WIN_ROWS - 1
        bias = jnp.concatenate([bias_ref[first + 2 * j] for j in range(NAT_WIN_ROWS // 2)], axis=1)
        s = jnp.concatenate([_dot_nt(qs, lat_ref[0, krows, NAT_W:2 * NAT_W]) + bias, _dot_nt(qs, kc)], axis=1)
        return s, krows

    def finish(i, s, krows):
        p = jnp.exp2((s - jnp.max(s, axis=-1, keepdims=True)).astype(BF16))
        den = jnp.sum(p, axis=-1, keepdims=True, dtype=F32)
        o = (_dot(p[:, 0:n_loc], lat_ref[0, krows, 2 * NAT_W:3 * NAT_W]) + _dot(p[:, n_loc:], vc)) / den
        o_ref[0, i * GRID_W:(i + 1) * GRID_W, :] = _head_unstack(o, HEAD_DIM, NAT_HEADS).astype(BF16)

    nxt = scores(0)
    for i in range(NAT_ROWS_PER_STEP):
        cur = nxt
        if i + 1 < NAT_ROWS_PER_STEP:
            nxt = scores(i + 1)
        finish(i, *cur)


def _nat(lat, ctx, bias):
    bsz, t, w = lat.shape
    m = ctx.shape[1]
    rows = NAT_ROWS_PER_STEP * GRID_W
    return pl.pallas_call(
        _nat_kernel,
        grid=(bsz, t // rows),
        in_specs=[
            pl.BlockSpec((1, t, w), lambda b, r: (b, 0, 0)),
            pl.BlockSpec((1, m, w), lambda b, r: (b, 0, 0)),
            pl.BlockSpec(bias.shape, lambda b, r: (0, 0, 0)),
        ],
        out_specs=pl.BlockSpec((1, rows, NAT_W), lambda b, r: (b, r, 0)),
        out_shape=jax.ShapeDtypeStruct((bsz, t, NAT_W), BF16),
        compiler_params=_cparams("parallel", "arbitrary"),
        name="nat_attn",
    )(lat, ctx, bias)


def _nat_bias_table(rpb):
    cols = np.arange(GRID_W)
    col_start = np.clip(cols - NAT_WIN_COLS // 2, 0, GRID_W - NAT_WIN_COLS)
    kc = np.arange(GRID_W)
    inside = (kc[None, :] >= col_start[:, None]) & (kc[None, :] < col_start[:, None] + NAT_WIN_COLS)
    rel = kc[None, :] - cols[:, None] + NAT_WIN_COLS - 1
    col_sel = (rel[:, :, None] == np.arange(2 * NAT_WIN_COLS - 1)) & inside[:, :, None]
    tab = jnp.einsum("hab,ckb->ahck", rpb.astype(F32), jnp.asarray(col_sel, F32), precision=lax.Precision.HIGHEST)
    tab = jnp.where(jnp.asarray(inside)[None, None], tab * LOG2E, NEG_BIG)
    tab = tab.reshape(2 * NAT_WIN_ROWS - 1, NAT_HEADS * GRID_W, GRID_W)
    return jnp.concatenate([tab[:-1], tab[1:]], axis=2)


def _sink_column(sink_ref, first_head, n_heads, rows):
    return jnp.concatenate(
        [jnp.broadcast_to(sink_ref[first_head + j:first_head + j + 1, 0:1], (rows, 1)) for j in range(n_heads)], axis=0)


def _softmax_pv_sink(s, v, sink):
    m = jnp.maximum(jnp.max(s, axis=-1, keepdims=True), sink)
    p = jnp.exp2((s - m).astype(BF16))
    o = _dot(p, jnp.concatenate([v, jnp.ones_like(v)], axis=1))
    w = v.shape[1]
    return o[:, 0:w] / (o[:, w:] + jnp.exp2(sink - m))


SWA_BLOCKS_PER_STEP = 4


def _swa_kernel(lat_ref, ctx_ref, sink_ref, o_ref):
    t = lat_ref.shape[1]
    m = ctx_ref.shape[1]
    blk = SWA_BLOCK
    n_loc = 3 * blk
    qi = lax.broadcasted_iota(jnp.int32, (blk, n_loc), 0)
    kj = lax.broadcasted_iota(jnp.int32, (blk, n_loc), 1)
    group = SWA_Q_HEADS // SWA_KV_HEADS
    chains = [(a, g) for a in range(SWA_BLOCKS_PER_STEP) for g in range(SWA_KV_HEADS)]

    def scores(a, g):
        n = pl.program_id(1) * SWA_BLOCKS_PER_STEP + a
        base = pl.multiple_of(jnp.clip((n - 1) * blk, 0, t - n_loc), blk)
        qrows = pl.ds(pl.multiple_of(n * blk, blk), blk)
        krows = pl.ds(base, n_loc)
        mask_add = jnp.where(jnp.abs(base + kj - n * blk - qi) <= SWA_WINDOW, 0.0, NEG_BIG)[None]
        kcol = pl.ds(SWA_QW + g * LANES, LANES)
        k = jnp.concatenate([lat_ref[0, krows, kcol], ctx_ref[0, :, kcol]], axis=0)
        qs = jnp.concatenate(
            [_head_stack(lat_ref[0, qrows, pl.ds((g * 2 + p) * LANES, LANES)], HEAD_DIM, 2) for p in range(2)], axis=0)
        s = _dot_nt(qs, k).reshape(group, blk, n_loc + m)
        s = jnp.concatenate([s[:, :, 0:n_loc] + mask_add, s[:, :, n_loc:]], axis=2).reshape(group * blk, n_loc + m)
        return s, krows

    def finish(a, g, s, krows):
        vcol = pl.ds(SWA_QW + 2 * SWA_KW + g * LANES, LANES)
        v = jnp.concatenate([lat_ref[0, krows, vcol], ctx_ref[0, :, vcol]], axis=0)
        o = _softmax_pv_sink(s, v, _sink_column(sink_ref, g * group, group, blk))
        for p in range(2):
            pair = _head_unstack(o[2 * p * blk:(2 * p + 2) * blk], HEAD_DIM, 2)
            o_ref[0, a * blk:(a + 1) * blk, pl.ds((g * 2 + p) * LANES, LANES)] = pair.astype(BF16)

    nxt = scores(*chains[0])
    for i, (a, g) in enumerate(chains):
        cur = nxt
        if i + 1 < len(chains):
            nxt = scores(*chains[i + 1])
        finish(a, g, *cur)


def _swa(lat, ctx, sink_tab):
    bsz, t, w = lat.shape
    m = ctx.shape[1]
    rows = SWA_BLOCKS_PER_STEP * SWA_BLOCK
    return pl.pallas_call(
        _swa_kernel,
        grid=(bsz, t // rows),
        in_specs=[
            pl.BlockSpec((1, t, w), lambda b, n: (b, 0, 0)),
            pl.BlockSpec((1, m, w), lambda b, n: (b, 0, 0)),
            pl.BlockSpec((SWA_Q_HEADS, LANES), lambda b, n: (0, 0)),
        ],
        out_specs=pl.BlockSpec((1, rows, SWA_QW), lambda b, n: (b, n, 0)),
        out_shape=jax.ShapeDtypeStruct((bsz, t, SWA_QW), BF16),
        compiler_params=_cparams("parallel", "arbitrary"),
        name="swa_attn",
    )(lat, ctx, sink_tab)


def _ctx_attn_kernel(nat_ref, swa_ref, sink_ref, on_ref, os_ref):
    m = nat_ref.shape[1]
    q = nat_ref[0, :, 0:NAT_W]
    k = nat_ref[0, :, NAT_W:2 * NAT_W]
    v = nat_ref[0, :, 2 * NAT_W:3 * NAT_W]
    group = SWA_Q_HEADS // SWA_KV_HEADS

    def swa_scores(g):
        k = swa_ref[0, :, pl.ds(SWA_QW + g * LANES, LANES)]
        qs = jnp.concatenate(
            [_head_stack(swa_ref[0, :, pl.ds((g * 2 + p) * LANES, LANES)], HEAD_DIM, 2) for p in range(2)], axis=0)
        return _dot_nt(qs, k)

    s = _dot_nt(_head_stack(q, HEAD_DIM, NAT_HEADS), k)
    s_swa = [swa_scores(g) for g in range(SWA_KV_HEADS)]
    p = jnp.exp2((s - jnp.max(s, axis=-1, keepdims=True)).astype(BF16))
    o = _dot(p, v) / jnp.sum(p, axis=-1, keepdims=True, dtype=F32)
    on_ref[0] = _head_unstack(o, HEAD_DIM, NAT_HEADS).astype(BF16)

    for g in range(SWA_KV_HEADS):
        v = swa_ref[0, :, pl.ds(SWA_QW + 2 * SWA_KW + g * LANES, LANES)]
        o = _softmax_pv_sink(s_swa[g], v, _sink_column(sink_ref, g * group, group, m))
        for pr in range(2):
            pair = _head_unstack(o[2 * pr * m:(2 * pr + 2) * m], HEAD_DIM, 2)
            os_ref[0, :, pl.ds((g * 2 + pr) * LANES, LANES)] = pair.astype(BF16)


def _ctx_attn(nat_c, swa_c, sink_tab):
    bsz, m, _ = nat_c.shape
    return pl.pallas_call(
        _ctx_attn_kernel,
        grid=(bsz,),
        in_specs=[
            pl.BlockSpec((1, m, NAT_PROJ_W), lambda b: (b, 0, 0)),
            pl.BlockSpec((1, m, SWA_PROJ_W), lambda b: (b, 0, 0)),
            pl.BlockSpec((SWA_Q_HEADS, LANES), lambda b: (0, 0)),
        ],
        out_specs=[
            pl.BlockSpec((1, m, NAT_W), lambda b: (b, 0, 0)),
            pl.BlockSpec((1, m, SWA_QW), lambda b: (b, 0, 0)),
        ],
        out_shape=[
            jax.ShapeDtypeStruct((bsz, m, NAT_W), BF16),
            jax.ShapeDtypeStruct((bsz, m, SWA_QW), BF16),
        ],
        compiler_params=_cparams("parallel"),
        name="ctx_attn",
    )(nat_c, swa_c, sink_tab)


OUT_PROJ_PARTS = 2


def _out_proj_kernel(gla_ref, nat_ref, swa_ref, x_ref, mod_ref, gain_ref, w_ref, wr_ref,
                     x1_ref, h2_ref, aff_ref, afft_ref):
    tm = x_ref.shape[1]
    half = tm // OUT_PROJ_PARTS

    def project(part):
        rows = pl.ds(part * half, half)
        return (_dot(gla_ref[0, rows, :], w_ref[0, 0:GLA_VW, :])
                + _dot(nat_ref[0, rows, :], w_ref[0, GLA_VW:GLA_VW + NAT_W, :])
                + _dot(swa_ref[0, rows, :], w_ref[0, GLA_VW + NAT_W:, :]))

    nxt = project(0)
    for part in range(OUT_PROJ_PARTS):
        rows = pl.ds(part * half, half)
        mix = nxt
        if part + 1 < OUT_PROJ_PARTS:
            nxt = project(part + 1)
        x1 = x_ref[0, rows, :] + mod_ref[0, 2:3, :] * _rms(mix, gain_ref[1:2, :])
        x1_ref[0, rows, :] = x1
        h2 = (_rms(x1, gain_ref[2:3, :]) * (1.0 + mod_ref[0, 4:5, :]) + mod_ref[0, 3:4, :]).astype(BF16)
        h2_ref[0, rows, :] = h2
        logits = _dot(h2, wr_ref[0])
        lane = lax.broadcasted_iota(jnp.int32, logits.shape, 1)
        logits = jnp.where(lane < N_EXPERTS, logits, NEG_BIG)
        e = jnp.exp(logits - jnp.max(logits, axis=-1, keepdims=True))
        aff = e / jnp.sum(e, axis=-1, keepdims=True)
        aff_ref[0, rows, :] = aff
        afft_ref[0, :, rows] = aff.T[0:N_EXPERTS, :]


def _out_proj(gla_o, nat_o, swa_o, x, mod, gains, w_out, wr, layer, tm):
    bsz, t, d = x.shape
    tok = lambda b, i: (b, i, 0)
    const = lambda b, i: (0, 0)
    return pl.pallas_call(
        _out_proj_kernel,
        grid=(bsz, t // tm),
        in_specs=[
            pl.BlockSpec((1, tm, GLA_VW), tok),
            pl.BlockSpec((1, tm, NAT_W), tok),
            pl.BlockSpec((1, tm, SWA_QW), tok),
            pl.BlockSpec((1, tm, d), tok),
            pl.BlockSpec((1, 6, d), lambda b, i: (b, 0, 0)),
            pl.BlockSpec((4, d), const),
            pl.BlockSpec((1, d, d), lambda b, i: (layer, 0, 0)),
            pl.BlockSpec((1, d, LANES), lambda b, i: (layer, 0, 0)),
        ],
        out_specs=[
            pl.BlockSpec((1, tm, d), tok),
            pl.BlockSpec((1, tm, d), tok),
            pl.BlockSpec((1, tm, LANES), tok),
            pl.BlockSpec((1, N_EXPERTS, tm), lambda b, i: (b, 0, i)),
        ],
        out_shape=[
            jax.ShapeDtypeStruct((bsz, t, d), F32),
            jax.ShapeDtypeStruct((bsz, t, d), BF16),
            jax.ShapeDtypeStruct((bsz, t, LANES), F32),
            jax.ShapeDtypeStruct((bsz, N_EXPERTS, t), F32),
        ],
        compiler_params=_cparams("parallel", "parallel"),
        name="out_proj_router",
    )(gla_o, nat_o, swa_o, x, mod, gains, w_out, wr)


def _excl_prefix(x):
    rows, n = x.shape
    nblk = n // LANES
    r = lax.broadcasted_iota(jnp.int32, (LANES, LANES), 0)
    c = lax.broadcasted_iota(jnp.int32, (LANES, LANES), 1)
    strict_upper = (r < c).astype(BF16)
    stacked = jnp.concatenate([x[:, j * LANES:(j + 1) * LANES] for j in range(nblk)], axis=0).astype(BF16)
    local = _dot(stacked, strict_upper)
    totals = jnp.sum(stacked.astype(F32), axis=-1, keepdims=True)
    out = []
    offs = [jnp.zeros((rows, 1), F32)]
    for j in range(nblk):
        out.append(local[j * rows:(j + 1) * rows] + offs[-1])
        offs.append(offs[-1] + totals[j * rows:(j + 1) * rows])
    return jnp.concatenate(out, axis=1), offs


def _topc_kernel(aff_ref, rank_ref, rankcol_ref, cnt_ref, *, cap):
    aff = aff_ref[...]

    def step(i, thr):
        cand = thr | (jnp.int32(1) << (30 - i))
        cnt = jnp.sum((aff >= pltpu.bitcast(cand, F32)).astype(F32), axis=-1, keepdims=True)
        return jnp.where(cnt >= cap, cand, thr)

    thr = lax.fori_loop(0, 31, step, jnp.zeros((aff.shape[0], 1), jnp.int32))
    thr = pltpu.bitcast(thr, F32)
    gt = aff > thr
    eq = aff == thr
    need = cap - jnp.sum(gt.astype(F32), axis=-1, keepdims=True)
    eq_before, _ = _excl_prefix(eq.astype(F32))
    sel = gt | (eq & (eq_before < need))
    sel_before, offs = _excl_prefix(sel.astype(F32))
    rank = jnp.where(sel, sel_before, -1.0)
    rank_ref[...] = rank
    lane = lax.broadcasted_iota(jnp.int32, (rank.shape[0], LANES), 1)
    cnt = jnp.zeros((rank.shape[0], LANES), F32)
    for j, off in enumerate(offs):
        cnt = jnp.where(lane == j, off, cnt)
    cnt_ref[...] = cnt.astype(jnp.int32)
    n = rank.shape[1]
    fill = jnp.full((LANES - N_EXPERTS, n), -1.0, F32)
    for b in range(rankcol_ref.shape[0]):
        rankcol_ref[b] = jnp.concatenate([rank[b * N_EXPERTS:(b + 1) * N_EXPERTS], fill], axis=0).T


def _topc(aff_t, cap):
    bsz, e, n = aff_t.shape
    rows = bsz * e
    rank, rankcol, cnt = pl.pallas_call(
        functools.partial(_topc_kernel, cap=cap),
        grid=(1,),
        in_specs=[pl.BlockSpec((rows, n), lambda i: (0, 0))],
        out_specs=[pl.BlockSpec((rows, n), lambda i: (0, 0)), pl.BlockSpec((bsz, n, LANES), lambda i: (0, 0, 0)),
                   pl.BlockSpec((rows, LANES), lambda i: (0, 0))],
        out_shape=[jax.ShapeDtypeStruct((rows, n), F32), jax.ShapeDtypeStruct((bsz, n, LANES), F32),
                   jax.ShapeDtypeStruct((rows, LANES), jnp.int32)],
        compiler_params=_cparams("arbitrary"),
        name="expert_topc",
    )(aff_t.reshape(rows, n))
    return rank.reshape(bsz, e, n), rankcol, cnt.reshape(bsz, e, LANES)


def _gather_kernel(h_ref, rank_ref, xs_ref, *, cap):
    n = h_ref.shape[1]
    slot = lax.broadcasted_iota(jnp.int32, (cap, n), 0).astype(F32)
    h = h_ref[0]
    for e in range(N_EXPERTS):
        onehot = (slot == rank_ref[0, e:e + 1, :]).astype(BF16)
        xs_ref[e] = _dot(onehot, h).astype(BF16)


def _gather(h, rank, cap):
    bsz, n, d = h.shape
    return pl.pallas_call(
        functools.partial(_gather_kernel, cap=cap),
        grid=(bsz,),
        in_specs=[
            pl.BlockSpec((1, n, d), lambda b: (b, 0, 0)),
            pl.BlockSpec((1, N_EXPERTS, n), lambda b: (b, 0, 0)),
        ],
        out_specs=pl.BlockSpec((N_EXPERTS, cap, d), lambda b: (0, b, 0)),
        out_shape=jax.ShapeDtypeStruct((N_EXPERTS, bsz * cap, d), BF16),
        compiler_params=_cparams("parallel"),
        name="moe_gather",
    )(h, rank)


GATHER_TILE = 256
GATHER_WIN = 5
GATHER_SLOTS = 128


def _gather_win_kernel(cnt_ref, h_ref, rank_ref, xs_ref, *, cap):
    b = pl.program_id(0)
    n = h_ref.shape[1]
    nt = n // GATHER_TILE
    win = GATHER_WIN * GATHER_TILE
    slot_w = lax.broadcasted_iota(jnp.int32, (GATHER_SLOTS, win), 0).astype(F32).astype(BF16)
    slot_f = lax.broadcasted_iota(jnp.int32, (cap, n), 0).astype(F32).astype(BF16)
    one = jnp.ones((), BF16)
    zero = jnp.zeros((), BF16)

    n_sb = cap // GATHER_SLOTS
    group = 4

    def per_group(g, carry):
        starts, fits = [], []
        for j in range(group):
            base = (g * group + j) * (nt + 1)
            c = [cnt_ref[b, base + kt] for kt in range(nt + 1)]
            for sb in range(n_sb):
                lo_slot = sb * GATHER_SLOTS
                first = sum((c[kt + 1] <= lo_slot).astype(jnp.int32) for kt in range(nt))
                last = sum((c[kt] < lo_slot + GATHER_SLOTS).astype(jnp.int32) for kt in range(nt))
                fits.append(last - first <= GATHER_WIN)
                starts.append(jnp.minimum(first, nt - GATHER_WIN))
        all_fit = functools.reduce(jnp.logical_and, fits)

        @pl.when(all_fit)
        def _():
            for j in range(group):
                e = g * group + j
                for sb in range(n_sb):
                    start = starts[j * n_sb + sb]
                    rk = jnp.concatenate([rank_ref[0, e, pl.ds(start + i, 1), :] for i in range(GATHER_WIN)], axis=1)
                    onehot = jnp.where(slot_w == (rk - sb * GATHER_SLOTS).astype(BF16), one, zero)
                    hw = h_ref[0, pl.ds(pl.multiple_of(start * GATHER_TILE, GATHER_TILE), win), :]
                    xs_ref[e, pl.ds(sb * GATHER_SLOTS, GATHER_SLOTS), :] = _dot(onehot, hw).astype(BF16)

        @pl.when(jnp.logical_not(all_fit))
        def _():
            for j in range(group):
                e = g * group + j
                rk = jnp.concatenate([rank_ref[0, e, kt:kt + 1, :] for kt in range(nt)], axis=1).astype(BF16)
                onehot = jnp.where(slot_f == rk, one, zero)
                xs_ref[e] = _dot(onehot, h_ref[0]).astype(BF16)
        return carry

    lax.fori_loop(0, N_EXPERTS // group, per_group, 0)


def _gather_win(h, rank, cnt, cap):
    bsz, n, d = h.shape
    nt = n // GATHER_TILE
    step = GATHER_TILE // LANES
    bounds = cnt[:, :, 0:nt * step + 1:step].reshape(bsz, N_EXPERTS * (nt + 1))
    return pl.pallas_call(
        functools.partial(_gather_win_kernel, cap=cap),
        grid_spec=pltpu.PrefetchScalarGridSpec(
            num_scalar_prefetch=1,
            grid=(bsz,),
            in_specs=[
                pl.BlockSpec((1, n, d), lambda b, c: (b, 0, 0)),
                pl.BlockSpec((1, N_EXPERTS, nt, GATHER_TILE), lambda b, c: (b, 0, 0, 0)),
            ],
            out_specs=pl.BlockSpec((N_EXPERTS, cap, d), lambda b, c: (0, b, 0)),
        ),
        out_shape=jax.ShapeDtypeStruct((N_EXPERTS, bsz * cap, d), BF16),
        compiler_params=_cparams("parallel"),
        name="moe_gather_win",
    )(bounds, h, rank.reshape(bsz, N_EXPERTS, nt, GATHER_TILE))


FFN_PARTS = 4


def _ffn_kernel(*refs, has_ctx):
    if has_ctx:
        xs_ref, xc_ref, wg_ref, wu_ref, wd_ref, y_ref, yc_ref, wg_bf, wu_bf, wd_bf = refs
    else:
        xs_ref, wg_ref, wu_ref, wd_ref, y_ref, wg_bf, wu_bf, wd_bf = refs
    s = pl.program_id(0)
    j = pl.program_id(1)
    n_exp = pl.num_programs(0) - 1
    piece = wg_ref.shape[2]

    @pl.when((s < n_exp) & (j < FFN_PARTS))
    def _():
        rows = pl.ds(pl.multiple_of(j * piece, piece), piece)
        wg_bf[s % 2, rows, :] = wg_ref[0, 0].astype(BF16)
        wu_bf[s % 2, rows, :] = wu_ref[0, 0].astype(BF16)
        wd_bf[s % 2, rows, :] = wd_ref[0, 0].astype(BF16)

    def swiglu(x_ref, o_ref):
        cur = (s + 1) % 2
        xs = x_ref[0]
        hid = (_silu(_dot(xs, wg_bf[cur])) * _dot(xs, wu_bf[cur])).astype(BF16)
        o_ref[0] = _dot(hid, wd_bf[cur]).astype(BF16)

    @pl.when((s > 0) & (j < FFN_PARTS))
    def _():
        swiglu(xs_ref, y_ref)

    if has_ctx:
        @pl.when((s > 0) & (j == FFN_PARTS))
        def _():
            swiglu(xc_ref, yc_ref)


def _ffn(xs, xc, w_gate, w_up, w_down, layer):
    _, e, d, ff = w_gate.shape
    rb = xs.shape[1] // FFN_PARTS
    has_ctx = xc is not None
    last = FFN_PARTS - 1

    def x_idx(s, j):
        return (jnp.maximum(s - 1, 0), jnp.where(s > 0, jnp.minimum(j, last), 0), 0)

    def w_idx(s, j):
        return (layer, jnp.minimum(s, e - 1), jnp.minimum(j, last), 0)

    xspec = pl.BlockSpec((1, rb, d), x_idx)
    cspecs, cargs, cshapes = [], [], []
    if has_ctx:
        cspecs = [pl.BlockSpec((1, xc.shape[1], d), lambda s, j: (jnp.maximum(s - 1, 0), 0, 0))]
        cargs = [xc]
        cshapes = [jax.ShapeDtypeStruct(xc.shape, BF16)]
    out = pl.pallas_call(
        functools.partial(_ffn_kernel, has_ctx=has_ctx),
        grid=(e + 1, FFN_PARTS + int(has_ctx)),
        in_specs=[xspec] + cspecs + [
            pl.BlockSpec((1, 1, d // FFN_PARTS, ff), w_idx),
            pl.BlockSpec((1, 1, d // FFN_PARTS, ff), w_idx),
            pl.BlockSpec((1, 1, ff // FFN_PARTS, d), w_idx),
        ],
        out_specs=[xspec] + cspecs,
        out_shape=[jax.ShapeDtypeStruct(xs.shape, BF16)] + cshapes,
        scratch_shapes=[pltpu.VMEM((2, d, ff), BF16), pltpu.VMEM((2, d, ff), BF16), pltpu.VMEM((2, ff, d), BF16)],
        compiler_params=_cparams("arbitrary", "arbitrary"),
        name="moe_ffn",
    )(xs, *cargs, w_gate, w_up, w_down)
    return out


def _combine_kernel(y_ref, rankcol_ref, aff_ref, x_ref, mod_ref, gain_ref, o_ref, *, cap):
    rb = x_ref.shape[1]
    slot = lax.broadcasted_iota(jnp.int32, (rb, cap), 1).astype(F32)
    rc = rankcol_ref[0]
    af = aff_ref[0]
    acc = jnp.zeros((rb, x_ref.shape[2]), F32)
    for e in range(N_EXPERTS):
        onehot = (rc[:, e:e + 1] == slot).astype(BF16)
        acc = acc + af[:, e:e + 1] * _dot(onehot, y_ref[e])
    o_ref[0] = x_ref[0] + mod_ref[0, 5:6, :] * _rms(acc, gain_ref[3:4, :])


def _combine(y, rankcol, aff, x, mod, gains, cap):
    bsz, n, d = x.shape
    rb = min(n, 512)
    tok = lambda b, i: (b, i, 0)
    return pl.pallas_call(
        functools.partial(_combine_kernel, cap=cap),
        grid=(bsz, n // rb),
        in_specs=[
            pl.BlockSpec((N_EXPERTS, cap, d), lambda b, i: (0, b, 0)),
            pl.BlockSpec((1, rb, LANES), tok),
            pl.BlockSpec((1, rb, LANES), tok),
            pl.BlockSpec((1, rb, d), tok),
            pl.BlockSpec((1, 6, d), lambda b, i: (b, 0, 0)),
            pl.BlockSpec((4, d), lambda b, i: (0, 0)),
        ],
        out_specs=pl.BlockSpec((1, rb, d), tok),
        out_shape=jax.ShapeDtypeStruct((bsz, n, d), F32),
        compiler_params=_cparams("parallel", "arbitrary"),
        name="moe_combine",
    )(y, rankcol, aff, x, mod, gains)


COMBINE_ROWS = 512
COMBINE_WIN = 128


def _combine_pair_kernel(cnt_ref, y_ref, rankcol_ref, aff_ref, x_ref, mod_ref, gain_ref, o_ref, extra_ref, *, cap):
    b = pl.program_id(0)
    i = pl.program_id(1)
    nrb = pl.num_programs(1)
    rb = x_ref.shape[1]
    slot_w = lax.broadcasted_iota(jnp.int32, (rb, COMBINE_WIN), 1).astype(F32)
    rc = rankcol_ref[0]
    af = aff_ref[0]
    los, unfit = [], []
    for e in range(N_EXPERTS):
        c0 = cnt_ref[b, e * (nrb + 1) + i]
        c1 = cnt_ref[b, e * (nrb + 1) + i + 1]
        lo = jnp.minimum((c0 // 16) * 16, cap - COMBINE_WIN)
        los.append(pl.multiple_of(lo, 16))
        unfit.append(c1 - lo > COMBINE_WIN)
    def operands(e0):
        hot, rows = [], []
        for e in (e0, e0 + 1):
            hot.append(jnp.where(rc[:, e:e + 1] - los[e].astype(F32) == slot_w, af[:, e:e + 1], 0.0).astype(BF16))
            rows.append(y_ref[e, pl.ds(los[e], COMBINE_WIN), :])
        return jnp.concatenate(hot, axis=1), jnp.concatenate(rows, axis=0)

    acc = jnp.zeros((rb, x_ref.shape[2]), F32)
    nxt = operands(0)
    for e0 in range(0, N_EXPERTS, 2):
        cur = nxt
        if e0 + 2 < N_EXPERTS:
            nxt = operands(e0 + 2)
        acc = acc + _dot(*cur)

    def finish(total):
        o_ref[0] = x_ref[0] + mod_ref[0, 5:6, :] * _rms(total, gain_ref[3:4, :])

    any_unfit = functools.reduce(jnp.logical_or, unfit)

    @pl.when(jnp.logical_not(any_unfit))
    def _():
        finish(acc)

    @pl.when(any_unfit)
    def _():
        extra_ref[...] = jnp.zeros_like(extra_ref)
        slot_f = lax.broadcasted_iota(jnp.int32, (rb, cap), 1)
        for e in range(N_EXPERTS):
            @pl.when(unfit[e])
            def _():
                outside = (slot_f < los[e]) | (slot_f >= los[e] + COMBINE_WIN)
                hot = jnp.where((rc[:, e:e + 1] == slot_f.astype(F32)) & outside, af[:, e:e + 1], 0.0).astype(BF16)
                extra_ref[...] += _dot(hot, y_ref[e])
        finish(acc + extra_ref[...])


def _combine_pair(y, rankcol, aff, cnt, x, mod, gains, cap):
    bsz, n, d = x.shape
    rb = COMBINE_ROWS
    nrb = n // rb
    step = rb // LANES
    bounds = cnt[:, :, 0:nrb * step + 1:step].reshape(bsz, N_EXPERTS * (nrb + 1))
    tok = lambda b, i, c: (b, i, 0)
    return pl.pallas_call(
        functools.partial(_combine_pair_kernel, cap=cap),
        grid_spec=pltpu.PrefetchScalarGridSpec(
            num_scalar_prefetch=1,
            grid=(bsz, nrb),
            in_specs=[
                pl.BlockSpec((N_EXPERTS, cap, d), lambda b, i, c: (0, b, 0)),
                pl.BlockSpec((1, rb, LANES), tok),
                pl.BlockSpec((1, rb, LANES), tok),
                pl.BlockSpec((1, rb, d), tok),
                pl.BlockSpec((1, 6, d), lambda b, i, c: (b, 0, 0)),
                pl.BlockSpec((4, d), lambda b, i, c: (0, 0)),
            ],
            out_specs=pl.BlockSpec((1, rb, d), tok),
            scratch_shapes=[pltpu.VMEM((rb, d), F32)],
        ),
        out_shape=jax.ShapeDtypeStruct((bsz, n, d), F32),
        compiler_params=_cparams("parallel", "arbitrary"),
        name="moe_combine_pair",
    )(bounds, y, rankcol, aff, x, mod, gains)


def _relayout_w_in(w):
    cut = GLA_KW + GLA_VW + 2 * GLA_RANK
    pad = jnp.zeros(w.shape[:2] + (LANES - 2 * GLA_RANK,), w.dtype)
    return jnp.concatenate([w[:, :, :cut], pad, w[:, :, cut:]], axis=2).astype(BF16)


def _rope_tables(t):
    half = HEAD_DIM // 4
    freqs = ROPE_BASE ** (-np.arange(half, dtype=np.float32) / half)
    pos = np.arange(t)
    ang_r = (pos // GRID_W).astype(np.float32)[:, None] * freqs
    ang_c = (pos % GRID_W).astype(np.float32)[:, None] * freqs
    cos = np.concatenate([np.cos(ang_r), np.cos(ang_r), np.cos(ang_c), np.cos(ang_c)], axis=1)
    sin = np.concatenate([-np.sin(ang_r), np.sin(ang_r), -np.sin(ang_c), np.sin(ang_c)], axis=1)
    reps = LANES // HEAD_DIM
    return jnp.asarray(np.tile(cos, (1, reps)), F32), jnp.asarray(np.tile(sin, (1, reps)), F32)


def _moe(h_list, afft_list, aff_list, x_list, mod_list, gains, w_gate, w_up, w_down, layer):
    caps = [CAPACITY_FACTOR * h.shape[1] // N_EXPERTS for h in h_list]
    ranks = [_topc(a, cap) for a, cap in zip(afft_list, caps)]
    long = [h.shape[1] >= GATHER_WIN * GATHER_TILE and h.shape[1] % COMBINE_ROWS == 0 and cap >= 2 * COMBINE_WIN
            for h, cap in zip(h_list, caps)]
    xs = [_gather_win(h, r[0], r[2], cap) if lg else _gather(h, r[0], cap)
          for h, r, cap, lg in zip(h_list, ranks, caps, long)]
    ys = _ffn(xs[0], xs[1] if len(xs) > 1 else None, w_gate, w_up, w_down, layer)
    return [_combine_pair(y, r[1], aff, r[2], x, mod, gains, cap) if lg else _combine(y, r[1], aff, x, mod, gains, cap)
            for y, r, aff, x, mod, cap, lg in zip(ys, ranks, aff_list, x_list, mod_list, caps, long)]


def kernel(x, c, ctx, c_ctx, w_mod, b_mod, norm_gains, w_in, w_out, gla_a_up, gla_a_bias, gla_norm,
           nat_rpb, swa_sink, w_router, w_gate, w_up, w_down):
    bsz, t, d = x.shape
    m = ctx.shape[1]
    depth = w_mod.shape[0]

    cc = jnp.concatenate([c, c_ctx[None], jnp.zeros((16 - bsz - 1, d), F32)], axis=0)
    mod_all = _modulation(cc, w_mod, b_mod)
    rope_tabs = _rope_tables(t)
    zero_state = jnp.zeros((bsz, GLA_VW, GLA_KW), F32)
    w_in_b = _relayout_w_in(w_in)
    w_out_b = w_out.astype(BF16)
    wr_b = jnp.pad(w_router, ((0, 0), (0, 0), (0, LANES - N_EXPERTS))).astype(BF16)

    xc = ctx
    for l in range(depth):
        update_ctx = l < depth - 1
        mod = mod_all[l, :bsz].reshape(bsz, 6, d)
        mod_c = jnp.broadcast_to(mod_all[l, bsz].reshape(1, 6, d), (bsz, 6, d))
        gains = norm_gains[l]
        aup = jnp.zeros((LANES, 2 * GLA_KW), F32)
        aup = aup.at[0:GLA_RANK, 0:GLA_KW].set(gla_a_up[l, 0]).at[GLA_RANK:2 * GLA_RANK, GLA_KW:].set(gla_a_up[l, 1])
        abias = gla_a_bias[l].reshape(1, 2 * GLA_KW)
        gla_gain = jnp.tile(gla_norm[l], GLA_HEADS).reshape(1, GLA_VW)
        sink_tab = jnp.broadcast_to(swa_sink[l][:, None] * LOG2E, (SWA_Q_HEADS, LANES))

        gla_p, nat_p, swa_p = _in_proj(x, mod, gains, w_in_b, l, aup, abias, rope_tabs, 1024)
        gla_c, nat_c, swa_c = _in_proj(xc, mod_c, gains, w_in_b, l, aup, abias, None, m)

        gla_co, s_f, s_b = _gla(gla_c, zero_state, zero_state, gla_gain)
        gla_o, _, _ = _gla(gla_p, s_f, s_b, gla_gain)
        nat_o = _nat(nat_p, nat_c, _nat_bias_table(nat_rpb[l]))
        swa_o = _swa(swa_p, swa_c, sink_tab)

        x1, h2, aff, aff_t = _out_proj(gla_o, nat_o, swa_o, x, mod, gains, w_out_b, wr_b, l, 1024)
        if update_ctx:
            nat_co, swa_co = _ctx_attn(nat_c, swa_c, sink_tab)
            xc1, hc2, aff_c, aff_ct = _out_proj(gla_co, nat_co, swa_co, xc, mod_c, gains, w_out_b, wr_b, l, m)
            x, xc = _moe([h2, hc2], [aff_t, aff_ct], [aff, aff_c], [x1, xc1], [mod, mod_c], gains,
                         w_gate, w_up, w_down, l)
        else:
            (x,) = _moe([h2], [aff_t], [aff], [x1], [mod], gains, w_gate, w_up, w_down, l)
    return x
```

```python
import functools

import jax
import jax.numpy as jnp
import numpy as np
from jax import lax
from jax.experimental import pallas as pl
from jax.experimental.pallas import tpu as pltpu

F32 = jnp.float32
BF16 = jnp.bfloat16

D_MODEL = 1024
GRID_W = 64
HEAD_DIM = 64
GLA_HEADS = 4
GLA_DK = 32
GLA_DV = 64
GLA_RANK = 16
GLA_TAU = 16.0
GLA_CHUNK = 64
NAT_HEADS = 4
NAT_WIN_ROWS = 8
NAT_WIN_COLS = 16
SWA_Q_HEADS = 8
SWA_KV_HEADS = 2
SWA_WINDOW = 128
SWA_BLOCK = 128
ROPE_BASE = 10000.0
N_EXPERTS = 16
EXPERT_FF = 1024
CAPACITY_FACTOR = 2
NORM_EPS = 1e-6

GLA_KW = GLA_HEADS * GLA_DK
GLA_VW = GLA_HEADS * GLA_DV
NAT_W = NAT_HEADS * HEAD_DIM
SWA_QW = SWA_Q_HEADS * HEAD_DIM
SWA_KW = SWA_KV_HEADS * HEAD_DIM
LANES = 128
NEG_BIG = -1e30
LOG2E = 1.4426950408889634

GLA_OUT_W = 2 * GLA_KW + 2 * GLA_VW + 2 * GLA_KW
NAT_PROJ_W = 3 * NAT_W
SWA_PROJ_W = SWA_QW + 4 * SWA_KW
W_GK = 0
W_GV = W_GK + GLA_KW
W_GA = W_GV + GLA_VW
W_NK = W_GA + LANES
W_SK = W_NK + 2 * NAT_W
W_SV = W_SK + SWA_KW
W_Q0 = W_SV + SWA_KW
W_GQ = W_Q0
W_GG = W_GQ + GLA_KW
W_NQ = W_GG + GLA_VW
W_SQ = W_NQ + NAT_W
IN_PROJ_W = W_SQ + SWA_QW

VMEM_LIMIT = 56 * 1024 * 1024


def _cparams(*sem):
    return pltpu.CompilerParams(dimension_semantics=sem, vmem_limit_bytes=VMEM_LIMIT)


def _dot(a, b):
    return jnp.dot(a, b, preferred_element_type=F32)


def _dot_nt(a, b):
    return lax.dot_general(a, b, (((1,), (1,)), ((), ())), preferred_element_type=F32)


def _dot_tn(a, b):
    return lax.dot_general(a, b, (((0,), (0,)), ((), ())), preferred_element_type=F32)


def _split2(a):
    hi = a.astype(BF16)
    lo = (a - hi.astype(F32)).astype(BF16)
    return hi, lo


def _split3(a):
    hi = a.astype(BF16)
    r = a - hi.astype(F32)
    mid = r.astype(BF16)
    lo = (r - mid.astype(F32)).astype(BF16)
    return hi, mid, lo


def _dot_f32(a, b):
    ah, al = _split2(a)
    bh, bl = _split2(b)
    return _dot(ah, bh) + _dot(al, bh) + _dot(ah, bl)


def _rms(x, gain):
    return x * lax.rsqrt(jnp.mean(x * x, axis=-1, keepdims=True) + NORM_EPS) * gain


def _silu(x):
    return x * jax.nn.sigmoid(x)


def _mod_kernel(c_ref, w_ref, b_ref, o_ref):
    a = _silu(c_ref[...])
    o_ref[0] = _dot_f32(a, w_ref[0]) + b_ref[0]


def _modulation(cc, w_mod, b_mod):
    depth, d, n = w_mod.shape
    r = cc.shape[0]
    tn = 1536
    return pl.pallas_call(
        _mod_kernel,
        grid=(depth, n // tn),
        in_specs=[
            pl.BlockSpec((r, d), lambda l, j: (0, 0)),
            pl.BlockSpec((1, d, tn), lambda l, j: (l, 0, j)),
            pl.BlockSpec((1, 1, tn), lambda l, j: (l, 0, j)),
        ],
        out_specs=pl.BlockSpec((1, r, tn), lambda l, j: (l, 0, j)),
        out_shape=jax.ShapeDtypeStruct((depth, r, n), F32),
        compiler_params=_cparams("parallel", "parallel"),
        name="adaln_mod",
    )(cc, w_mod, b_mod.reshape(depth, 1, n))


def _rope_rotate(x, first_half):
    up = pltpu.roll(x, LANES - 16, 1)
    down = pltpu.roll(x, 16, 1)
    return jnp.where(first_half, up, down)


IN_PROJ_PARTS = 1
CUMSUM_ROWS = 512


def _in_proj_kernel(*refs, rope):
    if rope:
        x_ref, mod_ref, gain_ref, w_ref, aup_ref, ab_ref, tri_ref, cos_ref, sin_ref, gla_ref, nat_ref, swa_ref = refs
    else:
        x_ref, mod_ref, gain_ref, w_ref, aup_ref, ab_ref, tri_ref, gla_ref, nat_ref, swa_ref = refs
    tm = x_ref.shape[1]
    pm = tm // IN_PROJ_PARTS
    kq = 2 * GLA_KW + 2 * GLA_VW
    scale = HEAD_DIM ** -0.5 * LOG2E
    nck = pm // GLA_CHUNK
    tri = tri_ref[...]
    for part in range(IN_PROJ_PARTS):
        rows = pl.ds(part * pm, pm)
        x = x_ref[0, rows, :]
        y = _rms(x, gain_ref[0:1, :])
        h = (y * (1.0 + mod_ref[0, 1:2, :]) + mod_ref[0, 0:1, :]).astype(BF16)

        pkv = _dot(h, w_ref[0, :, 0:W_Q0])
        pq = _dot(h, w_ref[0, :, W_Q0:])

        gla_ref[0, rows, 0:GLA_KW] = pkv[:, W_GK:W_GK + GLA_KW]
        gla_ref[0, rows, GLA_KW:2 * GLA_KW] = pq[:, W_GQ - W_Q0:W_GQ - W_Q0 + GLA_KW]
        gla_ref[0, rows, 2 * GLA_KW:2 * GLA_KW + GLA_VW] = pkv[:, W_GV:W_GV + GLA_VW]
        gla_ref[0, rows, 2 * GLA_KW + GLA_VW:kq] = pq[:, W_GG - W_Q0:W_GG - W_Q0 + GLA_VW]
        z = _dot_f32(pkv[:, W_GA:W_GA + LANES], aup_ref[...]) + ab_ref[...]
        log_a = (jnp.minimum(z, 0.0) - jnp.log1p(jnp.exp(-jnp.abs(z)))) * (1.0 / GLA_TAU)
        hi, lo = _split2(log_a)
        cr = tri.shape[0]
        cum = jnp.concatenate([_dot(tri, hi[i * cr:(i + 1) * cr]) + _dot(tri, lo[i * cr:(i + 1) * cr])
                               for i in range(pm // cr)], axis=0)
        gla_ref[0, rows, kq:kq + GLA_KW] = cum[:, 0:GLA_KW]
        cb = cum[:, GLA_KW:].reshape(nck, GLA_CHUNK, GLA_KW)
        lb = log_a[:, GLA_KW:].reshape(nck, GLA_CHUNK, GLA_KW)
        from_end = cb[:, GLA_CHUNK - 1:GLA_CHUNK, :] - cb + lb
        gla_ref[0, rows, kq + GLA_KW:kq + 2 * GLA_KW] = from_end.reshape(pm, GLA_KW)

        nat_ref[0, rows, 0:NAT_W] = (pq[:, W_NQ - W_Q0:W_NQ - W_Q0 + NAT_W] * scale).astype(BF16)
        nat_ref[0, rows, NAT_W:] = pkv[:, W_NK:W_NK + 2 * NAT_W].astype(BF16)

        if rope:
            cos = cos_ref[rows, :]
            sin = sin_ref[rows, :]
        lane = lax.broadcasted_iota(jnp.int32, (pm, LANES), 1)
        first_half = (lane % 32) < 16
        low_head = lane < HEAD_DIM

        def rotary(t):
            return t * cos + _rope_rotate(t, first_half) * sin if rope else t

        for j in range(SWA_QW // LANES):
            t = rotary(pq[:, W_SQ - W_Q0 + j * LANES:W_SQ - W_Q0 + (j + 1) * LANES]) * scale
            swa_ref[0, rows, j * LANES:(j + 1) * LANES] = t.astype(BF16)
        for j, t in enumerate((rotary(pkv[:, W_SK:W_SK + SWA_KW]), pkv[:, W_SV:W_SV + SWA_KW])):
            other = pltpu.roll(t, HEAD_DIM, 1)
            o = SWA_QW + 2 * j * SWA_KW
            swa_ref[0, rows, o:o + LANES] = jnp.where(low_head, t, other).astype(BF16)
            swa_ref[0, rows, o + LANES:o + 2 * LANES] = jnp.where(low_head, other, t).astype(BF16)


def _in_proj(x, mod, gains, w, layer, aup, abias, rope_tabs, tm):
    bsz, t, d = x.shape
    rope = rope_tabs is not None
    cum_rows = min(tm // IN_PROJ_PARTS, CUMSUM_ROWS)
    in_specs = [
        pl.BlockSpec((1, tm, d), lambda b, i: (b, i, 0)),
        pl.BlockSpec((1, 6, d), lambda b, i: (b, 0, 0)),
        pl.BlockSpec((4, d), lambda b, i: (0, 0)),
        pl.BlockSpec((1, d, IN_PROJ_W), lambda b, i: (layer, 0, 0)),
        pl.BlockSpec((LANES, 2 * GLA_KW), lambda b, i: (0, 0)),
        pl.BlockSpec((1, 2 * GLA_KW), lambda b, i: (0, 0)),
        pl.BlockSpec((cum_rows, cum_rows), lambda b, i: (0, 0)),
    ]
    r = np.arange(cum_rows)
    tri = (r[:, None] // GLA_CHUNK == r[None, :] // GLA_CHUNK) & (r[:, None] >= r[None, :])
    args = [x, mod, gains, w, aup, abias, jnp.asarray(tri, BF16)]
    if rope:
        in_specs += [pl.BlockSpec((tm, LANES), lambda b, i: (i, 0))] * 2
        args += list(rope_tabs)
    return pl.pallas_call(
        functools.partial(_in_proj_kernel, rope=rope),
        grid=(bsz, t // tm),
        in_specs=in_specs,
        out_specs=[
            pl.BlockSpec((1, tm, GLA_OUT_W), lambda b, i: (b, i, 0)),
            pl.BlockSpec((1, tm, NAT_PROJ_W), lambda b, i: (b, i, 0)),
            pl.BlockSpec((1, tm, SWA_PROJ_W), lambda b, i: (b, i, 0)),
        ],
        out_shape=[
            jax.ShapeDtypeStruct((bsz, t, GLA_OUT_W), F32),
            jax.ShapeDtypeStruct((bsz, t, NAT_PROJ_W), BF16),
            jax.ShapeDtypeStruct((bsz, t, SWA_PROJ_W), BF16),
        ],
        compiler_params=_cparams("parallel", "parallel"),
        name="in_proj_rope" if rope else "in_proj",
    )(*args)


GLA_SAMPLES_PER_STEP = 2


def _gla_kernel(p_ref, s0f_ref, s0b_ref, gain_ref, o_ref, sff_ref, sfb_ref, acc_ref, sf_ref, sb_ref):
    nb, t, _ = p_ref.shape
    c = GLA_CHUNK
    nc = t // c
    qi = lax.broadcasted_iota(jnp.int32, (c, GLA_HEADS * c), 0)
    kj = lax.broadcasted_iota(jnp.int32, (c, GLA_HEADS * c), 1) % c
    lower = qi >= kj
    upper = qi <= kj
    srow = lax.broadcasted_iota(jnp.int32, (GLA_VW, GLA_KW), 0) // GLA_DV
    scol = lax.broadcasted_iota(jnp.int32, (GLA_VW, GLA_KW), 1) // GLA_DK
    head_diag = srow == scol
    q_scale = GLA_DK ** -0.5

    sf_ref[...] = s0f_ref[...]
    sb_ref[...] = s0b_ref[...]
    acc_ref[...] = jnp.zeros_like(acc_ref)

    def state_free(n, base, end_row, cum_off):
        rows = pl.ds(base, c)
        k = p_ref[n, rows, 0:GLA_KW]
        q = p_ref[n, rows, GLA_KW:2 * GLA_KW] * q_scale
        vb = p_ref[n, rows, 2 * GLA_KW:2 * GLA_KW + GLA_VW].astype(BF16)
        cum = p_ref[n, rows, cum_off:cum_off + GLA_KW]
        mid = cum[c // 2:c // 2 + 1, :]
        total = cum[end_row:end_row + 1, :]
        qs = (q * jnp.exp(cum - mid)).astype(BF16)
        ks = (k * jnp.exp(mid - cum)).astype(BF16)
        s = _dot_nt(qs, _head_stack(ks, GLA_DK, GLA_HEADS))
        kd = (k * jnp.exp(total - cum)).astype(BF16)
        upd = _dot_tn(vb, kd)
        qe = (q * jnp.exp(cum)).astype(BF16)
        return rows, s, upd, qe, vb, jnp.exp(total)

    def with_state(n, keep, st_ref, rows, s, upd, qe, vb, decay):
        st = st_ref[n]
        inter = _dot_nt(qe, st.astype(BF16))
        s = jnp.where(keep, s, 0.0).astype(BF16)
        intra = _dot(s, _head_stack(vb, GLA_DV, GLA_HEADS))
        acc_ref[n, rows, :] += inter + intra
        st_ref[n] = st * decay + jnp.where(head_diag, upd, 0.0)

    def body(i, carry):
        fwd = [state_free(n, pl.multiple_of(i * c, c), c - 1, 2 * GLA_KW + 2 * GLA_VW) for n in range(nb)]
        bwd = [state_free(n, pl.multiple_of((nc - 1 - i) * c, c), 0, 3 * GLA_KW + 2 * GLA_VW) for n in range(nb)]
        for n in range(nb):
            with_state(n, lower, sf_ref, *fwd[n])
            with_state(n, upper, sb_ref, *bwd[n])
        return carry

    lax.fori_loop(0, nc, body, 0, unroll=4)
    sff_ref[...] = sf_ref[...]
    sfb_ref[...] = sb_ref[...]

    hr = lax.broadcasted_iota(jnp.int32, (GLA_VW, GLA_VW), 0) // GLA_DV
    hc = lax.broadcasted_iota(jnp.int32, (GLA_VW, GLA_VW), 1) // GLA_DV
    head_ones = (hr == hc).astype(BF16)
    blk = 256
    blocks = [(n, j) for n in range(nb) for j in range(t // blk)]

    def mean_square(n, j):
        o = acc_ref[n, pl.ds(j * blk, blk), :]
        hi, mid, lo = _split3(o * o)
        return o, (_dot(hi, head_ones) + _dot(mid, head_ones) + _dot(lo, head_ones)) * (1.0 / GLA_DV)

    nxt = mean_square(*blocks[0])
    for i, (n, j) in enumerate(blocks):
        rows = pl.ds(j * blk, blk)
        o, ms = nxt
        if i + 1 < len(blocks):
            nxt = mean_square(*blocks[i + 1])
        gate = p_ref[n, rows, 2 * GLA_KW + GLA_VW:2 * GLA_KW + 2 * GLA_VW]
        o_ref[n, rows, :] = (o * lax.rsqrt(ms + NORM_EPS) * gain_ref[...] * _silu(gate)).astype(BF16)


def _gla(p, s0f, s0b, gain):
    bsz, t, w = p.shape
    nb = GLA_SAMPLES_PER_STEP if bsz % GLA_SAMPLES_PER_STEP == 0 else 1
    st_spec = pl.BlockSpec((nb, GLA_VW, GLA_KW), lambda b: (b, 0, 0))
    st_shape = jax.ShapeDtypeStruct((bsz, GLA_VW, GLA_KW), F32)
    return pl.pallas_call(
        _gla_kernel,
        grid=(bsz // nb,),
        in_specs=[
            pl.BlockSpec((nb, t, w), lambda b: (b, 0, 0)),
            st_spec, st_spec,
            pl.BlockSpec((1, GLA_VW), lambda b: (0, 0)),
        ],
        out_specs=[pl.BlockSpec((nb, t, GLA_VW), lambda b: (b, 0, 0)), st_spec, st_spec],
        out_shape=[jax.ShapeDtypeStruct((bsz, t, GLA_VW), BF16), st_shape, st_shape],
        scratch_shapes=[
            pltpu.VMEM((nb, t, GLA_VW), F32),
            pltpu.VMEM((nb, GLA_VW, GLA_KW), F32),
            pltpu.VMEM((nb, GLA_VW, GLA_KW), F32),
        ],
        compiler_params=_cparams("parallel"),
        name="gla_bidir",
    )(p, s0f, s0b, gain)


def _head_stack(q, width, n):
    lane = lax.broadcasted_iota(jnp.int32, (1, q.shape[1]), 1) // width
    return jnp.concatenate([jnp.where(lane == h, q, jnp.zeros_like(q)) for h in range(n)], axis=0)


def _head_unstack(o, width, n):
    rows = o.shape[0] // n
    lane = lax.broadcasted_iota(jnp.int32, (1, o.shape[1]), 1) // width
    out = o[0:rows]
    for h in range(1, n):
        out = jnp.where(lane == h, o[h * rows:(h + 1) * rows], out)
    return out


NAT_ROWS_PER_STEP = 16


def _nat_kernel(lat_ref, ctx_ref, bias_ref, o_ref):
    n_rows = lat_ref.shape[1] // GRID_W
    kc = ctx_ref[0, :, NAT_W:2 * NAT_W]
    vc = ctx_ref[0, :, 2 * NAT_W:3 * NAT_W]
    n_loc = NAT_WIN_ROWS * GRID_W

    def scores(i):
        r = pl.program_id(1) * NAT_ROWS_PER_STEP + i
        r0 = jnp.clip(r - NAT_WIN_ROWS // 2, 0, n_rows - NAT_WIN_ROWS)
        q = lat_ref[0, pl.ds(pl.multiple_of(r * GRID_W, GRID_W), GRID_W), 0:NAT_W]
        krows = pl.ds(pl.multiple_of(r0 * GRID_W, GRID_W), n_loc)
        qs = _head_stack(q, HEAD_DIM, NAT_HEADS)
        first = r0 - r + NAT_WIN_ROWS - 1
        bias = jnp.concatenate([bias_ref[first + 2 * j] for j in range(NAT_WIN_ROWS // 2)], axis=1)
        s = jnp.concatenate([_dot_nt(qs, lat_ref[0, krows, NAT_W:2 * NAT_W]) + bias, _dot_nt(qs, kc)], axis=1)
        return s, krows

    def finish(i, s, krows):
        p = jnp.exp2((s - jnp.max(s, axis=-1, keepdims=True)).astype(BF16))
        den = jnp.sum(p, axis=-1, keepdims=True, dtype=F32)
        o = (_dot(p[:, 0:n_loc], lat_ref[0, krows, 2 * NAT_W:3 * NAT_W]) + _dot(p[:, n_loc:], vc)) / den
        o_ref[0, i * GRID_W:(i + 1) * GRID_W, :] = _head_unstack(o, HEAD_DIM, NAT_HEADS).astype(BF16)

    nxt = scores(0)
    for i in range(NAT_ROWS_PER_STEP):
        cur = nxt
        if i + 1 < NAT_ROWS_PER_STEP:
            nxt = scores(i + 1)
        finish(i, *cur)


def _nat(lat, ctx, bias):
    bsz, t, w = lat.shape
    m = ctx.shape[1]
    rows = NAT_ROWS_PER_STEP * GRID_W
    return pl.pallas_call(
        _nat_kernel,
        grid=(bsz, t // rows),
        in_specs=[
            pl.BlockSpec((1, t, w), lambda b, r: (b, 0, 0)),
            pl.BlockSpec((1, m, w), lambda b, r: (b, 0, 0)),
            pl.BlockSpec(bias.shape, lambda b, r: (0, 0, 0)),
        ],
        out_specs=pl.BlockSpec((1, rows, NAT_W), lambda b, r: (b, r, 0)),
        out_shape=jax.ShapeDtypeStruct((bsz, t, NAT_W), BF16),
        compiler_params=_cparams("parallel", "arbitrary"),
        name="nat_attn",
    )(lat, ctx, bias)


def _nat_bias_table(rpb):
    cols = np.arange(GRID_W)
    col_start = np.clip(cols - NAT_WIN_COLS // 2, 0, GRID_W - NAT_WIN_COLS)
    kc = np.arange(GRID_W)
    inside = (kc[None, :] >= col_start[:, None]) & (kc[None, :] < col_start[:, None] + NAT_WIN_COLS)
    rel = kc[None, :] - cols[:, None] + NAT_WIN_COLS - 1
    col_sel = (rel[:, :, None] == np.arange(2 * NAT_WIN_COLS - 1)) & inside[:, :, None]
    tab = jnp.einsum("hab,ckb->ahck", rpb.astype(F32), jnp.asarray(col_sel, F32), precision=lax.Precision.HIGHEST)
    tab = jnp.where(jnp.asarray(inside)[None, None], tab * LOG2E, NEG_BIG)
    tab = tab.reshape(2 * NAT_WIN_ROWS - 1, NAT_HEADS * GRID_W, GRID_W)
    return jnp.concatenate([tab[:-1], tab[1:]], axis=2)


def _sink_column(sink_ref, first_head, n_heads, rows):
    return jnp.concatenate(
        [jnp.broadcast_to(sink_ref[first_head + j:first_head + j + 1, 0:1], (rows, 1)) for j in range(n_heads)], axis=0)


def _softmax_pv_sink(s, v, sink):
    m = jnp.maximum(jnp.max(s, axis=-1, keepdims=True), sink)
    p = jnp.exp2((s - m).astype(BF16))
    o = _dot(p, jnp.concatenate([v, jnp.ones_like(v)], axis=1))
    w = v.shape[1]
    return o[:, 0:w] / (o[:, w:] + jnp.exp2(sink - m))


SWA_BLOCKS_PER_STEP = 4


def _swa_kernel(lat_ref, ctx_ref, sink_ref, o_ref):
    t = lat_ref.shape[1]
    m = ctx_ref.shape[1]
    blk = SWA_BLOCK
    n_loc = 3 * blk
    qi = lax.broadcasted_iota(jnp.int32, (blk, n_loc), 0)
    kj = lax.broadcasted_iota(jnp.int32, (blk, n_loc), 1)
    group = SWA_Q_HEADS // SWA_KV_HEADS
    chains = [(a, g) for a in range(SWA_BLOCKS_PER_STEP) for g in range(SWA_KV_HEADS)]

    def scores(a, g):
        n = pl.program_id(1) * SWA_BLOCKS_PER_STEP + a
        base = pl.multiple_of(jnp.clip((n - 1) * blk, 0, t - n_loc), blk)
        qrows = pl.ds(pl.multiple_of(n * blk, blk), blk)
        krows = pl.ds(base, n_loc)
        mask_add = jnp.where(jnp.abs(base + kj - n * blk - qi) <= SWA_WINDOW, 0.0, NEG_BIG)[None]
        kcol = pl.ds(SWA_QW + g * LANES, LANES)
        k = jnp.concatenate([lat_ref[0, krows, kcol], ctx_ref[0, :, kcol]], axis=0)
        qs = jnp.concatenate(
            [_head_stack(lat_ref[0, qrows, pl.ds((g * 2 + p) * LANES, LANES)], HEAD_DIM, 2) for p in range(2)], axis=0)
        s = _dot_nt(qs, k).reshape(group, blk, n_loc + m)
        s = jnp.concatenate([s[:, :, 0:n_loc] + mask_add, s[:, :, n_loc:]], axis=2).reshape(group * blk, n_loc + m)
        return s, krows

    def finish(a, g, s, krows):
        vcol = pl.ds(SWA_QW + 2 * SWA_KW + g * LANES, LANES)
        v = jnp.concatenate([lat_ref[0, krows, vcol], ctx_ref[0, :, vcol]], axis=0)
        o = _softmax_pv_sink(s, v, _sink_column(sink_ref, g * group, group, blk))
        for p in range(2):
            pair = _head_unstack(o[2 * p * blk:(2 * p + 2) * blk], HEAD_DIM, 2)
            o_ref[0, a * blk:(a + 1) * blk, pl.ds((g * 2 + p) * LANES, LANES)] = pair.astype(BF16)

    nxt = scores(*chains[0])
    for i, (a, g) in enumerate(chains):
        cur = nxt
        if i + 1 < len(chains):
            nxt = scores(*chains[i + 1])
        finish(a, g, *cur)


def _swa(lat, ctx, sink_tab):
    bsz, t, w = lat.shape
    m = ctx.shape[1]
    rows = SWA_BLOCKS_PER_STEP * SWA_BLOCK
    return pl.pallas_call(
        _swa_kernel,
        grid=(bsz, t // rows),
        in_specs=[
            pl.BlockSpec((1, t, w), lambda b, n: (b, 0, 0)),
            pl.BlockSpec((1, m, w), lambda b, n: (b, 0, 0)),
            pl.BlockSpec((SWA_Q_HEADS, LANES), lambda b, n: (0, 0)),
        ],
        out_specs=pl.BlockSpec((1, rows, SWA_QW), lambda b, n: (b, n, 0)),
        out_shape=jax.ShapeDtypeStruct((bsz, t, SWA_QW), BF16),
        compiler_params=_cparams("parallel", "arbitrary"),
        name="swa_attn",
    )(lat, ctx, sink_tab)


def _ctx_attn_kernel(nat_ref, swa_ref, sink_ref, on_ref, os_ref):
    m = nat_ref.shape[1]
    q = nat_ref[0, :, 0:NAT_W]
    k = nat_ref[0, :, NAT_W:2 * NAT_W]
    v = nat_ref[0, :, 2 * NAT_W:3 * NAT_W]
    group = SWA_Q_HEADS // SWA_KV_HEADS

    def swa_scores(g):
        k = swa_ref[0, :, pl.ds(SWA_QW + g * LANES, LANES)]
        qs = jnp.concatenate(
            [_head_stack(swa_ref[0, :, pl.ds((g * 2 + p) * LANES, LANES)], HEAD_DIM, 2) for p in range(2)], axis=0)
        return _dot_nt(qs, k)

    s = _dot_nt(_head_stack(q, HEAD_DIM, NAT_HEADS), k)
    s_swa = [swa_scores(g) for g in range(SWA_KV_HEADS)]
    p = jnp.exp2((s - jnp.max(s, axis=-1, keepdims=True)).astype(BF16))
    o = _dot(p, v) / jnp.sum(p, axis=-1, keepdims=True, dtype=F32)
    on_ref[0] = _head_unstack(o, HEAD_DIM, NAT_HEADS).astype(BF16)

    for g in range(SWA_KV_HEADS):
        v = swa_ref[0, :, pl.ds(SWA_QW + 2 * SWA_KW + g * LANES, LANES)]
        o = _softmax_pv_sink(s_swa[g], v, _sink_column(sink_ref, g * group, group, m))
        for pr in range(2):
            pair = _head_unstack(o[2 * pr * m:(2 * pr + 2) * m], HEAD_DIM, 2)
            os_ref[0, :, pl.ds((g * 2 + pr) * LANES, LANES)] = pair.astype(BF16)


def _ctx_attn(nat_c, swa_c, sink_tab):
    bsz, m, _ = nat_c.shape
    return pl.pallas_call(
        _ctx_attn_kernel,
        grid=(bsz,),
        in_specs=[
            pl.BlockSpec((1, m, NAT_PROJ_W), lambda b: (b, 0, 0)),
            pl.BlockSpec((1, m, SWA_PROJ_W), lambda b: (b, 0, 0)),
            pl.BlockSpec((SWA_Q_HEADS, LANES), lambda b: (0, 0)),
        ],
        out_specs=[
            pl.BlockSpec((1, m, NAT_W), lambda b: (b, 0, 0)),
            pl.BlockSpec((1, m, SWA_QW), lambda b: (b, 0, 0)),
        ],
        out_shape=[
            jax.ShapeDtypeStruct((bsz, m, NAT_W), BF16),
            jax.ShapeDtypeStruct((bsz, m, SWA_QW), BF16),
        ],
        compiler_params=_cparams("parallel"),
        name="ctx_attn",
    )(nat_c, swa_c, sink_tab)


OUT_PROJ_PARTS = 2


def _out_proj_kernel(gla_ref, nat_ref, swa_ref, x_ref, mod_ref, gain_ref, w_ref, wr_ref,
                     x1_ref, h2_ref, aff_ref, afft_ref):
    tm = x_ref.shape[1]
    half = tm // OUT_PROJ_PARTS

    def project(part):
        rows = pl.ds(part * half, half)
        return (_dot(gla_ref[0, rows, :], w_ref[0, 0:GLA_VW, :])
                + _dot(nat_ref[0, rows, :], w_ref[0, GLA_VW:GLA_VW + NAT_W, :])
                + _dot(swa_ref[0, rows, :], w_ref[0, GLA_VW + NAT_W:, :]))

    nxt = project(0)
    for part in range(OUT_PROJ_PARTS):
        rows = pl.ds(part * half, half)
        mix = nxt
        if part + 1 < OUT_PROJ_PARTS:
            nxt = project(part + 1)
        x1 = x_ref[0, rows, :] + mod_ref[0, 2:3, :] * _rms(mix, gain_ref[1:2, :])
        x1_ref[0, rows, :] = x1
        h2 = (_rms(x1, gain_ref[2:3, :]) * (1.0 + mod_ref[0, 4:5, :]) + mod_ref[0, 3:4, :]).astype(BF16)
        h2_ref[0, rows, :] = h2
        logits = _dot(h2, wr_ref[0])
        lane = lax.broadcasted_iota(jnp.int32, logits.shape, 1)
        logits = jnp.where(lane < N_EXPERTS, logits, NEG_BIG)
        e = jnp.exp(logits - jnp.max(logits, axis=-1, keepdims=True))
        aff = e / jnp.sum(e, axis=-1, keepdims=True)
        aff_ref[0, rows, :] = aff
        afft_ref[0, :, rows] = aff.T[0:N_EXPERTS, :]


def _out_proj(gla_o, nat_o, swa_o, x, mod, gains, w_out, wr, layer, tm):
    bsz, t, d = x.shape
    tok = lambda b, i: (b, i, 0)
    const = lambda b, i: (0, 0)
    return pl.pallas_call(
        _out_proj_kernel,
        grid=(bsz, t // tm),
        in_specs=[
            pl.BlockSpec((1, tm, GLA_VW), tok),
            pl.BlockSpec((1, tm, NAT_W), tok),
            pl.BlockSpec((1, tm, SWA_QW), tok),
            pl.BlockSpec((1, tm, d), tok),
            pl.BlockSpec((1, 6, d), lambda b, i: (b, 0, 0)),
            pl.BlockSpec((4, d), const),
            pl.BlockSpec((1, d, d), lambda b, i: (layer, 0, 0)),
            pl.BlockSpec((1, d, LANES), lambda b, i: (layer, 0, 0)),
        ],
        out_specs=[
            pl.BlockSpec((1, tm, d), tok),
            pl.BlockSpec((1, tm, d), tok),
            pl.BlockSpec((1, tm, LANES), tok),
            pl.BlockSpec((1, N_EXPERTS, tm), lambda b, i: (b, 0, i)),
        ],
        out_shape=[
            jax.ShapeDtypeStruct((bsz, t, d), F32),
            jax.ShapeDtypeStruct((bsz, t, d), BF16),
            jax.ShapeDtypeStruct((bsz, t, LANES), F32),
            jax.ShapeDtypeStruct((bsz, N_EXPERTS, t), F32),
        ],
        compiler_params=_cparams("parallel", "parallel"),
        name="out_proj_router",
    )(gla_o, nat_o, swa_o, x, mod, gains, w_out, wr)


def _excl_prefix(x):
    rows, n = x.shape
    nblk = n // LANES
    r = lax.broadcasted_iota(jnp.int32, (LANES, LANES), 0)
    c = lax.broadcasted_iota(jnp.int32, (LANES, LANES), 1)
    strict_upper = (r < c).astype(BF16)
    stacked = jnp.concatenate([x[:, j * LANES:(j + 1) * LANES] for j in range(nblk)], axis=0).astype(BF16)
    local = _dot(stacked, strict_upper)
    totals = jnp.sum(stacked.astype(F32), axis=-1, keepdims=True)
    out = []
    offs = [jnp.zeros((rows, 1), F32)]
    for j in range(nblk):
        out.append(local[j * rows:(j + 1) * rows] + offs[-1])
        offs.append(offs[-1] + totals[j * rows:(j + 1) * rows])
    return jnp.concatenate(out, axis=1), offs


def _topc_kernel(aff_ref, rank_ref, rankcol_ref, cnt_ref, *, cap):
    aff = aff_ref[...]

    def step(i, thr):
        cand = thr | (jnp.int32(1) << (30 - i))
        cnt = jnp.sum((aff >= pltpu.bitcast(cand, F32)).astype(F32), axis=-1, keepdims=True)
        return jnp.where(cnt >= cap, cand, thr)

    thr = lax.fori_loop(0, 31, step, jnp.zeros((aff.shape[0], 1), jnp.int32))
    thr = pltpu.bitcast(thr, F32)
    gt = aff > thr
    eq = aff == thr
    need = cap - jnp.sum(gt.astype(F32), axis=-1, keepdims=True)
    eq_before, _ = _excl_prefix(eq.astype(F32))
    sel = gt | (eq & (eq_before < need))
    sel_before, offs = _excl_prefix(sel.astype(F32))
    rank = jnp.where(sel, sel_before, -1.0)
    rank_ref[...] = rank
    lane = lax.broadcasted_iota(jnp.int32, (rank.shape[0], LANES), 1)
    cnt = jnp.zeros((rank.shape[0], LANES), F32)
    for j, off in enumerate(offs):
        cnt = jnp.where(lane == j, off, cnt)
    cnt_ref[...] = cnt.astype(jnp.int32)
    n = rank.shape[1]
    fill = jnp.full((LANES - N_EXPERTS, n), -1.0, F32)
    for b in range(rankcol_ref.shape[0]):
        rankcol_ref[b] = jnp.concatenate([rank[b * N_EXPERTS:(b + 1) * N_EXPERTS], fill], axis=0).T


def _topc(aff_t, cap):
    bsz, e, n = aff_t.shape
    rows = bsz * e
    rank, rankcol, cnt = pl.pallas_call(
        functools.partial(_topc_kernel, cap=cap),
        grid=(1,),
        in_specs=[pl.BlockSpec((rows, n), lambda i: (0, 0))],
        out_specs=[pl.BlockSpec((rows, n), lambda i: (0, 0)), pl.BlockSpec((bsz, n, LANES), lambda i: (0, 0, 0)),
                   pl.BlockSpec((rows, LANES), lambda i: (0, 0))],
        out_shape=[jax.ShapeDtypeStruct((rows, n), F32), jax.ShapeDtypeStruct((bsz, n, LANES), F32),
                   jax.ShapeDtypeStruct((rows, LANES), jnp.int32)],
        compiler_params=_cparams("arbitrary"),
        name="expert_topc",
    )(aff_t.reshape(rows, n))
    return rank.reshape(bsz, e, n), rankcol, cnt.reshape(bsz, e, LANES)


def _gather_kernel(h_ref, rank_ref, xs_ref, *, cap):
    n = h_ref.shape[1]
    slot = lax.broadcasted_iota(jnp.int32, (cap, n), 0).astype(F32)
    h = h_ref[0]
    for e in range(N_EXPERTS):
        onehot = (slot == rank_ref[0, e:e + 1, :]).astype(BF16)
        xs_ref[e] = _dot(onehot, h).astype(BF16)


def _gather(h, rank, cap):
    bsz, n, d = h.shape
    return pl.pallas_call(
        functools.partial(_gather_kernel, cap=cap),
        grid=(bsz,),
        in_specs=[
            pl.BlockSpec((1, n, d), lambda b: (b, 0, 0)),
            pl.BlockSpec((1, N_EXPERTS, n), lambda b: (b, 0, 0)),
        ],
        out_specs=pl.BlockSpec((N_EXPERTS, cap, d), lambda b: (0, b, 0)),
        out_shape=jax.ShapeDtypeStruct((N_EXPERTS, bsz * cap, d), BF16),
        compiler_params=_cparams("parallel"),
        name="moe_gather",
    )(h, rank)


GATHER_TILE = 256
GATHER_WIN = 5
GATHER_SLOTS = 128


def _gather_win_kernel(cnt_ref, h_ref, rank_ref, xs_ref, *, cap):
    b = pl.program_id(0)
    n = h_ref.shape[1]
    nt = n // GATHER_TILE
    win = GATHER_WIN * GATHER_TILE
    slot_w = lax.broadcasted_iota(jnp.int32, (GATHER_SLOTS, win), 0).astype(F32).astype(BF16)
    slot_f = lax.broadcasted_iota(jnp.int32, (cap, n), 0).astype(F32).astype(BF16)
    one = jnp.ones((), BF16)
    zero = jnp.zeros((), BF16)

    n_sb = cap // GATHER_SLOTS
    group = 4

    def per_group(g, carry):
        starts, fits = [], []
        for j in range(group):
            base = (g * group + j) * (nt + 1)
            c = [cnt_ref[b, base + kt] for kt in range(nt + 1)]
            for sb in range(n_sb):
                lo_slot = sb * GATHER_SLOTS
                first = sum((c[kt + 1] <= lo_slot).astype(jnp.int32) for kt in range(nt))
                last = sum((c[kt] < lo_slot + GATHER_SLOTS).astype(jnp.int32) for kt in range(nt))
                fits.append(last - first <= GATHER_WIN)
                starts.append(jnp.minimum(first, nt - GATHER_WIN))
        all_fit = functools.reduce(jnp.logical_and, fits)

        @pl.when(all_fit)
        def _():
            for j in range(group):
                e = g * group + j
                for sb in range(n_sb):
                    start = starts[j * n_sb + sb]
                    rk = jnp.concatenate([rank_ref[0, e, pl.ds(start + i, 1), :] for i in range(GATHER_WIN)], axis=1)
                    onehot = jnp.where(slot_w == (rk - sb * GATHER_SLOTS).astype(BF16), one, zero)
                    hw = h_ref[0, pl.ds(pl.multiple_of(start * GATHER_TILE, GATHER_TILE), win), :]
                    xs_ref[e, pl.ds(sb * GATHER_SLOTS, GATHER_SLOTS), :] = _dot(onehot, hw).astype(BF16)

        @pl.when(jnp.logical_not(all_fit))
        def _():
            for j in range(group):
                e = g * group + j
                rk = jnp.concatenate([rank_ref[0, e, kt:kt + 1, :] for kt in range(nt)], axis=1).astype(BF16)
                onehot = jnp.where(slot_f == rk, one, zero)
                xs_ref[e] = _dot(onehot, h_ref[0]).astype(BF16)
        return carry

    lax.fori_loop(0, N_EXPERTS // group, per_group, 0)


def _gather_win(h, rank, cnt, cap):
    bsz, n, d = h.shape
    nt = n // GATHER_TILE
    step = GATHER_TILE // LANES
    bounds = cnt[:, :, 0:nt * step + 1:step].reshape(bsz, N_EXPERTS * (nt + 1))
    return pl.pallas_call(
        functools.partial(_gather_win_kernel, cap=cap),
        grid_spec=pltpu.PrefetchScalarGridSpec(
            num_scalar_prefetch=1,
            grid=(bsz,),
            in_specs=[
                pl.BlockSpec((1, n, d), lambda b, c: (b, 0, 0)),
                pl.BlockSpec((1, N_EXPERTS, nt, GATHER_TILE), lambda b, c: (b, 0, 0, 0)),
            ],
            out_specs=pl.BlockSpec((N_EXPERTS, cap, d), lambda b, c: (0, b, 0)),
        ),
        out_shape=jax.ShapeDtypeStruct((N_EXPERTS, bsz * cap, d), BF16),
        compiler_params=_cparams("parallel"),
        name="moe_gather_win",
    )(bounds, h, rank.reshape(bsz, N_EXPERTS, nt, GATHER_TILE))


FFN_PARTS = 4


def _ffn_kernel(*refs, has_ctx):
    if has_ctx:
        xs_ref, xc_ref, wg_ref, wu_ref, wd_ref, y_ref, yc_ref, wg_bf, wu_bf, wd_bf = refs
    else:
        xs_ref, wg_ref, wu_ref, wd_ref, y_ref, wg_bf, wu_bf, wd_bf = refs
    s = pl.program_id(0)
    j = pl.program_id(1)
    n_exp = pl.num_programs(0) - 1
    piece = wg_ref.shape[2]

    @pl.when((s < n_exp) & (j < FFN_PARTS))
    def _():
        rows = pl.ds(pl.multiple_of(j * piece, piece), piece)
        wg_bf[s % 2, rows, :] = wg_ref[0, 0].astype(BF16)
        wu_bf[s % 2, rows, :] = wu_ref[0, 0].astype(BF16)
        wd_bf[s % 2, rows, :] = wd_ref[0, 0].astype(BF16)

    def swiglu(x_ref, o_ref):
        cur = (s + 1) % 2
        xs = x_ref[0]
        hid = (_silu(_dot(xs, wg_bf[cur])) * _dot(xs, wu_bf[cur])).astype(BF16)
        o_ref[0] = _dot(hid, wd_bf[cur]).astype(BF16)

    @pl.when((s > 0) & (j < FFN_PARTS))
    def _():
        swiglu(xs_ref, y_ref)

    if has_ctx:
        @pl.when((s > 0) & (j == FFN_PARTS))
        def _():
            swiglu(xc_ref, yc_ref)


def _ffn(xs, xc, w_gate, w_up, w_down, layer):
    _, e, d, ff = w_gate.shape
    rb = xs.shape[1] // FFN_PARTS
    has_ctx = xc is not None
    last = FFN_PARTS - 1

    def x_idx(s, j):
        return (jnp.maximum(s - 1, 0), jnp.where(s > 0, jnp.minimum(j, last), 0), 0)

    def w_idx(s, j):
        return (layer, jnp.minimum(s, e - 1), jnp.minimum(j, last), 0)

    xspec = pl.BlockSpec((1, rb, d), x_idx)
    cspecs, cargs, cshapes = [], [], []
    if has_ctx:
        cspecs = [pl.BlockSpec((1, xc.shape[1], d), lambda s, j: (jnp.maximum(s - 1, 0), 0, 0))]
        cargs = [xc]
        cshapes = [jax.ShapeDtypeStruct(xc.shape, BF16)]
    out = pl.pallas_call(
        functools.partial(_ffn_kernel, has_ctx=has_ctx),
        grid=(e + 1, FFN_PARTS + int(has_ctx)),
        in_specs=[xspec] + cspecs + [
            pl.BlockSpec((1, 1, d // FFN_PARTS, ff), w_idx),
            pl.BlockSpec((1, 1, d // FFN_PARTS, ff), w_idx),
            pl.BlockSpec((1, 1, ff // FFN_PARTS, d), w_idx),
        ],
        out_specs=[xspec] + cspecs,
        out_shape=[jax.ShapeDtypeStruct(xs.shape, BF16)] + cshapes,
        scratch_shapes=[pltpu.VMEM((2, d, ff), BF16), pltpu.VMEM((2, d, ff), BF16), pltpu.VMEM((2, ff, d), BF16)],
        compiler_params=_cparams("arbitrary", "arbitrary"),
        name="moe_ffn",
    )(xs, *cargs, w_gate, w_up, w_down)
    return out


def _combine_kernel(y_ref, rankcol_ref, aff_ref, x_ref, mod_ref, gain_ref, o_ref, *, cap):
    rb = x_ref.shape[1]
    slot = lax.broadcasted_iota(jnp.int32, (rb, cap), 1).astype(F32)
    rc = rankcol_ref[0]
    af = aff_ref[0]
    acc = jnp.zeros((rb, x_ref.shape[2]), F32)
    for e in range(N_EXPERTS):
        onehot = (rc[:, e:e + 1] == slot).astype(BF16)
        acc = acc + af[:, e:e + 1] * _dot(onehot, y_ref[e])
    o_ref[0] = x_ref[0] + mod_ref[0, 5:6, :] * _rms(acc, gain_ref[3:4, :])


def _combine(y, rankcol, aff, x, mod, gains, cap):
    bsz, n, d = x.shape
    rb = min(n, 512)
    tok = lambda b, i: (b, i, 0)
    return pl.pallas_call(
        functools.partial(_combine_kernel, cap=cap),
        grid=(bsz, n // rb),
        in_specs=[
            pl.BlockSpec((N_EXPERTS, cap, d), lambda b, i: (0, b, 0)),
            pl.BlockSpec((1, rb, LANES), tok),
            pl.BlockSpec((1, rb, LANES), tok),
            pl.BlockSpec((1, rb, d), tok),
            pl.BlockSpec((1, 6, d), lambda b, i: (b, 0, 0)),
            pl.BlockSpec((4, d), lambda b, i: (0, 0)),
        ],
        out_specs=pl.BlockSpec((1, rb, d), tok),
        out_shape=jax.ShapeDtypeStruct((bsz, n, d), F32),
        compiler_params=_cparams("parallel", "arbitrary"),
        name="moe_combine",
    )(y, rankcol, aff, x, mod, gains)


COMBINE_ROWS = 512
COMBINE_WIN = 128


def _combine_pair_kernel(cnt_ref, y_ref, rankcol_ref, aff_ref, x_ref, mod_ref, gain_ref, o_ref, extra_ref, *, cap):
    b = pl.program_id(0)
    i = pl.program_id(1)
    nrb = pl.num_programs(1)
    rb = x_ref.shape[1]
    slot_w = lax.broadcasted_iota(jnp.int32, (rb, COMBINE_WIN), 1).astype(F32)
    rc = rankcol_ref[0]
    af = aff_ref[0]
    los, unfit = [], []
    for e in range(N_EXPERTS):
        c0 = cnt_ref[b, e * (nrb + 1) + i]
        c1 = cnt_ref[b, e * (nrb + 1) + i + 1]
        lo = jnp.minimum((c0 // 16) * 16, cap - COMBINE_WIN)
        los.append(pl.multiple_of(lo, 16))
        unfit.append(c1 - lo > COMBINE_WIN)
    def operands(e0):
        hot, rows = [], []
        for e in (e0, e0 + 1):
            hot.append(jnp.where(rc[:, e:e + 1] - los[e].astype(F32) == slot_w, af[:, e:e + 1], 0.0).astype(BF16))
            rows.append(y_ref[e, pl.ds(los[e], COMBINE_WIN), :])
        return jnp.concatenate(hot, axis=1), jnp.concatenate(rows, axis=0)

    acc = jnp.zeros((rb, x_ref.shape[2]), F32)
    nxt = operands(0)
    for e0 in range(0, N_EXPERTS, 2):
        cur = nxt
        if e0 + 2 < N_EXPERTS:
            nxt = operands(e0 + 2)
        acc = acc + _dot(*cur)

    def finish(total):
        o_ref[0] = x_ref[0] + mod_ref[0, 5:6, :] * _rms(total, gain_ref[3:4, :])

    any_unfit = functools.reduce(jnp.logical_or, unfit)

    @pl.when(jnp.logical_not(any_unfit))
    def _():
        finish(acc)

    @pl.when(any_unfit)
    def _():
        extra_ref[...] = jnp.zeros_like(extra_ref)
        slot_f = lax.broadcasted_iota(jnp.int32, (rb, cap), 1)
        for e in range(N_EXPERTS):
            @pl.when(unfit[e])
            def _():
                outside = (slot_f < los[e]) | (slot_f >= los[e] + COMBINE_WIN)
                hot = jnp.where((rc[:, e:e + 1] == slot_f.astype(F32)) & outside, af[:, e:e + 1], 0.0).astype(BF16)
                extra_ref[...] += _dot(hot, y_ref[e])
        finish(acc + extra_ref[...])


def _combine_pair(y, rankcol, aff, cnt, x, mod, gains, cap):
    bsz, n, d = x.shape
    rb = COMBINE_ROWS
    nrb = n // rb
    step = rb // LANES
    bounds = cnt[:, :, 0:nrb * step + 1:step].reshape(bsz, N_EXPERTS * (nrb + 1))
    tok = lambda b, i, c: (b, i, 0)
    return pl.pallas_call(
        functools.partial(_combine_pair_kernel, cap=cap),
        grid_spec=pltpu.PrefetchScalarGridSpec(
            num_scalar_prefetch=1,
            grid=(bsz, nrb),
            in_specs=[
                pl.BlockSpec((N_EXPERTS, cap, d), lambda b, i, c: (0, b, 0)),
                pl.BlockSpec((1, rb, LANES), tok),
                pl.BlockSpec((1, rb, LANES), tok),
                pl.BlockSpec((1, rb, d), tok),
                pl.BlockSpec((1, 6, d), lambda b, i, c: (b, 0, 0)),
                pl.BlockSpec((4, d), lambda b, i, c: (0, 0)),
            ],
            out_specs=pl.BlockSpec((1, rb, d), tok),
            scratch_shapes=[pltpu.VMEM((rb, d), F32)],
        ),
        out_shape=jax.ShapeDtypeStruct((bsz, n, d), F32),
        compiler_params=_cparams("parallel", "arbitrary"),
        name="moe_combine_pair",
    )(bounds, y, rankcol, aff, x, mod, gains)


def _relayout_w_in(w):
    cut = GLA_KW + GLA_VW + 2 * GLA_RANK
    pad = jnp.zeros(w.shape[:2] + (LANES - 2 * GLA_RANK,), w.dtype)
    return jnp.concatenate([w[:, :, :cut], pad, w[:, :, cut:]], axis=2).astype(BF16)


def _rope_tables(t):
    half = HEAD_DIM // 4
    freqs = ROPE_BASE ** (-np.arange(half, dtype=np.float32) / half)
    pos = np.arange(t)
    ang_r = (pos // GRID_W).astype(np.float32)[:, None] * freqs
    ang_c = (pos % GRID_W).astype(np.float32)[:, None] * freqs
    cos = np.concatenate([np.cos(ang_r), np.cos(ang_r), np.cos(ang_c), np.cos(ang_c)], axis=1)
    sin = np.concatenate([-np.sin(ang_r), np.sin(ang_r), -np.sin(ang_c), np.sin(ang_c)], axis=1)
    reps = LANES // HEAD_DIM
    return jnp.asarray(np.tile(cos, (1, reps)), F32), jnp.asarray(np.tile(sin, (1, reps)), F32)


def _moe(h_list, afft_list, aff_list, x_list, mod_list, gains, w_gate, w_up, w_down, layer):
    caps = [CAPACITY_FACTOR * h.shape[1] // N_EXPERTS for h in h_list]
    ranks = [_topc(a, cap) for a, cap in zip(afft_list, caps)]
    long = [h.shape[1] >= GATHER_WIN * GATHER_TILE and h.shape[1] % COMBINE_ROWS == 0 and cap >= 2 * COMBINE_WIN
            for h, cap in zip(h_list, caps)]
    xs = [_gather_win(h, r[0], r[2], cap) if lg else _gather(h, r[0], cap)
          for h, r, cap, lg in zip(h_list, ranks, caps, long)]
    ys = _ffn(xs[0], xs[1] if len(xs) > 1 else None, w_gate, w_up, w_down, layer)
    return [_combine_pair(y, r[1], aff, r[2], x, mod, gains, cap) if lg else _combine(y, r[1], aff, x, mod, gains, cap)
            for y, r, aff, x, mod, cap, lg in zip(ys, ranks, aff_list, x_list, mod_list, caps, long)]


def kernel(x, c, ctx, c_ctx, w_mod, b_mod, norm_gains, w_in, w_out, gla_a_up, gla_a_bias, gla_norm,
           nat_rpb, swa_sink, w_router, w_gate, w_up, w_down):
    bsz, t, d = x.shape
    m = ctx.shape[1]
    depth = w_mod.shape[0]

    cc = jnp.concatenate([c, c_ctx[None], jnp.zeros((16 - bsz - 1, d), F32)], axis=0)
    mod_all = _modulation(cc, w_mod, b_mod)
    rope_tabs = _rope_tables(t)
    zero_state = jnp.zeros((bsz, GLA_VW, GLA_KW), F32)
    w_in_b = _relayout_w_in(w_in)
    w_out_b = w_out.astype(BF16)
    wr_b = jnp.pad(w_router, ((0, 0), (0, 0), (0, LANES - N_EXPERTS))).astype(BF16)

    xc = ctx
    for l in range(depth):
        update_ctx = l < depth - 1
        mod = mod_all[l, :bsz].reshape(bsz, 6, d)
        mod_c = jnp.broadcast_to(mod_all[l, bsz].reshape(1, 6, d), (bsz, 6, d))
        gains = norm_gains[l]
        aup = jnp.zeros((LANES, 2 * GLA_KW), F32)
        aup = aup.at[0:GLA_RANK, 0:GLA_KW].set(gla_a_up[l, 0]).at[GLA_RANK:2 * GLA_RANK, GLA_KW:].set(gla_a_up[l, 1])
        abias = gla_a_bias[l].reshape(1, 2 * GLA_KW)
        gla_gain = jnp.tile(gla_norm[l], GLA_HEADS).reshape(1, GLA_VW)
        sink_tab = jnp.broadcast_to(swa_sink[l][:, None] * LOG2E, (SWA_Q_HEADS, LANES))

        gla_p, nat_p, swa_p = _in_proj(x, mod, gains, w_in_b, l, aup, abias, rope_tabs, 1024)
        gla_c, nat_c, swa_c = _in_proj(xc, mod_c, gains, w_in_b, l, aup, abias, None, m)

        gla_co, s_f, s_b = _gla(gla_c, zero_state, zero_state, gla_gain)
        gla_o, _, _ = _gla(gla_p, s_f, s_b, gla_gain)
        nat_o = _nat(nat_p, nat_c, _nat_bias_table(nat_rpb[l]))
        swa_o = _swa(swa_p, swa_c, sink_tab)

        x1, h2, aff, aff_t = _out_proj(gla_o, nat_o, swa_o, x, mod, gains, w_out_b, wr_b, l, 1024)
        if update_ctx:
            nat_co, swa_co = _ctx_attn(nat_c, swa_c, sink_tab)
            xc1, hc2, aff_c, aff_ct = _out_proj(gla_co, nat_co, swa_co, xc, mod_c, gains, w_out_b, wr_b, l, m)
            x, xc = _moe([h2, hc2], [aff_t, aff_ct], [aff, aff_c], [x1, xc1], [mod, mod_c], gains,
                         w_gate, w_up, w_down, l)
        else:
            (x,) = _moe([h2], [aff_t], [aff], [x1], [mod], gains, w_gate, w_up, w_down, l)
    return x
```

```python
import functools

import jax
import jax.numpy as jnp
import numpy as np
from jax import lax
from jax.experimental import pallas as pl
from jax.experimental.pallas import tpu as pltpu

F32 = jnp.float32
BF16 = jnp.bfloat16

D_MODEL = 1024
GRID_W = 64
HEAD_DIM = 64
GLA_HEADS = 4
GLA_DK = 32
GLA_DV = 64
GLA_RANK = 16
GLA_TAU = 16.0
GLA_CHUNK = 64
NAT_HEADS = 4
NAT_WIN_ROWS = 8
NAT_WIN_COLS = 16
SWA_Q_HEADS = 8
SWA_KV_HEADS = 2
SWA_WINDOW = 128
SWA_BLOCK = 128
ROPE_BASE = 10000.0
N_EXPERTS = 16
EXPERT_FF = 1024
CAPACITY_FACTOR = 2
NORM_EPS = 1e-6

GLA_KW = GLA_HEADS * GLA_DK
GLA_VW = GLA_HEADS * GLA_DV
NAT_W = NAT_HEADS * HEAD_DIM
SWA_QW = SWA_Q_HEADS * HEAD_DIM
SWA_KW = SWA_KV_HEADS * HEAD_DIM
LANES = 128
NEG_BIG = -1e30
LOG2E = 1.4426950408889634

GLA_OUT_W = 2 * GLA_KW + 2 * GLA_VW + 2 * GLA_KW
NAT_PROJ_W = 3 * NAT_W
SWA_PROJ_W = SWA_QW + 4 * SWA_KW
W_GK = 0
W_GV = W_GK + GLA_KW
W_GA = W_GV + GLA_VW
W_NK = W_GA + LANES
W_SK = W_NK + 2 * NAT_W
W_SV = W_SK + SWA_KW
W_Q0 = W_SV + SWA_KW
W_GQ = W_Q0
W_GG = W_GQ + GLA_KW
W_NQ = W_GG + GLA_VW
W_SQ = W_NQ + NAT_W
IN_PROJ_W = W_SQ + SWA_QW

VMEM_LIMIT = 56 * 1024 * 1024


def _cparams(*sem):
    return pltpu.CompilerParams(dimension_semantics=sem, vmem_limit_bytes=VMEM_LIMIT)


def _dot(a, b):
    return jnp.dot(a, b, preferred_element_type=F32)


def _dot_nt(a, b):
    return lax.dot_general(a, b, (((1,), (1,)), ((), ())), preferred_element_type=F32)


def _dot_tn(a, b):
    return lax.dot_general(a, b, (((0,), (0,)), ((), ())), preferred_element_type=F32)


def _split2(a):
    hi = a.astype(BF16)
    lo = (a - hi.astype(F32)).astype(BF16)
    return hi, lo


def _split3(a):
    hi = a.astype(BF16)
    r = a - hi.astype(F32)
    mid = r.astype(BF16)
    lo = (r - mid.astype(F32)).astype(BF16)
    return hi, mid, lo


def _dot_f32(a, b):
    ah, al = _split2(a)
    bh, bl = _split2(b)
    return _dot(ah, bh) + _dot(al, bh) + _dot(ah, bl)


def _rms(x, gain):
    return x * lax.rsqrt(jnp.mean(x * x, axis=-1, keepdims=True) + NORM_EPS) * gain


def _silu(x):
    return x * jax.nn.sigmoid(x)


def _mod_kernel(c_ref, w_ref, b_ref, o_ref):
    a = _silu(c_ref[...])
    o_ref[0] = _dot_f32(a, w_ref[0]) + b_ref[0]


def _modulation(cc, w_mod, b_mod):
    depth, d, n = w_mod.shape
    r = cc.shape[0]
    tn = 1536
    return pl.pallas_call(
        _mod_kernel,
        grid=(depth, n // tn),
        in_specs=[
            pl.BlockSpec((r, d), lambda l, j: (0, 0)),
            pl.BlockSpec((1, d, tn), lambda l, j: (l, 0, j)),
            pl.BlockSpec((1, 1, tn), lambda l, j: (l, 0, j)),
        ],
        out_specs=pl.BlockSpec((1, r, tn), lambda l, j: (l, 0, j)),
        out_shape=jax.ShapeDtypeStruct((depth, r, n), F32),
        compiler_params=_cparams("parallel", "parallel"),
        name="adaln_mod",
    )(cc, w_mod, b_mod.reshape(depth, 1, n))


def _rope_rotate(x, first_half):
    up = pltpu.roll(x, LANES - 16, 1)
    down = pltpu.roll(x, 16, 1)
    return jnp.where(first_half, up, down)


IN_PROJ_PARTS = 1
CUMSUM_ROWS = 512


def _in_proj_kernel(*refs, rope):
    if rope:
        x_ref, mod_ref, gain_ref, w_ref, aup_ref, ab_ref, tri_ref, cos_ref, sin_ref, gla_ref, nat_ref, swa_ref = refs
    else:
        x_ref, mod_ref, gain_ref, w_ref, aup_ref, ab_ref, tri_ref, gla_ref, nat_ref, swa_ref = refs
    tm = x_ref.shape[1]
    pm = tm // IN_PROJ_PARTS
    kq = 2 * GLA_KW + 2 * GLA_VW
    scale = HEAD_DIM ** -0.5 * LOG2E
    nck = pm // GLA_CHUNK
    tri = tri_ref[...]
    for part in range(IN_PROJ_PARTS):
        rows = pl.ds(part * pm, pm)
        x = x_ref[0, rows, :]
        y = _rms(x, gain_ref[0:1, :])
        h = (y * (1.0 + mod_ref[0, 1:2, :]) + mod_ref[0, 0:1, :]).astype(BF16)

        pkv = _dot(h, w_ref[0, :, 0:W_Q0])
        pq = _dot(h, w_ref[0, :, W_Q0:])

        gla_ref[0, rows, 0:GLA_KW] = pkv[:, W_GK:W_GK + GLA_KW]
        gla_ref[0, rows, GLA_KW:2 * GLA_KW] = pq[:, W_GQ - W_Q0:W_GQ - W_Q0 + GLA_KW]
        gla_ref[0, rows, 2 * GLA_KW:2 * GLA_KW + GLA_VW] = pkv[:, W_GV:W_GV + GLA_VW]
        gla_ref[0, rows, 2 * GLA_KW + GLA_VW:kq] = pq[:, W_GG - W_Q0:W_GG - W_Q0 + GLA_VW]
        z = _dot_f32(pkv[:, W_GA:W_GA + LANES], aup_ref[...]) + ab_ref[...]
        log_a = (jnp.minimum(z, 0.0) - jnp.log1p(jnp.exp(-jnp.abs(z)))) * (1.0 / GLA_TAU)
        hi, lo = _split2(log_a)
        cr = tri.shape[0]
        cum = jnp.concatenate([_dot(tri, hi[i * cr:(i + 1) * cr]) + _dot(tri, lo[i * cr:(i + 1) * cr])
                               for i in range(pm // cr)], axis=0)
        gla_ref[0, rows, kq:kq + GLA_KW] = cum[:, 0:GLA_KW]
        cb = cum[:, GLA_KW:].reshape(nck, GLA_CHUNK, GLA_KW)
        lb = log_a[:, GLA_KW:].reshape(nck, GLA_CHUNK, GLA_KW)
        from_end = cb[:, GLA_CHUNK - 1:GLA_CHUNK, :] - cb + lb
        gla_ref[0, rows, kq + GLA_KW:kq + 2 * GLA_KW] = from_end.reshape(pm, GLA_KW)

        nat_ref[0, rows, 0:NAT_W] = (pq[:, W_NQ - W_Q0:W_NQ - W_Q0 + NAT_W] * scale).astype(BF16)
        nat_ref[0, rows, NAT_W:] = pkv[:, W_NK:W_NK + 2 * NAT_W].astype(BF16)

        if rope:
            cos = cos_ref[rows, :]
            sin = sin_ref[rows, :]
        lane = lax.broadcasted_iota(jnp.int32, (pm, LANES), 1)
        first_half = (lane % 32) < 16
        low_head = lane < HEAD_DIM

        def rotary(t):
            return t * cos + _rope_rotate(t, first_half) * sin if rope else t

        for j in range(SWA_QW // LANES):
            t = rotary(pq[:, W_SQ - W_Q0 + j * LANES:W_SQ - W_Q0 + (j + 1) * LANES]) * scale
            swa_ref[0, rows, j * LANES:(j + 1) * LANES] = t.astype(BF16)
        for j, t in enumerate((rotary(pkv[:, W_SK:W_SK + SWA_KW]), pkv[:, W_SV:W_SV + SWA_KW])):
            other = pltpu.roll(t, HEAD_DIM, 1)
            o = SWA_QW + 2 * j * SWA_KW
            swa_ref[0, rows, o:o + LANES] = jnp.where(low_head, t, other).astype(BF16)
            swa_ref[0, rows, o + LANES:o + 2 * LANES] = jnp.where(low_head, other, t).astype(BF16)


def _in_proj(x, mod, gains, w, layer, aup, abias, rope_tabs, tm):
    bsz, t, d = x.shape
    rope = rope_tabs is not None
    cum_rows = min(tm // IN_PROJ_PARTS, CUMSUM_ROWS)
    in_specs = [
        pl.BlockSpec((1, tm, d), lambda b, i: (b, i, 0)),
        pl.BlockSpec((1, 6, d), lambda b, i: (b, 0, 0)),
        pl.BlockSpec((4, d), lambda b, i: (0, 0)),
        pl.BlockSpec((1, d, IN_PROJ_W), lambda b, i: (layer, 0, 0)),
        pl.BlockSpec((LANES, 2 * GLA_KW), lambda b, i: (0, 0)),
        pl.BlockSpec((1, 2 * GLA_KW), lambda b, i: (0, 0)),
        pl.BlockSpec((cum_rows, cum_rows), lambda b, i: (0, 0)),
    ]
    r = np.arange(cum_rows)
    tri = (r[:, None] // GLA_CHUNK == r[None, :] // GLA_CHUNK) & (r[:, None] >= r[None, :])
    args = [x, mod, gains, w, aup, abias, jnp.asarray(tri, BF16)]
    if rope:
        in_specs += [pl.BlockSpec((tm, LANES), lambda b, i: (i, 0))] * 2
        args += list(rope_tabs)
    return pl.pallas_call(
        functools.partial(_in_proj_kernel, rope=rope),
        grid=(bsz, t // tm),
        in_specs=in_specs,
        out_specs=[
            pl.BlockSpec((1, tm, GLA_OUT_W), lambda b, i: (b, i, 0)),
            pl.BlockSpec((1, tm, NAT_PROJ_W), lambda b, i: (b, i, 0)),
            pl.BlockSpec((1, tm, SWA_PROJ_W), lambda b, i: (b, i, 0)),
        ],
        out_shape=[
            jax.ShapeDtypeStruct((bsz, t, GLA_OUT_W), F32),
            jax.ShapeDtypeStruct((bsz, t, NAT_PROJ_W), BF16),
            jax.ShapeDtypeStruct((bsz, t, SWA_PROJ_W), BF16),
        ],
        compiler_params=_cparams("parallel", "parallel"),
        name="in_proj_rope" if rope else "in_proj",
    )(*args)


GLA_SAMPLES_PER_STEP = 2


def _gla_kernel(p_ref, s0f_ref, s0b_ref, gain_ref, o_ref, sff_ref, sfb_ref, acc_ref, sf_ref, sb_ref):
    nb, t, _ = p_ref.shape
    c = GLA_CHUNK
    nc = t // c
    qi = lax.broadcasted_iota(jnp.int32, (c, GLA_HEADS * c), 0)
    kj = lax.broadcasted_iota(jnp.int32, (c, GLA_HEADS * c), 1) % c
    lower = qi >= kj
    upper = qi <= kj
    srow = lax.broadcasted_iota(jnp.int32, (GLA_VW, GLA_KW), 0) // GLA_DV
    scol = lax.broadcasted_iota(jnp.int32, (GLA_VW, GLA_KW), 1) // GLA_DK
    head_diag = srow == scol
    q_scale = GLA_DK ** -0.5

    sf_ref[...] = s0f_ref[...]
    sb_ref[...] = s0b_ref[...]
    acc_ref[...] = jnp.zeros_like(acc_ref)

    def state_free(n, base, end_row, cum_off):
        rows = pl.ds(base, c)
        k = p_ref[n, rows, 0:GLA_KW]
        q = p_ref[n, rows, GLA_KW:2 * GLA_KW] * q_scale
        vb = p_ref[n, rows, 2 * GLA_KW:2 * GLA_KW + GLA_VW].astype(BF16)
        cum = p_ref[n, rows, cum_off:cum_off + GLA_KW]
        mid = cum[c // 2:c // 2 + 1, :]
        total = cum[end_row:end_row + 1, :]
        qs = (q * jnp.exp(cum - mid)).astype(BF16)
        ks = (k * jnp.exp(mid - cum)).astype(BF16)
        s = _dot_nt(qs, _head_stack(ks, GLA_DK, GLA_HEADS))
        kd = (k * jnp.exp(total - cum)).astype(BF16)
        upd = _dot_tn(vb, kd)
        qe = (q * jnp.exp(cum)).astype(BF16)
        return rows, s, upd, qe, vb, jnp.exp(total)

    def with_state(n, keep, st_ref, rows, s, upd, qe, vb, decay):
        st = st_ref[n]
        inter = _dot_nt(qe, st.astype(BF16))
        s = jnp.where(keep, s, 0.0).astype(BF16)
        intra = _dot(s, _head_stack(vb, GLA_DV, GLA_HEADS))
        acc_ref[n, rows, :] += inter + intra
        st_ref[n] = st * decay + jnp.where(head_diag, upd, 0.0)

    def body(i, carry):
        fwd = [state_free(n, pl.multiple_of(i * c, c), c - 1, 2 * GLA_KW + 2 * GLA_VW) for n in range(nb)]
        bwd = [state_free(n, pl.multiple_of((nc - 1 - i) * c, c), 0, 3 * GLA_KW + 2 * GLA_VW) for n in range(nb)]
        for n in range(nb):
            with_state(n, lower, sf_ref, *fwd[n])
            with_state(n, upper, sb_ref, *bwd[n])
        return carry

    lax.fori_loop(0, nc, body, 0, unroll=4)
    sff_ref[...] = sf_ref[...]
    sfb_ref[...] = sb_ref[...]

    hr = lax.broadcasted_iota(jnp.int32, (GLA_VW, GLA_VW), 0) // GLA_DV
    hc = lax.broadcasted_iota(jnp.int32, (GLA_VW, GLA_VW), 1) // GLA_DV
    head_ones = (hr == hc).astype(BF16)
    blk = 256
    blocks = [(n, j) for n in range(nb) for j in range(t // blk)]

    def mean_square(n, j):
        o = acc_ref[n, pl.ds(j * blk, blk), :]
        hi, mid, lo = _split3(o * o)
        return o, (_dot(hi, head_ones) + _dot(mid, head_ones) + _dot(lo, head_ones)) * (1.0 / GLA_DV)

    nxt = mean_square(*blocks[0])
    for i, (n, j) in enumerate(blocks):
        rows = pl.ds(j * blk, blk)
        o, ms = nxt
        if i + 1 < len(blocks):
            nxt = mean_square(*blocks[i + 1])
        gate = p_ref[n, rows, 2 * GLA_KW + GLA_VW:2 * GLA_KW + 2 * GLA_VW]
        o_ref[n, rows, :] = (o * lax.rsqrt(ms + NORM_EPS) * gain_ref[...] * _silu(gate)).astype(BF16)


def _gla(p, s0f, s0b, gain):
    bsz, t, w = p.shape
    nb = GLA_SAMPLES_PER_STEP if bsz % GLA_SAMPLES_PER_STEP == 0 else 1
    st_spec = pl.BlockSpec((nb, GLA_VW, GLA_KW), lambda b: (b, 0, 0))
    st_shape = jax.ShapeDtypeStruct((bsz, GLA_VW, GLA_KW), F32)
    return pl.pallas_call(
        _gla_kernel,
        grid=(bsz // nb,),
        in_specs=[
            pl.BlockSpec((nb, t, w), lambda b: (b, 0, 0)),
            st_spec, st_spec,
            pl.BlockSpec((1, GLA_VW), lambda b: (0, 0)),
        ],
        out_specs=[pl.BlockSpec((nb, t, GLA_VW), lambda b: (b, 0, 0)), st_spec, st_spec],
        out_shape=[jax.ShapeDtypeStruct((bsz, t, GLA_VW), BF16), st_shape, st_shape],
        scratch_shapes=[
            pltpu.VMEM((nb, t, GLA_VW), F32),
            pltpu.VMEM((nb, GLA_VW, GLA_KW), F32),
            pltpu.VMEM((nb, GLA_VW, GLA_KW), F32),
        ],
        compiler_params=_cparams("parallel"),
        name="gla_bidir",
    )(p, s0f, s0b, gain)


def _head_stack(q, width, n):
    lane = lax.broadcasted_iota(jnp.int32, (1, q.shape[1]), 1) // width
    return jnp.concatenate([jnp.where(lane == h, q, jnp.zeros_like(q)) for h in range(n)], axis=0)


def _head_unstack(o, width, n):
    rows = o.shape[0] // n
    lane = lax.broadcasted_iota(jnp.int32, (1, o.shape[1]), 1) // width
    out = o[0:rows]
    for h in range(1, n):
        out = jnp.where(lane == h, o[h * rows:(h + 1) * rows], out)
    return out


NAT_ROWS_PER_STEP = 16


def _nat_kernel(lat_ref, ctx_ref, bias_ref, o_ref):
    n_rows = lat_ref.shape[1] // GRID_W
    kc = ctx_ref[0, :, NAT_W:2 * NAT_W]
    vc = ctx_ref[0, :, 2 * NAT_W:3 * NAT_W]
    n_loc = NAT_WIN_ROWS * GRID_W

    def scores(i):
        r = pl.program_id(1) * NAT_ROWS_PER_STEP + i
        r0 = jnp.clip(r - NAT_WIN_ROWS // 2, 0, n_rows - NAT_WIN_ROWS)
        q = lat_ref[0, pl.ds(pl.multiple_of(r * GRID_W, GRID_W), GRID_W), 0:NAT_W]
        krows = pl.ds(pl.multiple_of(r0 * GRID_W, GRID_W), n_loc)
        qs = _head_stack(q, HEAD_DIM, NAT_HEADS)
        first = r0 - r + NAT_WIN_ROWS - 1
        bias = jnp.concatenate([bias_ref[first + 2 * j] for j in range(NAT_WIN_ROWS // 2)], axis=1)
        s = jnp.concatenate([_dot_nt(qs, lat_ref[0, krows, NAT_W:2 * NAT_W]) + bias, _dot_nt(qs, kc)], axis=1)
        return s, krows

    def finish(i, s, krows):
        p = jnp.exp2((s - jnp.max(s, axis=-1, keepdims=True)).astype(BF16))
        den = jnp.sum(p, axis=-1, keepdims=True, dtype=F32)
        o = (_dot(p[:, 0:n_loc], lat_ref[0, krows, 2 * NAT_W:3 * NAT_W]) + _dot(p[:, n_loc:], vc)) / den
        o_ref[0, i * GRID_W:(i + 1) * GRID_W, :] = _head_unstack(o, HEAD_DIM, NAT_HEADS).astype(BF16)

    nxt = scores(0)
    for i in range(NAT_ROWS_PER_STEP):
        cur = nxt
        if i + 1 < NAT_ROWS_PER_STEP:
            nxt = scores(i + 1)
        finish(i, *cur)


def _nat(lat, ctx, bias):
    bsz, t, w = lat.shape
    m = ctx.shape[1]
    rows = NAT_ROWS_PER_STEP * GRID_W
    return pl.pallas_call(
        _nat_kernel,
        grid=(bsz, t // rows),
        in_specs=[
            pl.BlockSpec((1, t, w), lambda b, r: (b, 0, 0)),
            pl.BlockSpec((1, m, w), lambda b, r: (b, 0, 0)),
            pl.BlockSpec(bias.shape, lambda b, r: (0, 0, 0)),
        ],
        out_specs=pl.BlockSpec((1, rows, NAT_W), lambda b, r: (b, r, 0)),
        out_shape=jax.ShapeDtypeStruct((bsz, t, NAT_W), BF16),
        compiler_params=_cparams("parallel", "arbitrary"),
        name="nat_attn",
    )(lat, ctx, bias)


def _nat_bias_table(rpb):
    cols = np.arange(GRID_W)
    col_start = np.clip(cols - NAT_WIN_COLS // 2, 0, GRID_W - NAT_WIN_COLS)
    kc = np.arange(GRID_W)
    inside = (kc[None, :] >= col_start[:, None]) & (kc[None, :] < col_start[:, None] + NAT_WIN_COLS)
    rel = kc[None, :] - cols[:, None] + NAT_WIN_COLS - 1
    col_sel = (rel[:, :, None] == np.arange(2 * NAT_WIN_COLS - 1)) & inside[:, :, None]
    tab = jnp.einsum("hab,ckb->ahck", rpb.astype(F32), jnp.asarray(col_sel, F32), precision=lax.Precision.HIGHEST)
    tab = jnp.where(jnp.asarray(inside)[None, None], tab * LOG2E, NEG_BIG)
    tab = tab.reshape(2 * NAT_WIN_ROWS - 1, NAT_HEADS * GRID_W, GRID_W)
    return jnp.concatenate([tab[:-1], tab[1:]], axis=2)


def _sink_column(sink_ref, first_head, n_heads, rows):
    return jnp.concatenate(
        [jnp.broadcast_to(sink_ref[first_head + j:first_head + j + 1, 0:1], (rows, 1)) for j in range(n_heads)], axis=0)


def _softmax_pv_sink(s, v, sink):
    m = jnp.maximum(jnp.max(s, axis=-1, keepdims=True), sink)
    p = jnp.exp2((s - m).astype(BF16))
    o = _dot(p, jnp.concatenate([v, jnp.ones_like(v)], axis=1))
    w = v.shape[1]
    return o[:, 0:w] / (o[:, w:] + jnp.exp2(sink - m))


SWA_BLOCKS_PER_STEP = 4


def _swa_kernel(lat_ref, ctx_ref, sink_ref, o_ref):
    t = lat_ref.shape[1]
    m = ctx_ref.shape[1]
    blk = SWA_BLOCK
    n_loc = 3 * blk
    qi = lax.broadcasted_iota(jnp.int32, (blk, n_loc), 0)
    kj = lax.broadcasted_iota(jnp.int32, (blk, n_loc), 1)
    group = SWA_Q_HEADS // SWA_KV_HEADS
    chains = [(a, g) for a in range(SWA_BLOCKS_PER_STEP) for g in range(SWA_KV_HEADS)]

    def scores(a, g):
        n = pl.program_id(1) * SWA_BLOCKS_PER_STEP + a
        base = pl.multiple_of(jnp.clip((n - 1) * blk, 0, t - n_loc), blk)
        qrows = pl.ds(pl.multiple_of(n * blk, blk), blk)
        krows = pl.ds(base, n_loc)
        mask_add = jnp.where(jnp.abs(base + kj - n * blk - qi) <= SWA_WINDOW, 0.0, NEG_BIG)[None]
        kcol = pl.ds(SWA_QW + g * LANES, LANES)
        k = jnp.concatenate([lat_ref[0, krows, kcol], ctx_ref[0, :, kcol]], axis=0)
        qs = jnp.concatenate(
            [_head_stack(lat_ref[0, qrows, pl.ds((g * 2 + p) * LANES, LANES)], HEAD_DIM, 2) for p in range(2)], axis=0)
        s = _dot_nt(qs, k).reshape(group, blk, n_loc + m)
        s = jnp.concatenate([s[:, :, 0:n_loc] + mask_add, s[:, :, n_loc:]], axis=2).reshape(group * blk, n_loc + m)
        return s, krows

    def finish(a, g, s, krows):
        vcol = pl.ds(SWA_QW + 2 * SWA_KW + g * LANES, LANES)
        v = jnp.concatenate([lat_ref[0, krows, vcol], ctx_ref[0, :, vcol]], axis=0)
        o = _softmax_pv_sink(s, v, _sink_column(sink_ref, g * group, group, blk))
        for p in range(2):
            pair = _head_unstack(o[2 * p * blk:(2 * p + 2) * blk], HEAD_DIM, 2)
            o_ref[0, a * blk:(a + 1) * blk, pl.ds((g * 2 + p) * LANES, LANES)] = pair.astype(BF16)

    nxt = scores(*chains[0])
    for i, (a, g) in enumerate(chains):
        cur = nxt
        if i + 1 < len(chains):
            nxt = scores(*chains[i + 1])
        finish(a, g, *cur)


def _swa(lat, ctx, sink_tab):
    bsz, t, w = lat.shape
    m = ctx.shape[1]
    rows = SWA_BLOCKS_PER_STEP * SWA_BLOCK
    return pl.pallas_call(
        _swa_kernel,
        grid=(bsz, t // rows),
        in_specs=[
            pl.BlockSpec((1, t, w), lambda b, n: (b, 0, 0)),
            pl.BlockSpec((1, m, w), lambda b, n: (b, 0, 0)),
            pl.BlockSpec((SWA_Q_HEADS, LANES), lambda b, n: (0, 0)),
        ],
        out_specs=pl.BlockSpec((1, rows, SWA_QW), lambda b, n: (b, n, 0)),
        out_shape=jax.ShapeDtypeStruct((bsz, t, SWA_QW), BF16),
        compiler_params=_cparams("parallel", "arbitrary"),
        name="swa_attn",
    )(lat, ctx, sink_tab)


def _ctx_attn_kernel(nat_ref, swa_ref, sink_ref, on_ref, os_ref):
    m = nat_ref.shape[1]
    q = nat_ref[0, :, 0:NAT_W]
    k = nat_ref[0, :, NAT_W:2 * NAT_W]
    v = nat_ref[0, :, 2 * NAT_W:3 * NAT_W]
    group = SWA_Q_HEADS // SWA_KV_HEADS

    def swa_scores(g):
        k = swa_ref[0, :, pl.ds(SWA_QW + g * LANES, LANES)]
        qs = jnp.concatenate(
            [_head_stack(swa_ref[0, :, pl.ds((g * 2 + p) * LANES, LANES)], HEAD_DIM, 2) for p in range(2)], axis=0)
        return _dot_nt(qs, k)

    s = _dot_nt(_head_stack(q, HEAD_DIM, NAT_HEADS), k)
    s_swa = [swa_scores(g) for g in range(SWA_KV_HEADS)]
    p = jnp.exp2((s - jnp.max(s, axis=-1, keepdims=True)).astype(BF16))
    o = _dot(p, v) / jnp.sum(p, axis=-1, keepdims=True, dtype=F32)
    on_ref[0] = _head_unstack(o, HEAD_DIM, NAT_HEADS).astype(BF16)

    for g in range(SWA_KV_HEADS):
        v = swa_ref[0, :, pl.ds(SWA_QW + 2 * SWA_KW + g * LANES, LANES)]
        o = _softmax_pv_sink(s_swa[g], v, _sink_column(sink_ref, g * group, group, m))
        for pr in range(2):
            pair = _head_unstack(o[2 * pr * m:(2 * pr + 2) * m], HEAD_DIM, 2)
            os_ref[0, :, pl.ds((g * 2 + pr) * LANES, LANES)] = pair.astype(BF16)


def _ctx_attn(nat_c, swa_c, sink_tab):
    bsz, m, _ = nat_c.shape
    return pl.pallas_call(
        _ctx_attn_kernel,
        grid=(bsz,),
        in_specs=[
            pl.BlockSpec((1, m, NAT_PROJ_W), lambda b: (b, 0, 0)),
            pl.BlockSpec((1, m, SWA_PROJ_W), lambda b: (b, 0, 0)),
            pl.BlockSpec((SWA_Q_HEADS, LANES), lambda b: (0, 0)),
        ],
        out_specs=[
            pl.BlockSpec((1, m, NAT_W), lambda b: (b, 0, 0)),
            pl.BlockSpec((1, m, SWA_QW), lambda b: (b, 0, 0)),
        ],
        out_shape=[
            jax.ShapeDtypeStruct((bsz, m, NAT_W), BF16),
            jax.ShapeDtypeStruct((bsz, m, SWA_QW), BF16),
        ],
        compiler_params=_cparams("parallel"),
        name="ctx_attn",
    )(nat_c, swa_c, sink_tab)


OUT_PROJ_PART_ROWS = 256


def _out_proj_kernel(gla_ref, nat_ref, swa_ref, x_ref, mod_ref, gain_ref, w_ref, wr_ref,
                     x1_ref, h2_ref, aff_ref, afft_ref):
    tm = x_ref.shape[1]
    half = min(tm, OUT_PROJ_PART_ROWS)
    n_parts = tm // half

    def project(part):
        rows = pl.ds(part * half, half)
        return (_dot(gla_ref[0, rows, :], w_ref[0, 0:GLA_VW, :])
                + _dot(nat_ref[0, rows, :], w_ref[0, GLA_VW:GLA_VW + NAT_W, :])
                + _dot(swa_ref[0, rows, :], w_ref[0, GLA_VW + NAT_W:, :]))

    nxt = project(0)
    for part in range(n_parts):
        rows = pl.ds(part * half, half)
        mix = nxt
        if part + 1 < n_parts:
            nxt = project(part + 1)
        x1 = x_ref[0, rows, :] + mod_ref[0, 2:3, :] * _rms(mix, gain_ref[1:2, :])
        x1_ref[0, rows, :] = x1
        h2 = (_rms(x1, gain_ref[2:3, :]) * (1.0 + mod_ref[0, 4:5, :]) + mod_ref[0, 3:4, :]).astype(BF16)
        h2_ref[0, rows, :] = h2
        logits = _dot(h2, wr_ref[0])
        lane = lax.broadcasted_iota(jnp.int32, logits.shape, 1)
        logits = jnp.where(lane < N_EXPERTS, logits, NEG_BIG)
        e = jnp.exp(logits - jnp.max(logits, axis=-1, keepdims=True))
        aff = e / jnp.sum(e, axis=-1, keepdims=True)
        aff_ref[0, rows, :] = aff
        afft_ref[0, :, rows] = aff.T[0:N_EXPERTS, :]


def _out_proj(gla_o, nat_o, swa_o, x, mod, gains, w_out, wr, layer, tm):
    bsz, t, d = x.shape
    tok = lambda b, i: (b, i, 0)
    const = lambda b, i: (0, 0)
    return pl.pallas_call(
        _out_proj_kernel,
        grid=(bsz, t // tm),
        in_specs=[
            pl.BlockSpec((1, tm, GLA_VW), tok),
            pl.BlockSpec((1, tm, NAT_W), tok),
            pl.BlockSpec((1, tm, SWA_QW), tok),
            pl.BlockSpec((1, tm, d), tok),
            pl.BlockSpec((1, 6, d), lambda b, i: (b, 0, 0)),
            pl.BlockSpec((4, d), const),
            pl.BlockSpec((1, d, d), lambda b, i: (layer, 0, 0)),
            pl.BlockSpec((1, d, LANES), lambda b, i: (layer, 0, 0)),
        ],
        out_specs=[
            pl.BlockSpec((1, tm, d), tok),
            pl.BlockSpec((1, tm, d), tok),
            pl.BlockSpec((1, tm, LANES), tok),
            pl.BlockSpec((1, N_EXPERTS, tm), lambda b, i: (b, 0, i)),
        ],
        out_shape=[
            jax.ShapeDtypeStruct((bsz, t, d), F32),
            jax.ShapeDtypeStruct((bsz, t, d), BF16),
            jax.ShapeDtypeStruct((bsz, t, LANES), F32),
            jax.ShapeDtypeStruct((bsz, N_EXPERTS, t), F32),
        ],
        compiler_params=_cparams("parallel", "parallel"),
        name="out_proj_router",
    )(gla_o, nat_o, swa_o, x, mod, gains, w_out, wr)


def _excl_prefix(x):
    rows, n = x.shape
    nblk = n // LANES
    r = lax.broadcasted_iota(jnp.int32, (LANES, LANES), 0)
    c = lax.broadcasted_iota(jnp.int32, (LANES, LANES), 1)
    strict_upper = (r < c).astype(BF16)
    stacked = jnp.concatenate([x[:, j * LANES:(j + 1) * LANES] for j in range(nblk)], axis=0).astype(BF16)
    local = _dot(stacked, strict_upper)
    totals = jnp.sum(stacked.astype(F32), axis=-1, keepdims=True)
    out = []
    offs = [jnp.zeros((rows, 1), F32)]
    for j in range(nblk):
        out.append(local[j * rows:(j + 1) * rows] + offs[-1])
        offs.append(offs[-1] + totals[j * rows:(j + 1) * rows])
    return jnp.concatenate(out, axis=1), offs


def _topc_kernel(aff_ref, rank_ref, rankcol_ref, cnt_ref, *, cap):
    aff = aff_ref[...]

    def step(i, thr):
        cand = thr | (jnp.int32(1) << (30 - i))
        cnt = jnp.sum((aff >= pltpu.bitcast(cand, F32)).astype(F32), axis=-1, keepdims=True)
        return jnp.where(cnt >= cap, cand, thr)

    thr = lax.fori_loop(0, 31, step, jnp.zeros((aff.shape[0], 1), jnp.int32))
    thr = pltpu.bitcast(thr, F32)
    gt = aff > thr
    eq = aff == thr
    need = cap - jnp.sum(gt.astype(F32), axis=-1, keepdims=True)
    eq_before, _ = _excl_prefix(eq.astype(F32))
    sel = gt | (eq & (eq_before < need))
    sel_before, offs = _excl_prefix(sel.astype(F32))
    rank = jnp.where(sel, sel_before, -1.0)
    rank_ref[...] = rank
    lane = lax.broadcasted_iota(jnp.int32, (rank.shape[0], LANES), 1)
    cnt = jnp.zeros((rank.shape[0], LANES), F32)
    for j, off in enumerate(offs):
        cnt = jnp.where(lane == j, off, cnt)
    cnt_ref[...] = cnt.astype(jnp.int32)
    n = rank.shape[1]
    fill = jnp.full((LANES - N_EXPERTS, n), -1.0, F32)
    for b in range(rankcol_ref.shape[0]):
        rankcol_ref[b] = jnp.concatenate([rank[b * N_EXPERTS:(b + 1) * N_EXPERTS], fill], axis=0).T


def _topc(aff_t, cap):
    bsz, e, n = aff_t.shape
    rows = bsz * e
    rank, rankcol, cnt = pl.pallas_call(
        functools.partial(_topc_kernel, cap=cap),
        grid=(1,),
        in_specs=[pl.BlockSpec((rows, n), lambda i: (0, 0))],
        out_specs=[pl.BlockSpec((rows, n), lambda i: (0, 0)), pl.BlockSpec((bsz, n, LANES), lambda i: (0, 0, 0)),
                   pl.BlockSpec((rows, LANES), lambda i: (0, 0))],
        out_shape=[jax.ShapeDtypeStruct((rows, n), F32), jax.ShapeDtypeStruct((bsz, n, LANES), F32),
                   jax.ShapeDtypeStruct((rows, LANES), jnp.int32)],
        compiler_params=_cparams("arbitrary"),
        name="expert_topc",
    )(aff_t.reshape(rows, n))
    return rank.reshape(bsz, e, n), rankcol, cnt.reshape(bsz, e, LANES)


def _gather_kernel(h_ref, rank_ref, xs_ref, *, cap):
    n = h_ref.shape[1]
    slot = lax.broadcasted_iota(jnp.int32, (cap, n), 0).astype(F32)
    onehot = jnp.concatenate([(slot == rank_ref[0, e:e + 1, :]).astype(BF16) for e in range(N_EXPERTS)], axis=0)
    xs = _dot(onehot, h_ref[0]).astype(BF16)
    xs_ref[...] = xs.reshape(N_EXPERTS, cap, xs.shape[1])


def _gather(h, rank, cap):
    bsz, n, d = h.shape
    return pl.pallas_call(
        functools.partial(_gather_kernel, cap=cap),
        grid=(bsz,),
        in_specs=[
            pl.BlockSpec((1, n, d), lambda b: (b, 0, 0)),
            pl.BlockSpec((1, N_EXPERTS, n), lambda b: (b, 0, 0)),
        ],
        out_specs=pl.BlockSpec((N_EXPERTS, cap, d), lambda b: (0, b, 0)),
        out_shape=jax.ShapeDtypeStruct((N_EXPERTS, bsz * cap, d), BF16),
        compiler_params=_cparams("parallel"),
        name="moe_gather",
    )(h, rank)


GATHER_TILE = 256
GATHER_WIN = 5
GATHER_SLOTS = 128


def _gather_win_kernel(cnt_ref, h_ref, rank_ref, xs_ref, *, cap):
    b = pl.program_id(0)
    n = h_ref.shape[1]
    nt = n // GATHER_TILE
    win = GATHER_WIN * GATHER_TILE
    slot_w = lax.broadcasted_iota(jnp.int32, (GATHER_SLOTS, win), 0).astype(F32).astype(BF16)
    slot_f = lax.broadcasted_iota(jnp.int32, (cap, n), 0).astype(F32).astype(BF16)
    one = jnp.ones((), BF16)
    zero = jnp.zeros((), BF16)

    n_sb = cap // GATHER_SLOTS
    group = 4

    def per_group(g, carry):
        starts, fits = [], []
        for j in range(group):
            base = (g * group + j) * (nt + 1)
            c = [cnt_ref[b, base + kt] for kt in range(nt + 1)]
            for sb in range(n_sb):
                lo_slot = sb * GATHER_SLOTS
                first = sum((c[kt + 1] <= lo_slot).astype(jnp.int32) for kt in range(nt))
                last = sum((c[kt] < lo_slot + GATHER_SLOTS).astype(jnp.int32) for kt in range(nt))
                fits.append(last - first <= GATHER_WIN)
                starts.append(jnp.minimum(first, nt - GATHER_WIN))
        all_fit = functools.reduce(jnp.logical_and, fits)

        @pl.when(all_fit)
        def _():
            for j in range(group):
                e = g * group + j
                for sb in range(n_sb):
                    start = starts[j * n_sb + sb]
                    rk = jnp.concatenate([rank_ref[0, e, pl.ds(start + i, 1), :] for i in range(GATHER_WIN)], axis=1)
                    onehot = jnp.where(slot_w == (rk - sb * GATHER_SLOTS).astype(BF16), one, zero)
                    hw = h_ref[0, pl.ds(pl.multiple_of(start * GATHER_TILE, GATHER_TILE), win), :]
                    xs_ref[e, pl.ds(sb * GATHER_SLOTS, GATHER_SLOTS), :] = _dot(onehot, hw).astype(BF16)

        @pl.when(jnp.logical_not(all_fit))
        def _():
            for j in range(group):
                e = g * group + j
                rk = jnp.concatenate([rank_ref[0, e, kt:kt + 1, :] for kt in range(nt)], axis=1).astype(BF16)
                onehot = jnp.where(slot_f == rk, one, zero)
                xs_ref[e] = _dot(onehot, h_ref[0]).astype(BF16)
        return carry

    lax.fori_loop(0, N_EXPERTS // group, per_group, 0)


def _gather_win(h, rank, cnt, cap):
    bsz, n, d = h.shape
    nt = n // GATHER_TILE
    step = GATHER_TILE // LANES
    bounds = cnt[:, :, 0:nt * step + 1:step].reshape(bsz, N_EXPERTS * (nt + 1))
    return pl.pallas_call(
        functools.partial(_gather_win_kernel, cap=cap),
        grid_spec=pltpu.PrefetchScalarGridSpec(
            num_scalar_prefetch=1,
            grid=(bsz,),
            in_specs=[
                pl.BlockSpec((1, n, d), lambda b, c: (b, 0, 0)),
                pl.BlockSpec((1, N_EXPERTS, nt, GATHER_TILE), lambda b, c: (b, 0, 0, 0)),
            ],
            out_specs=pl.BlockSpec((N_EXPERTS, cap, d), lambda b, c: (0, b, 0)),
        ),
        out_shape=jax.ShapeDtypeStruct((N_EXPERTS, bsz * cap, d), BF16),
        compiler_params=_cparams("parallel"),
        name="moe_gather_win",
    )(bounds, h, rank.reshape(bsz, N_EXPERTS, nt, GATHER_TILE))


FFN_PARTS = 4


def _ffn_kernel(*refs, has_ctx):
    if has_ctx:
        xs_ref, xc_ref, wg_ref, wu_ref, wd_ref, y_ref, yc_ref, wg_bf, wu_bf, wd_bf = refs
    else:
        xs_ref, wg_ref, wu_ref, wd_ref, y_ref, wg_bf, wu_bf, wd_bf = refs
    s = pl.program_id(0)
    j = pl.program_id(1)
    n_exp = pl.num_programs(0) - 1
    piece = wg_ref.shape[2]

    @pl.when((s < n_exp) & (j < FFN_PARTS))
    def _():
        rows = pl.ds(pl.multiple_of(j * piece, piece), piece)
        wg_bf[s % 2, rows, :] = wg_ref[0, 0].astype(BF16)
        wu_bf[s % 2, rows, :] = wu_ref[0, 0].astype(BF16)
        wd_bf[s % 2, rows, :] = wd_ref[0, 0].astype(BF16)

    def swiglu(x_ref, o_ref):
        cur = (s + 1) % 2
        xs = x_ref[0]
        hid = (_silu(_dot(xs, wg_bf[cur])) * _dot(xs, wu_bf[cur])).astype(BF16)
        o_ref[0] = _dot(hid, wd_bf[cur]).astype(BF16)

    @pl.when((s > 0) & (j < FFN_PARTS))
    def _():
        swiglu(xs_ref, y_ref)

    if has_ctx:
        @pl.when((s > 0) & (j == FFN_PARTS))
        def _():
            swiglu(xc_ref, yc_ref)


def _ffn(xs, xc, w_gate, w_up, w_down, layer):
    _, e, d, ff = w_gate.shape
    rb = xs.shape[1] // FFN_PARTS
    has_ctx = xc is not None
    last = FFN_PARTS - 1

    def x_idx(s, j):
        return (jnp.maximum(s - 1, 0), jnp.where(s > 0, jnp.minimum(j, last), 0), 0)

    def w_idx(s, j):
        return (layer, jnp.minimum(s, e - 1), jnp.minimum(j, last), 0)

    xspec = pl.BlockSpec((1, rb, d), x_idx)
    cspecs, cargs, cshapes = [], [], []
    if has_ctx:
        cspecs = [pl.BlockSpec((1, xc.shape[1], d), lambda s, j: (jnp.maximum(s - 1, 0), 0, 0))]
        cargs = [xc]
        cshapes = [jax.ShapeDtypeStruct(xc.shape, BF16)]
    out = pl.pallas_call(
        functools.partial(_ffn_kernel, has_ctx=has_ctx),
        grid=(e + 1, FFN_PARTS + int(has_ctx)),
        in_specs=[xspec] + cspecs + [
            pl.BlockSpec((1, 1, d // FFN_PARTS, ff), w_idx),
            pl.BlockSpec((1, 1, d // FFN_PARTS, ff), w_idx),
            pl.BlockSpec((1, 1, ff // FFN_PARTS, d), w_idx),
        ],
        out_specs=[xspec] + cspecs,
        out_shape=[jax.ShapeDtypeStruct(xs.shape, BF16)] + cshapes,
        scratch_shapes=[pltpu.VMEM((2, d, ff), BF16), pltpu.VMEM((2, d, ff), BF16), pltpu.VMEM((2, ff, d), BF16)],
        compiler_params=_cparams("arbitrary", "arbitrary"),
        name="moe_ffn",
    )(xs, *cargs, w_gate, w_up, w_down)
    return out


def _combine_kernel(y_ref, rankcol_ref, aff_ref, x_ref, mod_ref, gain_ref, o_ref, *, cap):
    rb = x_ref.shape[1]
    slot = lax.broadcasted_iota(jnp.int32, (rb, cap), 1).astype(F32)
    rc = rankcol_ref[0]
    af = aff_ref[0]
    hot = jnp.concatenate([jnp.where(rc[:, e:e + 1] == slot, af[:, e:e + 1], 0.0).astype(BF16)
                           for e in range(N_EXPERTS)], axis=1)
    y = y_ref[...]
    acc = _dot(hot, y.reshape(N_EXPERTS * cap, y.shape[2]))
    o_ref[0] = x_ref[0] + mod_ref[0, 5:6, :] * _rms(acc, gain_ref[3:4, :])


def _combine(y, rankcol, aff, x, mod, gains, cap):
    bsz, n, d = x.shape
    rb = min(n, 512)
    tok = lambda b, i: (b, i, 0)
    return pl.pallas_call(
        functools.partial(_combine_kernel, cap=cap),
        grid=(bsz, n // rb),
        in_specs=[
            pl.BlockSpec((N_EXPERTS, cap, d), lambda b, i: (0, b, 0)),
            pl.BlockSpec((1, rb, LANES), tok),
            pl.BlockSpec((1, rb, LANES), tok),
            pl.BlockSpec((1, rb, d), tok),
            pl.BlockSpec((1, 6, d), lambda b, i: (b, 0, 0)),
            pl.BlockSpec((4, d), lambda b, i: (0, 0)),
        ],
        out_specs=pl.BlockSpec((1, rb, d), tok),
        out_shape=jax.ShapeDtypeStruct((bsz, n, d), F32),
        compiler_params=_cparams("parallel", "arbitrary"),
        name="moe_combine",
    )(y, rankcol, aff, x, mod, gains)


COMBINE_ROWS = 512
COMBINE_WIN = 128


def _combine_pair_kernel(cnt_ref, y_ref, rankcol_ref, aff_ref, x_ref, mod_ref, gain_ref, o_ref, extra_ref, *, cap):
    b = pl.program_id(0)
    i = pl.program_id(1)
    nrb = pl.num_programs(1)
    rb = x_ref.shape[1]
    slot_w = lax.broadcasted_iota(jnp.int32, (rb, COMBINE_WIN), 1).astype(F32)
    rc = rankcol_ref[0]
    af = aff_ref[0]
    los, unfit = [], []
    for e in range(N_EXPERTS):
        c0 = cnt_ref[b, e * (nrb + 1) + i]
        c1 = cnt_ref[b, e * (nrb + 1) + i + 1]
        lo = jnp.minimum((c0 // 16) * 16, cap - COMBINE_WIN)
        los.append(pl.multiple_of(lo, 16))
        unfit.append(c1 - lo > COMBINE_WIN)
    def operands(e0):
        hot, rows = [], []
        for e in (e0, e0 + 1):
            hot.append(jnp.where(rc[:, e:e + 1] - los[e].astype(F32) == slot_w, af[:, e:e + 1], 0.0).astype(BF16))
            rows.append(y_ref[e, pl.ds(los[e], COMBINE_WIN), :])
        return jnp.concatenate(hot, axis=1), jnp.concatenate(rows, axis=0)

    acc = jnp.zeros((rb, x_ref.shape[2]), F32)
    nxt = operands(0)
    for e0 in range(0, N_EXPERTS, 2):
        cur = nxt
        if e0 + 2 < N_EXPERTS:
            nxt = operands(e0 + 2)
        acc = acc + _dot(*cur)

    def finish(total):
        o_ref[0] = x_ref[0] + mod_ref[0, 5:6, :] * _rms(total, gain_ref[3:4, :])

    any_unfit = functools.reduce(jnp.logical_or, unfit)

    @pl.when(jnp.logical_not(any_unfit))
    def _():
        finish(acc)

    @pl.when(any_unfit)
    def _():
        extra_ref[...] = jnp.zeros_like(extra_ref)
        slot_f = lax.broadcasted_iota(jnp.int32, (rb, cap), 1)
        for e in range(N_EXPERTS):
            @pl.when(unfit[e])
            def _():
                outside = (slot_f < los[e]) | (slot_f >= los[e] + COMBINE_WIN)
                hot = jnp.where((rc[:, e:e + 1] == slot_f.astype(F32)) & outside, af[:, e:e + 1], 0.0).astype(BF16)
                extra_ref[...] += _dot(hot, y_ref[e])
        finish(acc + extra_ref[...])


def _combine_pair(y, rankcol, aff, cnt, x, mod, gains, cap):
    bsz, n, d = x.shape
    rb = COMBINE_ROWS
    nrb = n // rb
    step = rb // LANES
    bounds = cnt[:, :, 0:nrb * step + 1:step].reshape(bsz, N_EXPERTS * (nrb + 1))
    tok = lambda b, i, c: (b, i, 0)
    return pl.pallas_call(
        functools.partial(_combine_pair_kernel, cap=cap),
        grid_spec=pltpu.PrefetchScalarGridSpec(
            num_scalar_prefetch=1,
            grid=(bsz, nrb),
            in_specs=[
                pl.BlockSpec((N_EXPERTS, cap, d), lambda b, i, c: (0, b, 0)),
                pl.BlockSpec((1, rb, LANES), tok),
                pl.BlockSpec((1, rb, LANES), tok),
                pl.BlockSpec((1, rb, d), tok),
                pl.BlockSpec((1, 6, d), lambda b, i, c: (b, 0, 0)),
                pl.BlockSpec((4, d), lambda b, i, c: (0, 0)),
            ],
            out_specs=pl.BlockSpec((1, rb, d), tok),
            scratch_shapes=[pltpu.VMEM((rb, d), F32)],
        ),
        out_shape=jax.ShapeDtypeStruct((bsz, n, d), F32),
        compiler_params=_cparams("parallel", "arbitrary"),
        name="moe_combine_pair",
    )(bounds, y, rankcol, aff, x, mod, gains)


def _relayout_w_in(w):
    cut = GLA_KW + GLA_VW + 2 * GLA_RANK
    pad = jnp.zeros(w.shape[:2] + (LANES - 2 * GLA_RANK,), w.dtype)
    return jnp.concatenate([w[:, :, :cut], pad, w[:, :, cut:]], axis=2).astype(BF16)


def _rope_tables(t):
    half = HEAD_DIM // 4
    freqs = ROPE_BASE ** (-np.arange(half, dtype=np.float32) / half)
    pos = np.arange(t)
    ang_r = (pos // GRID_W).astype(np.float32)[:, None] * freqs
    ang_c = (pos % GRID_W).astype(np.float32)[:, None] * freqs
    cos = np.concatenate([np.cos(ang_r), np.cos(ang_r), np.cos(ang_c), np.cos(ang_c)], axis=1)
    sin = np.concatenate([-np.sin(ang_r), np.sin(ang_r), -np.sin(ang_c), np.sin(ang_c)], axis=1)
    reps = LANES // HEAD_DIM
    return jnp.asarray(np.tile(cos, (1, reps)), F32), jnp.asarray(np.tile(sin, (1, reps)), F32)


def _moe(h_list, afft_list, aff_list, x_list, mod_list, gains, w_gate, w_up, w_down, layer):
    caps = [CAPACITY_FACTOR * h.shape[1] // N_EXPERTS for h in h_list]
    ranks = [_topc(a, cap) for a, cap in zip(afft_list, caps)]
    long = [h.shape[1] >= GATHER_WIN * GATHER_TILE and h.shape[1] % COMBINE_ROWS == 0 and cap >= 2 * COMBINE_WIN
            for h, cap in zip(h_list, caps)]
    xs = [_gather_win(h, r[0], r[2], cap) if lg else _gather(h, r[0], cap)
          for h, r, cap, lg in zip(h_list, ranks, caps, long)]
    ys = _ffn(xs[0], xs[1] if len(xs) > 1 else None, w_gate, w_up, w_down, layer)
    return [_combine_pair(y, r[1], aff, r[2], x, mod, gains, cap) if lg else _combine(y, r[1], aff, x, mod, gains, cap)
            for y, r, aff, x, mod, cap, lg in zip(ys, ranks, aff_list, x_list, mod_list, caps, long)]


def kernel(x, c, ctx, c_ctx, w_mod, b_mod, norm_gains, w_in, w_out, gla_a_up, gla_a_bias, gla_norm,
           nat_rpb, swa_sink, w_router, w_gate, w_up, w_down):
    bsz, t, d = x.shape
    m = ctx.shape[1]
    depth = w_mod.shape[0]

    cc = jnp.concatenate([c, c_ctx[None], jnp.zeros((16 - bsz - 1, d), F32)], axis=0)
    mod_all = _modulation(cc, w_mod, b_mod)
    rope_tabs = _rope_tables(t)
    zero_state = jnp.zeros((bsz, GLA_VW, GLA_KW), F32)
    w_in_b = _relayout_w_in(w_in)
    w_out_b = w_out.astype(BF16)
    wr_b = jnp.pad(w_router, ((0, 0), (0, 0), (0, LANES - N_EXPERTS))).astype(BF16)

    xc = ctx
    for l in range(depth):
        update_ctx = l < depth - 1
        mod = mod_all[l, :bsz].reshape(bsz, 6, d)
        mod_c = jnp.broadcast_to(mod_all[l, bsz].reshape(1, 6, d), (bsz, 6, d))
        gains = norm_gains[l]
        aup = jnp.zeros((LANES, 2 * GLA_KW), F32)
        aup = aup.at[0:GLA_RANK, 0:GLA_KW].set(gla_a_up[l, 0]).at[GLA_RANK:2 * GLA_RANK, GLA_KW:].set(gla_a_up[l, 1])
        abias = gla_a_bias[l].reshape(1, 2 * GLA_KW)
        gla_gain = jnp.tile(gla_norm[l], GLA_HEADS).reshape(1, GLA_VW)
        sink_tab = jnp.broadcast_to(swa_sink[l][:, None] * LOG2E, (SWA_Q_HEADS, LANES))

        gla_p, nat_p, swa_p = _in_proj(x, mod, gains, w_in_b, l, aup, abias, rope_tabs, 1024)
        gla_c, nat_c, swa_c = _in_proj(xc, mod_c, gains, w_in_b, l, aup, abias, None, m)

        gla_co, s_f, s_b = _gla(gla_c, zero_state, zero_state, gla_gain)
        gla_o, _, _ = _gla(gla_p, s_f, s_b, gla_gain)
        nat_o = _nat(nat_p, nat_c, _nat_bias_table(nat_rpb[l]))
        swa_o = _swa(swa_p, swa_c, sink_tab)

        x1, h2, aff, aff_t = _out_proj(gla_o, nat_o, swa_o, x, mod, gains, w_out_b, wr_b, l, 1024)
        if update_ctx:
            nat_co, swa_co = _ctx_attn(nat_c, swa_c, sink_tab)
            xc1, hc2, aff_c, aff_ct = _out_proj(gla_co, nat_co, swa_co, xc, mod_c, gains, w_out_b, wr_b, l, m)
            x, xc = _moe([h2, hc2], [aff_t, aff_ct], [aff, aff_c], [x1, xc1], [mod, mod_c], gains,
                         w_gate, w_up, w_down, l)
        else:
            (x,) = _moe([h2], [aff_t], [aff], [x1], [mod], gains, w_gate, w_up, w_down, l)
    return x
```

```python
import functools

import jax
import jax.numpy as jnp
import numpy as np
from jax import lax
from jax.experimental import pallas as pl
from jax.experimental.pallas import tpu as pltpu

F32 = jnp.float32
BF16 = jnp.bfloat16

D_MODEL = 1024
GRID_W = 64
HEAD_DIM = 64
GLA_HEADS = 4
GLA_DK = 32
GLA_DV = 64
GLA_RANK = 16
GLA_TAU = 16.0
GLA_CHUNK = 64
NAT_HEADS = 4
NAT_WIN_ROWS = 8
NAT_WIN_COLS = 16
SWA_Q_HEADS = 8
SWA_KV_HEADS = 2
SWA_WINDOW = 128
SWA_BLOCK = 128
ROPE_BASE = 10000.0
N_EXPERTS = 16
EXPERT_FF = 1024
CAPACITY_FACTOR = 2
NORM_EPS = 1e-6

GLA_KW = GLA_HEADS * GLA_DK
GLA_VW = GLA_HEADS * GLA_DV
NAT_W = NAT_HEADS * HEAD_DIM
SWA_QW = SWA_Q_HEADS * HEAD_DIM
SWA_KW = SWA_KV_HEADS * HEAD_DIM
LANES = 128
NEG_BIG = -1e30
LOG2E = 1.4426950408889634

GLA_OUT_W = 2 * GLA_KW + 2 * GLA_VW + 2 * GLA_KW
NAT_PROJ_W = 3 * NAT_W
SWA_PROJ_W = SWA_QW + 4 * SWA_KW
W_GK = 0
W_GV = W_GK + GLA_KW
W_GA = W_GV + GLA_VW
W_NK = W_GA + LANES
W_SK = W_NK + 2 * NAT_W
W_SV = W_SK + SWA_KW
W_Q0 = W_SV + SWA_KW
W_GQ = W_Q0
W_GG = W_GQ + GLA_KW
W_NQ = W_GG + GLA_VW
W_SQ = W_NQ + NAT_W
IN_PROJ_W = W_SQ + SWA_QW

VMEM_LIMIT = 56 * 1024 * 1024


def _cparams(*sem):
    return pltpu.CompilerParams(dimension_semantics=sem, vmem_limit_bytes=VMEM_LIMIT)


def _dot(a, b):
    return jnp.dot(a, b, preferred_element_type=F32)


def _dot_nt(a, b):
    return lax.dot_general(a, b, (((1,), (1,)), ((), ())), preferred_element_type=F32)


def _dot_tn(a, b):
    return lax.dot_general(a, b, (((0,), (0,)), ((), ())), preferred_element_type=F32)


def _split2(a):
    hi = a.astype(BF16)
    lo = (a - hi.astype(F32)).astype(BF16)
    return hi, lo


def _split3(a):
    hi = a.astype(BF16)
    r = a - hi.astype(F32)
    mid = r.astype(BF16)
    lo = (r - mid.astype(F32)).astype(BF16)
    return hi, mid, lo


def _dot_f32(a, b):
    ah, al = _split2(a)
    bh, bl = _split2(b)
    return _dot(ah, bh) + _dot(al, bh) + _dot(ah, bl)


def _rms(x, gain):
    return x * lax.rsqrt(jnp.mean(x * x, axis=-1, keepdims=True) + NORM_EPS) * gain


def _silu(x):
    return x * jax.nn.sigmoid(x)


def _mod_kernel(c_ref, w_ref, b_ref, o_ref):
    a = _silu(c_ref[...])
    o_ref[0] = _dot_f32(a, w_ref[0]) + b_ref[0]


def _modulation(cc, w_mod, b_mod):
    depth, d, n = w_mod.shape
    r = cc.shape[0]
    tn = 1536
    return pl.pallas_call(
        _mod_kernel,
        grid=(depth, n // tn),
        in_specs=[
            pl.BlockSpec((r, d), lambda l, j: (0, 0)),
            pl.BlockSpec((1, d, tn), lambda l, j: (l, 0, j)),
            pl.BlockSpec((1, 1, tn), lambda l, j: (l, 0, j)),
        ],
        out_specs=pl.BlockSpec((1, r, tn), lambda l, j: (l, 0, j)),
        out_shape=jax.ShapeDtypeStruct((depth, r, n), F32),
        compiler_params=_cparams("parallel", "parallel"),
        name="adaln_mod",
    )(cc, w_mod, b_mod.reshape(depth, 1, n))


def _rope_rotate(x, first_half):
    up = pltpu.roll(x, LANES - 16, 1)
    down = pltpu.roll(x, 16, 1)
    return jnp.where(first_half, up, down)


IN_PROJ_PARTS = 1
CUMSUM_ROWS = 512


def _in_proj_kernel(*refs, rope):
    if rope:
        x_ref, mod_ref, gain_ref, w_ref, aup_ref, ab_ref, tri_ref, cos_ref, sin_ref, gla_ref, nat_ref, swa_ref = refs
    else:
        x_ref, mod_ref, gain_ref, w_ref, aup_ref, ab_ref, tri_ref, gla_ref, nat_ref, swa_ref = refs
    tm = x_ref.shape[1]
    pm = tm // IN_PROJ_PARTS
    kq = 2 * GLA_KW + 2 * GLA_VW
    scale = HEAD_DIM ** -0.5 * LOG2E
    nck = pm // GLA_CHUNK
    tri = tri_ref[...]
    for part in range(IN_PROJ_PARTS):
        rows = pl.ds(part * pm, pm)
        x = x_ref[0, rows, :]
        y = _rms(x, gain_ref[0:1, :])
        h = (y * (1.0 + mod_ref[0, 1:2, :]) + mod_ref[0, 0:1, :]).astype(BF16)

        pkv = _dot(h, w_ref[0, :, 0:W_Q0])
        pq = _dot(h, w_ref[0, :, W_Q0:])

        gla_ref[0, rows, 0:GLA_KW] = pkv[:, W_GK:W_GK + GLA_KW]
        gla_ref[0, rows, GLA_KW:2 * GLA_KW] = pq[:, W_GQ - W_Q0:W_GQ - W_Q0 + GLA_KW]
        gla_ref[0, rows, 2 * GLA_KW:2 * GLA_KW + GLA_VW] = pkv[:, W_GV:W_GV + GLA_VW]
        gla_ref[0, rows, 2 * GLA_KW + GLA_VW:kq] = pq[:, W_GG - W_Q0:W_GG - W_Q0 + GLA_VW]
        z = _dot_f32(pkv[:, W_GA:W_GA + LANES], aup_ref[...]) + ab_ref[...]
        log_a = (jnp.minimum(z, 0.0) - jnp.log1p(jnp.exp(-jnp.abs(z)))) * (1.0 / GLA_TAU)
        hi, lo = _split2(log_a)
        cr = tri.shape[0]
        cum = jnp.concatenate([_dot(tri, hi[i * cr:(i + 1) * cr]) + _dot(tri, lo[i * cr:(i + 1) * cr])
                               for i in range(pm // cr)], axis=0)
        gla_ref[0, rows, kq:kq + GLA_KW] = cum[:, 0:GLA_KW]
        cb = cum[:, GLA_KW:].reshape(nck, GLA_CHUNK, GLA_KW)
        lb = log_a[:, GLA_KW:].reshape(nck, GLA_CHUNK, GLA_KW)
        from_end = cb[:, GLA_CHUNK - 1:GLA_CHUNK, :] - cb + lb
        gla_ref[0, rows, kq + GLA_KW:kq + 2 * GLA_KW] = from_end.reshape(pm, GLA_KW)

        nat_ref[0, rows, 0:NAT_W] = (pq[:, W_NQ - W_Q0:W_NQ - W_Q0 + NAT_W] * scale).astype(BF16)
        nat_ref[0, rows, NAT_W:] = pkv[:, W_NK:W_NK + 2 * NAT_W].astype(BF16)

        if rope:
            cos = cos_ref[rows, :]
            sin = sin_ref[rows, :]
        lane = lax.broadcasted_iota(jnp.int32, (pm, LANES), 1)
        first_half = (lane % 32) < 16
        low_head = lane < HEAD_DIM

        def rotary(t):
            return t * cos + _rope_rotate(t, first_half) * sin if rope else t

        for j in range(SWA_QW // LANES):
            t = rotary(pq[:, W_SQ - W_Q0 + j * LANES:W_SQ - W_Q0 + (j + 1) * LANES]) * scale
            swa_ref[0, rows, j * LANES:(j + 1) * LANES] = t.astype(BF16)
        for j, t in enumerate((rotary(pkv[:, W_SK:W_SK + SWA_KW]), pkv[:, W_SV:W_SV + SWA_KW])):
            other = pltpu.roll(t, HEAD_DIM, 1)
            o = SWA_QW + 2 * j * SWA_KW
            swa_ref[0, rows, o:o + LANES] = jnp.where(low_head, t, other).astype(BF16)
            swa_ref[0, rows, o + LANES:o + 2 * LANES] = jnp.where(low_head, other, t).astype(BF16)


def _in_proj(x, mod, gains, w, layer, aup, abias, rope_tabs, tm):
    bsz, t, d = x.shape
    rope = rope_tabs is not None
    cum_rows = min(tm // IN_PROJ_PARTS, CUMSUM_ROWS)
    in_specs = [
        pl.BlockSpec((1, tm, d), lambda b, i: (b, i, 0)),
        pl.BlockSpec((1, 6, d), lambda b, i: (b, 0, 0)),
        pl.BlockSpec((4, d), lambda b, i: (0, 0)),
        pl.BlockSpec((1, d, IN_PROJ_W), lambda b, i: (layer, 0, 0)),
        pl.BlockSpec((LANES, 2 * GLA_KW), lambda b, i: (0, 0)),
        pl.BlockSpec((1, 2 * GLA_KW), lambda b, i: (0, 0)),
        pl.BlockSpec((cum_rows, cum_rows), lambda b, i: (0, 0)),
    ]
    r = np.arange(cum_rows)
    tri = (r[:, None] // GLA_CHUNK == r[None, :] // GLA_CHUNK) & (r[:, None] >= r[None, :])
    args = [x, mod, gains, w, aup, abias, jnp.asarray(tri, BF16)]
    if rope:
        in_specs += [pl.BlockSpec((tm, LANES), lambda b, i: (i, 0))] * 2
        args += list(rope_tabs)
    return pl.pallas_call(
        functools.partial(_in_proj_kernel, rope=rope),
        grid=(bsz, t // tm),
        in_specs=in_specs,
        out_specs=[
            pl.BlockSpec((1, tm, GLA_OUT_W), lambda b, i: (b, i, 0)),
            pl.BlockSpec((1, tm, NAT_PROJ_W), lambda b, i: (b, i, 0)),
            pl.BlockSpec((1, tm, SWA_PROJ_W), lambda b, i: (b, i, 0)),
        ],
        out_shape=[
            jax.ShapeDtypeStruct((bsz, t, GLA_OUT_W), F32),
            jax.ShapeDtypeStruct((bsz, t, NAT_PROJ_W), BF16),
            jax.ShapeDtypeStruct((bsz, t, SWA_PROJ_W), BF16),
        ],
        compiler_params=_cparams("parallel", "parallel"),
        name="in_proj_rope" if rope else "in_proj",
    )(*args)


GLA_SAMPLES_PER_STEP = 2


def _gla_kernel(p_ref, s0f_ref, s0b_ref, gain_ref, o_ref, sff_ref, sfb_ref, acc_ref, sf_ref, sb_ref):
    nb, t, _ = p_ref.shape
    c = GLA_CHUNK
    nc = t // c
    qi = lax.broadcasted_iota(jnp.int32, (c, GLA_HEADS * c), 0)
    kj = lax.broadcasted_iota(jnp.int32, (c, GLA_HEADS * c), 1) % c
    lower = qi >= kj
    upper = qi <= kj
    srow = lax.broadcasted_iota(jnp.int32, (GLA_VW, GLA_KW), 0) // GLA_DV
    scol = lax.broadcasted_iota(jnp.int32, (GLA_VW, GLA_KW), 1) // GLA_DK
    head_diag = srow == scol
    q_scale = GLA_DK ** -0.5

    sf_ref[...] = s0f_ref[...]
    sb_ref[...] = s0b_ref[...]
    acc_ref[...] = jnp.zeros_like(acc_ref)

    def state_free(n, base, end_row, cum_off):
        rows = pl.ds(base, c)
        k = p_ref[n, rows, 0:GLA_KW]
        q = p_ref[n, rows, GLA_KW:2 * GLA_KW] * q_scale
        vb = p_ref[n, rows, 2 * GLA_KW:2 * GLA_KW + GLA_VW].astype(BF16)
        cum = p_ref[n, rows, cum_off:cum_off + GLA_KW]
        mid = cum[c // 2:c // 2 + 1, :]
        total = cum[end_row:end_row + 1, :]
        qs = (q * jnp.exp(cum - mid)).astype(BF16)
        ks = (k * jnp.exp(mid - cum)).astype(BF16)
        s = _dot_nt(qs, _head_stack(ks, GLA_DK, GLA_HEADS))
        kd = (k * jnp.exp(total - cum)).astype(BF16)
        upd = _dot_tn(vb, kd)
        qe = (q * jnp.exp(cum)).astype(BF16)
        return rows, s, upd, qe, vb, jnp.exp(total)

    def with_state(n, keep, st_ref, rows, s, upd, qe, vb, decay):
        st = st_ref[n]
        inter = _dot_nt(qe, st.astype(BF16))
        s = jnp.where(keep, s, 0.0).astype(BF16)
        intra = _dot(s, _head_stack(vb, GLA_DV, GLA_HEADS))
        acc_ref[n, rows, :] += inter + intra
        st_ref[n] = st * decay + jnp.where(head_diag, upd, 0.0)

    def body(i, carry):
        fwd = [state_free(n, pl.multiple_of(i * c, c), c - 1, 2 * GLA_KW + 2 * GLA_VW) for n in range(nb)]
        bwd = [state_free(n, pl.multiple_of((nc - 1 - i) * c, c), 0, 3 * GLA_KW + 2 * GLA_VW) for n in range(nb)]
        for n in range(nb):
            with_state(n, lower, sf_ref, *fwd[n])
            with_state(n, upper, sb_ref, *bwd[n])
        return carry

    lax.fori_loop(0, nc, body, 0, unroll=4)
    sff_ref[...] = sf_ref[...]
    sfb_ref[...] = sb_ref[...]

    hr = lax.broadcasted_iota(jnp.int32, (GLA_VW, GLA_VW), 0) // GLA_DV
    hc = lax.broadcasted_iota(jnp.int32, (GLA_VW, GLA_VW), 1) // GLA_DV
    head_ones = (hr == hc).astype(BF16)
    blk = 256
    blocks = [(n, j) for n in range(nb) for j in range(t // blk)]

    def mean_square(n, j):
        o = acc_ref[n, pl.ds(j * blk, blk), :]
        hi, mid, lo = _split3(o * o)
        return o, (_dot(hi, head_ones) + _dot(mid, head_ones) + _dot(lo, head_ones)) * (1.0 / GLA_DV)

    nxt = mean_square(*blocks[0])
    for i, (n, j) in enumerate(blocks):
        rows = pl.ds(j * blk, blk)
        o, ms = nxt
        if i + 1 < len(blocks):
            nxt = mean_square(*blocks[i + 1])
        gate = p_ref[n, rows, 2 * GLA_KW + GLA_VW:2 * GLA_KW + 2 * GLA_VW]
        o_ref[n, rows, :] = (o * lax.rsqrt(ms + NORM_EPS) * gain_ref[...] * _silu(gate)).astype(BF16)


def _gla(p, s0f, s0b, gain):
    bsz, t, w = p.shape
    nb = GLA_SAMPLES_PER_STEP if bsz % GLA_SAMPLES_PER_STEP == 0 else 1
    st_spec = pl.BlockSpec((nb, GLA_VW, GLA_KW), lambda b: (b, 0, 0))
    st_shape = jax.ShapeDtypeStruct((bsz, GLA_VW, GLA_KW), F32)
    return pl.pallas_call(
        _gla_kernel,
        grid=(bsz // nb,),
        in_specs=[
            pl.BlockSpec((nb, t, w), lambda b: (b, 0, 0)),
            st_spec, st_spec,
            pl.BlockSpec((1, GLA_VW), lambda b: (0, 0)),
        ],
        out_specs=[pl.BlockSpec((nb, t, GLA_VW), lambda b: (b, 0, 0)), st_spec, st_spec],
        out_shape=[jax.ShapeDtypeStruct((bsz, t, GLA_VW), BF16), st_shape, st_shape],
        scratch_shapes=[
            pltpu.VMEM((nb, t, GLA_VW), F32),
            pltpu.VMEM((nb, GLA_VW, GLA_KW), F32),
            pltpu.VMEM((nb, GLA_VW, GLA_KW), F32),
        ],
        compiler_params=_cparams("parallel"),
        name="gla_bidir",
    )(p, s0f, s0b, gain)


def _head_stack(q, width, n):
    lane = lax.broadcasted_iota(jnp.int32, (1, q.shape[1]), 1) // width
    return jnp.concatenate([jnp.where(lane == h, q, jnp.zeros_like(q)) for h in range(n)], axis=0)


def _head_unstack(o, width, n):
    rows = o.shape[0] // n
    lane = lax.broadcasted_iota(jnp.int32, (1, o.shape[1]), 1) // width
    out = o[0:rows]
    for h in range(1, n):
        out = jnp.where(lane == h, o[h * rows:(h + 1) * rows], out)
    return out


NAT_ROWS_PER_STEP = 16


def _nat_kernel(lat_ref, ctx_ref, bias_ref, o_ref):
    n_rows = lat_ref.shape[1] // GRID_W
    kc = ctx_ref[0, :, NAT_W:2 * NAT_W]
    vc = ctx_ref[0, :, 2 * NAT_W:3 * NAT_W]
    n_loc = NAT_WIN_ROWS * GRID_W

    def scores(i):
        r = pl.program_id(1) * NAT_ROWS_PER_STEP + i
        r0 = jnp.clip(r - NAT_WIN_ROWS // 2, 0, n_rows - NAT_WIN_ROWS)
        q = lat_ref[0, pl.ds(pl.multiple_of(r * GRID_W, GRID_W), GRID_W), 0:NAT_W]
        krows = pl.ds(pl.multiple_of(r0 * GRID_W, GRID_W), n_loc)
        qs = _head_stack(q, HEAD_DIM, NAT_HEADS)
        first = r0 - r + NAT_WIN_ROWS - 1
        bias = jnp.concatenate([bias_ref[first + 2 * j] for j in range(NAT_WIN_ROWS // 2)], axis=1)
        s = jnp.concatenate([_dot_nt(qs, lat_ref[0, krows, NAT_W:2 * NAT_W]) + bias, _dot_nt(qs, kc)], axis=1)
        return s, krows

    def finish(i, s, krows):
        p = jnp.exp2((s - jnp.max(s, axis=-1, keepdims=True)).astype(BF16))
        den = jnp.sum(p, axis=-1, keepdims=True, dtype=F32)
        o = (_dot(p[:, 0:n_loc], lat_ref[0, krows, 2 * NAT_W:3 * NAT_W]) + _dot(p[:, n_loc:], vc)) / den
        o_ref[0, i * GRID_W:(i + 1) * GRID_W, :] = _head_unstack(o, HEAD_DIM, NAT_HEADS).astype(BF16)

    nxt = scores(0)
    for i in range(NAT_ROWS_PER_STEP):
        cur = nxt
        if i + 1 < NAT_ROWS_PER_STEP:
            nxt = scores(i + 1)
        finish(i, *cur)


def _nat(lat, ctx, bias):
    bsz, t, w = lat.shape
    m = ctx.shape[1]
    rows = NAT_ROWS_PER_STEP * GRID_W
    return pl.pallas_call(
        _nat_kernel,
        grid=(bsz, t // rows),
        in_specs=[
            pl.BlockSpec((1, t, w), lambda b, r: (b, 0, 0)),
            pl.BlockSpec((1, m, w), lambda b, r: (b, 0, 0)),
            pl.BlockSpec(bias.shape, lambda b, r: (0, 0, 0)),
        ],
        out_specs=pl.BlockSpec((1, rows, NAT_W), lambda b, r: (b, r, 0)),
        out_shape=jax.ShapeDtypeStruct((bsz, t, NAT_W), BF16),
        compiler_params=_cparams("parallel", "arbitrary"),
        name="nat_attn",
    )(lat, ctx, bias)


def _nat_bias_table(rpb):
    cols = np.arange(GRID_W)
    col_start = np.clip(cols - NAT_WIN_COLS // 2, 0, GRID_W - NAT_WIN_COLS)
    kc = np.arange(GRID_W)
    inside = (kc[None, :] >= col_start[:, None]) & (kc[None, :] < col_start[:, None] + NAT_WIN_COLS)
    rel = kc[None, :] - cols[:, None] + NAT_WIN_COLS - 1
    col_sel = (rel[:, :, None] == np.arange(2 * NAT_WIN_COLS - 1)) & inside[:, :, None]
    tab = jnp.einsum("hab,ckb->ahck", rpb.astype(F32), jnp.asarray(col_sel, F32), precision=lax.Precision.HIGHEST)
    tab = jnp.where(jnp.asarray(inside)[None, None], tab * LOG2E, NEG_BIG)
    tab = tab.reshape(2 * NAT_WIN_ROWS - 1, NAT_HEADS * GRID_W, GRID_W)
    return jnp.concatenate([tab[:-1], tab[1:]], axis=2)


def _sink_column(sink_ref, first_head, n_heads, rows):
    return jnp.concatenate(
        [jnp.broadcast_to(sink_ref[first_head + j:first_head + j + 1, 0:1], (rows, 1)) for j in range(n_heads)], axis=0)


def _softmax_pv_sink(s, v, sink):
    m = jnp.maximum(jnp.max(s, axis=-1, keepdims=True), sink)
    p = jnp.exp2((s - m).astype(BF16))
    o = _dot(p, jnp.concatenate([v, jnp.ones_like(v)], axis=1))
    w = v.shape[1]
    return o[:, 0:w] / (o[:, w:] + jnp.exp2(sink - m))


SWA_BLOCKS_PER_STEP = 4


def _swa_kernel(lat_ref, ctx_ref, sink_ref, o_ref):
    t = lat_ref.shape[1]
    m = ctx_ref.shape[1]
    blk = SWA_BLOCK
    n_loc = 3 * blk
    qi = lax.broadcasted_iota(jnp.int32, (blk, n_loc), 0)
    kj = lax.broadcasted_iota(jnp.int32, (blk, n_loc), 1)
    group = SWA_Q_HEADS // SWA_KV_HEADS
    chains = [(a, g) for a in range(SWA_BLOCKS_PER_STEP) for g in range(SWA_KV_HEADS)]

    def scores(a, g):
        n = pl.program_id(1) * SWA_BLOCKS_PER_STEP + a
        base = pl.multiple_of(jnp.clip((n - 1) * blk, 0, t - n_loc), blk)
        qrows = pl.ds(pl.multiple_of(n * blk, blk), blk)
        krows = pl.ds(base, n_loc)
        mask_add = jnp.where(jnp.abs(base + kj - n * blk - qi) <= SWA_WINDOW, 0.0, NEG_BIG)[None]
        kcol = pl.ds(SWA_QW + g * LANES, LANES)
        k = jnp.concatenate([lat_ref[0, krows, kcol], ctx_ref[0, :, kcol]], axis=0)
        qs = jnp.concatenate(
            [_head_stack(lat_ref[0, qrows, pl.ds((g * 2 + p) * LANES, LANES)], HEAD_DIM, 2) for p in range(2)], axis=0)
        s = _dot_nt(qs, k).reshape(group, blk, n_loc + m)
        s = jnp.concatenate([s[:, :, 0:n_loc] + mask_add, s[:, :, n_loc:]], axis=2).reshape(group * blk, n_loc + m)
        return s, krows

    def finish(a, g, s, krows):
        vcol = pl.ds(SWA_QW + 2 * SWA_KW + g * LANES, LANES)
        v = jnp.concatenate([lat_ref[0, krows, vcol], ctx_ref[0, :, vcol]], axis=0)
        o = _softmax_pv_sink(s, v, _sink_column(sink_ref, g * group, group, blk))
        for p in range(2):
            pair = _head_unstack(o[2 * p * blk:(2 * p + 2) * blk], HEAD_DIM, 2)
            o_ref[0, a * blk:(a + 1) * blk, pl.ds((g * 2 + p) * LANES, LANES)] = pair.astype(BF16)

    nxt = scores(*chains[0])
    for i, (a, g) in enumerate(chains):
        cur = nxt
        if i + 1 < len(chains):
            nxt = scores(*chains[i + 1])
        finish(a, g, *cur)


def _swa(lat, ctx, sink_tab):
    bsz, t, w = lat.shape
    m = ctx.shape[1]
    rows = SWA_BLOCKS_PER_STEP * SWA_BLOCK
    return pl.pallas_call(
        _swa_kernel,
        grid=(bsz, t // rows),
        in_specs=[
            pl.BlockSpec((1, t, w), lambda b, n: (b, 0, 0)),
            pl.BlockSpec((1, m, w), lambda b, n: (b, 0, 0)),
            pl.BlockSpec((SWA_Q_HEADS, LANES), lambda b, n: (0, 0)),
        ],
        out_specs=pl.BlockSpec((1, rows, SWA_QW), lambda b, n: (b, n, 0)),
        out_shape=jax.ShapeDtypeStruct((bsz, t, SWA_QW), BF16),
        compiler_params=_cparams("parallel", "arbitrary"),
        name="swa_attn",
    )(lat, ctx, sink_tab)


def _ctx_attn_kernel(nat_ref, swa_ref, sink_ref, on_ref, os_ref):
    m = nat_ref.shape[1]
    q = nat_ref[0, :, 0:NAT_W]
    k = nat_ref[0, :, NAT_W:2 * NAT_W]
    v = nat_ref[0, :, 2 * NAT_W:3 * NAT_W]
    group = SWA_Q_HEADS // SWA_KV_HEADS

    def swa_scores(g):
        k = swa_ref[0, :, pl.ds(SWA_QW + g * LANES, LANES)]
        qs = jnp.concatenate(
            [_head_stack(swa_ref[0, :, pl.ds((g * 2 + p) * LANES, LANES)], HEAD_DIM, 2) for p in range(2)], axis=0)
        return _dot_nt(qs, k)

    s = _dot_nt(_head_stack(q, HEAD_DIM, NAT_HEADS), k)
    s_swa = [swa_scores(g) for g in range(SWA_KV_HEADS)]
    p = jnp.exp2((s - jnp.max(s, axis=-1, keepdims=True)).astype(BF16))
    o = _dot(p, v) / jnp.sum(p, axis=-1, keepdims=True, dtype=F32)
    on_ref[0] = _head_unstack(o, HEAD_DIM, NAT_HEADS).astype(BF16)

    for g in range(SWA_KV_HEADS):
        v = swa_ref[0, :, pl.ds(SWA_QW + 2 * SWA_KW + g * LANES, LANES)]
        o = _softmax_pv_sink(s_swa[g], v, _sink_column(sink_ref, g * group, group, m))
        for pr in range(2):
            pair = _head_unstack(o[2 * pr * m:(2 * pr + 2) * m], HEAD_DIM, 2)
            os_ref[0, :, pl.ds((g * 2 + pr) * LANES, LANES)] = pair.astype(BF16)


def _ctx_attn(nat_c, swa_c, sink_tab):
    bsz, m, _ = nat_c.shape
    return pl.pallas_call(
        _ctx_attn_kernel,
        grid=(bsz,),
        in_specs=[
            pl.BlockSpec((1, m, NAT_PROJ_W), lambda b: (b, 0, 0)),
            pl.BlockSpec((1, m, SWA_PROJ_W), lambda b: (b, 0, 0)),
            pl.BlockSpec((SWA_Q_HEADS, LANES), lambda b: (0, 0)),
        ],
        out_specs=[
            pl.BlockSpec((1, m, NAT_W), lambda b: (b, 0, 0)),
            pl.BlockSpec((1, m, SWA_QW), lambda b: (b, 0, 0)),
        ],
        out_shape=[
            jax.ShapeDtypeStruct((bsz, m, NAT_W), BF16),
            jax.ShapeDtypeStruct((bsz, m, SWA_QW), BF16),
        ],
        compiler_params=_cparams("parallel"),
        name="ctx_attn",
    )(nat_c, swa_c, sink_tab)


OUT_PROJ_PART_ROWS = 256


def _out_proj_kernel(gla_ref, nat_ref, swa_ref, x_ref, mod_ref, gain_ref, w_ref, wr_ref,
                     x1_ref, h2_ref, aff_ref, afft_ref):
    tm = x_ref.shape[1]
    half = min(tm, OUT_PROJ_PART_ROWS)
    n_parts = tm // half

    def project(part):
        rows = pl.ds(part * half, half)
        return (_dot(gla_ref[0, rows, :], w_ref[0, 0:GLA_VW, :])
                + _dot(nat_ref[0, rows, :], w_ref[0, GLA_VW:GLA_VW + NAT_W, :])
                + _dot(swa_ref[0, rows, :], w_ref[0, GLA_VW + NAT_W:, :]))

    nxt = project(0)
    for part in range(n_parts):
        rows = pl.ds(part * half, half)
        mix = nxt
        if part + 1 < n_parts:
            nxt = project(part + 1)
        x1 = x_ref[0, rows, :] + mod_ref[0, 2:3, :] * _rms(mix, gain_ref[1:2, :])
        x1_ref[0, rows, :] = x1
        h2 = (_rms(x1, gain_ref[2:3, :]) * (1.0 + mod_ref[0, 4:5, :]) + mod_ref[0, 3:4, :]).astype(BF16)
        h2_ref[0, rows, :] = h2
        logits = _dot(h2, wr_ref[0])
        lane = lax.broadcasted_iota(jnp.int32, logits.shape, 1)
        logits = jnp.where(lane < N_EXPERTS, logits, NEG_BIG)
        e = jnp.exp(logits - jnp.max(logits, axis=-1, keepdims=True))
        aff = e / jnp.sum(e, axis=-1, keepdims=True)
        aff_ref[0, rows, :] = aff
        afft_ref[0, :, rows] = aff.T[0:N_EXPERTS, :]


def _out_proj(gla_o, nat_o, swa_o, x, mod, gains, w_out, wr, layer, tm):
    bsz, t, d = x.shape
    tok = lambda b, i: (b, i, 0)
    const = lambda b, i: (0, 0)
    return pl.pallas_call(
        _out_proj_kernel,
        grid=(bsz, t // tm),
        in_specs=[
            pl.BlockSpec((1, tm, GLA_VW), tok),
            pl.BlockSpec((1, tm, NAT_W), tok),
            pl.BlockSpec((1, tm, SWA_QW), tok),
            pl.BlockSpec((1, tm, d), tok),
            pl.BlockSpec((1, 6, d), lambda b, i: (b, 0, 0)),
            pl.BlockSpec((4, d), const),
            pl.BlockSpec((1, d, d), lambda b, i: (layer, 0, 0)),
            pl.BlockSpec((1, d, LANES), lambda b, i: (layer, 0, 0)),
        ],
        out_specs=[
            pl.BlockSpec((1, tm, d), tok),
            pl.BlockSpec((1, tm, d), tok),
            pl.BlockSpec((1, tm, LANES), tok),
            pl.BlockSpec((1, N_EXPERTS, tm), lambda b, i: (b, 0, i)),
        ],
        out_shape=[
            jax.ShapeDtypeStruct((bsz, t, d), F32),
            jax.ShapeDtypeStruct((bsz, t, d), BF16),
            jax.ShapeDtypeStruct((bsz, t, LANES), F32),
            jax.ShapeDtypeStruct((bsz, N_EXPERTS, t), F32),
        ],
        compiler_params=_cparams("parallel", "parallel"),
        name="out_proj_router",
    )(gla_o, nat_o, swa_o, x, mod, gains, w_out, wr)


def _excl_prefix(x):
    rows, n = x.shape
    nblk = n // LANES
    r = lax.broadcasted_iota(jnp.int32, (LANES, LANES), 0)
    c = lax.broadcasted_iota(jnp.int32, (LANES, LANES), 1)
    strict_upper = (r < c).astype(BF16)
    stacked = jnp.concatenate([x[:, j * LANES:(j + 1) * LANES] for j in range(nblk)], axis=0).astype(BF16)
    local = _dot(stacked, strict_upper)
    totals = jnp.sum(stacked.astype(F32), axis=-1, keepdims=True)
    out = []
    offs = [jnp.zeros((rows, 1), F32)]
    for j in range(nblk):
        out.append(local[j * rows:(j + 1) * rows] + offs[-1])
        offs.append(offs[-1] + totals[j * rows:(j + 1) * rows])
    return jnp.concatenate(out, axis=1), offs


def _topc_kernel(aff_ref, rank_ref, rankcol_ref, cnt_ref, *, cap):
    aff = aff_ref[...]

    def step(i, thr):
        cand = thr | (jnp.int32(1) << (30 - i))
        cnt = jnp.sum((aff >= pltpu.bitcast(cand, F32)).astype(F32), axis=-1, keepdims=True)
        return jnp.where(cnt >= cap, cand, thr)

    thr = lax.fori_loop(0, 31, step, jnp.zeros((aff.shape[0], 1), jnp.int32))
    thr = pltpu.bitcast(thr, F32)
    gt = aff > thr
    eq = aff == thr
    need = cap - jnp.sum(gt.astype(F32), axis=-1, keepdims=True)
    eq_before, _ = _excl_prefix(eq.astype(F32))
    sel = gt | (eq & (eq_before < need))
    sel_before, offs = _excl_prefix(sel.astype(F32))
    rank = jnp.where(sel, sel_before, -1.0)
    rank_ref[...] = rank
    lane = lax.broadcasted_iota(jnp.int32, (rank.shape[0], LANES), 1)
    cnt = jnp.zeros((rank.shape[0], LANES), F32)
    for j, off in enumerate(offs):
        cnt = jnp.where(lane == j, off, cnt)
    cnt_ref[...] = cnt.astype(jnp.int32)
    n = rank.shape[1]
    fill = jnp.full((LANES - N_EXPERTS, n), -1.0, F32)
    for b in range(rankcol_ref.shape[0]):
        rankcol_ref[b] = jnp.concatenate([rank[b * N_EXPERTS:(b + 1) * N_EXPERTS], fill], axis=0).T


def _topc(aff_t, cap):
    bsz, e, n = aff_t.shape
    rows = bsz * e
    rank, rankcol, cnt = pl.pallas_call(
        functools.partial(_topc_kernel, cap=cap),
        grid=(1,),
        in_specs=[pl.BlockSpec((rows, n), lambda i: (0, 0))],
        out_specs=[pl.BlockSpec((rows, n), lambda i: (0, 0)), pl.BlockSpec((bsz, n, LANES), lambda i: (0, 0, 0)),
                   pl.BlockSpec((rows, LANES), lambda i: (0, 0))],
        out_shape=[jax.ShapeDtypeStruct((rows, n), F32), jax.ShapeDtypeStruct((bsz, n, LANES), F32),
                   jax.ShapeDtypeStruct((rows, LANES), jnp.int32)],
        compiler_params=_cparams("arbitrary"),
        name="expert_topc",
    )(aff_t.reshape(rows, n))
    return rank.reshape(bsz, e, n), rankcol, cnt.reshape(bsz, e, LANES)


def _gather_kernel(h_ref, rank_ref, xs_ref, *, cap):
    n = h_ref.shape[1]
    slot = lax.broadcasted_iota(jnp.int32, (cap, n), 0).astype(F32)
    onehot = jnp.concatenate([(slot == rank_ref[0, e:e + 1, :]).astype(BF16) for e in range(N_EXPERTS)], axis=0)
    xs = _dot(onehot, h_ref[0]).astype(BF16)
    xs_ref[...] = xs.reshape(N_EXPERTS, cap, xs.shape[1])


def _gather(h, rank, cap):
    bsz, n, d = h.shape
    return pl.pallas_call(
        functools.partial(_gather_kernel, cap=cap),
        grid=(bsz,),
        in_specs=[
            pl.BlockSpec((1, n, d), lambda b: (b, 0, 0)),
            pl.BlockSpec((1, N_EXPERTS, n), lambda b: (b, 0, 0)),
        ],
        out_specs=pl.BlockSpec((N_EXPERTS, cap, d), lambda b: (0, b, 0)),
        out_shape=jax.ShapeDtypeStruct((N_EXPERTS, bsz * cap, d), BF16),
        compiler_params=_cparams("parallel"),
        name="moe_gather",
    )(h, rank)


GATHER_TILE = 256
GATHER_WIN = 5
GATHER_SLOTS = 128


def _gather_win_kernel(cnt_ref, h_ref, rank_ref, xs_ref, *, cap):
    b = pl.program_id(0)
    n = h_ref.shape[1]
    nt = n // GATHER_TILE
    win = GATHER_WIN * GATHER_TILE
    slot_w = lax.broadcasted_iota(jnp.int32, (GATHER_SLOTS, win), 0).astype(F32).astype(BF16)
    slot_f = lax.broadcasted_iota(jnp.int32, (cap, n), 0).astype(F32).astype(BF16)
    one = jnp.ones((), BF16)
    zero = jnp.zeros((), BF16)

    n_sb = cap // GATHER_SLOTS
    group = 4

    def per_group(g, carry):
        starts, fits = [], []
        for j in range(group):
            base = (g * group + j) * (nt + 1)
            c = [cnt_ref[b, base + kt] for kt in range(nt + 1)]
            for sb in range(n_sb):
                lo_slot = sb * GATHER_SLOTS
                first = sum((c[kt + 1] <= lo_slot).astype(jnp.int32) for kt in range(nt))
                last = sum((c[kt] < lo_slot + GATHER_SLOTS).astype(jnp.int32) for kt in range(nt))
                fits.append(last - first <= GATHER_WIN)
                starts.append(jnp.minimum(first, nt - GATHER_WIN))
        all_fit = functools.reduce(jnp.logical_and, fits)

        @pl.when(all_fit)
        def _():
            for j in range(group):
                e = g * group + j
                for sb in range(n_sb):
                    start = starts[j * n_sb + sb]
                    rk = jnp.concatenate([rank_ref[0, e, pl.ds(start + i, 1), :] for i in range(GATHER_WIN)], axis=1)
                    onehot = jnp.where(slot_w == (rk - sb * GATHER_SLOTS).astype(BF16), one, zero)
                    hw = h_ref[0, pl.ds(pl.multiple_of(start * GATHER_TILE, GATHER_TILE), win), :]
                    xs_ref[e, pl.ds(sb * GATHER_SLOTS, GATHER_SLOTS), :] = _dot(onehot, hw).astype(BF16)

        @pl.when(jnp.logical_not(all_fit))
        def _():
            for j in range(group):
                e = g * group + j
                rk = jnp.concatenate([rank_ref[0, e, kt:kt + 1, :] for kt in range(nt)], axis=1).astype(BF16)
                onehot = jnp.where(slot_f == rk, one, zero)
                xs_ref[e] = _dot(onehot, h_ref[0]).astype(BF16)
        return carry

    lax.fori_loop(0, N_EXPERTS // group, per_group, 0)


def _gather_win(h, rank, cnt, cap):
    bsz, n, d = h.shape
    nt = n // GATHER_TILE
    step = GATHER_TILE // LANES
    bounds = cnt[:, :, 0:nt * step + 1:step].reshape(bsz, N_EXPERTS * (nt + 1))
    return pl.pallas_call(
        functools.partial(_gather_win_kernel, cap=cap),
        grid_spec=pltpu.PrefetchScalarGridSpec(
            num_scalar_prefetch=1,
            grid=(bsz,),
            in_specs=[
                pl.BlockSpec((1, n, d), lambda b, c: (b, 0, 0)),
                pl.BlockSpec((1, N_EXPERTS, nt, GATHER_TILE), lambda b, c: (b, 0, 0, 0)),
            ],
            out_specs=pl.BlockSpec((N_EXPERTS, cap, d), lambda b, c: (0, b, 0)),
        ),
        out_shape=jax.ShapeDtypeStruct((N_EXPERTS, bsz * cap, d), BF16),
        compiler_params=_cparams("parallel"),
        name="moe_gather_win",
    )(bounds, h, rank.reshape(bsz, N_EXPERTS, nt, GATHER_TILE))


FFN_PARTS = 4


def _ffn_kernel(*refs, has_ctx):
    if has_ctx:
        xs_ref, xc_ref, wg_ref, wu_ref, wd_ref, y_ref, yc_ref, wg_bf, wu_bf, wd_bf = refs
    else:
        xs_ref, wg_ref, wu_ref, wd_ref, y_ref, wg_bf, wu_bf, wd_bf = refs
    s = pl.program_id(0)
    j = pl.program_id(1)
    n_exp = pl.num_programs(0) - 1
    piece = wg_ref.shape[2]

    @pl.when((s < n_exp) & (j < FFN_PARTS))
    def _():
        rows = pl.ds(pl.multiple_of(j * piece, piece), piece)
        wg_bf[s % 2, rows, :] = wg_ref[0, 0].astype(BF16)
        wu_bf[s % 2, rows, :] = wu_ref[0, 0].astype(BF16)
        wd_bf[s % 2, rows, :] = wd_ref[0, 0].astype(BF16)

    def swiglu(x_ref, o_ref):
        cur = (s + 1) % 2
        xs = x_ref[0]
        hid = (_silu(_dot(xs, wg_bf[cur])) * _dot(xs, wu_bf[cur])).astype(BF16)
        o_ref[0] = _dot(hid, wd_bf[cur]).astype(BF16)

    @pl.when((s > 0) & (j < FFN_PARTS))
    def _():
        swiglu(xs_ref, y_ref)

    if has_ctx:
        @pl.when((s > 0) & (j == FFN_PARTS))
        def _():
            swiglu(xc_ref, yc_ref)


def _ffn(xs, xc, w_gate, w_up, w_down, layer):
    _, e, d, ff = w_gate.shape
    rb = xs.shape[1] // FFN_PARTS
    has_ctx = xc is not None
    last = FFN_PARTS - 1

    def x_idx(s, j):
        return (jnp.maximum(s - 1, 0), jnp.where(s > 0, jnp.minimum(j, last), 0), 0)

    def w_idx(s, j):
        return (layer, jnp.minimum(s, e - 1), jnp.minimum(j, last), 0)

    xspec = pl.BlockSpec((1, rb, d), x_idx)
    cspecs, cargs, cshapes = [], [], []
    if has_ctx:
        cspecs = [pl.BlockSpec((1, xc.shape[1], d), lambda s, j: (jnp.maximum(s - 1, 0), 0, 0))]
        cargs = [xc]
        cshapes = [jax.ShapeDtypeStruct(xc.shape, BF16)]
    out = pl.pallas_call(
        functools.partial(_ffn_kernel, has_ctx=has_ctx),
        grid=(e + 1, FFN_PARTS + int(has_ctx)),
        in_specs=[xspec] + cspecs + [
            pl.BlockSpec((1, 1, d // FFN_PARTS, ff), w_idx),
            pl.BlockSpec((1, 1, d // FFN_PARTS, ff), w_idx),
            pl.BlockSpec((1, 1, ff // FFN_PARTS, d), w_idx),
        ],
        out_specs=[xspec] + cspecs,
        out_shape=[jax.ShapeDtypeStruct(xs.shape, BF16)] + cshapes,
        scratch_shapes=[pltpu.VMEM((2, d, ff), BF16), pltpu.VMEM((2, d, ff), BF16), pltpu.VMEM((2, ff, d), BF16)],
        compiler_params=_cparams("arbitrary", "arbitrary"),
        name="moe_ffn",
    )(xs, *cargs, w_gate, w_up, w_down)
    return out


def _combine_kernel(y_ref, rankcol_ref, aff_ref, x_ref, mod_ref, gain_ref, o_ref, *, cap):
    rb = x_ref.shape[1]
    slot = lax.broadcasted_iota(jnp.int32, (rb, cap), 1).astype(F32)
    rc = rankcol_ref[0]
    af = aff_ref[0]
    hot = jnp.concatenate([jnp.where(rc[:, e:e + 1] == slot, af[:, e:e + 1], 0.0).astype(BF16)
                           for e in range(N_EXPERTS)], axis=1)
    y = y_ref[...]
    acc = _dot(hot, y.reshape(N_EXPERTS * cap, y.shape[2]))
    o_ref[0] = x_ref[0] + mod_ref[0, 5:6, :] * _rms(acc, gain_ref[3:4, :])


def _combine(y, rankcol, aff, x, mod, gains, cap):
    bsz, n, d = x.shape
    rb = min(n, 512)
    tok = lambda b, i: (b, i, 0)
    return pl.pallas_call(
        functools.partial(_combine_kernel, cap=cap),
        grid=(bsz, n // rb),
        in_specs=[
            pl.BlockSpec((N_EXPERTS, cap, d), lambda b, i: (0, b, 0)),
            pl.BlockSpec((1, rb, LANES), tok),
            pl.BlockSpec((1, rb, LANES), tok),
            pl.BlockSpec((1, rb, d), tok),
            pl.BlockSpec((1, 6, d), lambda b, i: (b, 0, 0)),
            pl.BlockSpec((4, d), lambda b, i: (0, 0)),
        ],
        out_specs=pl.BlockSpec((1, rb, d), tok),
        out_shape=jax.ShapeDtypeStruct((bsz, n, d), F32),
        compiler_params=_cparams("parallel", "arbitrary"),
        name="moe_combine",
    )(y, rankcol, aff, x, mod, gains)


COMBINE_ROWS = 512
COMBINE_WIN = 128


def _combine_pair_kernel(cnt_ref, y_ref, rankcol_ref, aff_ref, x_ref, mod_ref, gain_ref, o_ref, extra_ref, *, cap):
    b = pl.program_id(0)
    i = pl.program_id(1)
    nrb = pl.num_programs(1)
    rb = x_ref.shape[1]
    slot_w = lax.broadcasted_iota(jnp.int32, (rb, COMBINE_WIN), 1).astype(F32)
    rc = rankcol_ref[0]
    af = aff_ref[0]
    los, unfit = [], []
    for e in range(N_EXPERTS):
        c0 = cnt_ref[b, e * (nrb + 1) + i]
        c1 = cnt_ref[b, e * (nrb + 1) + i + 1]
        lo = jnp.minimum((c0 // 16) * 16, cap - COMBINE_WIN)
        los.append(pl.multiple_of(lo, 16))
        unfit.append(c1 - lo > COMBINE_WIN)
    def operands(e0):
        hot, rows = [], []
        for e in (e0, e0 + 1):
            hot.append(jnp.where(rc[:, e:e + 1] - los[e].astype(F32) == slot_w, af[:, e:e + 1], 0.0).astype(BF16))
            rows.append(y_ref[e, pl.ds(los[e], COMBINE_WIN), :])
        return jnp.concatenate(hot, axis=1), jnp.concatenate(rows, axis=0)

    acc = jnp.zeros((rb, x_ref.shape[2]), F32)
    nxt = operands(0)
    for e0 in range(0, N_EXPERTS, 2):
        cur = nxt
        if e0 + 2 < N_EXPERTS:
            nxt = operands(e0 + 2)
        acc = acc + _dot(*cur)

    def finish(total):
        o_ref[0] = x_ref[0] + mod_ref[0, 5:6, :] * _rms(total, gain_ref[3:4, :])

    any_unfit = functools.reduce(jnp.logical_or, unfit)

    @pl.when(jnp.logical_not(any_unfit))
    def _():
        finish(acc)

    @pl.when(any_unfit)
    def _():
        extra_ref[...] = jnp.zeros_like(extra_ref)
        slot_f = lax.broadcasted_iota(jnp.int32, (rb, cap), 1)
        for e in range(N_EXPERTS):
            @pl.when(unfit[e])
            def _():
                outside = (slot_f < los[e]) | (slot_f >= los[e] + COMBINE_WIN)
                hot = jnp.where((rc[:, e:e + 1] == slot_f.astype(F32)) & outside, af[:, e:e + 1], 0.0).astype(BF16)
                extra_ref[...] += _dot(hot, y_ref[e])
        finish(acc + extra_ref[...])


def _combine_pair(y, rankcol, aff, cnt, x, mod, gains, cap):
    bsz, n, d = x.shape
    rb = COMBINE_ROWS
    nrb = n // rb
    step = rb // LANES
    bounds = cnt[:, :, 0:nrb * step + 1:step].reshape(bsz, N_EXPERTS * (nrb + 1))
    tok = lambda b, i, c: (b, i, 0)
    return pl.pallas_call(
        functools.partial(_combine_pair_kernel, cap=cap),
        grid_spec=pltpu.PrefetchScalarGridSpec(
            num_scalar_prefetch=1,
            grid=(bsz, nrb),
            in_specs=[
                pl.BlockSpec((N_EXPERTS, cap, d), lambda b, i, c: (0, b, 0)),
                pl.BlockSpec((1, rb, LANES), tok),
                pl.BlockSpec((1, rb, LANES), tok),
                pl.BlockSpec((1, rb, d), tok),
                pl.BlockSpec((1, 6, d), lambda b, i, c: (b, 0, 0)),
                pl.BlockSpec((4, d), lambda b, i, c: (0, 0)),
            ],
            out_specs=pl.BlockSpec((1, rb, d), tok),
            scratch_shapes=[pltpu.VMEM((rb, d), F32)],
        ),
        out_shape=jax.ShapeDtypeStruct((bsz, n, d), F32),
        compiler_params=_cparams("parallel", "arbitrary"),
        name="moe_combine_pair",
    )(bounds, y, rankcol, aff, x, mod, gains)


def _relayout_w_in(w):
    cut = GLA_KW + GLA_VW + 2 * GLA_RANK
    pad = jnp.zeros(w.shape[:2] + (LANES - 2 * GLA_RANK,), w.dtype)
    return jnp.concatenate([w[:, :, :cut], pad, w[:, :, cut:]], axis=2).astype(BF16)


def _rope_tables(t):
    half = HEAD_DIM // 4
    freqs = ROPE_BASE ** (-np.arange(half, dtype=np.float32) / half)
    pos = np.arange(t)
    ang_r = (pos // GRID_W).astype(np.float32)[:, None] * freqs
    ang_c = (pos % GRID_W).astype(np.float32)[:, None] * freqs
    cos = np.concatenate([np.cos(ang_r), np.cos(ang_r), np.cos(ang_c), np.cos(ang_c)], axis=1)
    sin = np.concatenate([-np.sin(ang_r), np.sin(ang_r), -np.sin(ang_c), np.sin(ang_c)], axis=1)
    reps = LANES // HEAD_DIM
    return jnp.asarray(np.tile(cos, (1, reps)), F32), jnp.asarray(np.tile(sin, (1, reps)), F32)


def _moe(h_list, afft_list, aff_list, x_list, mod_list, gains, w_gate, w_up, w_down, layer):
    caps = [CAPACITY_FACTOR * h.shape[1] // N_EXPERTS for h in h_list]
    ranks = [_topc(a, cap) for a, cap in zip(afft_list, caps)]
    long = [h.shape[1] >= GATHER_WIN * GATHER_TILE and h.shape[1] % COMBINE_ROWS == 0 and cap >= 2 * COMBINE_WIN
            for h, cap in zip(h_list, caps)]
    xs = [_gather_win(h, r[0], r[2], cap) if lg else _gather(h, r[0], cap)
          for h, r, cap, lg in zip(h_list, ranks, caps, long)]
    ys = _ffn(xs[0], xs[1] if len(xs) > 1 else None, w_gate, w_up, w_down, layer)
    return [_combine_pair(y, r[1], aff, r[2], x, mod, gains, cap) if lg else _combine(y, r[1], aff, x, mod, gains, cap)
            for y, r, aff, x, mod, cap, lg in zip(ys, ranks, aff_list, x_list, mod_list, caps, long)]


def kernel(x, c, ctx, c_ctx, w_mod, b_mod, norm_gains, w_in, w_out, gla_a_up, gla_a_bias, gla_norm,
           nat_rpb, swa_sink, w_router, w_gate, w_up, w_down):
    bsz, t, d = x.shape
    m = ctx.shape[1]
    depth = w_mod.shape[0]

    cc = jnp.concatenate([c, c_ctx[None], jnp.zeros((16 - bsz - 1, d), F32)], axis=0)
    mod_all = _modulation(cc, w_mod, b_mod)
    rope_tabs = _rope_tables(t)
    zero_state = jnp.zeros((bsz, GLA_VW, GLA_KW), F32)
    w_in_b = _relayout_w_in(w_in)
    w_out_b = w_out.astype(BF16)
    wr_b = jnp.pad(w_router, ((0, 0), (0, 0), (0, LANES - N_EXPERTS))).astype(BF16)

    xc = ctx
    for l in range(depth):
        update_ctx = l < depth - 1
        mod = mod_all[l, :bsz].reshape(bsz, 6, d)
        mod_c = jnp.broadcast_to(mod_all[l, bsz].reshape(1, 6, d), (bsz, 6, d))
        gains = norm_gains[l]
        aup = jnp.zeros((LANES, 2 * GLA_KW), F32)
        aup = aup.at[0:GLA_RANK, 0:GLA_KW].set(gla_a_up[l, 0]).at[GLA_RANK:2 * GLA_RANK, GLA_KW:].set(gla_a_up[l, 1])
        abias = gla_a_bias[l].reshape(1, 2 * GLA_KW)
        gla_gain = jnp.tile(gla_norm[l], GLA_HEADS).reshape(1, GLA_VW)
        sink_tab = jnp.broadcast_to(swa_sink[l][:, None] * LOG2E, (SWA_Q_HEADS, LANES))

        gla_p, nat_p, swa_p = _in_proj(x, mod, gains, w_in_b, l, aup, abias, rope_tabs, 1024)
        flat = lambda a: a.reshape(1, bsz * m, a.shape[-1])
        unflat = lambda a: a.reshape(bsz, m, a.shape[-1])
        ctx_tile = min(bsz * m, 1024)
        gla_c, nat_c, swa_c = map(unflat, _in_proj(flat(xc), mod_c[:1], gains, w_in_b, l, aup, abias, None, ctx_tile))

        gla_co, s_f, s_b = _gla(gla_c, zero_state, zero_state, gla_gain)
        gla_o, _, _ = _gla(gla_p, s_f, s_b, gla_gain)
        nat_o = _nat(nat_p, nat_c, _nat_bias_table(nat_rpb[l]))
        swa_o = _swa(swa_p, swa_c, sink_tab)

        x1, h2, aff, aff_t = _out_proj(gla_o, nat_o, swa_o, x, mod, gains, w_out_b, wr_b, l, 1024)
        if update_ctx:
            nat_co, swa_co = _ctx_attn(nat_c, swa_c, sink_tab)
            xc1, hc2, aff_c, aff_ct = _out_proj(flat(gla_co), flat(nat_co), flat(swa_co), flat(xc), mod_c[:1], gains,
                                                w_out_b, wr_b, l, ctx_tile)
            xc1, hc2, aff_c = unflat(xc1), unflat(hc2), unflat(aff_c)
            aff_ct = aff_ct.reshape(N_EXPERTS, bsz, m).transpose(1, 0, 2)
            x, xc = _moe([h2, hc2], [aff_t, aff_ct], [aff, aff_c], [x1, xc1], [mod, mod_c], gains,
                         w_gate, w_up, w_down, l)
        else:
            (x,) = _moe([h2], [aff_t], [aff], [x1], [mod], gains, w_gate, w_up, w_down, l)
    return x
```

```python
import functools

import jax
import jax.numpy as jnp
import numpy as np
from jax import lax
from jax.experimental import pallas as pl
from jax.experimental.pallas import tpu as pltpu

F32 = jnp.float32
BF16 = jnp.bfloat16

D_MODEL = 1024
GRID_W = 64
HEAD_DIM = 64
GLA_HEADS = 4
GLA_DK = 32
GLA_DV = 64
GLA_RANK = 16
GLA_TAU = 16.0
GLA_CHUNK = 64
NAT_HEADS = 4
NAT_WIN_ROWS = 8
NAT_WIN_COLS = 16
SWA_Q_HEADS = 8
SWA_KV_HEADS = 2
SWA_WINDOW = 128
SWA_BLOCK = 128
ROPE_BASE = 10000.0
N_EXPERTS = 16
EXPERT_FF = 1024
CAPACITY_FACTOR = 2
NORM_EPS = 1e-6

GLA_KW = GLA_HEADS * GLA_DK
GLA_VW = GLA_HEADS * GLA_DV
NAT_W = NAT_HEADS * HEAD_DIM
SWA_QW = SWA_Q_HEADS * HEAD_DIM
SWA_KW = SWA_KV_HEADS * HEAD_DIM
LANES = 128
NEG_BIG = -1e30
LOG2E = 1.4426950408889634

GLA_OUT_W = 2 * GLA_KW + 2 * GLA_VW + 2 * GLA_KW
NAT_PROJ_W = 3 * NAT_W
SWA_PROJ_W = SWA_QW + 4 * SWA_KW
W_GK = 0
W_GV = W_GK + GLA_KW
W_GA = W_GV + GLA_VW
W_NK = W_GA + LANES
W_SK = W_NK + 2 * NAT_W
W_SV = W_SK + SWA_KW
W_Q0 = W_SV + SWA_KW
W_GQ = W_Q0
W_GG = W_GQ + GLA_KW
W_NQ = W_GG + GLA_VW
W_SQ = W_NQ + NAT_W
IN_PROJ_W = W_SQ + SWA_QW

VMEM_LIMIT = 56 * 1024 * 1024


def _cparams(*sem):
    return pltpu.CompilerParams(dimension_semantics=sem, vmem_limit_bytes=VMEM_LIMIT)


def _dot(a, b):
    return jnp.dot(a, b, preferred_element_type=F32)


def _dot_nt(a, b):
    return lax.dot_general(a, b, (((1,), (1,)), ((), ())), preferred_element_type=F32)


def _dot_tn(a, b):
    return lax.dot_general(a, b, (((0,), (0,)), ((), ())), preferred_element_type=F32)


def _split2(a):
    hi = a.astype(BF16)
    lo = (a - hi.astype(F32)).astype(BF16)
    return hi, lo


def _split3(a):
    hi = a.astype(BF16)
    r = a - hi.astype(F32)
    mid = r.astype(BF16)
    lo = (r - mid.astype(F32)).astype(BF16)
    return hi, mid, lo


def _dot_f32(a, b):
    ah, al = _split2(a)
    bh, bl = _split2(b)
    return _dot(ah, bh) + _dot(al, bh) + _dot(ah, bl)


def _rms(x, gain):
    return x * lax.rsqrt(jnp.mean(x * x, axis=-1, keepdims=True) + NORM_EPS) * gain


def _silu(x):
    return x * jax.nn.sigmoid(x)


def _mod_kernel(c_ref, w_ref, b_ref, o_ref):
    a = _silu(c_ref[...])
    o_ref[0] = _dot_f32(a, w_ref[0]) + b_ref[0]


def _modulation(cc, w_mod, b_mod):
    depth, d, n = w_mod.shape
    r = cc.shape[0]
    tn = 1536
    return pl.pallas_call(
        _mod_kernel,
        grid=(depth, n // tn),
        in_specs=[
            pl.BlockSpec((r, d), lambda l, j: (0, 0)),
            pl.BlockSpec((1, d, tn), lambda l, j: (l, 0, j)),
            pl.BlockSpec((1, 1, tn), lambda l, j: (l, 0, j)),
        ],
        out_specs=pl.BlockSpec((1, r, tn), lambda l, j: (l, 0, j)),
        out_shape=jax.ShapeDtypeStruct((depth, r, n), F32),
        compiler_params=_cparams("parallel", "parallel"),
        name="adaln_mod",
    )(cc, w_mod, b_mod.reshape(depth, 1, n))


def _rope_rotate(x, first_half):
    up = pltpu.roll(x, LANES - 16, 1)
    down = pltpu.roll(x, 16, 1)
    return jnp.where(first_half, up, down)


IN_PROJ_PART_ROWS = 512
CUMSUM_ROWS = 512


def _in_proj_kernel(*refs, rope):
    if rope:
        x_ref, mod_ref, gain_ref, w_ref, aup_ref, ab_ref, tri_ref, cos_ref, sin_ref, gla_ref, nat_ref, swa_ref = refs
    else:
        x_ref, mod_ref, gain_ref, w_ref, aup_ref, ab_ref, tri_ref, gla_ref, nat_ref, swa_ref = refs
    tm = x_ref.shape[1]
    pm = min(tm, IN_PROJ_PART_ROWS)
    n_parts = tm // pm
    kq = 2 * GLA_KW + 2 * GLA_VW
    scale = HEAD_DIM ** -0.5 * LOG2E
    nck = pm // GLA_CHUNK
    tri = tri_ref[...]

    def project(part):
        x = x_ref[0, pl.ds(part * pm, pm), :]
        y = _rms(x, gain_ref[0:1, :])
        h = (y * (1.0 + mod_ref[0, 1:2, :]) + mod_ref[0, 0:1, :]).astype(BF16)
        return _dot(h, w_ref[0, :, 0:W_Q0]), _dot(h, w_ref[0, :, W_Q0:])

    nxt = project(0)
    for part in range(n_parts):
        rows = pl.ds(part * pm, pm)
        pkv, pq = nxt
        if part + 1 < n_parts:
            nxt = project(part + 1)

        gla_ref[0, rows, 0:GLA_KW] = pkv[:, W_GK:W_GK + GLA_KW]
        gla_ref[0, rows, GLA_KW:2 * GLA_KW] = pq[:, W_GQ - W_Q0:W_GQ - W_Q0 + GLA_KW]
        gla_ref[0, rows, 2 * GLA_KW:2 * GLA_KW + GLA_VW] = pkv[:, W_GV:W_GV + GLA_VW]
        gla_ref[0, rows, 2 * GLA_KW + GLA_VW:kq] = pq[:, W_GG - W_Q0:W_GG - W_Q0 + GLA_VW]
        z = _dot_f32(pkv[:, W_GA:W_GA + LANES], aup_ref[...]) + ab_ref[...]
        log_a = (jnp.minimum(z, 0.0) - jnp.log1p(jnp.exp(-jnp.abs(z)))) * (1.0 / GLA_TAU)
        hi, lo = _split2(log_a)
        cr = tri.shape[0]
        cum = jnp.concatenate([_dot(tri, hi[i * cr:(i + 1) * cr]) + _dot(tri, lo[i * cr:(i + 1) * cr])
                               for i in range(pm // cr)], axis=0)
        gla_ref[0, rows, kq:kq + GLA_KW] = cum[:, 0:GLA_KW]
        cb = cum[:, GLA_KW:].reshape(nck, GLA_CHUNK, GLA_KW)
        lb = log_a[:, GLA_KW:].reshape(nck, GLA_CHUNK, GLA_KW)
        from_end = cb[:, GLA_CHUNK - 1:GLA_CHUNK, :] - cb + lb
        gla_ref[0, rows, kq + GLA_KW:kq + 2 * GLA_KW] = from_end.reshape(pm, GLA_KW)

        nat_ref[0, rows, 0:NAT_W] = (pq[:, W_NQ - W_Q0:W_NQ - W_Q0 + NAT_W] * scale).astype(BF16)
        nat_ref[0, rows, NAT_W:] = pkv[:, W_NK:W_NK + 2 * NAT_W].astype(BF16)

        if rope:
            cos = cos_ref[rows, :]
            sin = sin_ref[rows, :]
        lane = lax.broadcasted_iota(jnp.int32, (pm, LANES), 1)
        first_half = (lane % 32) < 16
        low_head = lane < HEAD_DIM

        def rotary(t):
            return t * cos + _rope_rotate(t, first_half) * sin if rope else t

        for j in range(SWA_QW // LANES):
            t = rotary(pq[:, W_SQ - W_Q0 + j * LANES:W_SQ - W_Q0 + (j + 1) * LANES]) * scale
            swa_ref[0, rows, j * LANES:(j + 1) * LANES] = t.astype(BF16)
        for j, t in enumerate((rotary(pkv[:, W_SK:W_SK + SWA_KW]), pkv[:, W_SV:W_SV + SWA_KW])):
            other = pltpu.roll(t, HEAD_DIM, 1)
            o = SWA_QW + 2 * j * SWA_KW
            swa_ref[0, rows, o:o + LANES] = jnp.where(low_head, t, other).astype(BF16)
            swa_ref[0, rows, o + LANES:o + 2 * LANES] = jnp.where(low_head, other, t).astype(BF16)


def _in_proj(x, mod, gains, w, layer, aup, abias, rope_tabs, tm):
    bsz, t, d = x.shape
    rope = rope_tabs is not None
    cum_rows = min(tm, IN_PROJ_PART_ROWS, CUMSUM_ROWS)
    in_specs = [
        pl.BlockSpec((1, tm, d), lambda b, i: (b, i, 0)),
        pl.BlockSpec((1, 6, d), lambda b, i: (b, 0, 0)),
        pl.BlockSpec((4, d), lambda b, i: (0, 0)),
        pl.BlockSpec((1, d, IN_PROJ_W), lambda b, i: (layer, 0, 0)),
        pl.BlockSpec((LANES, 2 * GLA_KW), lambda b, i: (0, 0)),
        pl.BlockSpec((1, 2 * GLA_KW), lambda b, i: (0, 0)),
        pl.BlockSpec((cum_rows, cum_rows), lambda b, i: (0, 0)),
    ]
    r = np.arange(cum_rows)
    tri = (r[:, None] // GLA_CHUNK == r[None, :] // GLA_CHUNK) & (r[:, None] >= r[None, :])
    args = [x, mod, gains, w, aup, abias, jnp.asarray(tri, BF16)]
    if rope:
        in_specs += [pl.BlockSpec((tm, LANES), lambda b, i: (i, 0))] * 2
        args += list(rope_tabs)
    return pl.pallas_call(
        functools.partial(_in_proj_kernel, rope=rope),
        grid=(bsz, t // tm),
        in_specs=in_specs,
        out_specs=[
            pl.BlockSpec((1, tm, GLA_OUT_W), lambda b, i: (b, i, 0)),
            pl.BlockSpec((1, tm, NAT_PROJ_W), lambda b, i: (b, i, 0)),
            pl.BlockSpec((1, tm, SWA_PROJ_W), lambda b, i: (b, i, 0)),
        ],
        out_shape=[
            jax.ShapeDtypeStruct((bsz, t, GLA_OUT_W), F32),
            jax.ShapeDtypeStruct((bsz, t, NAT_PROJ_W), BF16),
            jax.ShapeDtypeStruct((bsz, t, SWA_PROJ_W), BF16),
        ],
        compiler_params=_cparams("parallel", "parallel"),
        name="in_proj_rope" if rope else "in_proj",
    )(*args)


GLA_SAMPLES_PER_STEP = 2


def _gla_kernel(p_ref, s0f_ref, s0b_ref, gain_ref, o_ref, sff_ref, sfb_ref, acc_ref, sf_ref, sb_ref):
    nb, t, _ = p_ref.shape
    c = GLA_CHUNK
    nc = t // c
    qi = lax.broadcasted_iota(jnp.int32, (c, GLA_HEADS * c), 0)
    kj = lax.broadcasted_iota(jnp.int32, (c, GLA_HEADS * c), 1) % c
    lower = qi >= kj
    upper = qi <= kj
    srow = lax.broadcasted_iota(jnp.int32, (GLA_VW, GLA_KW), 0) // GLA_DV
    scol = lax.broadcasted_iota(jnp.int32, (GLA_VW, GLA_KW), 1) // GLA_DK
    head_diag = srow == scol
    q_scale = GLA_DK ** -0.5

    sf_ref[...] = s0f_ref[...]
    sb_ref[...] = s0b_ref[...]
    acc_ref[...] = jnp.zeros_like(acc_ref)

    def state_free(n, base, end_row, cum_off):
        rows = pl.ds(base, c)
        k = p_ref[n, rows, 0:GLA_KW]
        q = p_ref[n, rows, GLA_KW:2 * GLA_KW] * q_scale
        vb = p_ref[n, rows, 2 * GLA_KW:2 * GLA_KW + GLA_VW].astype(BF16)
        cum = p_ref[n, rows, cum_off:cum_off + GLA_KW]
        mid = cum[c // 2:c // 2 + 1, :]
        total = cum[end_row:end_row + 1, :]
        qs = (q * jnp.exp(cum - mid)).astype(BF16)
        ks = (k * jnp.exp(mid - cum)).astype(BF16)
        s = _dot_nt(qs, _head_stack(ks, GLA_DK, GLA_HEADS))
        kd = (k * jnp.exp(total - cum)).astype(BF16)
        upd = _dot_tn(vb, kd)
        qe = (q * jnp.exp(cum)).astype(BF16)
        return rows, s, upd, qe, vb, jnp.exp(total)

    def with_state(n, keep, st_ref, rows, s, upd, qe, vb, decay):
        st = st_ref[n]
        inter = _dot_nt(qe, st.astype(BF16))
        s = jnp.where(keep, s, 0.0).astype(BF16)
        intra = _dot(s, _head_stack(vb, GLA_DV, GLA_HEADS))
        acc_ref[n, rows, :] += inter + intra
        st_ref[n] = st * decay + jnp.where(head_diag, upd, 0.0)

    def body(i, carry):
        fwd = [state_free(n, pl.multiple_of(i * c, c), c - 1, 2 * GLA_KW + 2 * GLA_VW) for n in range(nb)]
        bwd = [state_free(n, pl.multiple_of((nc - 1 - i) * c, c), 0, 3 * GLA_KW + 2 * GLA_VW) for n in range(nb)]
        for n in range(nb):
            with_state(n, lower, sf_ref, *fwd[n])
            with_state(n, upper, sb_ref, *bwd[n])
        return carry

    lax.fori_loop(0, nc, body, 0, unroll=4)
    sff_ref[...] = sf_ref[...]
    sfb_ref[...] = sb_ref[...]

    hr = lax.broadcasted_iota(jnp.int32, (GLA_VW, GLA_VW), 0) // GLA_DV
    hc = lax.broadcasted_iota(jnp.int32, (GLA_VW, GLA_VW), 1) // GLA_DV
    head_ones = (hr == hc).astype(BF16)
    blk = 256
    blocks = [(n, j) for n in range(nb) for j in range(t // blk)]

    def mean_square(n, j):
        o = acc_ref[n, pl.ds(j * blk, blk), :]
        hi, mid, lo = _split3(o * o)
        return o, (_dot(hi, head_ones) + _dot(mid, head_ones) + _dot(lo, head_ones)) * (1.0 / GLA_DV)

    nxt = mean_square(*blocks[0])
    for i, (n, j) in enumerate(blocks):
        rows = pl.ds(j * blk, blk)
        o, ms = nxt
        if i + 1 < len(blocks):
            nxt = mean_square(*blocks[i + 1])
        gate = p_ref[n, rows, 2 * GLA_KW + GLA_VW:2 * GLA_KW + 2 * GLA_VW]
        o_ref[n, rows, :] = (o * lax.rsqrt(ms + NORM_EPS) * gain_ref[...] * _silu(gate)).astype(BF16)


def _gla(p, s0f, s0b, gain):
    bsz, t, w = p.shape
    nb = GLA_SAMPLES_PER_STEP if bsz % GLA_SAMPLES_PER_STEP == 0 else 1
    st_spec = pl.BlockSpec((nb, GLA_VW, GLA_KW), lambda b: (b, 0, 0))
    st_shape = jax.ShapeDtypeStruct((bsz, GLA_VW, GLA_KW), F32)
    return pl.pallas_call(
        _gla_kernel,
        grid=(bsz // nb,),
        in_specs=[
            pl.BlockSpec((nb, t, w), lambda b: (b, 0, 0)),
            st_spec, st_spec,
            pl.BlockSpec((1, GLA_VW), lambda b: (0, 0)),
        ],
        out_specs=[pl.BlockSpec((nb, t, GLA_VW), lambda b: (b, 0, 0)), st_spec, st_spec],
        out_shape=[jax.ShapeDtypeStruct((bsz, t, GLA_VW), BF16), st_shape, st_shape],
        scratch_shapes=[
            pltpu.VMEM((nb, t, GLA_VW), F32),
            pltpu.VMEM((nb, GLA_VW, GLA_KW), F32),
            pltpu.VMEM((nb, GLA_VW, GLA_KW), F32),
        ],
        compiler_params=_cparams("parallel"),
        name="gla_bidir",
    )(p, s0f, s0b, gain)


def _head_stack(q, width, n):
    lane = lax.broadcasted_iota(jnp.int32, (1, q.shape[1]), 1) // width
    return jnp.concatenate([jnp.where(lane == h, q, jnp.zeros_like(q)) for h in range(n)], axis=0)


def _head_unstack(o, width, n):
    rows = o.shape[0] // n
    lane = lax.broadcasted_iota(jnp.int32, (1, o.shape[1]), 1) // width
    out = o[0:rows]
    for h in range(1, n):
        out = jnp.where(lane == h, o[h * rows:(h + 1) * rows], out)
    return out


NAT_ROWS_PER_STEP = 16


def _nat_kernel(lat_ref, ctx_ref, bias_ref, o_ref):
    n_rows = lat_ref.shape[1] // GRID_W
    kc = ctx_ref[0, :, NAT_W:2 * NAT_W]
    vc = ctx_ref[0, :, 2 * NAT_W:3 * NAT_W]
    n_loc = NAT_WIN_ROWS * GRID_W

    def scores(i):
        r = pl.program_id(1) * NAT_ROWS_PER_STEP + i
        r0 = jnp.clip(r - NAT_WIN_ROWS // 2, 0, n_rows - NAT_WIN_ROWS)
        q = lat_ref[0, pl.ds(pl.multiple_of(r * GRID_W, GRID_W), GRID_W), 0:NAT_W]
        krows = pl.ds(pl.multiple_of(r0 * GRID_W, GRID_W), n_loc)
        qs = _head_stack(q, HEAD_DIM, NAT_HEADS)
        first = r0 - r + NAT_WIN_ROWS - 1
        bias = jnp.concatenate([bias_ref[first + 2 * j] for j in range(NAT_WIN_ROWS // 2)], axis=1)
        s = jnp.concatenate([_dot_nt(qs, lat_ref[0, krows, NAT_W:2 * NAT_W]) + bias, _dot_nt(qs, kc)], axis=1)
        return s, krows

    def finish(i, s, krows):
        p = jnp.exp2((s - jnp.max(s, axis=-1, keepdims=True)).astype(BF16))
        den = jnp.sum(p, axis=-1, keepdims=True, dtype=F32)
        o = (_dot(p[:, 0:n_loc], lat_ref[0, krows, 2 * NAT_W:3 * NAT_W]) + _dot(p[:, n_loc:], vc)) / den
        o_ref[0, i * GRID_W:(i + 1) * GRID_W, :] = _head_unstack(o, HEAD_DIM, NAT_HEADS).astype(BF16)

    nxt = scores(0)
    for i in range(NAT_ROWS_PER_STEP):
        cur = nxt
        if i + 1 < NAT_ROWS_PER_STEP:
            nxt = scores(i + 1)
        finish(i, *cur)


def _nat(lat, ctx, bias):
    bsz, t, w = lat.shape
    m = ctx.shape[1]
    rows = NAT_ROWS_PER_STEP * GRID_W
    return pl.pallas_call(
        _nat_kernel,
        grid=(bsz, t // rows),
        in_specs=[
            pl.BlockSpec((1, t, w), lambda b, r: (b, 0, 0)),
            pl.BlockSpec((1, m, w), lambda b, r: (b, 0, 0)),
            pl.BlockSpec(bias.shape, lambda b, r: (0, 0, 0)),
        ],
        out_specs=pl.BlockSpec((1, rows, NAT_W), lambda b, r: (b, r, 0)),
        out_shape=jax.ShapeDtypeStruct((bsz, t, NAT_W), BF16),
        compiler_params=_cparams("parallel", "arbitrary"),
        name="nat_attn",
    )(lat, ctx, bias)


def _nat_bias_table(rpb):
    cols = np.arange(GRID_W)
    col_start = np.clip(cols - NAT_WIN_COLS // 2, 0, GRID_W - NAT_WIN_COLS)
    kc = np.arange(GRID_W)
    inside = (kc[None, :] >= col_start[:, None]) & (kc[None, :] < col_start[:, None] + NAT_WIN_COLS)
    rel = kc[None, :] - cols[:, None] + NAT_WIN_COLS - 1
    col_sel = (rel[:, :, None] == np.arange(2 * NAT_WIN_COLS - 1)) & inside[:, :, None]
    tab = jnp.einsum("hab,ckb->ahck", rpb.astype(F32), jnp.asarray(col_sel, F32), precision=lax.Precision.HIGHEST)
    tab = jnp.where(jnp.asarray(inside)[None, None], tab * LOG2E, NEG_BIG)
    tab = tab.reshape(2 * NAT_WIN_ROWS - 1, NAT_HEADS * GRID_W, GRID_W)
    return jnp.concatenate([tab[:-1], tab[1:]], axis=2)


def _sink_column(sink_ref, first_head, n_heads, rows):
    return jnp.concatenate(
        [jnp.broadcast_to(sink_ref[first_head + j:first_head + j + 1, 0:1], (rows, 1)) for j in range(n_heads)], axis=0)


def _softmax_pv_sink(s, v, sink):
    m = jnp.maximum(jnp.max(s, axis=-1, keepdims=True), sink)
    p = jnp.exp2((s - m).astype(BF16))
    o = _dot(p, jnp.concatenate([v, jnp.ones_like(v)], axis=1))
    w = v.shape[1]
    return o[:, 0:w] / (o[:, w:] + jnp.exp2(sink - m))


SWA_BLOCKS_PER_STEP = 4


def _swa_kernel(lat_ref, ctx_ref, sink_ref, o_ref):
    t = lat_ref.shape[1]
    m = ctx_ref.shape[1]
    blk = SWA_BLOCK
    n_loc = 3 * blk
    qi = lax.broadcasted_iota(jnp.int32, (blk, n_loc), 0)
    kj = lax.broadcasted_iota(jnp.int32, (blk, n_loc), 1)
    group = SWA_Q_HEADS // SWA_KV_HEADS
    chains = [(a, g) for a in range(SWA_BLOCKS_PER_STEP) for g in range(SWA_KV_HEADS)]

    def scores(a, g):
        n = pl.program_id(1) * SWA_BLOCKS_PER_STEP + a
        base = pl.multiple_of(jnp.clip((n - 1) * blk, 0, t - n_loc), blk)
        qrows = pl.ds(pl.multiple_of(n * blk, blk), blk)
        krows = pl.ds(base, n_loc)
        mask_add = jnp.where(jnp.abs(base + kj - n * blk - qi) <= SWA_WINDOW, 0.0, NEG_BIG)[None]
        kcol = pl.ds(SWA_QW + g * LANES, LANES)
        k = jnp.concatenate([lat_ref[0, krows, kcol], ctx_ref[0, :, kcol]], axis=0)
        qs = jnp.concatenate(
            [_head_stack(lat_ref[0, qrows, pl.ds((g * 2 + p) * LANES, LANES)], HEAD_DIM, 2) for p in range(2)], axis=0)
        s = _dot_nt(qs, k).reshape(group, blk, n_loc + m)
        s = jnp.concatenate([s[:, :, 0:n_loc] + mask_add, s[:, :, n_loc:]], axis=2).reshape(group * blk, n_loc + m)
        return s, krows

    def finish(a, g, s, krows):
        vcol = pl.ds(SWA_QW + 2 * SWA_KW + g * LANES, LANES)
        v = jnp.concatenate([lat_ref[0, krows, vcol], ctx_ref[0, :, vcol]], axis=0)
        o = _softmax_pv_sink(s, v, _sink_column(sink_ref, g * group, group, blk))
        for p in range(2):
            pair = _head_unstack(o[2 * p * blk:(2 * p + 2) * blk], HEAD_DIM, 2)
            o_ref[0, a * blk:(a + 1) * blk, pl.ds((g * 2 + p) * LANES, LANES)] = pair.astype(BF16)

    nxt = scores(*chains[0])
    for i, (a, g) in enumerate(chains):
        cur = nxt
        if i + 1 < len(chains):
            nxt = scores(*chains[i + 1])
        finish(a, g, *cur)


def _swa(lat, ctx, sink_tab):
    bsz, t, w = lat.shape
    m = ctx.shape[1]
    rows = SWA_BLOCKS_PER_STEP * SWA_BLOCK
    return pl.pallas_call(
        _swa_kernel,
        grid=(bsz, t // rows),
        in_specs=[
            pl.BlockSpec((1, t, w), lambda b, n: (b, 0, 0)),
            pl.BlockSpec((1, m, w), lambda b, n: (b, 0, 0)),
            pl.BlockSpec((SWA_Q_HEADS, LANES), lambda b, n: (0, 0)),
        ],
        out_specs=pl.BlockSpec((1, rows, SWA_QW), lambda b, n: (b, n, 0)),
        out_shape=jax.ShapeDtypeStruct((bsz, t, SWA_QW), BF16),
        compiler_params=_cparams("parallel", "arbitrary"),
        name="swa_attn",
    )(lat, ctx, sink_tab)


def _ctx_attn_kernel(nat_ref, swa_ref, sink_ref, on_ref, os_ref):
    m = nat_ref.shape[1]
    q = nat_ref[0, :, 0:NAT_W]
    k = nat_ref[0, :, NAT_W:2 * NAT_W]
    v = nat_ref[0, :, 2 * NAT_W:3 * NAT_W]
    group = SWA_Q_HEADS // SWA_KV_HEADS

    def swa_scores(g):
        k = swa_ref[0, :, pl.ds(SWA_QW + g * LANES, LANES)]
        qs = jnp.concatenate(
            [_head_stack(swa_ref[0, :, pl.ds((g * 2 + p) * LANES, LANES)], HEAD_DIM, 2) for p in range(2)], axis=0)
        return _dot_nt(qs, k)

    s = _dot_nt(_head_stack(q, HEAD_DIM, NAT_HEADS), k)
    s_swa = [swa_scores(g) for g in range(SWA_KV_HEADS)]
    p = jnp.exp2((s - jnp.max(s, axis=-1, keepdims=True)).astype(BF16))
    o = _dot(p, v) / jnp.sum(p, axis=-1, keepdims=True, dtype=F32)
    on_ref[0] = _head_unstack(o, HEAD_DIM, NAT_HEADS).astype(BF16)

    for g in range(SWA_KV_HEADS):
        v = swa_ref[0, :, pl.ds(SWA_QW + 2 * SWA_KW + g * LANES, LANES)]
        o = _softmax_pv_sink(s_swa[g], v, _sink_column(sink_ref, g * group, group, m))
        for pr in range(2):
            pair = _head_unstack(o[2 * pr * m:(2 * pr + 2) * m], HEAD_DIM, 2)
            os_ref[0, :, pl.ds((g * 2 + pr) * LANES, LANES)] = pair.astype(BF16)


def _ctx_attn(nat_c, swa_c, sink_tab):
    bsz, m, _ = nat_c.shape
    return pl.pallas_call(
        _ctx_attn_kernel,
        grid=(bsz,),
        in_specs=[
            pl.BlockSpec((1, m, NAT_PROJ_W), lambda b: (b, 0, 0)),
            pl.BlockSpec((1, m, SWA_PROJ_W), lambda b: (b, 0, 0)),
            pl.BlockSpec((SWA_Q_HEADS, LANES), lambda b: (0, 0)),
        ],
        out_specs=[
            pl.BlockSpec((1, m, NAT_W), lambda b: (b, 0, 0)),
            pl.BlockSpec((1, m, SWA_QW), lambda b: (b, 0, 0)),
        ],
        out_shape=[
            jax.ShapeDtypeStruct((bsz, m, NAT_W), BF16),
            jax.ShapeDtypeStruct((bsz, m, SWA_QW), BF16),
        ],
        compiler_params=_cparams("parallel"),
        name="ctx_attn",
    )(nat_c, swa_c, sink_tab)


OUT_PROJ_PART_ROWS = 256


def _out_proj_kernel(gla_ref, nat_ref, swa_ref, x_ref, mod_ref, gain_ref, w_ref, wr_ref,
                     x1_ref, h2_ref, aff_ref, afft_ref):
    tm = x_ref.shape[1]
    half = min(tm, OUT_PROJ_PART_ROWS)
    n_parts = tm // half

    def project(part):
        rows = pl.ds(part * half, half)
        return (_dot(gla_ref[0, rows, :], w_ref[0, 0:GLA_VW, :])
                + _dot(nat_ref[0, rows, :], w_ref[0, GLA_VW:GLA_VW + NAT_W, :])
                + _dot(swa_ref[0, rows, :], w_ref[0, GLA_VW + NAT_W:, :]))

    nxt = project(0)
    for part in range(n_parts):
        rows = pl.ds(part * half, half)
        mix = nxt
        if part + 1 < n_parts:
            nxt = project(part + 1)
        x1 = x_ref[0, rows, :] + mod_ref[0, 2:3, :] * _rms(mix, gain_ref[1:2, :])
        x1_ref[0, rows, :] = x1
        h2 = (_rms(x1, gain_ref[2:3, :]) * (1.0 + mod_ref[0, 4:5, :]) + mod_ref[0, 3:4, :]).astype(BF16)
        h2_ref[0, rows, :] = h2
        logits = _dot(h2, wr_ref[0])
        lane = lax.broadcasted_iota(jnp.int32, logits.shape, 1)
        logits = jnp.where(lane < N_EXPERTS, logits, NEG_BIG)
        e = jnp.exp(logits - jnp.max(logits, axis=-1, keepdims=True))
        aff = e / jnp.sum(e, axis=-1, keepdims=True)
        aff_ref[0, rows, :] = aff
        afft_ref[0, :, rows] = aff.T[0:N_EXPERTS, :]


def _out_proj(gla_o, nat_o, swa_o, x, mod, gains, w_out, wr, layer, tm):
    bsz, t, d = x.shape
    tok = lambda b, i: (b, i, 0)
    const = lambda b, i: (0, 0)
    return pl.pallas_call(
        _out_proj_kernel,
        grid=(bsz, t // tm),
        in_specs=[
            pl.BlockSpec((1, tm, GLA_VW), tok),
            pl.BlockSpec((1, tm, NAT_W), tok),
            pl.BlockSpec((1, tm, SWA_QW), tok),
            pl.BlockSpec((1, tm, d), tok),
            pl.BlockSpec((1, 6, d), lambda b, i: (b, 0, 0)),
            pl.BlockSpec((4, d), const),
            pl.BlockSpec((1, d, d), lambda b, i: (layer, 0, 0)),
            pl.BlockSpec((1, d, LANES), lambda b, i: (layer, 0, 0)),
        ],
        out_specs=[
            pl.BlockSpec((1, tm, d), tok),
            pl.BlockSpec((1, tm, d), tok),
            pl.BlockSpec((1, tm, LANES), tok),
            pl.BlockSpec((1, N_EXPERTS, tm), lambda b, i: (b, 0, i)),
        ],
        out_shape=[
            jax.ShapeDtypeStruct((bsz, t, d), F32),
            jax.ShapeDtypeStruct((bsz, t, d), BF16),
            jax.ShapeDtypeStruct((bsz, t, LANES), F32),
            jax.ShapeDtypeStruct((bsz, N_EXPERTS, t), F32),
        ],
        compiler_params=_cparams("parallel", "parallel"),
        name="out_proj_router",
    )(gla_o, nat_o, swa_o, x, mod, gains, w_out, wr)


def _excl_prefix(x):
    rows, n = x.shape
    nblk = n // LANES
    r = lax.broadcasted_iota(jnp.int32, (LANES, LANES), 0)
    c = lax.broadcasted_iota(jnp.int32, (LANES, LANES), 1)
    strict_upper = (r < c).astype(BF16)
    stacked = jnp.concatenate([x[:, j * LANES:(j + 1) * LANES] for j in range(nblk)], axis=0).astype(BF16)
    local = _dot(stacked, strict_upper)
    totals = jnp.sum(stacked.astype(F32), axis=-1, keepdims=True)
    out = []
    offs = [jnp.zeros((rows, 1), F32)]
    for j in range(nblk):
        out.append(local[j * rows:(j + 1) * rows] + offs[-1])
        offs.append(offs[-1] + totals[j * rows:(j + 1) * rows])
    return jnp.concatenate(out, axis=1), offs


def _topc_kernel(aff_ref, rank_ref, rankcol_ref, cnt_ref, *, cap):
    aff = aff_ref[...]

    def step(i, thr):
        cand = thr | (jnp.int32(1) << (30 - i))
        cnt = jnp.sum((aff >= pltpu.bitcast(cand, F32)).astype(F32), axis=-1, keepdims=True)
        return jnp.where(cnt >= cap, cand, thr)

    thr = lax.fori_loop(0, 31, step, jnp.zeros((aff.shape[0], 1), jnp.int32))
    thr = pltpu.bitcast(thr, F32)
    gt = aff > thr
    eq = aff == thr
    need = cap - jnp.sum(gt.astype(F32), axis=-1, keepdims=True)
    eq_before, _ = _excl_prefix(eq.astype(F32))
    sel = gt | (eq & (eq_before < need))
    sel_before, offs = _excl_prefix(sel.astype(F32))
    rank = jnp.where(sel, sel_before, -1.0)
    rank_ref[...] = rank
    lane = lax.broadcasted_iota(jnp.int32, (rank.shape[0], LANES), 1)
    cnt = jnp.zeros((rank.shape[0], LANES), F32)
    for j, off in enumerate(offs):
        cnt = jnp.where(lane == j, off, cnt)
    cnt_ref[...] = cnt.astype(jnp.int32)
    n = rank.shape[1]
    fill = jnp.full((LANES - N_EXPERTS, n), -1.0, F32)
    for b in range(rankcol_ref.shape[0]):
        rankcol_ref[b] = jnp.concatenate([rank[b * N_EXPERTS:(b + 1) * N_EXPERTS], fill], axis=0).T


def _topc(aff_t, cap):
    bsz, e, n = aff_t.shape
    rows = bsz * e
    rank, rankcol, cnt = pl.pallas_call(
        functools.partial(_topc_kernel, cap=cap),
        grid=(1,),
        in_specs=[pl.BlockSpec((rows, n), lambda i: (0, 0))],
        out_specs=[pl.BlockSpec((rows, n), lambda i: (0, 0)), pl.BlockSpec((bsz, n, LANES), lambda i: (0, 0, 0)),
                   pl.BlockSpec((rows, LANES), lambda i: (0, 0))],
        out_shape=[jax.ShapeDtypeStruct((rows, n), F32), jax.ShapeDtypeStruct((bsz, n, LANES), F32),
                   jax.ShapeDtypeStruct((rows, LANES), jnp.int32)],
        compiler_params=_cparams("arbitrary"),
        name="expert_topc",
    )(aff_t.reshape(rows, n))
    return rank.reshape(bsz, e, n), rankcol, cnt.reshape(bsz, e, LANES)


def _gather_kernel(h_ref, rank_ref, xs_ref, *, cap):
    n = h_ref.shape[1]
    slot = lax.broadcasted_iota(jnp.int32, (cap, n), 0).astype(F32)
    onehot = jnp.concatenate([(slot == rank_ref[0, e:e + 1, :]).astype(BF16) for e in range(N_EXPERTS)], axis=0)
    xs = _dot(onehot, h_ref[0]).astype(BF16)
    xs_ref[...] = xs.reshape(N_EXPERTS, cap, xs.shape[1])


def _gather(h, rank, cap):
    bsz, n, d = h.shape
    return pl.pallas_call(
        functools.partial(_gather_kernel, cap=cap),
        grid=(bsz,),
        in_specs=[
            pl.BlockSpec((1, n, d), lambda b: (b, 0, 0)),
            pl.BlockSpec((1, N_EXPERTS, n), lambda b: (b, 0, 0)),
        ],
        out_specs=pl.BlockSpec((N_EXPERTS, cap, d), lambda b: (0, b, 0)),
        out_shape=jax.ShapeDtypeStruct((N_EXPERTS, bsz * cap, d), BF16),
        compiler_params=_cparams("parallel"),
        name="moe_gather",
    )(h, rank)


GATHER_TILE = 256
GATHER_WIN = 5
GATHER_SLOTS = 128


def _gather_win_kernel(cnt_ref, h_ref, rank_ref, xs_ref, *, cap):
    b = pl.program_id(0)
    n = h_ref.shape[1]
    nt = n // GATHER_TILE
    win = GATHER_WIN * GATHER_TILE
    slot_w = lax.broadcasted_iota(jnp.int32, (GATHER_SLOTS, win), 0).astype(F32).astype(BF16)
    slot_f = lax.broadcasted_iota(jnp.int32, (cap, n), 0).astype(F32).astype(BF16)
    one = jnp.ones((), BF16)
    zero = jnp.zeros((), BF16)

    n_sb = cap // GATHER_SLOTS
    group = 4

    def per_group(g, carry):
        starts, fits = [], []
        for j in range(group):
            base = (g * group + j) * (nt + 1)
            c = [cnt_ref[b, base + kt] for kt in range(nt + 1)]
            for sb in range(n_sb):
                lo_slot = sb * GATHER_SLOTS
                first = sum((c[kt + 1] <= lo_slot).astype(jnp.int32) for kt in range(nt))
                last = sum((c[kt] < lo_slot + GATHER_SLOTS).astype(jnp.int32) for kt in range(nt))
                fits.append(last - first <= GATHER_WIN)
                starts.append(jnp.minimum(first, nt - GATHER_WIN))
        all_fit = functools.reduce(jnp.logical_and, fits)

        @pl.when(all_fit)
        def _():
            for j in range(group):
                e = g * group + j
                for sb in range(n_sb):
                    start = starts[j * n_sb + sb]
                    rk = jnp.concatenate([rank_ref[0, e, pl.ds(start + i, 1), :] for i in range(GATHER_WIN)], axis=1)
                    onehot = jnp.where(slot_w == (rk - sb * GATHER_SLOTS).astype(BF16), one, zero)
                    hw = h_ref[0, pl.ds(pl.multiple_of(start * GATHER_TILE, GATHER_TILE), win), :]
                    xs_ref[e, pl.ds(sb * GATHER_SLOTS, GATHER_SLOTS), :] = _dot(onehot, hw).astype(BF16)

        @pl.when(jnp.logical_not(all_fit))
        def _():
            for j in range(group):
                e = g * group + j
                rk = jnp.concatenate([rank_ref[0, e, kt:kt + 1, :] for kt in range(nt)], axis=1).astype(BF16)
                onehot = jnp.where(slot_f == rk, one, zero)
                xs_ref[e] = _dot(onehot, h_ref[0]).astype(BF16)
        return carry

    lax.fori_loop(0, N_EXPERTS // group, per_group, 0)


def _gather_win(h, rank, cnt, cap):
    bsz, n, d = h.shape
    nt = n // GATHER_TILE
    step = GATHER_TILE // LANES
    bounds = cnt[:, :, 0:nt * step + 1:step].reshape(bsz, N_EXPERTS * (nt + 1))
    return pl.pallas_call(
        functools.partial(_gather_win_kernel, cap=cap),
        grid_spec=pltpu.PrefetchScalarGridSpec(
            num_scalar_prefetch=1,
            grid=(bsz,),
            in_specs=[
                pl.BlockSpec((1, n, d), lambda b, c: (b, 0, 0)),
                pl.BlockSpec((1, N_EXPERTS, nt, GATHER_TILE), lambda b, c: (b, 0, 0, 0)),
            ],
            out_specs=pl.BlockSpec((N_EXPERTS, cap, d), lambda b, c: (0, b, 0)),
        ),
        out_shape=jax.ShapeDtypeStruct((N_EXPERTS, bsz * cap, d), BF16),
        compiler_params=_cparams("parallel"),
        name="moe_gather_win",
    )(bounds, h, rank.reshape(bsz, N_EXPERTS, nt, GATHER_TILE))


FFN_PARTS = 4


def _ffn_kernel(*refs, has_ctx):
    if has_ctx:
        xs_ref, xc_ref, wg_ref, wu_ref, wd_ref, y_ref, yc_ref, wg_bf, wu_bf, wd_bf = refs
    else:
        xs_ref, wg_ref, wu_ref, wd_ref, y_ref, wg_bf, wu_bf, wd_bf = refs
    s = pl.program_id(0)
    j = pl.program_id(1)
    n_exp = pl.num_programs(0) - 1
    piece = wg_ref.shape[2]

    @pl.when((s < n_exp) & (j < FFN_PARTS))
    def _():
        rows = pl.ds(pl.multiple_of(j * piece, piece), piece)
        wg_bf[s % 2, rows, :] = wg_ref[0, 0].astype(BF16)
        wu_bf[s % 2, rows, :] = wu_ref[0, 0].astype(BF16)
        wd_bf[s % 2, rows, :] = wd_ref[0, 0].astype(BF16)

    def swiglu(x_ref, o_ref):
        cur = (s + 1) % 2
        xs = x_ref[0]
        hid = (_silu(_dot(xs, wg_bf[cur])) * _dot(xs, wu_bf[cur])).astype(BF16)
        o_ref[0] = _dot(hid, wd_bf[cur]).astype(BF16)

    @pl.when((s > 0) & (j < FFN_PARTS))
    def _():
        swiglu(xs_ref, y_ref)

    if has_ctx:
        @pl.when((s > 0) & (j == FFN_PARTS))
        def _():
            swiglu(xc_ref, yc_ref)


def _ffn(xs, xc, w_gate, w_up, w_down, layer):
    _, e, d, ff = w_gate.shape
    rb = xs.shape[1] // FFN_PARTS
    has_ctx = xc is not None
    last = FFN_PARTS - 1

    def x_idx(s, j):
        return (jnp.maximum(s - 1, 0), jnp.where(s > 0, jnp.minimum(j, last), 0), 0)

    def w_idx(s, j):
        return (layer, jnp.minimum(s, e - 1), jnp.minimum(j, last), 0)

    xspec = pl.BlockSpec((1, rb, d), x_idx)
    cspecs, cargs, cshapes = [], [], []
    if has_ctx:
        cspecs = [pl.BlockSpec((1, xc.shape[1], d), lambda s, j: (jnp.maximum(s - 1, 0), 0, 0))]
        cargs = [xc]
        cshapes = [jax.ShapeDtypeStruct(xc.shape, BF16)]
    out = pl.pallas_call(
        functools.partial(_ffn_kernel, has_ctx=has_ctx),
        grid=(e + 1, FFN_PARTS + int(has_ctx)),
        in_specs=[xspec] + cspecs + [
            pl.BlockSpec((1, 1, d // FFN_PARTS, ff), w_idx),
            pl.BlockSpec((1, 1, d // FFN_PARTS, ff), w_idx),
            pl.BlockSpec((1, 1, ff // FFN_PARTS, d), w_idx),
        ],
        out_specs=[xspec] + cspecs,
        out_shape=[jax.ShapeDtypeStruct(xs.shape, BF16)] + cshapes,
        scratch_shapes=[pltpu.VMEM((2, d, ff), BF16), pltpu.VMEM((2, d, ff), BF16), pltpu.VMEM((2, ff, d), BF16)],
        compiler_params=_cparams("arbitrary", "arbitrary"),
        name="moe_ffn",
    )(xs, *cargs, w_gate, w_up, w_down)
    return out


def _combine_kernel(y_ref, rankcol_ref, aff_ref, x_ref, mod_ref, gain_ref, o_ref, *, cap):
    rb = x_ref.shape[1]
    slot = lax.broadcasted_iota(jnp.int32, (rb, cap), 1).astype(F32)
    rc = rankcol_ref[0]
    af = aff_ref[0]
    hot = jnp.concatenate([jnp.where(rc[:, e:e + 1] == slot, af[:, e:e + 1], 0.0).astype(BF16)
                           for e in range(N_EXPERTS)], axis=1)
    y = y_ref[...]
    acc = _dot(hot, y.reshape(N_EXPERTS * cap, y.shape[2]))
    o_ref[0] = x_ref[0] + mod_ref[0, 5:6, :] * _rms(acc, gain_ref[3:4, :])


def _combine(y, rankcol, aff, x, mod, gains, cap):
    bsz, n, d = x.shape
    rb = min(n, 512)
    tok = lambda b, i: (b, i, 0)
    return pl.pallas_call(
        functools.partial(_combine_kernel, cap=cap),
        grid=(bsz, n // rb),
        in_specs=[
            pl.BlockSpec((N_EXPERTS, cap, d), lambda b, i: (0, b, 0)),
            pl.BlockSpec((1, rb, LANES), tok),
            pl.BlockSpec((1, rb, LANES), tok),
            pl.BlockSpec((1, rb, d), tok),
            pl.BlockSpec((1, 6, d), lambda b, i: (b, 0, 0)),
            pl.BlockSpec((4, d), lambda b, i: (0, 0)),
        ],
        out_specs=pl.BlockSpec((1, rb, d), tok),
        out_shape=jax.ShapeDtypeStruct((bsz, n, d), F32),
        compiler_params=_cparams("parallel", "arbitrary"),
        name="moe_combine",
    )(y, rankcol, aff, x, mod, gains)


COMBINE_ROWS = 512
COMBINE_WIN = 128


def _combine_pair_kernel(cnt_ref, y_ref, rankcol_ref, aff_ref, x_ref, mod_ref, gain_ref, o_ref, extra_ref, *, cap):
    b = pl.program_id(0)
    i = pl.program_id(1)
    nrb = pl.num_programs(1)
    rb = x_ref.shape[1]
    slot_w = lax.broadcasted_iota(jnp.int32, (rb, COMBINE_WIN), 1).astype(F32)
    rc = rankcol_ref[0]
    af = aff_ref[0]
    los, unfit = [], []
    for e in range(N_EXPERTS):
        c0 = cnt_ref[b, e * (nrb + 1) + i]
        c1 = cnt_ref[b, e * (nrb + 1) + i + 1]
        lo = jnp.minimum((c0 // 16) * 16, cap - COMBINE_WIN)
        los.append(pl.multiple_of(lo, 16))
        unfit.append(c1 - lo > COMBINE_WIN)
    def operands(e0):
        hot, rows = [], []
        for e in (e0, e0 + 1):
            hot.append(jnp.where(rc[:, e:e + 1] - los[e].astype(F32) == slot_w, af[:, e:e + 1], 0.0).astype(BF16))
            rows.append(y_ref[e, pl.ds(los[e], COMBINE_WIN), :])
        return jnp.concatenate(hot, axis=1), jnp.concatenate(rows, axis=0)

    acc = jnp.zeros((rb, x_ref.shape[2]), F32)
    nxt = operands(0)
    for e0 in range(0, N_EXPERTS, 2):
        cur = nxt
        if e0 + 2 < N_EXPERTS:
            nxt = operands(e0 + 2)
        acc = acc + _dot(*cur)

    def finish(total):
        o_ref[0] = x_ref[0] + mod_ref[0, 5:6, :] * _rms(total, gain_ref[3:4, :])

    any_unfit = functools.reduce(jnp.logical_or, unfit)

    @pl.when(jnp.logical_not(any_unfit))
    def _():
        finish(acc)

    @pl.when(any_unfit)
    def _():
        extra_ref[...] = jnp.zeros_like(extra_ref)
        slot_f = lax.broadcasted_iota(jnp.int32, (rb, cap), 1)
        for e in range(N_EXPERTS):
            @pl.when(unfit[e])
            def _():
                outside = (slot_f < los[e]) | (slot_f >= los[e] + COMBINE_WIN)
                hot = jnp.where((rc[:, e:e + 1] == slot_f.astype(F32)) & outside, af[:, e:e + 1], 0.0).astype(BF16)
                extra_ref[...] += _dot(hot, y_ref[e])
        finish(acc + extra_ref[...])


def _combine_pair(y, rankcol, aff, cnt, x, mod, gains, cap):
    bsz, n, d = x.shape
    rb = COMBINE_ROWS
    nrb = n // rb
    step = rb // LANES
    bounds = cnt[:, :, 0:nrb * step + 1:step].reshape(bsz, N_EXPERTS * (nrb + 1))
    tok = lambda b, i, c: (b, i, 0)
    return pl.pallas_call(
        functools.partial(_combine_pair_kernel, cap=cap),
        grid_spec=pltpu.PrefetchScalarGridSpec(
            num_scalar_prefetch=1,
            grid=(bsz, nrb),
            in_specs=[
                pl.BlockSpec((N_EXPERTS, cap, d), lambda b, i, c: (0, b, 0)),
                pl.BlockSpec((1, rb, LANES), tok),
                pl.BlockSpec((1, rb, LANES), tok),
                pl.BlockSpec((1, rb, d), tok),
                pl.BlockSpec((1, 6, d), lambda b, i, c: (b, 0, 0)),
                pl.BlockSpec((4, d), lambda b, i, c: (0, 0)),
            ],
            out_specs=pl.BlockSpec((1, rb, d), tok),
            scratch_shapes=[pltpu.VMEM((rb, d), F32)],
        ),
        out_shape=jax.ShapeDtypeStruct((bsz, n, d), F32),
        compiler_params=_cparams("parallel", "arbitrary"),
        name="moe_combine_pair",
    )(bounds, y, rankcol, aff, x, mod, gains)


def _relayout_w_in(w):
    cut = GLA_KW + GLA_VW + 2 * GLA_RANK
    pad = jnp.zeros(w.shape[:2] + (LANES - 2 * GLA_RANK,), w.dtype)
    return jnp.concatenate([w[:, :, :cut], pad, w[:, :, cut:]], axis=2).astype(BF16)


def _rope_tables(t):
    half = HEAD_DIM // 4
    freqs = ROPE_BASE ** (-np.arange(half, dtype=np.float32) / half)
    pos = np.arange(t)
    ang_r = (pos // GRID_W).astype(np.float32)[:, None] * freqs
    ang_c = (pos % GRID_W).astype(np.float32)[:, None] * freqs
    cos = np.concatenate([np.cos(ang_r), np.cos(ang_r), np.cos(ang_c), np.cos(ang_c)], axis=1)
    sin = np.concatenate([-np.sin(ang_r), np.sin(ang_r), -np.sin(ang_c), np.sin(ang_c)], axis=1)
    reps = LANES // HEAD_DIM
    return jnp.asarray(np.tile(cos, (1, reps)), F32), jnp.asarray(np.tile(sin, (1, reps)), F32)


def _moe(h_list, afft_list, aff_list, x_list, mod_list, gains, w_gate, w_up, w_down, layer):
    caps = [CAPACITY_FACTOR * h.shape[1] // N_EXPERTS for h in h_list]
    ranks = [_topc(a, cap) for a, cap in zip(afft_list, caps)]
    long = [h.shape[1] >= GATHER_WIN * GATHER_TILE and h.shape[1] % COMBINE_ROWS == 0 and cap >= 2 * COMBINE_WIN
            for h, cap in zip(h_list, caps)]
    xs = [_gather_win(h, r[0], r[2], cap) if lg else _gather(h, r[0], cap)
          for h, r, cap, lg in zip(h_list, ranks, caps, long)]
    ys = _ffn(xs[0], xs[1] if len(xs) > 1 else None, w_gate, w_up, w_down, layer)
    return [_combine_pair(y, r[1], aff, r[2], x, mod, gains, cap) if lg else _combine(y, r[1], aff, x, mod, gains, cap)
            for y, r, aff, x, mod, cap, lg in zip(ys, ranks, aff_list, x_list, mod_list, caps, long)]


def kernel(x, c, ctx, c_ctx, w_mod, b_mod, norm_gains, w_in, w_out, gla_a_up, gla_a_bias, gla_norm,
           nat_rpb, swa_sink, w_router, w_gate, w_up, w_down):
    bsz, t, d = x.shape
    m = ctx.shape[1]
    depth = w_mod.shape[0]

    cc = jnp.concatenate([c, c_ctx[None], jnp.zeros((16 - bsz - 1, d), F32)], axis=0)
    mod_all = _modulation(cc, w_mod, b_mod)
    rope_tabs = _rope_tables(t)
    zero_state = jnp.zeros((bsz, GLA_VW, GLA_KW), F32)
    w_in_b = _relayout_w_in(w_in)
    w_out_b = w_out.astype(BF16)
    wr_b = jnp.pad(w_router, ((0, 0), (0, 0), (0, LANES - N_EXPERTS))).astype(BF16)

    xc = ctx
    for l in range(depth):
        update_ctx = l < depth - 1
        mod = mod_all[l, :bsz].reshape(bsz, 6, d)
        mod_c = jnp.broadcast_to(mod_all[l, bsz].reshape(1, 6, d), (bsz, 6, d))
        gains = norm_gains[l]
        aup = jnp.zeros((LANES, 2 * GLA_KW), F32)
        aup = aup.at[0:GLA_RANK, 0:GLA_KW].set(gla_a_up[l, 0]).at[GLA_RANK:2 * GLA_RANK, GLA_KW:].set(gla_a_up[l, 1])
        abias = gla_a_bias[l].reshape(1, 2 * GLA_KW)
        gla_gain = jnp.tile(gla_norm[l], GLA_HEADS).reshape(1, GLA_VW)
        sink_tab = jnp.broadcast_to(swa_sink[l][:, None] * LOG2E, (SWA_Q_HEADS, LANES))

        gla_p, nat_p, swa_p = _in_proj(x, mod, gains, w_in_b, l, aup, abias, rope_tabs, 1024)
        gla_c, nat_c, swa_c = _in_proj(xc, mod_c, gains, w_in_b, l, aup, abias, None, m)

        gla_co, s_f, s_b = _gla(gla_c, zero_state, zero_state, gla_gain)
        gla_o, _, _ = _gla(gla_p, s_f, s_b, gla_gain)
        nat_o = _nat(nat_p, nat_c, _nat_bias_table(nat_rpb[l]))
        swa_o = _swa(swa_p, swa_c, sink_tab)

        x1, h2, aff, aff_t = _out_proj(gla_o, nat_o, swa_o, x, mod, gains, w_out_b, wr_b, l, 1024)
        if update_ctx:
            nat_co, swa_co = _ctx_attn(nat_c, swa_c, sink_tab)
            xc1, hc2, aff_c, aff_ct = _out_proj(gla_co, nat_co, swa_co, xc, mod_c, gains, w_out_b, wr_b, l, m)
            x, xc = _moe([h2, hc2], [aff_t, aff_ct], [aff, aff_c], [x1, xc1], [mod, mod_c], gains,
                         w_gate, w_up, w_down, l)
        else:
            (x,) = _moe([h2], [aff_t], [aff], [x1], [mod], gains, w_gate, w_up, w_down, l)
    return x
```
